```python
import math
import jax, jax.numpy as jnp
from jax import lax
import numpy as np

D_MODEL = 2048
BATCH = 8
SEQ = 2048
DEPTH = 1

CONV_DIM = 1024
CONV_WIDTH = 3
DIL_PATTERNS = ((128, 1), (512, 4), (2048, 16))
N_DIL_GROUPS = 3
HEADS_PER_GROUP = 4
HEAD_DIM = 128
DIL_DIM = N_DIL_GROUPS * HEADS_PER_GROUP * HEAD_DIM
DIL_OUT_DIM = HEADS_PER_GROUP * HEAD_DIM
ATT_BLOCK = 128
ALIBI_MAX = 8.0
MEM_LEN = 256
MEM_HEADS = 4
MEM_HEAD_DIM = 256
MEM_DIM = MEM_HEADS * MEM_HEAD_DIM
N_BRANCHES = 3
IN_DIM = 3 * CONV_DIM + 3 * DIL_DIM + MEM_DIM + N_BRANCHES * D_MODEL
N_GROUPS = 4
EXPERTS_PER_GROUP = 8
TOP_K = 2
D_EXPERT = 512
ALPHA = (2 * DEPTH) ** 0.25
BETA = (8 * DEPTH) ** -0.25
LN_EPS = 1e-5

kernel_name = "hybrid_conv_dilattn_mem_hmoe_deepnorm"


def _in_splits():
    sizes = [CONV_DIM, CONV_DIM, CONV_DIM, DIL_DIM, DIL_DIM, DIL_DIM, MEM_DIM, N_BRANCHES * D_MODEL]
    out, acc = [], 0
    for s in sizes[:-1]:
        acc += s
        out.append(acc)
    return out


def _alibi_slopes():
    n = N_DIL_GROUPS * HEADS_PER_GROUP
    s = 2.0 ** (-ALIBI_MAX * np.arange(1, n + 1, dtype=np.float32) / n)
    return jnp.asarray(s.reshape(N_DIL_GROUPS, HEADS_PER_GROUP), dtype=jnp.float32)


def layer_norm(x, g, b):
    xf = x.astype(jnp.float32)
    mu = jnp.mean(xf, -1, keepdims=True)
    var = jnp.mean(jnp.square(xf - mu), -1, keepdims=True)
    y = (xf - mu) * lax.rsqrt(var + LN_EPS) * g.astype(jnp.float32) + b.astype(jnp.float32)
    return y.astype(x.dtype)


def short_conv(bg, cg, h, w_conv):
    u = cg * h
    S = u.shape[1]
    up = jnp.pad(u, ((0, 0), (CONV_WIDTH - 1, 0), (0, 0)))
    y = w_conv[0] * up[:, 0:S]
    for j in range(1, CONV_WIDTH):
        y = y + w_conv[j] * up[:, j:j + S]
    return bg * y


def dilated_group_attention(q, k, v, slopes, window, dilation):
    B, S, H, Dh = q.shape
    n_back = window // dilation
    L = S // dilation
    nb = -(-L // ATT_BLOCK)
    Lp = nb * ATT_BLOCK

    def to_sub(t, front):
        t = jnp.swapaxes(t.reshape(B, L, dilation, H, Dh), 1, 2)
        return jnp.pad(t, ((0, 0), (0, 0), (front, Lp - L), (0, 0), (0, 0)))

    def ctx(t):
        tb = to_sub(t, ATT_BLOCK).reshape(B, dilation, nb + 1, ATT_BLOCK, H, Dh)
        return jnp.concatenate([tb[:, :, :-1], tb[:, :, 1:]], axis=3)

    qb = to_sub(q, 0).reshape(B, dilation, nb, ATT_BLOCK, H, Dh)
    kc, vc = ctx(k), ctx(v)
    s = jnp.einsum('bdnqhe,bdnkhe->bdnhqk', qb, kc,
                   preferred_element_type=jnp.float32) * (Dh ** -0.5)
    qi = jnp.arange(ATT_BLOCK)[:, None] + ATT_BLOCK
    kj = jnp.arange(2 * ATT_BLOCK)[None, :]
    jrel = qi - kj
    kabs = (jnp.arange(nb)[:, None, None] - 1) * ATT_BLOCK + kj[None]
    valid = (jrel >= 0) & (jrel <= n_back) & (kabs >= 0)
    bias = -slopes[:, None, None] * (jrel * dilation).astype(jnp.float32)
    s = jnp.where(valid[None, None, :, None], s + bias[None, None, None], -jnp.inf)
    m = jnp.max(s, -1, keepdims=True)
    p = jnp.exp(s - m)
    den = jnp.sum(p, -1, keepdims=True)
    o = jnp.einsum('bdnhqk,bdnkhe->bdnqhe', p.astype(v.dtype), vc,
                   preferred_element_type=jnp.float32)
    o = o / jnp.swapaxes(den, 3, 4)
    lse = jnp.swapaxes((m + jnp.log(den))[..., 0], 3, 4)

    def from_sub(t):
        t = t.reshape((B, dilation, Lp) + t.shape[4:])[:, :, :L]
        t = jnp.swapaxes(t, 1, 2)
        return t.reshape((B, S) + t.shape[3:])

    return from_sub(o), from_sub(lse)


def dilated_mixture_attention(q, k, v):
    B, S, _ = q.shape
    shp = (B, S, N_DIL_GROUPS, HEADS_PER_GROUP, HEAD_DIM)
    q, k, v = q.reshape(shp), k.reshape(shp), v.reshape(shp)
    slopes = _alibi_slopes()
    outs, lses = [], []
    for g, (window, dilation) in enumerate(DIL_PATTERNS):
        o, l = dilated_group_attention(q[:, :, g], k[:, :, g], v[:, :, g], slopes[g], window, dilation)
        outs.append(o)
        lses.append(l)
    outs = jnp.stack(outs, 0)
    wts = jax.nn.softmax(jnp.stack(lses, 0), axis=0)
    o = jnp.sum(wts[..., None] * outs, axis=0)
    return o.reshape(B, S, DIL_OUT_DIM).astype(q.dtype)


def memory_attention(mq, mk, mv):
    B, S, _ = mq.shape
    q = mq.reshape(B, S, MEM_HEADS, MEM_HEAD_DIM)
    k = mk.reshape(B, -1, MEM_HEADS, MEM_HEAD_DIM)
    v = mv.reshape(B, -1, MEM_HEADS, MEM_HEAD_DIM)
    s = jnp.einsum('bshe,bmhe->bhsm', q, k, preferred_element_type=jnp.float32) * (MEM_HEAD_DIM ** -0.5)
    p = jax.nn.softmax(s, axis=-1)
    o = jnp.einsum('bhsm,bmhe->bshe', p.astype(v.dtype), v)
    return o.reshape(B, S, MEM_DIM)


def hierarchical_moe(h, w_group, b_group, w_router, b_router, w_gate, w_up, w_down):
    glog = (h @ w_group + b_group).astype(jnp.float32)
    gprob = jax.nn.softmax(glog, axis=-1)
    gsel = jnp.argmax(glog, axis=-1)
    gw = jnp.take_along_axis(gprob, gsel[..., None], axis=-1)
    elog_all = jnp.einsum('bsd,gde->bsge', h, w_router) + b_router
    elog = jnp.take_along_axis(elog_all, gsel[..., None, None], axis=2)[:, :, 0].astype(jnp.float32)
    top_v, top_i = lax.top_k(elog, TOP_K)
    top_w = jax.nn.softmax(top_v, axis=-1) * gw
    local = jnp.sum(jax.nn.one_hot(top_i, EXPERTS_PER_GROUP, dtype=jnp.float32) * top_w[..., None], axis=2)
    combine = (jax.nn.one_hot(gsel, N_GROUPS, dtype=jnp.float32)[..., None] * local[:, :, None, :]).astype(h.dtype)
    y = jnp.zeros_like(h)
    for g in range(N_GROUPS):
        a = jnp.einsum('bsd,edf->bsef', h, w_gate[g])
        u = jnp.einsum('bsd,edf->bsef', h, w_up[g])
        hid = jax.nn.silu(a) * u * combine[:, :, g, :, None]
        y = y + jnp.einsum('bsef,efd->bsd', hid, w_down[g])
    return y


def setup_inputs(seed: int = 0) -> dict:
    key = jax.random.key(seed)
    ks = jax.random.split(key, 24)
    f32 = jnp.float32
    D, L = D_MODEL, DEPTH

    def nrm(k, shape, scale):
        return jax.random.normal(k, shape, f32) * scale

    v0 = 3 * CONV_DIM + 2 * DIL_DIM
    col_scale = jnp.ones((IN_DIM,), f32).at[v0:v0 + DIL_DIM].set(BETA)
    mem_col = jnp.ones((2 * MEM_DIM,), f32).at[MEM_DIM:].set(BETA)
    G, E, F = N_GROUPS, EXPERTS_PER_GROUP, D_EXPERT
    return {
        "x": nrm(ks[0], (BATCH, SEQ, D), 1.0),
        "mem": nrm(ks[1], (BATCH, MEM_LEN, D), 1.0),
        "ln_mem_g": 1.0 + nrm(ks[2], (D,), 0.02),
        "ln_mem_b": nrm(ks[3], (D,), 0.02),
        "w_in": nrm(ks[4], (L, D, IN_DIM), D ** -0.5) * col_scale,
        "b_in": nrm(ks[5], (L, IN_DIM), 0.02),
        "w_conv": nrm(ks[6], (L, CONV_WIDTH, CONV_DIM), CONV_WIDTH ** -0.5),
        "w_conv_out": nrm(ks[7], (L, CONV_DIM, D), BETA * CONV_DIM ** -0.5),
        "w_dil_out": nrm(ks[8], (L, DIL_OUT_DIM, D), BETA * DIL_OUT_DIM ** -0.5),
        "w_mem_kv": nrm(ks[9], (L, D, 2 * MEM_DIM), D ** -0.5) * mem_col,
        "w_mem_out": nrm(ks[10], (L, MEM_DIM, D), BETA * MEM_DIM ** -0.5),
        "w_o": nrm(ks[11], (L, D, D), BETA * D ** -0.5),
        "ln1_g": 1.0 + nrm(ks[12], (L, D), 0.02),
        "ln1_b": nrm(ks[13], (L, D), 0.02),
        "w_group": nrm(ks[14], (L, D, G), D ** -0.5),
        "b_group": nrm(ks[15], (L, G), 0.01),
        "w_router": nrm(ks[16], (L, G, D, E), D ** -0.5),
        "b_router": nrm(ks[17], (L, G, E), 0.01),
        "w_gate": nrm(ks[18], (L, G, E, D, F), D ** -0.5),
        "w_up": nrm(ks[19], (L, G, E, D, F), D ** -0.5),
        "w_down": nrm(ks[20], (L, G, E, F, D), BETA * F ** -0.5),
        "ln2_g": 1.0 + nrm(ks[21], (L, D), 0.02),
        "ln2_b": nrm(ks[22], (L, D), 0.02),
    }


def reference(x, mem, ln_mem_g, ln_mem_b, w_in, b_in, w_conv, w_conv_out, w_dil_out,
              w_mem_kv, w_mem_out, w_o, ln1_g, ln1_b, w_group, b_group, w_router, b_router,
              w_gate, w_up, w_down, ln2_g, ln2_b):
    B, S, D = x.shape
    mem_n = layer_norm(mem, ln_mem_g, ln_mem_b)
    splits = _in_splits()
    for l in range(DEPTH):
        proj = x @ w_in[l] + b_in[l]
        cb, cc, ch, q, k, v, mq, gates = jnp.split(proj, splits, axis=-1)
        y_conv = short_conv(cb, cc, ch, w_conv[l]) @ w_conv_out[l]
        y_dil = dilated_mixture_attention(q, k, v) @ w_dil_out[l]
        mk, mv = jnp.split(mem_n @ w_mem_kv[l], 2, axis=-1)
        y_mem = memory_attention(mq, mk, mv) @ w_mem_out[l]
        g = jax.nn.sigmoid(gates).reshape(B, S, N_BRANCHES, D)
        merged = g[:, :, 0] * y_conv + g[:, :, 1] * y_dil + g[:, :, 2] * y_mem
        x = layer_norm(ALPHA * x + merged @ w_o[l], ln1_g[l], ln1_b[l])
        y_moe = hierarchical_moe(x, w_group[l], b_group[l], w_router[l], b_router[l],
                                 w_gate[l], w_up[l], w_down[l])
        x = layer_norm(ALPHA * x + y_moe, ln2_g[l], ln2_b[l])
    return x
```

```python
import functools
import math

import numpy as np
import jax
import jax.numpy as jnp
from jax import lax
from jax.experimental import pallas as pl
from jax.experimental.pallas import tpu as pltpu

D_MODEL = 2048
BATCH = 8
SEQ = 2048
TOKENS = BATCH * SEQ
CONV_DIM = 1024
CONV_WIDTH = 3
DIL_PATTERNS = ((128, 1), (512, 4), (2048, 16))
N_DIL_GROUPS = 3
HEADS_PER_GROUP = 4
HEAD_DIM = 128
DIL_DIM = N_DIL_GROUPS * HEADS_PER_GROUP * HEAD_DIM
DIL_OUT_DIM = HEADS_PER_GROUP * HEAD_DIM
ATT_BLOCK = 128
ALIBI_MAX = 8.0
MEM_LEN = 256
MEM_HEADS = 4
MEM_HEAD_DIM = 256
MEM_DIM = MEM_HEADS * MEM_HEAD_DIM
N_BRANCHES = 3
IN_DIM = 3 * CONV_DIM + 3 * DIL_DIM + MEM_DIM + N_BRANCHES * D_MODEL
N_GROUPS = 4
EXPERTS_PER_GROUP = 8
N_EXPERTS = N_GROUPS * EXPERTS_PER_GROUP
D_EXPERT = 512
ALPHA = 2.0 ** 0.25
LN_EPS = 1e-5

OFF_CB = 0
OFF_CC = CONV_DIM
OFF_CH = 2 * CONV_DIM
OFF_Q = 3 * CONV_DIM
OFF_MQ = OFF_Q + 3 * DIL_DIM
OFF_GATE = OFF_MQ + MEM_DIM

LANES = 128
HALF = SEQ // 2
VMEM_LIMIT = 56 * 1024 * 1024

ROUTE_TILE = 512
EXPERT_TILE = 256
N_EXPERT_TILES = 2 * TOKENS // EXPERT_TILE + N_EXPERTS
COMBINE_TILE = 256
ROUTER_LANE0 = N_GROUPS


def _params(sem, limit=VMEM_LIMIT):
    return pltpu.CompilerParams(dimension_semantics=sem, vmem_limit_bytes=limit)


def _layer_norm(x, g, b):
    mu = jnp.mean(x, axis=-1, keepdims=True)
    xc = x - mu
    var = jnp.mean(xc * xc, axis=-1, keepdims=True)
    return xc * lax.rsqrt(var + LN_EPS) * g + b


def _dot(a, b):
    return jnp.dot(a, b, preferred_element_type=jnp.float32)


def _dot_t(a, b):
    return lax.dot_general(a, b, (((1,), (1,)), ((), ())), preferred_element_type=jnp.float32)


def _memkv_kernel(mem_ref, g_ref, b_ref, w_ref, kv_ref):
    y = _layer_norm(mem_ref[...], g_ref[...], b_ref[...])
    kv_ref[...] = _dot(y, w_ref[...]).astype(kv_ref.dtype)


def _memkv(mem, g, b, w):
    return pl.pallas_call(
        _memkv_kernel,
        grid=(BATCH, 2),
        in_specs=[pl.BlockSpec((None, MEM_LEN, D_MODEL), lambda i, j: (i, 0, 0)),
                  pl.BlockSpec((1, D_MODEL), lambda i, j: (0, 0)),
                  pl.BlockSpec((1, D_MODEL), lambda i, j: (0, 0)),
                  pl.BlockSpec((D_MODEL, MEM_DIM), lambda i, j: (0, j))],
        out_specs=pl.BlockSpec((None, MEM_LEN, MEM_DIM), lambda i, j: (i, 0, j)),
        out_shape=jax.ShapeDtypeStruct((BATCH, MEM_LEN, 2 * MEM_DIM), jnp.bfloat16),
        compiler_params=_params(("parallel", "arbitrary")),
        name="mem_kv",
    )(mem, g, b, w)


CONV_TC = 256


def _conv_kernel(x_ref, wb_ref, wc_ref, wh_ref, bb_ref, bc_ref, bh_ref, wconv_ref, s_ref,
                 u_sc, carry_sc):
    half = pl.program_id(1)
    c = pl.program_id(2)
    x = x_ref[...]
    cb = _dot(x, wb_ref[...]) + bb_ref[...]
    cc = _dot(x, wc_ref[...]) + bc_ref[...]
    ch = _dot(x, wh_ref[...]) + bh_ref[...]
    u = cc * ch
    u_sc[0:8, :] = jnp.where(half == 0, 0.0, carry_sc[c])
    u_sc[8:8 + HALF, :] = u
    carry_sc[c] = u[HALF - 8:HALF, :]
    wconv = wconv_ref[...]
    y = (wconv[2:3, :] * u
         + wconv[1:2, :] * u_sc[7:7 + HALF, :]
         + wconv[0:1, :] * u_sc[6:6 + HALF, :])
    s_ref[...] = (cb * y).astype(s_ref.dtype)


def _conv_branch(x, w_in, b_in, w_conv):
    nb = lambda off: off // CONV_TC
    wspec = lambda off: pl.BlockSpec((D_MODEL, CONV_TC), lambda b, h, c, o=nb(off): (0, o + c))
    bspec = lambda off: pl.BlockSpec((1, CONV_TC), lambda b, h, c, o=nb(off): (0, o + c))
    return pl.pallas_call(
        _conv_kernel,
        grid=(BATCH, 2, CONV_DIM // CONV_TC),
        in_specs=[pl.BlockSpec((None, HALF, D_MODEL), lambda b, h, c: (b, h, 0)),
                  wspec(OFF_CB), wspec(OFF_CC), wspec(OFF_CH),
                  bspec(OFF_CB), bspec(OFF_CC), bspec(OFF_CH),
                  pl.BlockSpec((CONV_WIDTH, CONV_TC), lambda b, h, c: (0, c))],
        out_specs=pl.BlockSpec((None, HALF, CONV_TC), lambda b, h, c: (b, h, c)),
        out_shape=jax.ShapeDtypeStruct((BATCH, SEQ, CONV_DIM), jnp.bfloat16),
        scratch_shapes=[pltpu.VMEM((HALF + 8, CONV_TC), jnp.float32),
                        pltpu.VMEM((CONV_DIM // CONV_TC, 8, CONV_TC), jnp.float32)],
        compiler_params=_params(("arbitrary", "arbitrary", "arbitrary")),
        name="conv_branch",
    )(x, w_in, w_in, w_in, b_in, b_in, b_in, w_conv)


QKV_TN = 512
QKV_CHUNKS = QKV_TN // LANES


def _qkv_kernel(x_ref, w_ref, b_ref, o_ref, sc_ref):
    g = lax.rem(pl.program_id(2), N_DIL_GROUPS)
    acc = _dot(x_ref[...], w_ref[...]) + b_ref[...]

    @pl.when(g == 0)
    def _():
        o_ref[...] = acc.astype(o_ref.dtype)

    for gi in (1, 2):
        d = DIL_PATTERNS[gi][1]
        rows = HALF // d

        @pl.when(g == gi)
        def _(d=d, rows=rows):
            for c in range(QKV_CHUNKS):
                sc_ref[c] = acc[:, c * LANES:(c + 1) * LANES]
            for c in range(QKV_CHUNKS):
                for r in range(d):
                    o_ref[r * rows:(r + 1) * rows, c * LANES:(c + 1) * LANES] = (
                        sc_ref[c, pl.ds(r, rows, stride=d), :].astype(o_ref.dtype))


def _qkv_proj(x, w_in, b_in):
    n0 = OFF_Q // QKV_TN
    return pl.pallas_call(
        _qkv_kernel,
        grid=(BATCH, 2, 3 * DIL_DIM // QKV_TN),
        in_specs=[pl.BlockSpec((None, HALF, D_MODEL), lambda b, h, n: (b, h, 0)),
                  pl.BlockSpec((D_MODEL, QKV_TN), lambda b, h, n: (0, n0 + n)),
                  pl.BlockSpec((1, QKV_TN), lambda b, h, n: (0, n0 + n))],
        out_specs=pl.BlockSpec((None, HALF, QKV_TN), lambda b, h, n: (b, h, n)),
        out_shape=jax.ShapeDtypeStruct((BATCH, SEQ, 3 * DIL_DIM), jnp.bfloat16),
        scratch_shapes=[pltpu.VMEM((QKV_CHUNKS, HALF, LANES), jnp.float32)],
        compiler_params=_params(("parallel", "parallel", "arbitrary")),
        name="qkv_proj",
    )(x, w_in, b_in)


def _memattn_kernel(x_ref, w_ref, b_ref, mk_ref, mv_ref, o_ref):
    mq = (_dot(x_ref[...], w_ref[...]) + b_ref[...]).astype(jnp.bfloat16)
    s = _dot_t(mq, mk_ref[...]) * (MEM_HEAD_DIM ** -0.5)
    m = jnp.max(s, axis=-1, keepdims=True)
    p = jnp.exp(s - m)
    den = jnp.sum(p, axis=-1, keepdims=True)
    o = _dot(p.astype(jnp.bfloat16), mv_ref[...]) / den
    o_ref[...] = o.astype(o_ref.dtype)


def _mem_branch(x, w_in, b_in, kv):
    n0 = OFF_MQ // MEM_HEAD_DIM
    return pl.pallas_call(
        _memattn_kernel,
        grid=(BATCH, 2, MEM_HEADS),
        in_specs=[pl.BlockSpec((None, HALF, D_MODEL), lambda b, h, n: (b, h, 0)),
                  pl.BlockSpec((D_MODEL, MEM_HEAD_DIM), lambda b, h, n: (0, n0 + n)),
                  pl.BlockSpec((1, MEM_HEAD_DIM), lambda b, h, n: (0, n0 + n)),
                  pl.BlockSpec((None, MEM_LEN, MEM_HEAD_DIM), lambda b, h, n: (b, 0, n)),
                  pl.BlockSpec((None, MEM_LEN, MEM_HEAD_DIM), lambda b, h, n: (b, 0, MEM_HEADS + n))],
        out_specs=pl.BlockSpec((None, HALF, MEM_HEAD_DIM), lambda b, h, n: (b, h, n)),
        out_shape=jax.ShapeDtypeStruct((BATCH, SEQ, MEM_DIM), jnp.bfloat16),
        compiler_params=_params(("parallel", "parallel", "arbitrary")),
        name="mem_branch",
    )(x, w_in, b_in, kv, kv)


def _softmax_block(s, v):
    m = jnp.max(s, axis=-1, keepdims=True)
    p = jnp.exp(s - m)
    den = jnp.sum(p, axis=-1, keepdims=True)
    o = _dot(p.astype(jnp.bfloat16), v) / den
    return o, m + jnp.log(den)


def _dilattn_kernel(slopes_ref,
                    q0_ref, q1_ref, q2_ref, k0_ref, k1_ref, k2_ref, v0_ref, v1_ref, v2_ref,
                    o_ref, o_sc, l_sc):
    h = pl.program_id(1)
    blk = ATT_BLOCK
    scale = HEAD_DIM ** -0.5
    qi = lax.broadcasted_iota(jnp.int32, (blk, 2 * blk), 0) + blk
    kj = lax.broadcasted_iota(jnp.int32, (blk, 2 * blk), 1)
    jrel = qi - kj
    valid = (jrel >= 0) & (jrel <= blk)
    jrel_f = jrel.astype(jnp.float32)

    def bias_for(g):
        slope = slopes_ref[g * HEADS_PER_GROUP + h]
        d = float(DIL_PATTERNS[g][1])
        return jnp.where(valid, (-slope * d) * jrel_f, -jnp.inf)

    def put(g, row_slice, o, lse):
        o_sc[g, row_slice, :] = o
        l_sc[g, row_slice, :] = jnp.broadcast_to(lse, (blk, HEAD_DIM))

    bias0 = bias_for(0)
    s = _dot_t(q0_ref[0:blk, :], k0_ref[0:blk, :]) * scale + bias0[:, blk:]
    o, lse = _softmax_block(s, v0_ref[0:blk, :])
    put(0, pl.ds(0, blk), o, lse)

    def g0_body(n, carry):
        q0 = pl.multiple_of(n * blk, blk)
        k0 = pl.multiple_of((n - 1) * blk, blk)
        s = _dot_t(q0_ref[pl.ds(q0, blk), :], k0_ref[pl.ds(k0, 2 * blk), :]) * scale + bias0
        o, lse = _softmax_block(s, v0_ref[pl.ds(k0, 2 * blk), :])
        put(0, pl.ds(q0, blk), o, lse)
        return carry

    lax.fori_loop(1, SEQ // blk, g0_body, 0)

    d1 = DIL_PATTERNS[1][1]
    cls1 = HALF // d1
    per_half = cls1 // blk
    bias1 = bias_for(1)

    def row1(r, n):
        return (n // per_half) * HALF + r * cls1 + (n % per_half) * blk

    def g1_body(r, carry):
        for n in range(SEQ // d1 // blk):
            cur = pl.multiple_of(row1(r, n), blk)
            q = q1_ref[pl.ds(cur, blk), :]
            if n == 0:
                s = _dot_t(q, k1_ref[pl.ds(cur, blk), :]) * scale + bias1[:, blk:]
                o, lse = _softmax_block(s, v1_ref[pl.ds(cur, blk), :])
            else:
                prev = pl.multiple_of(row1(r, n - 1), blk)
                kc = jnp.concatenate([k1_ref[pl.ds(prev, blk), :], k1_ref[pl.ds(cur, blk), :]], axis=0)
                vc = jnp.concatenate([v1_ref[pl.ds(prev, blk), :], v1_ref[pl.ds(cur, blk), :]], axis=0)
                s = _dot_t(q, kc) * scale + bias1
                o, lse = _softmax_block(s, vc)
            put(1, pl.ds(n * blk * d1 + r, blk, stride=d1), o, lse)
        return carry

    lax.fori_loop(0, d1, g1_body, 0)

    d2 = DIL_PATTERNS[2][1]
    cls2 = HALF // d2
    bias2 = bias_for(2)

    def g2_body(r, carry):
        a = pl.multiple_of(r * cls2, cls2)
        b = pl.multiple_of(HALF + r * cls2, cls2)
        cat = lambda ref: jnp.concatenate([ref[pl.ds(a, cls2), :], ref[pl.ds(b, cls2), :]], axis=0)
        s = _dot_t(cat(q2_ref), cat(k2_ref)) * scale + bias2[:, blk:]
        o, lse = _softmax_block(s, cat(v2_ref))
        put(2, pl.ds(r, blk, stride=d2), o, lse)
        return carry

    lax.fori_loop(0, d2, g2_body, 0)

    rows = 256
    for t in range(SEQ // rows):
        sl = pl.ds(t * rows, rows)
        l0, l1, l2 = l_sc[0, sl, :], l_sc[1, sl, :], l_sc[2, sl, :]
        m = jnp.maximum(jnp.maximum(l0, l1), l2)
        e0, e1, e2 = jnp.exp(l0 - m), jnp.exp(l1 - m), jnp.exp(l2 - m)
        mix = (e0 * o_sc[0, sl, :] + e1 * o_sc[1, sl, :] + e2 * o_sc[2, sl, :]) / (e0 + e1 + e2)
        o_ref[sl, :] = mix.astype(o_ref.dtype)


def _dil_branch(qkv, slopes):
    nq = DIL_DIM // HEAD_DIM

    def spec(section, g):
        return pl.BlockSpec((None, SEQ, HEAD_DIM),
                            lambda b, h, s_ref, o=section * nq + g * HEADS_PER_GROUP: (b, 0, o + h))

    grid_spec = pltpu.PrefetchScalarGridSpec(
        num_scalar_prefetch=1,
        grid=(BATCH, HEADS_PER_GROUP),
        in_specs=[spec(sec, g) for sec in range(3) for g in range(N_DIL_GROUPS)],
        out_specs=pl.BlockSpec((None, SEQ, HEAD_DIM), lambda b, h, s_ref: (b, 0, h)),
        scratch_shapes=[pltpu.VMEM((N_DIL_GROUPS, SEQ, HEAD_DIM), jnp.float32),
                        pltpu.VMEM((N_DIL_GROUPS, SEQ, HEAD_DIM), jnp.float32)],
    )
    return pl.pallas_call(
        _dilattn_kernel,
        grid_spec=grid_spec,
        out_shape=jax.ShapeDtypeStruct((BATCH, SEQ, DIL_OUT_DIM), jnp.bfloat16),
        compiler_params=_params(("parallel", "arbitrary")),
        name="dil_attn",
    )(slopes, *([qkv] * 9))


MERGE_TM = 1024
MERGE_TN = 256


def _merge_kernel(x_ref, wg0_ref, wg1_ref, wg2_ref, bg0_ref, bg1_ref, bg2_ref,
                  sc_ref, od_ref, om_ref, wco_ref, wdo_ref, wmo_ref, o_ref):
    x = x_ref[...]
    g0 = jax.nn.sigmoid(_dot(x, wg0_ref[...]) + bg0_ref[...])
    acc = g0 * _dot(sc_ref[...], wco_ref[...])
    g1 = jax.nn.sigmoid(_dot(x, wg1_ref[...]) + bg1_ref[...])
    acc = acc + g1 * _dot(od_ref[...], wdo_ref[...])
    g2 = jax.nn.sigmoid(_dot(x, wg2_ref[...]) + bg2_ref[...])
    acc = acc + g2 * _dot(om_ref[...], wmo_ref[...])
    o_ref[...] = acc.astype(o_ref.dtype)


def _merge(x2, w_in, b_in, s_conv, o_dil, o_mem, w_co, w_do, w_mo):
    nb = lambda br: (OFF_GATE + br * D_MODEL) // MERGE_TN
    gspec = lambda br: pl.BlockSpec((D_MODEL, MERGE_TN), lambda i, n, o=nb(br): (0, o + n))
    bspec = lambda br: pl.BlockSpec((1, MERGE_TN), lambda i, n, o=nb(br): (0, o + n))
    act = lambda width: pl.BlockSpec((MERGE_TM, width), lambda i, n: (i, 0))
    wout = lambda width: pl.BlockSpec((width, MERGE_TN), lambda i, n: (0, n))
    return pl.pallas_call(
        _merge_kernel,
        grid=(TOKENS // MERGE_TM, D_MODEL // MERGE_TN),
        in_specs=[act(D_MODEL), gspec(0), gspec(1), gspec(2), bspec(0), bspec(1), bspec(2),
                  act(CONV_DIM), act(DIL_OUT_DIM), act(MEM_DIM),
                  wout(CONV_DIM), wout(DIL_OUT_DIM), wout(MEM_DIM)],
        out_specs=pl.BlockSpec((MERGE_TM, MERGE_TN), lambda i, n: (i, n)),
        out_shape=jax.ShapeDtypeStruct((TOKENS, D_MODEL), jnp.bfloat16),
        compiler_params=_params(("parallel", "arbitrary")),
        name="gated_merge",
    )(x2, w_in, w_in, w_in, b_in, b_in, b_in, s_conv, o_dil, o_mem, w_co, w_do, w_mo)


OPROJ_TM = 512


def _oproj_kernel(x_ref, m_ref, wo_ref, g_ref, b_ref, wr_ref, br_ref, x1_ref, logit_ref):
    hres = ALPHA * x_ref[...] + _dot(m_ref[...], wo_ref[...])
    x1 = _layer_norm(hres, g_ref[...], b_ref[...])
    x1_ref[...] = x1
    logit_ref[...] = _dot(x1, wr_ref[...]) + br_ref[...]


def _oproj(x2, merged, w_o, g, b, w_route, b_route):
    row = lambda width: pl.BlockSpec((OPROJ_TM, width), lambda i: (i, 0))
    full = lambda r, c: pl.BlockSpec((r, c), lambda i: (0, 0))
    return pl.pallas_call(
        _oproj_kernel,
        grid=(TOKENS // OPROJ_TM,),
        in_specs=[row(D_MODEL), row(D_MODEL), full(D_MODEL, D_MODEL), full(1, D_MODEL), full(1, D_MODEL),
                  full(D_MODEL, LANES), full(1, LANES)],
        out_specs=[row(D_MODEL), row(LANES)],
        out_shape=[jax.ShapeDtypeStruct((TOKENS, D_MODEL), jnp.float32),
                   jax.ShapeDtypeStruct((TOKENS, LANES), jnp.float32)],
        compiler_params=_params(("parallel",)),
        name="oproj_ln1",
    )(x2, merged, w_o, g, b, w_route, b_route)


def _route_kernel(logit_ref, o_ref, cnt_ref, carry_sc):
    @pl.when(pl.program_id(0) == 0)
    def _():
        carry_sc[...] = jnp.zeros_like(carry_sc)

    tm = ROUTE_TILE
    z = logit_ref[...]
    lane = lax.broadcasted_iota(jnp.int32, (tm, LANES), 1)
    neg = -jnp.inf
    first = lambda hit: jnp.min(jnp.where(hit, lane, LANES), axis=-1, keepdims=True)

    glog = jnp.where(lane < N_GROUPS, z, neg)
    gmax = jnp.max(glog, axis=-1, keepdims=True)
    gsel = first(glog == gmax)
    gw = 1.0 / jnp.sum(jnp.exp(glog - gmax), axis=-1, keepdims=True)

    e_lane = lane - ROUTER_LANE0
    in_group = (e_lane >= 0) & (e_lane < N_EXPERTS) & ((e_lane // EXPERTS_PER_GROUP) == gsel)
    v = jnp.where(in_group, z, neg)
    top1 = jnp.max(v, axis=-1, keepdims=True)
    i1 = first(v == top1)
    v2 = jnp.where(lane == i1, neg, v)
    top2 = jnp.max(v2, axis=-1, keepdims=True)
    i2 = first(v2 == top2)
    t = jnp.exp(top2 - top1)
    w1 = gw / (1.0 + t)
    w2 = w1 * t

    onehot = ((lane == i1) | (lane == i2))
    ri = lax.broadcasted_iota(jnp.int32, (tm, tm), 0)
    ci = lax.broadcasted_iota(jnp.int32, (tm, tm), 1)
    tri = jnp.where(ci <= ri, 1.0, 0.0).astype(jnp.bfloat16)
    cum = _dot(tri, jnp.where(onehot, 1.0, 0.0).astype(jnp.bfloat16))
    carry = carry_sc[0:1, :]
    before = cum + carry - 1.0
    rank1 = jnp.sum(jnp.where(lane == i1, before, 0.0), axis=-1, keepdims=True)
    rank2 = jnp.sum(jnp.where(lane == i2, before, 0.0), axis=-1, keepdims=True)
    new_carry = carry + cum[tm - 1:tm, :]
    carry_sc[...] = jnp.broadcast_to(new_carry, carry_sc.shape)
    cnt_ref[...] = jnp.broadcast_to(new_carry, cnt_ref.shape)

    cols = [(i1 - ROUTER_LANE0).astype(jnp.float32), (i2 - ROUTER_LANE0).astype(jnp.float32),
            rank1, rank2, w1, w2]
    out = jnp.zeros((tm, LANES), jnp.float32)
    for k, col in enumerate(cols):
        out = jnp.where(lane == k, col, out)
    o_ref[...] = out


def _route(logits):
    return pl.pallas_call(
        _route_kernel,
        grid=(TOKENS // ROUTE_TILE,),
        in_specs=[pl.BlockSpec((ROUTE_TILE, LANES), lambda i: (i, 0))],
        out_specs=[pl.BlockSpec((ROUTE_TILE, LANES), lambda i: (i, 0)),
                   pl.BlockSpec((8, LANES), lambda i: (0, 0))],
        out_shape=[jax.ShapeDtypeStruct((TOKENS, LANES), jnp.float32),
                   jax.ShapeDtypeStruct((8, LANES), jnp.float32)],
        scratch_shapes=[pltpu.VMEM((8, LANES), jnp.float32)],
        compiler_params=_params(("arbitrary",)),
        name="route",
    )(logits)


def _gather_rows(idx_ref, n_rows, src_hbm, dst_ref, sem):
    def body(r, carry):
        tok = idx_ref[0, 0, r]
        pltpu.make_async_copy(src_hbm.at[pl.ds(tok, 1)], dst_ref.at[pl.ds(r, 1)], sem).start()
        return carry
    lax.fori_loop(0, n_rows, body, 0, unroll=8)


def _wait_rows(n_rows, src_hbm, dst_ref, sem):
    pltpu.make_async_copy(src_hbm.at[pl.ds(0, n_rows)], dst_ref, sem).wait()


def _ffn_kernel(texp_ref, tvalid_ref, src_cur_ref, src_next_ref, x1_hbm, wg_ref, wu_ref, wd_ref,
                y_ref, xbuf, sem):
    j = pl.program_id(0)
    nt = pl.num_programs(0)
    slot = lax.rem(j, 2)

    @pl.when((j == 0) & (tvalid_ref[0] == 1))
    def _():
        _gather_rows(src_cur_ref, EXPERT_TILE, x1_hbm, xbuf.at[0], sem.at[0])

    nxt = jnp.minimum(j + 1, nt - 1)

    @pl.when((j + 1 < nt) & (tvalid_ref[nxt] == 1))
    def _():
        _gather_rows(src_next_ref, EXPERT_TILE, x1_hbm, xbuf.at[1 - slot], sem.at[1 - slot])

    @pl.when(tvalid_ref[j] == 1)
    def _():
        _wait_rows(EXPERT_TILE, x1_hbm, xbuf.at[slot], sem.at[slot])
        xt = xbuf[slot]
        a = _dot(xt, wg_ref[...])
        u = _dot(xt, wu_ref[...])
        hid = (a * jax.nn.sigmoid(a)) * u
        y_ref[...] = _dot(hid, wd_ref[...])

    @pl.when(tvalid_ref[j] == 0)
    def _():
        y_ref[...] = jnp.zeros_like(y_ref)


def _expert_ffn(tile_expert, tile_valid, src3, x1, w_gate, w_up, w_down):
    nt = N_EXPERT_TILES
    eg = lambda j, te, tv: (te[j] // EXPERTS_PER_GROUP, lax.rem(te[j], EXPERTS_PER_GROUP), 0, 0)
    grid_spec = pltpu.PrefetchScalarGridSpec(
        num_scalar_prefetch=2,
        grid=(nt,),
        in_specs=[pl.BlockSpec((1, 1, EXPERT_TILE), lambda j, te, tv: (j, 0, 0), memory_space=pltpu.SMEM),
                  pl.BlockSpec((1, 1, EXPERT_TILE), lambda j, te, tv: (jnp.minimum(j + 1, nt - 1), 0, 0),
                               memory_space=pltpu.SMEM),
                  pl.BlockSpec(memory_space=pl.ANY),
                  pl.BlockSpec((None, None, D_MODEL, D_EXPERT), eg),
                  pl.BlockSpec((None, None, D_MODEL, D_EXPERT), eg),
                  pl.BlockSpec((None, None, D_EXPERT, D_MODEL), eg)],
        out_specs=pl.BlockSpec((EXPERT_TILE, D_MODEL), lambda j, te, tv: (j, 0)),
        scratch_shapes=[pltpu.VMEM((2, EXPERT_TILE, D_MODEL), jnp.float32),
                        pltpu.SemaphoreType.DMA((2,))],
    )
    return pl.pallas_call(
        _ffn_kernel,
        grid_spec=grid_spec,
        out_shape=jax.ShapeDtypeStruct((nt * EXPERT_TILE, D_MODEL), jnp.float32),
        compiler_params=_params(("arbitrary",)),
        name="expert_ffn",
    )(tile_expert, tile_valid, src3, src3, x1, w_gate, w_up, w_down)


def _combine_kernel(pos_cur_ref, pos_next_ref, r_ref, x1_ref, y_hbm, g_ref, b_ref, o_ref, ybuf, sem):
    j = pl.program_id(0)
    nt = pl.num_programs(0)
    slot = lax.rem(j, 2)
    rows = 2 * COMBINE_TILE

    @pl.when(j == 0)
    def _():
        _gather_rows(pos_cur_ref, rows, y_hbm, ybuf.at[0], sem.at[0])

    @pl.when(j + 1 < nt)
    def _():
        _gather_rows(pos_next_ref, rows, y_hbm, ybuf.at[1 - slot], sem.at[1 - slot])

    _wait_rows(rows, y_hbm, ybuf.at[slot], sem.at[slot])
    r = r_ref[...]
    wa = r[:, 4:5]
    wb = r[:, 5:6]
    ya = ybuf[slot, 0:COMBINE_TILE, :]
    yb = ybuf[slot, COMBINE_TILE:rows, :]
    hres = ALPHA * x1_ref[...] + wa * ya + wb * yb
    o_ref[...] = _layer_norm(hres, g_ref[...], b_ref[...])


def _combine(pos3, routing, x1, y_sorted, g, b):
    nt = TOKENS // COMBINE_TILE
    rows = 2 * COMBINE_TILE
    return pl.pallas_call(
        _combine_kernel,
        grid=(nt,),
        in_specs=[pl.BlockSpec((1, 1, rows), lambda j: (j, 0, 0), memory_space=pltpu.SMEM),
                  pl.BlockSpec((1, 1, rows), lambda j: (jnp.minimum(j + 1, nt - 1), 0, 0),
                               memory_space=pltpu.SMEM),
                  pl.BlockSpec((COMBINE_TILE, LANES), lambda j: (j, 0)),
                  pl.BlockSpec((COMBINE_TILE, D_MODEL), lambda j: (j, 0)),
                  pl.BlockSpec(memory_space=pl.ANY),
                  pl.BlockSpec((1, D_MODEL), lambda j: (0, 0)),
                  pl.BlockSpec((1, D_MODEL), lambda j: (0, 0))],
        out_specs=pl.BlockSpec((COMBINE_TILE, D_MODEL), lambda j: (j, 0)),
        out_shape=jax.ShapeDtypeStruct((TOKENS, D_MODEL), jnp.float32),
        scratch_shapes=[pltpu.VMEM((2, rows, D_MODEL), jnp.float32),
                        pltpu.SemaphoreType.DMA((2,))],
        compiler_params=_params(("arbitrary",)),
        name="combine_ln2",
    )(pos3, pos3, routing, x1, y_sorted, g, b)


def _alibi_slopes():
    n = N_DIL_GROUPS * HEADS_PER_GROUP
    return jnp.asarray(2.0 ** (-ALIBI_MAX * np.arange(1, n + 1, dtype=np.float32) / n), jnp.float32)


def kernel(x, mem, ln_mem_g, ln_mem_b, w_in, b_in, w_conv, w_conv_out, w_dil_out, w_mem_kv, w_mem_out, w_o, ln1_g, ln1_b, w_group, b_group, w_router, b_router, w_gate, w_up, w_down, ln2_g, ln2_b):
    assert x.shape == (BATCH, SEQ, D_MODEL) and w_in.shape == (1, D_MODEL, IN_DIM)
    bf16 = jnp.bfloat16
    row = lambda v: v.reshape(1, -1)
    w_in2 = w_in[0]
    b_in2 = b_in

    kv = _memkv(mem, row(ln_mem_g), row(ln_mem_b), w_mem_kv[0])
    s_conv = _conv_branch(x, w_in2, b_in2, w_conv[0])
    qkv = _qkv_proj(x, w_in2, b_in2)
    o_mem = _mem_branch(x, w_in2, b_in2, kv)
    o_dil = _dil_branch(qkv, _alibi_slopes())

    x2 = x.reshape(TOKENS, D_MODEL)
    merged = _merge(x2, w_in2, b_in2,
                    s_conv.reshape(TOKENS, CONV_DIM), o_dil.reshape(TOKENS, DIL_OUT_DIM),
                    o_mem.reshape(TOKENS, MEM_DIM),
                    w_conv_out[0].astype(bf16), w_dil_out[0].astype(bf16), w_mem_out[0].astype(bf16))

    w_route = jnp.concatenate(
        [w_group[0], jnp.transpose(w_router[0], (1, 0, 2)).reshape(D_MODEL, N_EXPERTS),
         jnp.zeros((D_MODEL, LANES - N_GROUPS - N_EXPERTS), jnp.float32)], axis=1)
    b_route = jnp.concatenate(
        [b_group[0], b_router[0].reshape(N_EXPERTS),
         jnp.zeros((LANES - N_GROUPS - N_EXPERTS,), jnp.float32)]).reshape(1, LANES)
    x1, logits = _oproj(x2, merged, w_o[0].astype(bf16), ln1_g, ln1_b, w_route, b_route)

    routing, counts8 = _route(logits)
    expert = routing[:, 0:2].astype(jnp.int32)
    rank = routing[:, 2:4].astype(jnp.int32)
    counts = counts8[0, ROUTER_LANE0:ROUTER_LANE0 + N_EXPERTS].astype(jnp.int32)
    padded = ((counts + EXPERT_TILE - 1) // EXPERT_TILE) * EXPERT_TILE
    ends = jnp.cumsum(padded)
    offs = ends - padded
    pos = offs[expert] + rank
    tile_start = jnp.arange(N_EXPERT_TILES, dtype=jnp.int32) * EXPERT_TILE
    tile_expert = jnp.minimum(jnp.searchsorted(ends, tile_start, side="right"),
                              N_EXPERTS - 1).astype(jnp.int32)
    tile_valid = (tile_start < ends[-1]).astype(jnp.int32)
    tok = jnp.broadcast_to(jnp.arange(TOKENS, dtype=jnp.int32)[:, None], (TOKENS, 2))
    src = jnp.zeros((N_EXPERT_TILES * EXPERT_TILE,), jnp.int32).at[pos.reshape(-1)].set(tok.reshape(-1))

    y_sorted = _expert_ffn(tile_expert, tile_valid, src.reshape(N_EXPERT_TILES, 1, EXPERT_TILE),
                           x1, w_gate[0], w_up[0], w_down[0])

    pos3 = jnp.transpose(pos.reshape(TOKENS // COMBINE_TILE, COMBINE_TILE, 2), (0, 2, 1))
    pos3 = pos3.reshape(TOKENS // COMBINE_TILE, 1, 2 * COMBINE_TILE)
    out = _combine(pos3, routing, x1, y_sorted, ln2_g, ln2_b)
    return out.reshape(BATCH, SEQ, D_MODEL)
```

```python
import functools
import math

import numpy as np
import jax
import jax.numpy as jnp
from jax import lax
from jax.experimental import pallas as pl
from jax.experimental.pallas import tpu as pltpu

D_MODEL = 2048
BATCH = 8
SEQ = 2048
TOKENS = BATCH * SEQ
CONV_DIM = 1024
CONV_WIDTH = 3
DIL_PATTERNS = ((128, 1), (512, 4), (2048, 16))
N_DIL_GROUPS = 3
HEADS_PER_GROUP = 4
HEAD_DIM = 128
DIL_DIM = N_DIL_GROUPS * HEADS_PER_GROUP * HEAD_DIM
DIL_OUT_DIM = HEADS_PER_GROUP * HEAD_DIM
ATT_BLOCK = 128
ALIBI_MAX = 8.0
MEM_LEN = 256
MEM_HEADS = 4
MEM_HEAD_DIM = 256
MEM_DIM = MEM_HEADS * MEM_HEAD_DIM
N_BRANCHES = 3
IN_DIM = 3 * CONV_DIM + 3 * DIL_DIM + MEM_DIM + N_BRANCHES * D_MODEL
N_GROUPS = 4
EXPERTS_PER_GROUP = 8
N_EXPERTS = N_GROUPS * EXPERTS_PER_GROUP
D_EXPERT = 512
ALPHA = 2.0 ** 0.25
LN_EPS = 1e-5

OFF_CB = 0
OFF_CC = CONV_DIM
OFF_CH = 2 * CONV_DIM
OFF_Q = 3 * CONV_DIM
OFF_MQ = OFF_Q + 3 * DIL_DIM
OFF_GATE = OFF_MQ + MEM_DIM

LANES = 128
HALF = SEQ // 2
VMEM_LIMIT = 56 * 1024 * 1024

ROUTE_TILE = 512
EXPERT_TILE = 256
N_EXPERT_TILES = 2 * TOKENS // EXPERT_TILE + N_EXPERTS
COMBINE_TILE = 256
ROUTER_LANE0 = N_GROUPS


def _params(sem, limit=VMEM_LIMIT):
    return pltpu.CompilerParams(dimension_semantics=sem, vmem_limit_bytes=limit)


def _layer_norm(x, g, b):
    mu = jnp.mean(x, axis=-1, keepdims=True)
    xc = x - mu
    var = jnp.mean(xc * xc, axis=-1, keepdims=True)
    return xc * lax.rsqrt(var + LN_EPS) * g + b


def _dot(a, b):
    return jnp.dot(a, b, preferred_element_type=jnp.float32)


def _dot_t(a, b):
    return lax.dot_general(a, b, (((1,), (1,)), ((), ())), preferred_element_type=jnp.float32)


def _memkv_kernel(mem_ref, g_ref, b_ref, w_ref, kv_ref):
    y = _layer_norm(mem_ref[...], g_ref[...], b_ref[...])
    kv_ref[...] = _dot(y, w_ref[...]).astype(kv_ref.dtype)


def _memkv(mem, g, b, w):
    return pl.pallas_call(
        _memkv_kernel,
        grid=(BATCH, 2),
        in_specs=[pl.BlockSpec((None, MEM_LEN, D_MODEL), lambda i, j: (i, 0, 0)),
                  pl.BlockSpec((1, D_MODEL), lambda i, j: (0, 0)),
                  pl.BlockSpec((1, D_MODEL), lambda i, j: (0, 0)),
                  pl.BlockSpec((D_MODEL, MEM_DIM), lambda i, j: (0, j))],
        out_specs=pl.BlockSpec((None, MEM_LEN, MEM_DIM), lambda i, j: (i, 0, j)),
        out_shape=jax.ShapeDtypeStruct((BATCH, MEM_LEN, 2 * MEM_DIM), jnp.bfloat16),
        compiler_params=_params(("parallel", "arbitrary")),
        name="mem_kv",
    )(mem, g, b, w)


CONV_TC = 256


def _conv_kernel(x_ref, wb_ref, wc_ref, wh_ref, bb_ref, bc_ref, bh_ref, wconv_ref, s_ref,
                 u_sc, carry_sc):
    half = pl.program_id(1)
    c = pl.program_id(2)
    x = x_ref[...]
    cb = _dot(x, wb_ref[...]) + bb_ref[...]
    cc = _dot(x, wc_ref[...]) + bc_ref[...]
    ch = _dot(x, wh_ref[...]) + bh_ref[...]
    u = cc * ch
    u_sc[0:8, :] = jnp.where(half == 0, 0.0, carry_sc[c])
    u_sc[8:8 + HALF, :] = u
    carry_sc[c] = u[HALF - 8:HALF, :]
    wconv = wconv_ref[...]
    y = (wconv[2:3, :] * u
         + wconv[1:2, :] * u_sc[7:7 + HALF, :]
         + wconv[0:1, :] * u_sc[6:6 + HALF, :])
    s_ref[...] = (cb * y).astype(s_ref.dtype)


def _conv_branch(x, w_in, b_in, w_conv):
    nb = lambda off: off // CONV_TC
    wspec = lambda off: pl.BlockSpec((D_MODEL, CONV_TC), lambda b, h, c, o=nb(off): (0, o + c))
    bspec = lambda off: pl.BlockSpec((1, CONV_TC), lambda b, h, c, o=nb(off): (0, o + c))
    return pl.pallas_call(
        _conv_kernel,
        grid=(BATCH, 2, CONV_DIM // CONV_TC),
        in_specs=[pl.BlockSpec((None, HALF, D_MODEL), lambda b, h, c: (b, h, 0)),
                  wspec(OFF_CB), wspec(OFF_CC), wspec(OFF_CH),
                  bspec(OFF_CB), bspec(OFF_CC), bspec(OFF_CH),
                  pl.BlockSpec((CONV_WIDTH, CONV_TC), lambda b, h, c: (0, c))],
        out_specs=pl.BlockSpec((None, HALF, CONV_TC), lambda b, h, c: (b, h, c)),
        out_shape=jax.ShapeDtypeStruct((BATCH, SEQ, CONV_DIM), jnp.bfloat16),
        scratch_shapes=[pltpu.VMEM((HALF + 8, CONV_TC), jnp.float32),
                        pltpu.VMEM((CONV_DIM // CONV_TC, 8, CONV_TC), jnp.float32)],
        compiler_params=_params(("arbitrary", "arbitrary", "arbitrary")),
        name="conv_branch",
    )(x, w_in, w_in, w_in, b_in, b_in, b_in, w_conv)


QKV_TN = 512
QKV_CHUNKS = QKV_TN // LANES


def _qkv_kernel(x_ref, w_ref, b_ref, o_ref, sc_ref):
    g = lax.rem(pl.program_id(2), N_DIL_GROUPS)
    acc = _dot(x_ref[...], w_ref[...]) + b_ref[...]

    @pl.when(g == 0)
    def _():
        o_ref[...] = acc.astype(o_ref.dtype)

    for gi in (1, 2):
        d = DIL_PATTERNS[gi][1]
        rows = HALF // d

        @pl.when(g == gi)
        def _(d=d, rows=rows):
            for c in range(QKV_CHUNKS):
                sc_ref[c] = acc[:, c * LANES:(c + 1) * LANES]
            for c in range(QKV_CHUNKS):
                for r in range(d):
                    o_ref[r * rows:(r + 1) * rows, c * LANES:(c + 1) * LANES] = (
                        sc_ref[c, pl.ds(r, rows, stride=d), :].astype(o_ref.dtype))


def _qkv_proj(x, w_in, b_in):
    n0 = OFF_Q // QKV_TN
    return pl.pallas_call(
        _qkv_kernel,
        grid=(BATCH, 2, 3 * DIL_DIM // QKV_TN),
        in_specs=[pl.BlockSpec((None, HALF, D_MODEL), lambda b, h, n: (b, h, 0)),
                  pl.BlockSpec((D_MODEL, QKV_TN), lambda b, h, n: (0, n0 + n)),
                  pl.BlockSpec((1, QKV_TN), lambda b, h, n: (0, n0 + n))],
        out_specs=pl.BlockSpec((None, HALF, QKV_TN), lambda b, h, n: (b, h, n)),
        out_shape=jax.ShapeDtypeStruct((BATCH, SEQ, 3 * DIL_DIM), jnp.bfloat16),
        scratch_shapes=[pltpu.VMEM((QKV_CHUNKS, HALF, LANES), jnp.float32)],
        compiler_params=_params(("parallel", "parallel", "arbitrary")),
        name="qkv_proj",
    )(x, w_in, b_in)


def _memattn_kernel(x_ref, w_ref, b_ref, mk_ref, mv_ref, o_ref):
    mq = (_dot(x_ref[...], w_ref[...]) + b_ref[...]).astype(jnp.bfloat16)
    s = _dot_t(mq, mk_ref[...]) * (MEM_HEAD_DIM ** -0.5)
    m = jnp.max(s, axis=-1, keepdims=True)
    p = jnp.exp(s - m)
    den = jnp.sum(p, axis=-1, keepdims=True)
    o = _dot(p.astype(jnp.bfloat16), mv_ref[...]) / den
    o_ref[...] = o.astype(o_ref.dtype)


def _mem_branch(x, w_in, b_in, kv):
    n0 = OFF_MQ // MEM_HEAD_DIM
    return pl.pallas_call(
        _memattn_kernel,
        grid=(BATCH, 2, MEM_HEADS),
        in_specs=[pl.BlockSpec((None, HALF, D_MODEL), lambda b, h, n: (b, h, 0)),
                  pl.BlockSpec((D_MODEL, MEM_HEAD_DIM), lambda b, h, n: (0, n0 + n)),
                  pl.BlockSpec((1, MEM_HEAD_DIM), lambda b, h, n: (0, n0 + n)),
                  pl.BlockSpec((None, MEM_LEN, MEM_HEAD_DIM), lambda b, h, n: (b, 0, n)),
                  pl.BlockSpec((None, MEM_LEN, MEM_HEAD_DIM), lambda b, h, n: (b, 0, MEM_HEADS + n))],
        out_specs=pl.BlockSpec((None, HALF, MEM_HEAD_DIM), lambda b, h, n: (b, h, n)),
        out_shape=jax.ShapeDtypeStruct((BATCH, SEQ, MEM_DIM), jnp.bfloat16),
        compiler_params=_params(("parallel", "parallel", "arbitrary")),
        name="mem_branch",
    )(x, w_in, b_in, kv, kv)


ATT_UNROLL = 16


def _softmax_block(s, v):
    m = jnp.max(s, axis=-1, keepdims=True)
    p = jnp.exp(s - m)
    den = jnp.sum(p, axis=-1, keepdims=True)
    o = _dot(p.astype(jnp.bfloat16), v) / den
    return o, m + jnp.log(den)


def _dilattn_kernel(slopes_ref,
                    q0_ref, q1_ref, q2_ref, k0_ref, k1_ref, k2_ref, v0_ref, v1_ref, v2_ref,
                    o_ref, o_sc, l_sc):
    h = pl.program_id(1)
    blk = ATT_BLOCK
    scale = HEAD_DIM ** -0.5
    qi = lax.broadcasted_iota(jnp.int32, (blk, 2 * blk), 0) + blk
    kj = lax.broadcasted_iota(jnp.int32, (blk, 2 * blk), 1)
    jrel = qi - kj
    valid = (jrel >= 0) & (jrel <= blk)
    jrel_f = jrel.astype(jnp.float32)

    def bias_for(g):
        slope = slopes_ref[g * HEADS_PER_GROUP + h]
        d = float(DIL_PATTERNS[g][1])
        return jnp.where(valid, (-slope * d) * jrel_f, -jnp.inf)

    def put(g, row_slice, o, lse):
        o_sc[g, row_slice, :] = o
        l_sc[g, row_slice, :] = jnp.broadcast_to(lse, (blk, HEAD_DIM))

    def run_blocks(g, blocks):
        scores = [_dot_t(q, k) * scale + bias for q, k, _, bias, _ in blocks]
        stats = []
        for s in scores:
            m = jnp.max(s, axis=-1, keepdims=True)
            p = jnp.exp(s - m)
            stats.append((m, p, jnp.sum(p, axis=-1, keepdims=True)))
        outs = [_dot(p.astype(jnp.bfloat16), blkdef[2]) / den
                for (m, p, den), blkdef in zip(stats, blocks)]
        for o, (m, p, den), blkdef in zip(outs, stats, blocks):
            put(g, blkdef[4], o, m + jnp.log(den))

    bias0 = bias_for(0)
    prev_cols = kj < blk

    def g0_body(it, carry):
        blocks = []
        for k in range(ATT_UNROLL):
            n = it * ATT_UNROLL + k
            q0 = pl.multiple_of(n * blk, blk)
            k0 = pl.multiple_of(jnp.maximum(n - 1, 0) * blk, blk)
            bias = jnp.where(prev_cols & (n == 0), -jnp.inf, bias0)
            blocks.append((q0_ref[pl.ds(q0, blk), :], k0_ref[pl.ds(k0, 2 * blk), :],
                           v0_ref[pl.ds(k0, 2 * blk), :], bias, pl.ds(q0, blk)))
        run_blocks(0, blocks)
        return carry

    lax.fori_loop(0, SEQ // blk // ATT_UNROLL, g0_body, 0)

    d1 = DIL_PATTERNS[1][1]
    cls1 = HALF // d1
    per_half = cls1 // blk
    bias1 = bias_for(1)

    def row1(r, n):
        return (n // per_half) * HALF + r * cls1 + (n % per_half) * blk

    nblk1 = SEQ // d1 // blk
    cls_per_trip = ATT_UNROLL // nblk1

    def g1_body(it, carry):
        blocks = []
        for c in range(cls_per_trip):
            r = it * cls_per_trip + c
            for n in range(nblk1):
                cur = pl.multiple_of(row1(r, n), blk)
                q = q1_ref[pl.ds(cur, blk), :]
                dst = pl.ds(n * blk * d1 + r, blk, stride=d1)
                if n == 0:
                    blocks.append((q, k1_ref[pl.ds(cur, blk), :], v1_ref[pl.ds(cur, blk), :],
                                   bias1[:, blk:], dst))
                else:
                    prev = pl.multiple_of(row1(r, n - 1), blk)
                    kc = jnp.concatenate([k1_ref[pl.ds(prev, blk), :], k1_ref[pl.ds(cur, blk), :]], axis=0)
                    vc = jnp.concatenate([v1_ref[pl.ds(prev, blk), :], v1_ref[pl.ds(cur, blk), :]], axis=0)
                    blocks.append((q, kc, vc, bias1, dst))
        run_blocks(1, blocks)
        return carry

    lax.fori_loop(0, d1 // cls_per_trip, g1_body, 0)

    d2 = DIL_PATTERNS[2][1]
    cls2 = HALF // d2
    bias2 = bias_for(2)

    def g2_body(it, carry):
        blocks = []
        for k in range(ATT_UNROLL):
            r = it * ATT_UNROLL + k
            a = pl.multiple_of(r * cls2, cls2)
            b = pl.multiple_of(HALF + r * cls2, cls2)
            cat = lambda ref, a=a, b=b: jnp.concatenate(
                [ref[pl.ds(a, cls2), :], ref[pl.ds(b, cls2), :]], axis=0)
            blocks.append((cat(q2_ref), cat(k2_ref), cat(v2_ref), bias2[:, blk:],
                           pl.ds(r, blk, stride=d2)))
        run_blocks(2, blocks)
        return carry

    lax.fori_loop(0, d2 // ATT_UNROLL, g2_body, 0)

    rows = 256
    for t in range(SEQ // rows):
        sl = pl.ds(t * rows, rows)
        l0, l1, l2 = l_sc[0, sl, :], l_sc[1, sl, :], l_sc[2, sl, :]
        m = jnp.maximum(jnp.maximum(l0, l1), l2)
        e0, e1, e2 = jnp.exp(l0 - m), jnp.exp(l1 - m), jnp.exp(l2 - m)
        mix = (e0 * o_sc[0, sl, :] + e1 * o_sc[1, sl, :] + e2 * o_sc[2, sl, :]) / (e0 + e1 + e2)
        o_ref[sl, :] = mix.astype(o_ref.dtype)


def _dil_branch(qkv, slopes):
    nq = DIL_DIM // HEAD_DIM

    def spec(section, g):
        return pl.BlockSpec((None, SEQ, HEAD_DIM),
                            lambda b, h, s_ref, o=section * nq + g * HEADS_PER_GROUP: (b, 0, o + h))

    grid_spec = pltpu.PrefetchScalarGridSpec(
        num_scalar_prefetch=1,
        grid=(BATCH, HEADS_PER_GROUP),
        in_specs=[spec(sec, g) for sec in range(3) for g in range(N_DIL_GROUPS)],
        out_specs=pl.BlockSpec((None, SEQ, HEAD_DIM), lambda b, h, s_ref: (b, 0, h)),
        scratch_shapes=[pltpu.VMEM((N_DIL_GROUPS, SEQ, HEAD_DIM), jnp.float32),
                        pltpu.VMEM((N_DIL_GROUPS, SEQ, HEAD_DIM), jnp.float32)],
    )
    return pl.pallas_call(
        _dilattn_kernel,
        grid_spec=grid_spec,
        out_shape=jax.ShapeDtypeStruct((BATCH, SEQ, DIL_OUT_DIM), jnp.bfloat16),
        compiler_params=_params(("parallel", "arbitrary")),
        name="dil_attn",
    )(slopes, *([qkv] * 9))


MERGE_TM = 1024
MERGE_TN = 256


def _merge_kernel(x_ref, wg0_ref, wg1_ref, wg2_ref, bg0_ref, bg1_ref, bg2_ref,
                  sc_ref, od_ref, om_ref, wco_ref, wdo_ref, wmo_ref, o_ref):
    x = x_ref[...]
    g0 = jax.nn.sigmoid(_dot(x, wg0_ref[...]) + bg0_ref[...])
    acc = g0 * _dot(sc_ref[...], wco_ref[...])
    g1 = jax.nn.sigmoid(_dot(x, wg1_ref[...]) + bg1_ref[...])
    acc = acc + g1 * _dot(od_ref[...], wdo_ref[...])
    g2 = jax.nn.sigmoid(_dot(x, wg2_ref[...]) + bg2_ref[...])
    acc = acc + g2 * _dot(om_ref[...], wmo_ref[...])
    o_ref[...] = acc.astype(o_ref.dtype)


def _merge(x2, w_in, b_in, s_conv, o_dil, o_mem, w_co, w_do, w_mo):
    nb = lambda br: (OFF_GATE + br * D_MODEL) // MERGE_TN
    gspec = lambda br: pl.BlockSpec((D_MODEL, MERGE_TN), lambda i, n, o=nb(br): (0, o + n))
    bspec = lambda br: pl.BlockSpec((1, MERGE_TN), lambda i, n, o=nb(br): (0, o + n))
    act = lambda width: pl.BlockSpec((MERGE_TM, width), lambda i, n: (i, 0))
    wout = lambda width: pl.BlockSpec((width, MERGE_TN), lambda i, n: (0, n))
    return pl.pallas_call(
        _merge_kernel,
        grid=(TOKENS // MERGE_TM, D_MODEL // MERGE_TN),
        in_specs=[act(D_MODEL), gspec(0), gspec(1), gspec(2), bspec(0), bspec(1), bspec(2),
                  act(CONV_DIM), act(DIL_OUT_DIM), act(MEM_DIM),
                  wout(CONV_DIM), wout(DIL_OUT_DIM), wout(MEM_DIM)],
        out_specs=pl.BlockSpec((MERGE_TM, MERGE_TN), lambda i, n: (i, n)),
        out_shape=jax.ShapeDtypeStruct((TOKENS, D_MODEL), jnp.bfloat16),
        compiler_params=_params(("parallel", "arbitrary")),
        name="gated_merge",
    )(x2, w_in, w_in, w_in, b_in, b_in, b_in, s_conv, o_dil, o_mem, w_co, w_do, w_mo)


OPROJ_TM = 512


def _oproj_kernel(x_ref, m_ref, wo_ref, g_ref, b_ref, wr_ref, br_ref, x1_ref, logit_ref):
    hres = ALPHA * x_ref[...] + _dot(m_ref[...], wo_ref[...])
    x1 = _layer_norm(hres, g_ref[...], b_ref[...])
    x1_ref[...] = x1
    logit_ref[...] = _dot(x1, wr_ref[...]) + br_ref[...]


def _oproj(x2, merged, w_o, g, b, w_route, b_route):
    row = lambda width: pl.BlockSpec((OPROJ_TM, width), lambda i: (i, 0))
    full = lambda r, c: pl.BlockSpec((r, c), lambda i: (0, 0))
    return pl.pallas_call(
        _oproj_kernel,
        grid=(TOKENS // OPROJ_TM,),
        in_specs=[row(D_MODEL), row(D_MODEL), full(D_MODEL, D_MODEL), full(1, D_MODEL), full(1, D_MODEL),
                  full(D_MODEL, LANES), full(1, LANES)],
        out_specs=[row(D_MODEL), row(LANES)],
        out_shape=[jax.ShapeDtypeStruct((TOKENS, D_MODEL), jnp.float32),
                   jax.ShapeDtypeStruct((TOKENS, LANES), jnp.float32)],
        compiler_params=_params(("parallel",)),
        name="oproj_ln1",
    )(x2, merged, w_o, g, b, w_route, b_route)


def _route_kernel(logit_ref, o_ref, cnt_ref, carry_sc):
    @pl.when(pl.program_id(0) == 0)
    def _():
        carry_sc[...] = jnp.zeros_like(carry_sc)

    tm = ROUTE_TILE
    z = logit_ref[...]
    lane = lax.broadcasted_iota(jnp.int32, (tm, LANES), 1)
    neg = -jnp.inf
    first = lambda hit: jnp.min(jnp.where(hit, lane, LANES), axis=-1, keepdims=True)

    glog = jnp.where(lane < N_GROUPS, z, neg)
    gmax = jnp.max(glog, axis=-1, keepdims=True)
    gsel = first(glog == gmax)
    gw = 1.0 / jnp.sum(jnp.exp(glog - gmax), axis=-1, keepdims=True)

    e_lane = lane - ROUTER_LANE0
    in_group = (e_lane >= 0) & (e_lane < N_EXPERTS) & ((e_lane // EXPERTS_PER_GROUP) == gsel)
    v = jnp.where(in_group, z, neg)
    top1 = jnp.max(v, axis=-1, keepdims=True)
    i1 = first(v == top1)
    v2 = jnp.where(lane == i1, neg, v)
    top2 = jnp.max(v2, axis=-1, keepdims=True)
    i2 = first(v2 == top2)
    t = jnp.exp(top2 - top1)
    w1 = gw / (1.0 + t)
    w2 = w1 * t

    onehot = ((lane == i1) | (lane == i2))
    ri = lax.broadcasted_iota(jnp.int32, (tm, tm), 0)
    ci = lax.broadcasted_iota(jnp.int32, (tm, tm), 1)
    tri = jnp.where(ci <= ri, 1.0, 0.0).astype(jnp.bfloat16)
    cum = _dot(tri, jnp.where(onehot, 1.0, 0.0).astype(jnp.bfloat16))
    carry = carry_sc[0:1, :]
    before = cum + carry - 1.0
    rank1 = jnp.sum(jnp.where(lane == i1, before, 0.0), axis=-1, keepdims=True)
    rank2 = jnp.sum(jnp.where(lane == i2, before, 0.0), axis=-1, keepdims=True)
    new_carry = carry + cum[tm - 1:tm, :]
    carry_sc[...] = jnp.broadcast_to(new_carry, carry_sc.shape)
    cnt_ref[...] = jnp.broadcast_to(new_carry, cnt_ref.shape)

    cols = [(i1 - ROUTER_LANE0).astype(jnp.float32), (i2 - ROUTER_LANE0).astype(jnp.float32),
            rank1, rank2, w1, w2]
    out = jnp.zeros((tm, LANES), jnp.float32)
    for k, col in enumerate(cols):
        out = jnp.where(lane == k, col, out)
    o_ref[...] = out


def _route(logits):
    return pl.pallas_call(
        _route_kernel,
        grid=(TOKENS // ROUTE_TILE,),
        in_specs=[pl.BlockSpec((ROUTE_TILE, LANES), lambda i: (i, 0))],
        out_specs=[pl.BlockSpec((ROUTE_TILE, LANES), lambda i: (i, 0)),
                   pl.BlockSpec((8, LANES), lambda i: (0, 0))],
        out_shape=[jax.ShapeDtypeStruct((TOKENS, LANES), jnp.float32),
                   jax.ShapeDtypeStruct((8, LANES), jnp.float32)],
        scratch_shapes=[pltpu.VMEM((8, LANES), jnp.float32)],
        compiler_params=_params(("arbitrary",)),
        name="route",
    )(logits)


def _gather_rows(idx_ref, n_rows, src_hbm, dst_ref, sem):
    def body(r, carry):
        tok = idx_ref[0, 0, r]
        pltpu.make_async_copy(src_hbm.at[pl.ds(tok, 1)], dst_ref.at[pl.ds(r, 1)], sem).start()
        return carry
    lax.fori_loop(0, n_rows, body, 0, unroll=8)


def _wait_rows(n_rows, src_hbm, dst_ref, sem):
    pltpu.make_async_copy(src_hbm.at[pl.ds(0, n_rows)], dst_ref, sem).wait()


def _ffn_kernel(texp_ref, tvalid_ref, src_cur_ref, src_next_ref, x1_hbm, wg_ref, wu_ref, wd_ref,
                y_ref, xbuf, sem):
    j = pl.program_id(0)
    nt = pl.num_programs(0)
    slot = lax.rem(j, 2)

    @pl.when((j == 0) & (tvalid_ref[0] == 1))
    def _():
        _gather_rows(src_cur_ref, EXPERT_TILE, x1_hbm, xbuf.at[0], sem.at[0])

    nxt = jnp.minimum(j + 1, nt - 1)

    @pl.when((j + 1 < nt) & (tvalid_ref[nxt] == 1))
    def _():
        _gather_rows(src_next_ref, EXPERT_TILE, x1_hbm, xbuf.at[1 - slot], sem.at[1 - slot])

    @pl.when(tvalid_ref[j] == 1)
    def _():
        _wait_rows(EXPERT_TILE, x1_hbm, xbuf.at[slot], sem.at[slot])
        xt = xbuf[slot]
        a = _dot(xt, wg_ref[...])
        u = _dot(xt, wu_ref[...])
        hid = (a * jax.nn.sigmoid(a)) * u
        y_ref[...] = _dot(hid, wd_ref[...])

    @pl.when(tvalid_ref[j] == 0)
    def _():
        y_ref[...] = jnp.zeros_like(y_ref)


def _expert_ffn(tile_expert, tile_valid, src3, x1, w_gate, w_up, w_down):
    nt = N_EXPERT_TILES
    eg = lambda j, te, tv: (te[j] // EXPERTS_PER_GROUP, lax.rem(te[j], EXPERTS_PER_GROUP), 0, 0)
    grid_spec = pltpu.PrefetchScalarGridSpec(
        num_scalar_prefetch=2,
        grid=(nt,),
        in_specs=[pl.BlockSpec((1, 1, EXPERT_TILE), lambda j, te, tv: (j, 0, 0), memory_space=pltpu.SMEM),
                  pl.BlockSpec((1, 1, EXPERT_TILE), lambda j, te, tv: (jnp.minimum(j + 1, nt - 1), 0, 0),
                               memory_space=pltpu.SMEM),
                  pl.BlockSpec(memory_space=pl.ANY),
                  pl.BlockSpec((None, None, D_MODEL, D_EXPERT), eg),
                  pl.BlockSpec((None, None, D_MODEL, D_EXPERT), eg),
                  pl.BlockSpec((None, None, D_EXPERT, D_MODEL), eg)],
        out_specs=pl.BlockSpec((EXPERT_TILE, D_MODEL), lambda j, te, tv: (j, 0)),
        scratch_shapes=[pltpu.VMEM((2, EXPERT_TILE, D_MODEL), jnp.float32),
                        pltpu.SemaphoreType.DMA((2,))],
    )
    return pl.pallas_call(
        _ffn_kernel,
        grid_spec=grid_spec,
        out_shape=jax.ShapeDtypeStruct((nt * EXPERT_TILE, D_MODEL), jnp.float32),
        compiler_params=_params(("arbitrary",)),
        name="expert_ffn",
    )(tile_expert, tile_valid, src3, src3, x1, w_gate, w_up, w_down)


def _combine_kernel(pos_cur_ref, pos_next_ref, r_ref, x1_ref, y_hbm, g_ref, b_ref, o_ref, ybuf, sem):
    j = pl.program_id(0)
    nt = pl.num_programs(0)
    slot = lax.rem(j, 2)
    rows = 2 * COMBINE_TILE

    @pl.when(j == 0)
    def _():
        _gather_rows(pos_cur_ref, rows, y_hbm, ybuf.at[0], sem.at[0])

    @pl.when(j + 1 < nt)
    def _():
        _gather_rows(pos_next_ref, rows, y_hbm, ybuf.at[1 - slot], sem.at[1 - slot])

    _wait_rows(rows, y_hbm, ybuf.at[slot], sem.at[slot])
    r = r_ref[...]
    wa = r[:, 4:5]
    wb = r[:, 5:6]
    ya = ybuf[slot, 0:COMBINE_TILE, :]
    yb = ybuf[slot, COMBINE_TILE:rows, :]
    hres = ALPHA * x1_ref[...] + wa * ya + wb * yb
    o_ref[...] = _layer_norm(hres, g_ref[...], b_ref[...])


def _combine(pos3, routing, x1, y_sorted, g, b):
    nt = TOKENS // COMBINE_TILE
    rows = 2 * COMBINE_TILE
    return pl.pallas_call(
        _combine_kernel,
        grid=(nt,),
        in_specs=[pl.BlockSpec((1, 1, rows), lambda j: (j, 0, 0), memory_space=pltpu.SMEM),
                  pl.BlockSpec((1, 1, rows), lambda j: (jnp.minimum(j + 1, nt - 1), 0, 0),
                               memory_space=pltpu.SMEM),
                  pl.BlockSpec((COMBINE_TILE, LANES), lambda j: (j, 0)),
                  pl.BlockSpec((COMBINE_TILE, D_MODEL), lambda j: (j, 0)),
                  pl.BlockSpec(memory_space=pl.ANY),
                  pl.BlockSpec((1, D_MODEL), lambda j: (0, 0)),
                  pl.BlockSpec((1, D_MODEL), lambda j: (0, 0))],
        out_specs=pl.BlockSpec((COMBINE_TILE, D_MODEL), lambda j: (j, 0)),
        out_shape=jax.ShapeDtypeStruct((TOKENS, D_MODEL), jnp.float32),
        scratch_shapes=[pltpu.VMEM((2, rows, D_MODEL), jnp.float32),
                        pltpu.SemaphoreType.DMA((2,))],
        compiler_params=_params(("arbitrary",)),
        name="combine_ln2",
    )(pos3, pos3, routing, x1, y_sorted, g, b)


def _alibi_slopes():
    n = N_DIL_GROUPS * HEADS_PER_GROUP
    return jnp.asarray(2.0 ** (-ALIBI_MAX * np.arange(1, n + 1, dtype=np.float32) / n), jnp.float32)


def kernel(x, mem, ln_mem_g, ln_mem_b, w_in, b_in, w_conv, w_conv_out, w_dil_out, w_mem_kv, w_mem_out, w_o, ln1_g, ln1_b, w_group, b_group, w_router, b_router, w_gate, w_up, w_down, ln2_g, ln2_b):
    assert x.shape == (BATCH, SEQ, D_MODEL) and w_in.shape == (1, D_MODEL, IN_DIM)
    bf16 = jnp.bfloat16
    row = lambda v: v.reshape(1, -1)
    w_in2 = w_in[0]
    b_in2 = b_in

    kv = _memkv(mem, row(ln_mem_g), row(ln_mem_b), w_mem_kv[0])
    s_conv = _conv_branch(x, w_in2, b_in2, w_conv[0])
    qkv = _qkv_proj(x, w_in2, b_in2)
    o_mem = _mem_branch(x, w_in2, b_in2, kv)
    o_dil = _dil_branch(qkv, _alibi_slopes())

    x2 = x.reshape(TOKENS, D_MODEL)
    merged = _merge(x2, w_in2, b_in2,
                    s_conv.reshape(TOKENS, CONV_DIM), o_dil.reshape(TOKENS, DIL_OUT_DIM),
                    o_mem.reshape(TOKENS, MEM_DIM),
                    w_conv_out[0].astype(bf16), w_dil_out[0].astype(bf16), w_mem_out[0].astype(bf16))

    w_route = jnp.concatenate(
        [w_group[0], jnp.transpose(w_router[0], (1, 0, 2)).reshape(D_MODEL, N_EXPERTS),
         jnp.zeros((D_MODEL, LANES - N_GROUPS - N_EXPERTS), jnp.float32)], axis=1)
    b_route = jnp.concatenate(
        [b_group[0], b_router[0].reshape(N_EXPERTS),
         jnp.zeros((LANES - N_GROUPS - N_EXPERTS,), jnp.float32)]).reshape(1, LANES)
    x1, logits = _oproj(x2, merged, w_o[0].astype(bf16), ln1_g, ln1_b, w_route, b_route)

    routing, counts8 = _route(logits)
    expert = routing[:, 0:2].astype(jnp.int32)
    rank = routing[:, 2:4].astype(jnp.int32)
    counts = counts8[0, ROUTER_LANE0:ROUTER_LANE0 + N_EXPERTS].astype(jnp.int32)
    padded = ((counts + EXPERT_TILE - 1) // EXPERT_TILE) * EXPERT_TILE
    ends = jnp.cumsum(padded)
    offs = ends - padded
    pos = offs[expert] + rank
    tile_start = jnp.arange(N_EXPERT_TILES, dtype=jnp.int32) * EXPERT_TILE
    tile_expert = jnp.minimum(jnp.sum((ends[None, :] <= tile_start[:, None]).astype(jnp.int32), axis=1),
                              N_EXPERTS - 1)
    tile_valid = (tile_start < ends[-1]).astype(jnp.int32)
    tok = jnp.broadcast_to(jnp.arange(TOKENS, dtype=jnp.int32)[:, None], (TOKENS, 2))
    src = jnp.zeros((N_EXPERT_TILES * EXPERT_TILE,), jnp.int32).at[pos.reshape(-1)].set(tok.reshape(-1))

    y_sorted = _expert_ffn(tile_expert, tile_valid, src.reshape(N_EXPERT_TILES, 1, EXPERT_TILE),
                           x1, w_gate[0], w_up[0], w_down[0])

    pos3 = jnp.transpose(pos.reshape(TOKENS // COMBINE_TILE, COMBINE_TILE, 2), (0, 2, 1))
    pos3 = pos3.reshape(TOKENS // COMBINE_TILE, 1, 2 * COMBINE_TILE)
    out = _combine(pos3, routing, x1, y_sorted, ln2_g, ln2_b)
    return out.reshape(BATCH, SEQ, D_MODEL)
```

```python
import functools
import math

import numpy as np
import jax
import jax.numpy as jnp
from jax import lax
from jax.experimental import pallas as pl
from jax.experimental.pallas import tpu as pltpu

D_MODEL = 2048
BATCH = 8
SEQ = 2048
TOKENS = BATCH * SEQ
CONV_DIM = 1024
CONV_WIDTH = 3
DIL_PATTERNS = ((128, 1), (512, 4), (2048, 16))
N_DIL_GROUPS = 3
HEADS_PER_GROUP = 4
HEAD_DIM = 128
DIL_DIM = N_DIL_GROUPS * HEADS_PER_GROUP * HEAD_DIM
DIL_OUT_DIM = HEADS_PER_GROUP * HEAD_DIM
ATT_BLOCK = 128
ALIBI_MAX = 8.0
MEM_LEN = 256
MEM_HEADS = 4
MEM_HEAD_DIM = 256
MEM_DIM = MEM_HEADS * MEM_HEAD_DIM
N_BRANCHES = 3
IN_DIM = 3 * CONV_DIM + 3 * DIL_DIM + MEM_DIM + N_BRANCHES * D_MODEL
N_GROUPS = 4
EXPERTS_PER_GROUP = 8
N_EXPERTS = N_GROUPS * EXPERTS_PER_GROUP
D_EXPERT = 512
ALPHA = 2.0 ** 0.25
LN_EPS = 1e-5

OFF_CB = 0
OFF_CC = CONV_DIM
OFF_CH = 2 * CONV_DIM
OFF_Q = 3 * CONV_DIM
OFF_MQ = OFF_Q + 3 * DIL_DIM
OFF_GATE = OFF_MQ + MEM_DIM

LANES = 128
HALF = SEQ // 2
VMEM_LIMIT = 56 * 1024 * 1024

ROUTE_TILE = 512
EXPERT_TILE = 256
N_EXPERT_TILES = 2 * TOKENS // EXPERT_TILE + N_EXPERTS
COMBINE_TILE = 256
ROUTER_LANE0 = N_GROUPS


def _params(sem, limit=VMEM_LIMIT):
    return pltpu.CompilerParams(dimension_semantics=sem, vmem_limit_bytes=limit)


def _layer_norm(x, g, b):
    mu = jnp.mean(x, axis=-1, keepdims=True)
    xc = x - mu
    var = jnp.mean(xc * xc, axis=-1, keepdims=True)
    return xc * lax.rsqrt(var + LN_EPS) * g + b


def _dot(a, b):
    return jnp.dot(a, b, preferred_element_type=jnp.float32)


def _dot_t(a, b):
    return lax.dot_general(a, b, (((1,), (1,)), ((), ())), preferred_element_type=jnp.float32)


def _memkv_kernel(mem_ref, g_ref, b_ref, w_ref, kv_ref):
    y = _layer_norm(mem_ref[...], g_ref[...], b_ref[...])
    kv_ref[...] = _dot(y, w_ref[...]).astype(kv_ref.dtype)


def _memkv(mem, g, b, w):
    return pl.pallas_call(
        _memkv_kernel,
        grid=(BATCH, 2),
        in_specs=[pl.BlockSpec((None, MEM_LEN, D_MODEL), lambda i, j: (i, 0, 0)),
                  pl.BlockSpec((1, D_MODEL), lambda i, j: (0, 0)),
                  pl.BlockSpec((1, D_MODEL), lambda i, j: (0, 0)),
                  pl.BlockSpec((D_MODEL, MEM_DIM), lambda i, j: (0, j))],
        out_specs=pl.BlockSpec((None, MEM_LEN, MEM_DIM), lambda i, j: (i, 0, j)),
        out_shape=jax.ShapeDtypeStruct((BATCH, MEM_LEN, 2 * MEM_DIM), jnp.bfloat16),
        compiler_params=_params(("parallel", "arbitrary")),
        name="mem_kv",
    )(mem, g, b, w)


CONV_TC = 256


def _conv_kernel(x_ref, wb_ref, wc_ref, wh_ref, bb_ref, bc_ref, bh_ref, wconv_ref, s_ref,
                 u_sc, carry_sc):
    half = pl.program_id(1)
    c = pl.program_id(2)
    x = x_ref[...]
    cb = _dot(x, wb_ref[...]) + bb_ref[...]
    cc = _dot(x, wc_ref[...]) + bc_ref[...]
    ch = _dot(x, wh_ref[...]) + bh_ref[...]
    u = cc * ch
    u_sc[0:8, :] = jnp.where(half == 0, 0.0, carry_sc[c])
    u_sc[8:8 + HALF, :] = u
    carry_sc[c] = u[HALF - 8:HALF, :]
    wconv = wconv_ref[...]
    y = (wconv[2:3, :] * u
         + wconv[1:2, :] * u_sc[7:7 + HALF, :]
         + wconv[0:1, :] * u_sc[6:6 + HALF, :])
    s_ref[...] = (cb * y).astype(s_ref.dtype)


def _conv_branch(x, w_in, b_in, w_conv):
    nb = lambda off: off // CONV_TC
    wspec = lambda off: pl.BlockSpec((D_MODEL, CONV_TC), lambda b, h, c, o=nb(off): (0, o + c))
    bspec = lambda off: pl.BlockSpec((1, CONV_TC), lambda b, h, c, o=nb(off): (0, o + c))
    return pl.pallas_call(
        _conv_kernel,
        grid=(BATCH, 2, CONV_DIM // CONV_TC),
        in_specs=[pl.BlockSpec((None, HALF, D_MODEL), lambda b, h, c: (b, h, 0)),
                  wspec(OFF_CB), wspec(OFF_CC), wspec(OFF_CH),
                  bspec(OFF_CB), bspec(OFF_CC), bspec(OFF_CH),
                  pl.BlockSpec((CONV_WIDTH, CONV_TC), lambda b, h, c: (0, c))],
        out_specs=pl.BlockSpec((None, HALF, CONV_TC), lambda b, h, c: (b, h, c)),
        out_shape=jax.ShapeDtypeStruct((BATCH, SEQ, CONV_DIM), jnp.bfloat16),
        scratch_shapes=[pltpu.VMEM((HALF + 8, CONV_TC), jnp.float32),
                        pltpu.VMEM((CONV_DIM // CONV_TC, 8, CONV_TC), jnp.float32)],
        compiler_params=_params(("arbitrary", "arbitrary", "arbitrary")),
        name="conv_branch",
    )(x, w_in, w_in, w_in, b_in, b_in, b_in, w_conv)


QKV_TN = 512
QKV_CHUNKS = QKV_TN // LANES


def _qkv_kernel(x_ref, w_ref, b_ref, o_ref, sc_ref):
    g = lax.rem(pl.program_id(2), N_DIL_GROUPS)
    acc = _dot(x_ref[...], w_ref[...]) + b_ref[...]

    @pl.when(g == 0)
    def _():
        o_ref[...] = acc.astype(o_ref.dtype)

    for gi in (1, 2):
        d = DIL_PATTERNS[gi][1]
        rows = HALF // d

        @pl.when(g == gi)
        def _(d=d, rows=rows):
            for c in range(QKV_CHUNKS):
                sc_ref[c] = acc[:, c * LANES:(c + 1) * LANES]
            for c in range(QKV_CHUNKS):
                for r in range(d):
                    o_ref[r * rows:(r + 1) * rows, c * LANES:(c + 1) * LANES] = (
                        sc_ref[c, pl.ds(r, rows, stride=d), :].astype(o_ref.dtype))


def _qkv_proj(x, w_in, b_in):
    n0 = OFF_Q // QKV_TN
    return pl.pallas_call(
        _qkv_kernel,
        grid=(BATCH, 2, 3 * DIL_DIM // QKV_TN),
        in_specs=[pl.BlockSpec((None, HALF, D_MODEL), lambda b, h, n: (b, h, 0)),
                  pl.BlockSpec((D_MODEL, QKV_TN), lambda b, h, n: (0, n0 + n)),
                  pl.BlockSpec((1, QKV_TN), lambda b, h, n: (0, n0 + n))],
        out_specs=pl.BlockSpec((None, HALF, QKV_TN), lambda b, h, n: (b, h, n)),
        out_shape=jax.ShapeDtypeStruct((BATCH, SEQ, 3 * DIL_DIM), jnp.bfloat16),
        scratch_shapes=[pltpu.VMEM((QKV_CHUNKS, HALF, LANES), jnp.float32)],
        compiler_params=_params(("parallel", "parallel", "arbitrary")),
        name="qkv_proj",
    )(x, w_in, b_in)


def _memattn_kernel(x_ref, w_ref, b_ref, mk_ref, mv_ref, o_ref):
    mq = (_dot(x_ref[...], w_ref[...]) + b_ref[...]).astype(jnp.bfloat16)
    s = _dot_t(mq, mk_ref[...]) * (MEM_HEAD_DIM ** -0.5)
    m = jnp.max(s, axis=-1, keepdims=True)
    p = jnp.exp(s - m)
    den = jnp.sum(p, axis=-1, keepdims=True)
    o = _dot(p.astype(jnp.bfloat16), mv_ref[...]) / den
    o_ref[...] = o.astype(o_ref.dtype)


def _mem_branch(x, w_in, b_in, kv):
    n0 = OFF_MQ // MEM_HEAD_DIM
    return pl.pallas_call(
        _memattn_kernel,
        grid=(BATCH, 2, MEM_HEADS),
        in_specs=[pl.BlockSpec((None, HALF, D_MODEL), lambda b, h, n: (b, h, 0)),
                  pl.BlockSpec((D_MODEL, MEM_HEAD_DIM), lambda b, h, n: (0, n0 + n)),
                  pl.BlockSpec((1, MEM_HEAD_DIM), lambda b, h, n: (0, n0 + n)),
                  pl.BlockSpec((None, MEM_LEN, MEM_HEAD_DIM), lambda b, h, n: (b, 0, n)),
                  pl.BlockSpec((None, MEM_LEN, MEM_HEAD_DIM), lambda b, h, n: (b, 0, MEM_HEADS + n))],
        out_specs=pl.BlockSpec((None, HALF, MEM_HEAD_DIM), lambda b, h, n: (b, h, n)),
        out_shape=jax.ShapeDtypeStruct((BATCH, SEQ, MEM_DIM), jnp.bfloat16),
        compiler_params=_params(("parallel", "parallel", "arbitrary")),
        name="mem_branch",
    )(x, w_in, b_in, kv, kv)


ATT_UNROLL = 16


def _softmax_block(s, v):
    m = jnp.max(s, axis=-1, keepdims=True)
    p = jnp.exp(s - m)
    den = jnp.sum(p, axis=-1, keepdims=True)
    o = _dot(p.astype(jnp.bfloat16), v) / den
    return o, m + jnp.log(den)


def _dilattn_kernel(slopes_ref,
                    q0_ref, q1_ref, q2_ref, k0_ref, k1_ref, k2_ref, v0_ref, v1_ref, v2_ref,
                    o_ref, o_sc, l_sc):
    h = pl.program_id(1)
    blk = ATT_BLOCK
    scale = HEAD_DIM ** -0.5
    qi = lax.broadcasted_iota(jnp.int32, (blk, 2 * blk), 0) + blk
    kj = lax.broadcasted_iota(jnp.int32, (blk, 2 * blk), 1)
    jrel = qi - kj
    valid = (jrel >= 0) & (jrel <= blk)
    jrel_f = jrel.astype(jnp.float32)

    def bias_for(g):
        slope = slopes_ref[g * HEADS_PER_GROUP + h]
        d = float(DIL_PATTERNS[g][1])
        return jnp.where(valid, (-slope * d) * jrel_f, -jnp.inf)

    def put(g, row_slice, o, lse):
        o_sc[g, row_slice, :] = o
        l_sc[g, row_slice, :] = jnp.broadcast_to(lse, (blk, HEAD_DIM))

    def run_blocks(g, blocks):
        scores = [_dot_t(q, k) * scale + bias for q, k, _, bias, _ in blocks]
        stats = []
        for s in scores:
            m = jnp.max(s, axis=-1, keepdims=True)
            p = jnp.exp(s - m)
            stats.append((m, p, jnp.sum(p, axis=-1, keepdims=True)))
        outs = [_dot(p.astype(jnp.bfloat16), blkdef[2]) / den
                for (m, p, den), blkdef in zip(stats, blocks)]
        for o, (m, p, den), blkdef in zip(outs, stats, blocks):
            put(g, blkdef[4], o, m + jnp.log(den))

    bias0 = bias_for(0)
    prev_cols = kj < blk

    def g0_body(it, carry):
        blocks = []
        for k in range(ATT_UNROLL):
            n = it * ATT_UNROLL + k
            q0 = pl.multiple_of(n * blk, blk)
            k0 = pl.multiple_of(jnp.maximum(n - 1, 0) * blk, blk)
            bias = jnp.where(prev_cols & (n == 0), -jnp.inf, bias0)
            blocks.append((q0_ref[pl.ds(q0, blk), :], k0_ref[pl.ds(k0, 2 * blk), :],
                           v0_ref[pl.ds(k0, 2 * blk), :], bias, pl.ds(q0, blk)))
        run_blocks(0, blocks)
        return carry

    lax.fori_loop(0, SEQ // blk // ATT_UNROLL, g0_body, 0)

    d1 = DIL_PATTERNS[1][1]
    cls1 = HALF // d1
    per_half = cls1 // blk
    bias1 = bias_for(1)

    def row1(r, n):
        return (n // per_half) * HALF + r * cls1 + (n % per_half) * blk

    nblk1 = SEQ // d1 // blk
    cls_per_trip = ATT_UNROLL // nblk1

    def g1_body(it, carry):
        blocks = []
        for c in range(cls_per_trip):
            r = it * cls_per_trip + c
            for n in range(nblk1):
                cur = pl.multiple_of(row1(r, n), blk)
                q = q1_ref[pl.ds(cur, blk), :]
                dst = pl.ds(n * blk * d1 + r, blk, stride=d1)
                if n == 0:
                    blocks.append((q, k1_ref[pl.ds(cur, blk), :], v1_ref[pl.ds(cur, blk), :],
                                   bias1[:, blk:], dst))
                else:
                    prev = pl.multiple_of(row1(r, n - 1), blk)
                    kc = jnp.concatenate([k1_ref[pl.ds(prev, blk), :], k1_ref[pl.ds(cur, blk), :]], axis=0)
                    vc = jnp.concatenate([v1_ref[pl.ds(prev, blk), :], v1_ref[pl.ds(cur, blk), :]], axis=0)
                    blocks.append((q, kc, vc, bias1, dst))
        run_blocks(1, blocks)
        return carry

    lax.fori_loop(0, d1 // cls_per_trip, g1_body, 0)

    d2 = DIL_PATTERNS[2][1]
    cls2 = HALF // d2
    bias2 = bias_for(2)

    def g2_body(it, carry):
        blocks = []
        for k in range(ATT_UNROLL):
            r = it * ATT_UNROLL + k
            a = pl.multiple_of(r * cls2, cls2)
            b = pl.multiple_of(HALF + r * cls2, cls2)
            cat = lambda ref, a=a, b=b: jnp.concatenate(
                [ref[pl.ds(a, cls2), :], ref[pl.ds(b, cls2), :]], axis=0)
            blocks.append((cat(q2_ref), cat(k2_ref), cat(v2_ref), bias2[:, blk:],
                           pl.ds(r, blk, stride=d2)))
        run_blocks(2, blocks)
        return carry

    lax.fori_loop(0, d2 // ATT_UNROLL, g2_body, 0)

    rows = 256
    for t in range(SEQ // rows):
        sl = pl.ds(t * rows, rows)
        l0, l1, l2 = l_sc[0, sl, :], l_sc[1, sl, :], l_sc[2, sl, :]
        m = jnp.maximum(jnp.maximum(l0, l1), l2)
        e0, e1, e2 = jnp.exp(l0 - m), jnp.exp(l1 - m), jnp.exp(l2 - m)
        mix = (e0 * o_sc[0, sl, :] + e1 * o_sc[1, sl, :] + e2 * o_sc[2, sl, :]) / (e0 + e1 + e2)
        o_ref[sl, :] = mix.astype(o_ref.dtype)


def _dil_branch(qkv, slopes):
    nq = DIL_DIM // HEAD_DIM

    def spec(section, g):
        return pl.BlockSpec((None, SEQ, HEAD_DIM),
                            lambda b, h, s_ref, o=section * nq + g * HEADS_PER_GROUP: (b, 0, o + h))

    grid_spec = pltpu.PrefetchScalarGridSpec(
        num_scalar_prefetch=1,
        grid=(BATCH, HEADS_PER_GROUP),
        in_specs=[spec(sec, g) for sec in range(3) for g in range(N_DIL_GROUPS)],
        out_specs=pl.BlockSpec((None, SEQ, HEAD_DIM), lambda b, h, s_ref: (b, 0, h)),
        scratch_shapes=[pltpu.VMEM((N_DIL_GROUPS, SEQ, HEAD_DIM), jnp.float32),
                        pltpu.VMEM((N_DIL_GROUPS, SEQ, HEAD_DIM), jnp.float32)],
    )
    return pl.pallas_call(
        _dilattn_kernel,
        grid_spec=grid_spec,
        out_shape=jax.ShapeDtypeStruct((BATCH, SEQ, DIL_OUT_DIM), jnp.bfloat16),
        compiler_params=_params(("parallel", "arbitrary")),
        name="dil_attn",
    )(slopes, *([qkv] * 9))


MERGE_TM = 1024
MERGE_TN = 256


def _merge_kernel(x_ref, wg0_ref, wg1_ref, wg2_ref, bg0_ref, bg1_ref, bg2_ref,
                  sc_ref, od_ref, om_ref, wco_ref, wdo_ref, wmo_ref, o_ref):
    x = x_ref[...]
    g0 = jax.nn.sigmoid(_dot(x, wg0_ref[...]) + bg0_ref[...])
    acc = g0 * _dot(sc_ref[...], wco_ref[...])
    g1 = jax.nn.sigmoid(_dot(x, wg1_ref[...]) + bg1_ref[...])
    acc = acc + g1 * _dot(od_ref[...], wdo_ref[...])
    g2 = jax.nn.sigmoid(_dot(x, wg2_ref[...]) + bg2_ref[...])
    acc = acc + g2 * _dot(om_ref[...], wmo_ref[...])
    o_ref[...] = acc.astype(o_ref.dtype)


def _merge(x2, w_in, b_in, s_conv, o_dil, o_mem, w_co, w_do, w_mo):
    nb = lambda br: (OFF_GATE + br * D_MODEL) // MERGE_TN
    gspec = lambda br: pl.BlockSpec((D_MODEL, MERGE_TN), lambda i, n, o=nb(br): (0, o + n))
    bspec = lambda br: pl.BlockSpec((1, MERGE_TN), lambda i, n, o=nb(br): (0, o + n))
    act = lambda width: pl.BlockSpec((MERGE_TM, width), lambda i, n: (i, 0))
    wout = lambda width: pl.BlockSpec((width, MERGE_TN), lambda i, n: (0, n))
    return pl.pallas_call(
        _merge_kernel,
        grid=(TOKENS // MERGE_TM, D_MODEL // MERGE_TN),
        in_specs=[act(D_MODEL), gspec(0), gspec(1), gspec(2), bspec(0), bspec(1), bspec(2),
                  act(CONV_DIM), act(DIL_OUT_DIM), act(MEM_DIM),
                  wout(CONV_DIM), wout(DIL_OUT_DIM), wout(MEM_DIM)],
        out_specs=pl.BlockSpec((MERGE_TM, MERGE_TN), lambda i, n: (i, n)),
        out_shape=jax.ShapeDtypeStruct((TOKENS, D_MODEL), jnp.bfloat16),
        compiler_params=_params(("parallel", "arbitrary")),
        name="gated_merge",
    )(x2, w_in, w_in, w_in, b_in, b_in, b_in, s_conv, o_dil, o_mem, w_co, w_do, w_mo)


OPROJ_TM = 512


def _oproj_kernel(x_ref, m_ref, wo_ref, g_ref, b_ref, wr_ref, br_ref, x1_ref, logit_ref):
    hres = ALPHA * x_ref[...] + _dot(m_ref[...], wo_ref[...])
    x1 = _layer_norm(hres, g_ref[...], b_ref[...])
    x1_ref[...] = x1
    logit_ref[...] = _dot(x1, wr_ref[...]) + br_ref[...]


def _oproj(x2, merged, w_o, g, b, w_route, b_route):
    row = lambda width: pl.BlockSpec((OPROJ_TM, width), lambda i: (i, 0))
    full = lambda r, c: pl.BlockSpec((r, c), lambda i: (0, 0))
    return pl.pallas_call(
        _oproj_kernel,
        grid=(TOKENS // OPROJ_TM,),
        in_specs=[row(D_MODEL), row(D_MODEL), full(D_MODEL, D_MODEL), full(1, D_MODEL), full(1, D_MODEL),
                  full(D_MODEL, LANES), full(1, LANES)],
        out_specs=[row(D_MODEL), row(LANES)],
        out_shape=[jax.ShapeDtypeStruct((TOKENS, D_MODEL), jnp.float32),
                   jax.ShapeDtypeStruct((TOKENS, LANES), jnp.float32)],
        compiler_params=_params(("parallel",)),
        name="oproj_ln1",
    )(x2, merged, w_o, g, b, w_route, b_route)


def _route_kernel(logit_ref, o_ref, cnt_ref, carry_sc):
    @pl.when(pl.program_id(0) == 0)
    def _():
        carry_sc[...] = jnp.zeros_like(carry_sc)

    tm = ROUTE_TILE
    z = logit_ref[...]
    lane = lax.broadcasted_iota(jnp.int32, (tm, LANES), 1)
    neg = -jnp.inf
    first = lambda hit: jnp.min(jnp.where(hit, lane, LANES), axis=-1, keepdims=True)

    glog = jnp.where(lane < N_GROUPS, z, neg)
    gmax = jnp.max(glog, axis=-1, keepdims=True)
    gsel = first(glog == gmax)
    gw = 1.0 / jnp.sum(jnp.exp(glog - gmax), axis=-1, keepdims=True)

    e_lane = lane - ROUTER_LANE0
    in_group = (e_lane >= 0) & (e_lane < N_EXPERTS) & ((e_lane // EXPERTS_PER_GROUP) == gsel)
    v = jnp.where(in_group, z, neg)
    top1 = jnp.max(v, axis=-1, keepdims=True)
    i1 = first(v == top1)
    v2 = jnp.where(lane == i1, neg, v)
    top2 = jnp.max(v2, axis=-1, keepdims=True)
    i2 = first(v2 == top2)
    t = jnp.exp(top2 - top1)
    w1 = gw / (1.0 + t)
    w2 = w1 * t

    onehot = ((lane == i1) | (lane == i2))
    ri = lax.broadcasted_iota(jnp.int32, (tm, tm), 0)
    ci = lax.broadcasted_iota(jnp.int32, (tm, tm), 1)
    tri = jnp.where(ci <= ri, 1.0, 0.0).astype(jnp.bfloat16)
    cum = _dot(tri, jnp.where(onehot, 1.0, 0.0).astype(jnp.bfloat16))
    carry = carry_sc[0:1, :]
    before = cum + carry - 1.0
    rank1 = jnp.sum(jnp.where(lane == i1, before, 0.0), axis=-1, keepdims=True)
    rank2 = jnp.sum(jnp.where(lane == i2, before, 0.0), axis=-1, keepdims=True)
    new_carry = carry + cum[tm - 1:tm, :]
    carry_sc[...] = jnp.broadcast_to(new_carry, carry_sc.shape)
    cnt_ref[...] = jnp.broadcast_to(new_carry, cnt_ref.shape)

    cols = [(i1 - ROUTER_LANE0).astype(jnp.float32), (i2 - ROUTER_LANE0).astype(jnp.float32),
            rank1, rank2, w1, w2]
    out = jnp.zeros((tm, LANES), jnp.float32)
    for k, col in enumerate(cols):
        out = jnp.where(lane == k, col, out)
    o_ref[...] = out


def _route(logits):
    return pl.pallas_call(
        _route_kernel,
        grid=(TOKENS // ROUTE_TILE,),
        in_specs=[pl.BlockSpec((ROUTE_TILE, LANES), lambda i: (i, 0))],
        out_specs=[pl.BlockSpec((ROUTE_TILE, LANES), lambda i: (i, 0)),
                   pl.BlockSpec((8, LANES), lambda i: (0, 0))],
        out_shape=[jax.ShapeDtypeStruct((TOKENS, LANES), jnp.float32),
                   jax.ShapeDtypeStruct((8, LANES), jnp.float32)],
        scratch_shapes=[pltpu.VMEM((8, LANES), jnp.float32)],
        compiler_params=_params(("arbitrary",)),
        name="route",
    )(logits)


def _slots_kernel(route_ref, cnt_ref, pos_ref):
    lane8 = lax.broadcasted_iota(jnp.int32, (8, LANES), 1)
    cnt = cnt_ref[...]
    padded = jnp.ceil(cnt * (1.0 / EXPERT_TILE)) * EXPERT_TILE
    incl = padded
    for sh in (1, 2, 4, 8, 16, 32, 64):
        incl = incl + jnp.where(lane8 >= sh, pltpu.roll(incl, sh, axis=1), 0.0)
    start = (incl - padded)[0:1, :]

    r = route_ref[...]
    lane = lax.broadcasted_iota(jnp.int32, r.shape, 1).astype(jnp.float32)
    pick = lambda e: jnp.sum(jnp.where(lane == e + float(ROUTER_LANE0), start, 0.0), axis=-1, keepdims=True)
    pos_a = pick(r[:, 0:1]) + r[:, 2:3]
    pos_b = pick(r[:, 1:2]) + r[:, 3:4]
    pos_ref[...] = jnp.where(lane == 0.0, pos_a, jnp.where(lane == 1.0, pos_b, 0.0)).astype(jnp.int32)


def _slots(routing, counts8):
    return pl.pallas_call(
        _slots_kernel,
        grid=(TOKENS // ROUTE_TILE,),
        in_specs=[pl.BlockSpec((ROUTE_TILE, LANES), lambda i: (i, 0)),
                  pl.BlockSpec((8, LANES), lambda i: (0, 0))],
        out_specs=pl.BlockSpec((ROUTE_TILE, LANES), lambda i: (i, 0)),
        out_shape=jax.ShapeDtypeStruct((TOKENS, LANES), jnp.int32),
        compiler_params=_params(("parallel",)),
        name="slots",
    )(routing, counts8)


SUBLANES = 8
DISPATCH_TILE = 512
SORTED_ROWS = N_EXPERT_TILES * EXPERT_TILE


def _dispatch_kernel(pad_start_ref, pad_n_ref, nvalid_ref, pos_ref, x_ref, xs_hbm, zero_sc, sem):
    groups = DISPATCH_TILE // SUBLANES

    @pl.when(pl.program_id(0) == 0)
    def _():
        zero_sc[...] = jnp.zeros_like(zero_sc)

        def tile_copy(t):
            return pltpu.make_async_copy(zero_sc, xs_hbm.at[pl.ds(pl.multiple_of(t * EXPERT_TILE, EXPERT_TILE),
                                                                  EXPERT_TILE)], sem.at[2])

        def start_tile(t, c):
            tile_copy(t).start()
            return c

        def wait_tile(t, c):
            tile_copy(t).wait()
            return c
        lax.fori_loop(nvalid_ref[0], N_EXPERT_TILES, start_tile, 0)
        lax.fori_loop(nvalid_ref[0], N_EXPERT_TILES, wait_tile, 0)

        def per_expert(e, total):
            n = pad_n_ref[e]

            def per_row(i, c):
                pltpu.make_async_copy(zero_sc.at[pl.ds(0, 1)], xs_hbm.at[pl.ds(pad_start_ref[e] + i, 1)],
                                      sem.at[1]).start()
                return c
            lax.fori_loop(0, n, per_row, 0)
            return total + n
        total = lax.fori_loop(0, N_EXPERTS, per_expert, 0)

        def wait_row(i, c):
            pltpu.make_async_copy(zero_sc.at[pl.ds(0, 1)], xs_hbm.at[pl.ds(0, 1)], sem.at[1]).wait()
            return c
        lax.fori_loop(0, total, wait_row, 0)

    def body(gi, c):
        for s in range(SUBLANES):
            for k in range(2):
                p = pos_ref[0, 0, (gi * SUBLANES + s) * 2 + k]
                pltpu.make_async_copy(x_ref.at[gi, pl.ds(s, 1)], xs_hbm.at[pl.ds(p, 1)],
                                      sem.at[0]).start(priority=k)
        return c
    lax.fori_loop(0, groups, body, 0)
    for _ in range(2 * groups):
        pltpu.make_async_copy(x_ref.at[0], xs_hbm.at[pl.ds(0, SUBLANES)], sem.at[0]).wait()


def _dispatch(pad_start, pad_n, n_valid_tiles, pos3, x1):
    nt = TOKENS // DISPATCH_TILE
    groups = DISPATCH_TILE // SUBLANES
    grid_spec = pltpu.PrefetchScalarGridSpec(
        num_scalar_prefetch=3,
        grid=(nt,),
        in_specs=[pl.BlockSpec((1, 1, 2 * DISPATCH_TILE), lambda j, *_: (j, 0, 0), memory_space=pltpu.SMEM),
                  pl.BlockSpec((groups, SUBLANES, D_MODEL), lambda j, *_: (j, 0, 0))],
        out_specs=pl.BlockSpec(memory_space=pl.ANY),
        scratch_shapes=[pltpu.VMEM((EXPERT_TILE, D_MODEL), jnp.float32),
                        pltpu.SemaphoreType.DMA((3,))],
    )
    return pl.pallas_call(
        _dispatch_kernel,
        grid_spec=grid_spec,
        out_shape=jax.ShapeDtypeStruct((SORTED_ROWS, D_MODEL), jnp.float32),
        compiler_params=_params(("arbitrary",)),
        name="dispatch",
    )(pad_start, pad_n, n_valid_tiles, pos3, x1.reshape(TOKENS // SUBLANES, SUBLANES, D_MODEL))


def _ffn_kernel(texp_ref, tvalid_ref, tslot_ref, tfirst_ref, tnext_ref, nvalid_ref,
                xs_ref, wg_hbm, wu_hbm, wd_hbm, y_ref, wg_buf, wu_buf, wd_buf, sem):
    j = pl.program_id(0)
    valid = tvalid_ref[j] == 1
    slot = tslot_ref[j]

    def weight_copies(e, s):
        return (pltpu.make_async_copy(wg_hbm.at[e], wg_buf.at[s], sem.at[s, 0]),
                pltpu.make_async_copy(wu_hbm.at[e], wu_buf.at[s], sem.at[s, 1]),
                pltpu.make_async_copy(wd_hbm.at[e], wd_buf.at[s], sem.at[s, 2]))

    @pl.when(valid & (tfirst_ref[j] == 1))
    def _():
        @pl.when(j == 0)
        def _():
            for cp in weight_copies(texp_ref[j], slot):
                cp.start()

        for cp in weight_copies(texp_ref[j], slot):
            cp.wait()

        @pl.when(tnext_ref[j] >= 0)
        def _():
            for cp in weight_copies(tnext_ref[j], 1 - slot):
                cp.start()

    @pl.when(valid)
    def _():
        xt = xs_ref[...]
        a = _dot(xt, wg_buf[slot])
        u = _dot(xt, wu_buf[slot])
        hid = (a * jax.nn.sigmoid(a)) * u
        y_ref[...] = _dot(hid, wd_buf[slot])

    @pl.when(jnp.logical_not(valid))
    def _():
        y_ref[...] = jnp.zeros_like(y_ref)


def _expert_ffn(tables, xs, w_gate, w_up, w_down):
    nt = N_EXPERT_TILES
    xmap = lambda j, te, tv, ts, tf, tn, nv: (jnp.minimum(j, nv[0] - 1), 0)
    grid_spec = pltpu.PrefetchScalarGridSpec(
        num_scalar_prefetch=6,
        grid=(nt,),
        in_specs=[pl.BlockSpec((EXPERT_TILE, D_MODEL), xmap),
                  pl.BlockSpec(memory_space=pl.ANY),
                  pl.BlockSpec(memory_space=pl.ANY),
                  pl.BlockSpec(memory_space=pl.ANY)],
        out_specs=pl.BlockSpec((EXPERT_TILE, D_MODEL), lambda j, *_: (j, 0)),
        scratch_shapes=[pltpu.VMEM((2, D_MODEL, D_EXPERT), jnp.float32),
                        pltpu.VMEM((2, D_MODEL, D_EXPERT), jnp.float32),
                        pltpu.VMEM((2, D_EXPERT, D_MODEL), jnp.float32),
                        pltpu.SemaphoreType.DMA((2, 3))],
    )
    return pl.pallas_call(
        _ffn_kernel,
        grid_spec=grid_spec,
        out_shape=jax.ShapeDtypeStruct((nt * EXPERT_TILE, D_MODEL), jnp.float32),
        compiler_params=_params(("arbitrary",)),
        name="expert_ffn",
    )(*tables, xs, w_gate, w_up, w_down)


def _combine_kernel(pos_cur_ref, pos_next_ref, r_ref, x1_ref, y_hbm, g_ref, b_ref, o_ref, ybuf, sem):
    j = pl.program_id(0)
    nt = pl.num_programs(0)
    slot = lax.rem(j, 2)
    groups = COMBINE_TILE // SUBLANES

    def gather(pos_ref, dst_slot):
        def body(gi, c):
            for s in range(SUBLANES):
                for k in range(2):
                    p = pos_ref[0, 0, (gi * SUBLANES + s) * 2 + k]
                    pltpu.make_async_copy(y_hbm.at[pl.ds(p, 1)], ybuf.at[dst_slot, k, gi, pl.ds(s, 1)],
                                          sem.at[dst_slot]).start(priority=k)
            return c
        lax.fori_loop(0, groups, body, 0)

    @pl.when(j == 0)
    def _():
        gather(pos_cur_ref, 0)

    @pl.when(j + 1 < nt)
    def _():
        gather(pos_next_ref, 1 - slot)

    for _ in range(2 * groups):
        pltpu.make_async_copy(y_hbm.at[pl.ds(0, SUBLANES)], ybuf.at[slot, 0, 0], sem.at[slot]).wait()
    r = r_ref[...]
    ya = ybuf[slot, 0].reshape(COMBINE_TILE, D_MODEL)
    yb = ybuf[slot, 1].reshape(COMBINE_TILE, D_MODEL)
    hres = ALPHA * x1_ref[...] + r[:, 4:5] * ya + r[:, 5:6] * yb
    o_ref[...] = _layer_norm(hres, g_ref[...], b_ref[...])


def _combine(pos3, routing, x1, y_sorted, g, b):
    nt = TOKENS // COMBINE_TILE
    groups = COMBINE_TILE // SUBLANES
    return pl.pallas_call(
        _combine_kernel,
        grid=(nt,),
        in_specs=[pl.BlockSpec((1, 1, 2 * COMBINE_TILE), lambda j: (j, 0, 0), memory_space=pltpu.SMEM),
                  pl.BlockSpec((1, 1, 2 * COMBINE_TILE), lambda j: (jnp.minimum(j + 1, nt - 1), 0, 0),
                               memory_space=pltpu.SMEM),
                  pl.BlockSpec((COMBINE_TILE, LANES), lambda j: (j, 0)),
                  pl.BlockSpec((COMBINE_TILE, D_MODEL), lambda j: (j, 0)),
                  pl.BlockSpec(memory_space=pl.ANY),
                  pl.BlockSpec((1, D_MODEL), lambda j: (0, 0)),
                  pl.BlockSpec((1, D_MODEL), lambda j: (0, 0))],
        out_specs=pl.BlockSpec((COMBINE_TILE, D_MODEL), lambda j: (j, 0)),
        out_shape=jax.ShapeDtypeStruct((TOKENS, D_MODEL), jnp.float32),
        scratch_shapes=[pltpu.VMEM((2, 2, groups, SUBLANES, D_MODEL), jnp.float32),
                        pltpu.SemaphoreType.DMA((2,))],
        compiler_params=_params(("arbitrary",)),
        name="combine_ln2",
    )(pos3, pos3, routing, x1, y_sorted, g, b)


def _alibi_slopes():
    n = N_DIL_GROUPS * HEADS_PER_GROUP
    return jnp.asarray(2.0 ** (-ALIBI_MAX * np.arange(1, n + 1, dtype=np.float32) / n), jnp.float32)


def kernel(x, mem, ln_mem_g, ln_mem_b, w_in, b_in, w_conv, w_conv_out, w_dil_out, w_mem_kv, w_mem_out, w_o, ln1_g, ln1_b, w_group, b_group, w_router, b_router, w_gate, w_up, w_down, ln2_g, ln2_b):
    assert x.shape == (BATCH, SEQ, D_MODEL) and w_in.shape == (1, D_MODEL, IN_DIM)
    bf16 = jnp.bfloat16
    row = lambda v: v.reshape(1, -1)
    w_in2 = w_in[0]
    b_in2 = b_in

    kv = _memkv(mem, row(ln_mem_g), row(ln_mem_b), w_mem_kv[0])
    s_conv = _conv_branch(x, w_in2, b_in2, w_conv[0])
    qkv = _qkv_proj(x, w_in2, b_in2)
    o_mem = _mem_branch(x, w_in2, b_in2, kv)
    o_dil = _dil_branch(qkv, _alibi_slopes())

    x2 = x.reshape(TOKENS, D_MODEL)
    merged = _merge(x2, w_in2, b_in2,
                    s_conv.reshape(TOKENS, CONV_DIM), o_dil.reshape(TOKENS, DIL_OUT_DIM),
                    o_mem.reshape(TOKENS, MEM_DIM),
                    w_conv_out[0].astype(bf16), w_dil_out[0].astype(bf16), w_mem_out[0].astype(bf16))

    w_route = jnp.concatenate(
        [w_group[0], jnp.transpose(w_router[0], (1, 0, 2)).reshape(D_MODEL, N_EXPERTS),
         jnp.zeros((D_MODEL, LANES - N_GROUPS - N_EXPERTS), jnp.float32)], axis=1)
    b_route = jnp.concatenate(
        [b_group[0], b_router[0].reshape(N_EXPERTS),
         jnp.zeros((LANES - N_GROUPS - N_EXPERTS,), jnp.float32)]).reshape(1, LANES)
    x1, logits = _oproj(x2, merged, w_o[0].astype(bf16), ln1_g, ln1_b, w_route, b_route)

    routing, counts8 = _route(logits)
    pos = _slots(routing, counts8)[:, 0:2]

    i32 = jnp.int32
    counts = counts8[0, ROUTER_LANE0:ROUTER_LANE0 + N_EXPERTS].astype(i32)
    padded = ((counts + EXPERT_TILE - 1) // EXPERT_TILE) * EXPERT_TILE
    ends = jnp.cumsum(padded)
    starts = ends - padded
    tile_start = jnp.arange(N_EXPERT_TILES, dtype=i32) * EXPERT_TILE
    tile_expert = jnp.minimum(jnp.sum((ends[None, :] <= tile_start[:, None]).astype(i32), axis=1),
                              N_EXPERTS - 1)
    tile_valid = tile_start < ends[-1]
    prev_expert = jnp.concatenate([jnp.full((1,), -1, i32), tile_expert[:-1]])
    tile_first = tile_valid & (tile_expert != prev_expert)
    tile_slot = (jnp.cumsum(tile_first.astype(i32)) - 1) & 1
    big = N_EXPERTS
    idx = jnp.where(counts > 0, jnp.arange(N_EXPERTS, dtype=i32), big)
    later = jnp.concatenate([lax.cummin(idx[::-1])[::-1][1:], jnp.full((1,), big, i32)])
    next_used = jnp.where(later == big, -1, later)
    n_valid_tiles = (ends[-1:] // EXPERT_TILE).astype(i32)
    tables = (tile_expert, tile_valid.astype(i32), tile_slot, tile_first.astype(i32),
              next_used[tile_expert], n_valid_tiles)

    xs = _dispatch(starts + counts, padded - counts, n_valid_tiles,
                   pos.reshape(TOKENS // DISPATCH_TILE, 1, 2 * DISPATCH_TILE), x1)
    y_sorted = _expert_ffn(tables, xs,
                           w_gate.reshape(N_EXPERTS, D_MODEL, D_EXPERT),
                           w_up.reshape(N_EXPERTS, D_MODEL, D_EXPERT),
                           w_down.reshape(N_EXPERTS, D_EXPERT, D_MODEL))
    out = _combine(pos.reshape(TOKENS // COMBINE_TILE, 1, 2 * COMBINE_TILE), routing, x1, y_sorted,
                   ln2_g, ln2_b)
    return out.reshape(BATCH, SEQ, D_MODEL)
```

```python
import functools
import math

import numpy as np
import jax
import jax.numpy as jnp
from jax import lax
from jax.experimental import pallas as pl
from jax.experimental.pallas import tpu as pltpu

D_MODEL = 2048
BATCH = 8
SEQ = 2048
TOKENS = BATCH * SEQ
CONV_DIM = 1024
CONV_WIDTH = 3
DIL_PATTERNS = ((128, 1), (512, 4), (2048, 16))
N_DIL_GROUPS = 3
HEADS_PER_GROUP = 4
HEAD_DIM = 128
DIL_DIM = N_DIL_GROUPS * HEADS_PER_GROUP * HEAD_DIM
DIL_OUT_DIM = HEADS_PER_GROUP * HEAD_DIM
ATT_BLOCK = 128
ALIBI_MAX = 8.0
MEM_LEN = 256
MEM_HEADS = 4
MEM_HEAD_DIM = 256
MEM_DIM = MEM_HEADS * MEM_HEAD_DIM
N_BRANCHES = 3
IN_DIM = 3 * CONV_DIM + 3 * DIL_DIM + MEM_DIM + N_BRANCHES * D_MODEL
N_GROUPS = 4
EXPERTS_PER_GROUP = 8
N_EXPERTS = N_GROUPS * EXPERTS_PER_GROUP
D_EXPERT = 512
ALPHA = 2.0 ** 0.25
LN_EPS = 1e-5

OFF_CB = 0
OFF_CC = CONV_DIM
OFF_CH = 2 * CONV_DIM
OFF_Q = 3 * CONV_DIM
OFF_MQ = OFF_Q + 3 * DIL_DIM
OFF_GATE = OFF_MQ + MEM_DIM

LANES = 128
HALF = SEQ // 2
VMEM_LIMIT = 56 * 1024 * 1024

ROUTE_TILE = 512
EXPERT_TILE = 256
N_EXPERT_TILES = 2 * TOKENS // EXPERT_TILE + N_EXPERTS
COMBINE_TILE = 256
ROUTER_LANE0 = N_GROUPS


def _params(sem, limit=VMEM_LIMIT):
    return pltpu.CompilerParams(dimension_semantics=sem, vmem_limit_bytes=limit)


def _layer_norm(x, g, b):
    mu = jnp.mean(x, axis=-1, keepdims=True)
    xc = x - mu
    var = jnp.mean(xc * xc, axis=-1, keepdims=True)
    return xc * lax.rsqrt(var + LN_EPS) * g + b


def _dot(a, b):
    return jnp.dot(a, b, preferred_element_type=jnp.float32)


def _dot_t(a, b):
    return lax.dot_general(a, b, (((1,), (1,)), ((), ())), preferred_element_type=jnp.float32)


def _memkv_kernel(mem_ref, g_ref, b_ref, w_ref, kv_ref):
    y = _layer_norm(mem_ref[...], g_ref[...], b_ref[...])
    kv_ref[...] = _dot(y, w_ref[...]).astype(kv_ref.dtype)


def _memkv(mem, g, b, w):
    return pl.pallas_call(
        _memkv_kernel,
        grid=(BATCH, 2),
        in_specs=[pl.BlockSpec((None, MEM_LEN, D_MODEL), lambda i, j: (i, 0, 0)),
                  pl.BlockSpec((1, D_MODEL), lambda i, j: (0, 0)),
                  pl.BlockSpec((1, D_MODEL), lambda i, j: (0, 0)),
                  pl.BlockSpec((D_MODEL, MEM_DIM), lambda i, j: (0, j))],
        out_specs=pl.BlockSpec((None, MEM_LEN, MEM_DIM), lambda i, j: (i, 0, j)),
        out_shape=jax.ShapeDtypeStruct((BATCH, MEM_LEN, 2 * MEM_DIM), jnp.bfloat16),
        compiler_params=_params(("parallel", "arbitrary")),
        name="mem_kv",
    )(mem, g, b, w)


CONV_TC = 256


def _conv_kernel(x_ref, wb_ref, wc_ref, wh_ref, bb_ref, bc_ref, bh_ref, wconv_ref, s_ref,
                 u_sc, carry_sc):
    half = pl.program_id(1)
    c = pl.program_id(2)
    x = x_ref[...]
    cb = _dot(x, wb_ref[...]) + bb_ref[...]
    cc = _dot(x, wc_ref[...]) + bc_ref[...]
    ch = _dot(x, wh_ref[...]) + bh_ref[...]
    u = cc * ch
    u_sc[0:8, :] = jnp.where(half == 0, 0.0, carry_sc[c])
    u_sc[8:8 + HALF, :] = u
    carry_sc[c] = u[HALF - 8:HALF, :]
    wconv = wconv_ref[...]
    y = (wconv[2:3, :] * u
         + wconv[1:2, :] * u_sc[7:7 + HALF, :]
         + wconv[0:1, :] * u_sc[6:6 + HALF, :])
    s_ref[...] = (cb * y).astype(s_ref.dtype)


def _conv_branch(x, w_in, b_in, w_conv):
    nb = lambda off: off // CONV_TC
    wspec = lambda off: pl.BlockSpec((D_MODEL, CONV_TC), lambda b, h, c, o=nb(off): (0, o + c))
    bspec = lambda off: pl.BlockSpec((1, CONV_TC), lambda b, h, c, o=nb(off): (0, o + c))
    return pl.pallas_call(
        _conv_kernel,
        grid=(BATCH, 2, CONV_DIM // CONV_TC),
        in_specs=[pl.BlockSpec((None, HALF, D_MODEL), lambda b, h, c: (b, h, 0)),
                  wspec(OFF_CB), wspec(OFF_CC), wspec(OFF_CH),
                  bspec(OFF_CB), bspec(OFF_CC), bspec(OFF_CH),
                  pl.BlockSpec((CONV_WIDTH, CONV_TC), lambda b, h, c: (0, c))],
        out_specs=pl.BlockSpec((None, HALF, CONV_TC), lambda b, h, c: (b, h, c)),
        out_shape=jax.ShapeDtypeStruct((BATCH, SEQ, CONV_DIM), jnp.bfloat16),
        scratch_shapes=[pltpu.VMEM((HALF + 8, CONV_TC), jnp.float32),
                        pltpu.VMEM((CONV_DIM // CONV_TC, 8, CONV_TC), jnp.float32)],
        compiler_params=_params(("arbitrary", "arbitrary", "arbitrary")),
        name="conv_branch",
    )(x, w_in, w_in, w_in, b_in, b_in, b_in, w_conv)


QKV_TN = 512
QKV_CHUNKS = QKV_TN // LANES


def _qkv_kernel(x_ref, w_ref, b_ref, o_ref, sc_ref):
    g = lax.rem(pl.program_id(2), N_DIL_GROUPS)
    acc = _dot(x_ref[...], w_ref[...]) + b_ref[...]

    @pl.when(g == 0)
    def _():
        o_ref[...] = acc.astype(o_ref.dtype)

    for gi in (1, 2):
        d = DIL_PATTERNS[gi][1]
        rows = HALF // d

        @pl.when(g == gi)
        def _(d=d, rows=rows):
            for c in range(QKV_CHUNKS):
                sc_ref[c] = acc[:, c * LANES:(c + 1) * LANES]
            for c in range(QKV_CHUNKS):
                for r in range(d):
                    o_ref[r * rows:(r + 1) * rows, c * LANES:(c + 1) * LANES] = (
                        sc_ref[c, pl.ds(r, rows, stride=d), :].astype(o_ref.dtype))


def _qkv_proj(x, w_in, b_in):
    n0 = OFF_Q // QKV_TN
    return pl.pallas_call(
        _qkv_kernel,
        grid=(BATCH, 2, 3 * DIL_DIM // QKV_TN),
        in_specs=[pl.BlockSpec((None, HALF, D_MODEL), lambda b, h, n: (b, h, 0)),
                  pl.BlockSpec((D_MODEL, QKV_TN), lambda b, h, n: (0, n0 + n)),
                  pl.BlockSpec((1, QKV_TN), lambda b, h, n: (0, n0 + n))],
        out_specs=pl.BlockSpec((None, HALF, QKV_TN), lambda b, h, n: (b, h, n)),
        out_shape=jax.ShapeDtypeStruct((BATCH, SEQ, 3 * DIL_DIM), jnp.bfloat16),
        scratch_shapes=[pltpu.VMEM((QKV_CHUNKS, HALF, LANES), jnp.float32)],
        compiler_params=_params(("parallel", "parallel", "arbitrary")),
        name="qkv_proj",
    )(x, w_in, b_in)


def _memattn_kernel(x_ref, w_ref, b_ref, mk_ref, mv_ref, o_ref):
    mq = (_dot(x_ref[...], w_ref[...]) + b_ref[...]).astype(jnp.bfloat16)
    s = _dot_t(mq, mk_ref[...]) * (MEM_HEAD_DIM ** -0.5)
    m = jnp.max(s, axis=-1, keepdims=True)
    p = jnp.exp(s - m)
    den = jnp.sum(p, axis=-1, keepdims=True)
    o = _dot(p.astype(jnp.bfloat16), mv_ref[...]) / den
    o_ref[...] = o.astype(o_ref.dtype)


def _mem_branch(x, w_in, b_in, kv):
    n0 = OFF_MQ // MEM_HEAD_DIM
    return pl.pallas_call(
        _memattn_kernel,
        grid=(BATCH, 2, MEM_HEADS),
        in_specs=[pl.BlockSpec((None, HALF, D_MODEL), lambda b, h, n: (b, h, 0)),
                  pl.BlockSpec((D_MODEL, MEM_HEAD_DIM), lambda b, h, n: (0, n0 + n)),
                  pl.BlockSpec((1, MEM_HEAD_DIM), lambda b, h, n: (0, n0 + n)),
                  pl.BlockSpec((None, MEM_LEN, MEM_HEAD_DIM), lambda b, h, n: (b, 0, n)),
                  pl.BlockSpec((None, MEM_LEN, MEM_HEAD_DIM), lambda b, h, n: (b, 0, MEM_HEADS + n))],
        out_specs=pl.BlockSpec((None, HALF, MEM_HEAD_DIM), lambda b, h, n: (b, h, n)),
        out_shape=jax.ShapeDtypeStruct((BATCH, SEQ, MEM_DIM), jnp.bfloat16),
        compiler_params=_params(("parallel", "parallel", "arbitrary")),
        name="mem_branch",
    )(x, w_in, b_in, kv, kv)


ATT_UNROLL = 16


def _softmax_block(s, v):
    m = jnp.max(s, axis=-1, keepdims=True)
    p = jnp.exp(s - m)
    den = jnp.sum(p, axis=-1, keepdims=True)
    o = _dot(p.astype(jnp.bfloat16), v) / den
    return o, m + jnp.log(den)


def _dilattn_kernel(slopes_ref,
                    q0_ref, q1_ref, q2_ref, k0_ref, k1_ref, k2_ref, v0_ref, v1_ref, v2_ref,
                    o_ref, o_sc, l_sc):
    h = pl.program_id(1)
    blk = ATT_BLOCK
    scale = HEAD_DIM ** -0.5
    qi = lax.broadcasted_iota(jnp.int32, (blk, 2 * blk), 0) + blk
    kj = lax.broadcasted_iota(jnp.int32, (blk, 2 * blk), 1)
    jrel = qi - kj
    valid = (jrel >= 0) & (jrel <= blk)
    jrel_f = jrel.astype(jnp.float32)

    def bias_for(g):
        slope = slopes_ref[g * HEADS_PER_GROUP + h]
        d = float(DIL_PATTERNS[g][1])
        return jnp.where(valid, (-slope * d) * jrel_f, -jnp.inf)

    def put(g, row_slice, o, lse):
        o_sc[g, row_slice, :] = o
        l_sc[g, row_slice, :] = jnp.broadcast_to(lse, (blk, HEAD_DIM))

    def run_blocks(g, blocks):
        scores = [_dot_t(q, k) * scale + bias for q, k, _, bias, _ in blocks]
        stats = []
        for s in scores:
            m = jnp.max(s, axis=-1, keepdims=True)
            p = jnp.exp(s - m)
            stats.append((m, p, jnp.sum(p, axis=-1, keepdims=True)))
        outs = [_dot(p.astype(jnp.bfloat16), blkdef[2]) / den
                for (m, p, den), blkdef in zip(stats, blocks)]
        for o, (m, p, den), blkdef in zip(outs, stats, blocks):
            put(g, blkdef[4], o, m + jnp.log(den))

    bias0 = bias_for(0)
    prev_cols = kj < blk

    def g0_body(it, carry):
        blocks = []
        for k in range(ATT_UNROLL):
            n = it * ATT_UNROLL + k
            q0 = pl.multiple_of(n * blk, blk)
            k0 = pl.multiple_of(jnp.maximum(n - 1, 0) * blk, blk)
            bias = jnp.where(prev_cols & (n == 0), -jnp.inf, bias0)
            blocks.append((q0_ref[pl.ds(q0, blk), :], k0_ref[pl.ds(k0, 2 * blk), :],
                           v0_ref[pl.ds(k0, 2 * blk), :], bias, pl.ds(q0, blk)))
        run_blocks(0, blocks)
        return carry

    lax.fori_loop(0, SEQ // blk // ATT_UNROLL, g0_body, 0)

    d1 = DIL_PATTERNS[1][1]
    cls1 = HALF // d1
    per_half = cls1 // blk
    bias1 = bias_for(1)

    def row1(r, n):
        return (n // per_half) * HALF + r * cls1 + (n % per_half) * blk

    nblk1 = SEQ // d1 // blk
    cls_per_trip = ATT_UNROLL // nblk1

    def g1_body(it, carry):
        blocks = []
        for c in range(cls_per_trip):
            r = it * cls_per_trip + c
            for n in range(nblk1):
                cur = pl.multiple_of(row1(r, n), blk)
                q = q1_ref[pl.ds(cur, blk), :]
                dst = pl.ds(n * blk * d1 + r, blk, stride=d1)
                if n == 0:
                    blocks.append((q, k1_ref[pl.ds(cur, blk), :], v1_ref[pl.ds(cur, blk), :],
                                   bias1[:, blk:], dst))
                else:
                    prev = pl.multiple_of(row1(r, n - 1), blk)
                    kc = jnp.concatenate([k1_ref[pl.ds(prev, blk), :], k1_ref[pl.ds(cur, blk), :]], axis=0)
                    vc = jnp.concatenate([v1_ref[pl.ds(prev, blk), :], v1_ref[pl.ds(cur, blk), :]], axis=0)
                    blocks.append((q, kc, vc, bias1, dst))
        run_blocks(1, blocks)
        return carry

    lax.fori_loop(0, d1 // cls_per_trip, g1_body, 0)

    d2 = DIL_PATTERNS[2][1]
    cls2 = HALF // d2
    bias2 = bias_for(2)

    def g2_body(it, carry):
        blocks = []
        for k in range(ATT_UNROLL):
            r = it * ATT_UNROLL + k
            a = pl.multiple_of(r * cls2, cls2)
            b = pl.multiple_of(HALF + r * cls2, cls2)
            cat = lambda ref, a=a, b=b: jnp.concatenate(
                [ref[pl.ds(a, cls2), :], ref[pl.ds(b, cls2), :]], axis=0)
            blocks.append((cat(q2_ref), cat(k2_ref), cat(v2_ref), bias2[:, blk:],
                           pl.ds(r, blk, stride=d2)))
        run_blocks(2, blocks)
        return carry

    lax.fori_loop(0, d2 // ATT_UNROLL, g2_body, 0)

    rows = 256
    for t in range(SEQ // rows):
        sl = pl.ds(t * rows, rows)
        l0, l1, l2 = l_sc[0, sl, :], l_sc[1, sl, :], l_sc[2, sl, :]
        m = jnp.maximum(jnp.maximum(l0, l1), l2)
        e0, e1, e2 = jnp.exp(l0 - m), jnp.exp(l1 - m), jnp.exp(l2 - m)
        mix = (e0 * o_sc[0, sl, :] + e1 * o_sc[1, sl, :] + e2 * o_sc[2, sl, :]) / (e0 + e1 + e2)
        o_ref[sl, :] = mix.astype(o_ref.dtype)


def _dil_branch(qkv, slopes):
    nq = DIL_DIM // HEAD_DIM

    def spec(section, g):
        return pl.BlockSpec((None, SEQ, HEAD_DIM),
                            lambda b, h, s_ref, o=section * nq + g * HEADS_PER_GROUP: (b, 0, o + h))

    grid_spec = pltpu.PrefetchScalarGridSpec(
        num_scalar_prefetch=1,
        grid=(BATCH, HEADS_PER_GROUP),
        in_specs=[spec(sec, g) for sec in range(3) for g in range(N_DIL_GROUPS)],
        out_specs=pl.BlockSpec((None, SEQ, HEAD_DIM), lambda b, h, s_ref: (b, 0, h)),
        scratch_shapes=[pltpu.VMEM((N_DIL_GROUPS, SEQ, HEAD_DIM), jnp.float32),
                        pltpu.VMEM((N_DIL_GROUPS, SEQ, HEAD_DIM), jnp.float32)],
    )
    return pl.pallas_call(
        _dilattn_kernel,
        grid_spec=grid_spec,
        out_shape=jax.ShapeDtypeStruct((BATCH, SEQ, DIL_OUT_DIM), jnp.bfloat16),
        compiler_params=_params(("parallel", "arbitrary")),
        name="dil_attn",
    )(slopes, *([qkv] * 9))


MERGE_TM = 1024
MERGE_TN = 256


def _merge_kernel(x_ref, wg0_ref, wg1_ref, wg2_ref, bg0_ref, bg1_ref, bg2_ref,
                  sc_ref, od_ref, om_ref, wco_ref, wdo_ref, wmo_ref, o_ref):
    x = x_ref[...]
    g0 = jax.nn.sigmoid(_dot(x, wg0_ref[...]) + bg0_ref[...])
    acc = g0 * _dot(sc_ref[...], wco_ref[...])
    g1 = jax.nn.sigmoid(_dot(x, wg1_ref[...]) + bg1_ref[...])
    acc = acc + g1 * _dot(od_ref[...], wdo_ref[...])
    g2 = jax.nn.sigmoid(_dot(x, wg2_ref[...]) + bg2_ref[...])
    acc = acc + g2 * _dot(om_ref[...], wmo_ref[...])
    o_ref[...] = acc.astype(o_ref.dtype)


def _merge(x2, w_in, b_in, s_conv, o_dil, o_mem, w_co, w_do, w_mo):
    nb = lambda br: (OFF_GATE + br * D_MODEL) // MERGE_TN
    gspec = lambda br: pl.BlockSpec((D_MODEL, MERGE_TN), lambda i, n, o=nb(br): (0, o + n))
    bspec = lambda br: pl.BlockSpec((1, MERGE_TN), lambda i, n, o=nb(br): (0, o + n))
    act = lambda width: pl.BlockSpec((MERGE_TM, width), lambda i, n: (i, 0))
    wout = lambda width: pl.BlockSpec((width, MERGE_TN), lambda i, n: (0, n))
    return pl.pallas_call(
        _merge_kernel,
        grid=(TOKENS // MERGE_TM, D_MODEL // MERGE_TN),
        in_specs=[act(D_MODEL), gspec(0), gspec(1), gspec(2), bspec(0), bspec(1), bspec(2),
                  act(CONV_DIM), act(DIL_OUT_DIM), act(MEM_DIM),
                  wout(CONV_DIM), wout(DIL_OUT_DIM), wout(MEM_DIM)],
        out_specs=pl.BlockSpec((MERGE_TM, MERGE_TN), lambda i, n: (i, n)),
        out_shape=jax.ShapeDtypeStruct((TOKENS, D_MODEL), jnp.bfloat16),
        compiler_params=_params(("parallel", "arbitrary")),
        name="gated_merge",
    )(x2, w_in, w_in, w_in, b_in, b_in, b_in, s_conv, o_dil, o_mem, w_co, w_do, w_mo)


OPROJ_TM = 512


def _oproj_kernel(x_ref, m_ref, wo_ref, g_ref, b_ref, wr_ref, br_ref, x1_ref, logit_ref):
    hres = ALPHA * x_ref[...] + _dot(m_ref[...], wo_ref[...])
    x1 = _layer_norm(hres, g_ref[...], b_ref[...])
    x1_ref[...] = x1
    logit_ref[...] = _dot(x1, wr_ref[...]) + br_ref[...]


def _oproj(x2, merged, w_o, g, b, w_route, b_route):
    row = lambda width: pl.BlockSpec((OPROJ_TM, width), lambda i: (i, 0))
    full = lambda r, c: pl.BlockSpec((r, c), lambda i: (0, 0))
    return pl.pallas_call(
        _oproj_kernel,
        grid=(TOKENS // OPROJ_TM,),
        in_specs=[row(D_MODEL), row(D_MODEL), full(D_MODEL, D_MODEL), full(1, D_MODEL), full(1, D_MODEL),
                  full(D_MODEL, LANES), full(1, LANES)],
        out_specs=[row(D_MODEL), row(LANES)],
        out_shape=[jax.ShapeDtypeStruct((TOKENS, D_MODEL), jnp.float32),
                   jax.ShapeDtypeStruct((TOKENS, LANES), jnp.float32)],
        compiler_params=_params(("parallel",)),
        name="oproj_ln1",
    )(x2, merged, w_o, g, b, w_route, b_route)


def _route_kernel(logit_ref, o_ref, cnt_ref, carry_sc):
    @pl.when(pl.program_id(0) == 0)
    def _():
        carry_sc[...] = jnp.zeros_like(carry_sc)

    tm = ROUTE_TILE
    z = logit_ref[...]
    lane = lax.broadcasted_iota(jnp.int32, (tm, LANES), 1)
    neg = -jnp.inf
    first = lambda hit: jnp.min(jnp.where(hit, lane, LANES), axis=-1, keepdims=True)

    glog = jnp.where(lane < N_GROUPS, z, neg)
    gmax = jnp.max(glog, axis=-1, keepdims=True)
    gsel = first(glog == gmax)
    gw = 1.0 / jnp.sum(jnp.exp(glog - gmax), axis=-1, keepdims=True)

    e_lane = lane - ROUTER_LANE0
    in_group = (e_lane >= 0) & (e_lane < N_EXPERTS) & ((e_lane // EXPERTS_PER_GROUP) == gsel)
    v = jnp.where(in_group, z, neg)
    top1 = jnp.max(v, axis=-1, keepdims=True)
    i1 = first(v == top1)
    v2 = jnp.where(lane == i1, neg, v)
    top2 = jnp.max(v2, axis=-1, keepdims=True)
    i2 = first(v2 == top2)
    t = jnp.exp(top2 - top1)
    w1 = gw / (1.0 + t)
    w2 = w1 * t

    onehot = ((lane == i1) | (lane == i2))
    ri = lax.broadcasted_iota(jnp.int32, (tm, tm), 0)
    ci = lax.broadcasted_iota(jnp.int32, (tm, tm), 1)
    tri = jnp.where(ci <= ri, 1.0, 0.0).astype(jnp.bfloat16)
    cum = _dot(tri, jnp.where(onehot, 1.0, 0.0).astype(jnp.bfloat16))
    carry = carry_sc[0:1, :]
    before = cum + carry - 1.0
    rank1 = jnp.sum(jnp.where(lane == i1, before, 0.0), axis=-1, keepdims=True)
    rank2 = jnp.sum(jnp.where(lane == i2, before, 0.0), axis=-1, keepdims=True)
    new_carry = carry + cum[tm - 1:tm, :]
    carry_sc[...] = jnp.broadcast_to(new_carry, carry_sc.shape)
    cnt_ref[...] = jnp.broadcast_to(new_carry, cnt_ref.shape)

    cols = [(i1 - ROUTER_LANE0).astype(jnp.float32), (i2 - ROUTER_LANE0).astype(jnp.float32),
            rank1, rank2, w1, w2]
    out = jnp.zeros((tm, LANES), jnp.float32)
    for k, col in enumerate(cols):
        out = jnp.where(lane == k, col, out)
    o_ref[...] = out


def _route(logits):
    return pl.pallas_call(
        _route_kernel,
        grid=(TOKENS // ROUTE_TILE,),
        in_specs=[pl.BlockSpec((ROUTE_TILE, LANES), lambda i: (i, 0))],
        out_specs=[pl.BlockSpec((ROUTE_TILE, LANES), lambda i: (i, 0)),
                   pl.BlockSpec((8, LANES), lambda i: (0, 0))],
        out_shape=[jax.ShapeDtypeStruct((TOKENS, LANES), jnp.float32),
                   jax.ShapeDtypeStruct((8, LANES), jnp.float32)],
        scratch_shapes=[pltpu.VMEM((8, LANES), jnp.float32)],
        compiler_params=_params(("arbitrary",)),
        name="route",
    )(logits)


def _slots_kernel(route_ref, cnt_ref, pos_ref):
    lane8 = lax.broadcasted_iota(jnp.int32, (8, LANES), 1)
    cnt = cnt_ref[...]
    padded = jnp.ceil(cnt * (1.0 / EXPERT_TILE)) * EXPERT_TILE
    incl = padded
    for sh in (1, 2, 4, 8, 16, 32, 64):
        incl = incl + jnp.where(lane8 >= sh, pltpu.roll(incl, sh, axis=1), 0.0)
    start = (incl - padded)[0:1, :]

    r = route_ref[...]
    lane = lax.broadcasted_iota(jnp.int32, r.shape, 1).astype(jnp.float32)
    pick = lambda e: jnp.sum(jnp.where(lane == e + float(ROUTER_LANE0), start, 0.0), axis=-1, keepdims=True)
    pos_a = pick(r[:, 0:1]) + r[:, 2:3]
    pos_b = pick(r[:, 1:2]) + r[:, 3:4]
    pos_ref[...] = jnp.where(lane == 0.0, pos_a, jnp.where(lane == 1.0, pos_b, 0.0)).astype(jnp.int32)


def _slots(routing, counts8):
    return pl.pallas_call(
        _slots_kernel,
        grid=(TOKENS // ROUTE_TILE,),
        in_specs=[pl.BlockSpec((ROUTE_TILE, LANES), lambda i: (i, 0)),
                  pl.BlockSpec((8, LANES), lambda i: (0, 0))],
        out_specs=pl.BlockSpec((ROUTE_TILE, LANES), lambda i: (i, 0)),
        out_shape=jax.ShapeDtypeStruct((TOKENS, LANES), jnp.int32),
        compiler_params=_params(("parallel",)),
        name="slots",
    )(routing, counts8)


SUBLANES = 8
DISPATCH_TILE = 512
SORTED_ROWS = N_EXPERT_TILES * EXPERT_TILE


PACKED = D_MODEL // 2


def _pack_rows(v):
    return pltpu.pack_elementwise([v[:, :PACKED], v[:, PACKED:]], packed_dtype=jnp.bfloat16)


def _unpack_rows(w):
    half = lambda i: pltpu.unpack_elementwise(w, index=i, packed_dtype=jnp.bfloat16,
                                              unpacked_dtype=jnp.float32)
    return half(0), half(1)


def _dispatch_kernel(pad_start_ref, pad_n_ref, nvalid_ref, pos_ref, x_ref, xs_hbm, zero_sc, pk_sc, sem):
    groups = DISPATCH_TILE // SUBLANES

    @pl.when(pl.program_id(0) == 0)
    def _():
        zero_sc[...] = jnp.zeros_like(zero_sc)

        def tile_copy(t):
            return pltpu.make_async_copy(zero_sc, xs_hbm.at[pl.ds(pl.multiple_of(t * EXPERT_TILE, EXPERT_TILE),
                                                                  EXPERT_TILE)], sem.at[2])

        def start_tile(t, c):
            tile_copy(t).start()
            return c

        def wait_tile(t, c):
            tile_copy(t).wait()
            return c
        lax.fori_loop(nvalid_ref[0], N_EXPERT_TILES, start_tile, 0)
        lax.fori_loop(nvalid_ref[0], N_EXPERT_TILES, wait_tile, 0)

        def per_expert(e, total):
            n = pad_n_ref[e]

            def per_row(i, c):
                pltpu.make_async_copy(zero_sc.at[pl.ds(0, 1)], xs_hbm.at[pl.ds(pad_start_ref[e] + i, 1)],
                                      sem.at[1]).start()
                return c
            lax.fori_loop(0, n, per_row, 0)
            return total + n
        total = lax.fori_loop(0, N_EXPERTS, per_expert, 0)

        def wait_row(i, c):
            pltpu.make_async_copy(zero_sc.at[pl.ds(0, 1)], xs_hbm.at[pl.ds(0, 1)], sem.at[1]).wait()
            return c
        lax.fori_loop(0, total, wait_row, 0)

    pk_sc[...] = _pack_rows(x_ref[...]).reshape(groups, SUBLANES, PACKED)

    def body(gi, c):
        for s in range(SUBLANES):
            for k in range(2):
                p = pos_ref[0, 0, (gi * SUBLANES + s) * 2 + k]
                pltpu.make_async_copy(pk_sc.at[gi, pl.ds(s, 1)], xs_hbm.at[pl.ds(p, 1)],
                                      sem.at[0]).start(priority=k)
        return c
    lax.fori_loop(0, groups, body, 0)
    for _ in range(2 * groups):
        pltpu.make_async_copy(pk_sc.at[0], xs_hbm.at[pl.ds(0, SUBLANES)], sem.at[0]).wait()


def _dispatch(pad_start, pad_n, n_valid_tiles, pos3, x1):
    nt = TOKENS // DISPATCH_TILE
    groups = DISPATCH_TILE // SUBLANES
    grid_spec = pltpu.PrefetchScalarGridSpec(
        num_scalar_prefetch=3,
        grid=(nt,),
        in_specs=[pl.BlockSpec((1, 1, 2 * DISPATCH_TILE), lambda j, *_: (j, 0, 0), memory_space=pltpu.SMEM),
                  pl.BlockSpec((DISPATCH_TILE, D_MODEL), lambda j, *_: (j, 0))],
        out_specs=pl.BlockSpec(memory_space=pl.ANY),
        scratch_shapes=[pltpu.VMEM((EXPERT_TILE, PACKED), jnp.int32),
                        pltpu.VMEM((groups, SUBLANES, PACKED), jnp.int32),
                        pltpu.SemaphoreType.DMA((3,))],
    )
    return pl.pallas_call(
        _dispatch_kernel,
        grid_spec=grid_spec,
        out_shape=jax.ShapeDtypeStruct((SORTED_ROWS, PACKED), jnp.int32),
        compiler_params=_params(("arbitrary",)),
        name="dispatch",
    )(pad_start, pad_n, n_valid_tiles, pos3, x1)


def _ffn_kernel(texp_ref, tvalid_ref, tslot_ref, tfirst_ref, tnext_ref, nvalid_ref,
                xs_ref, wg_hbm, wu_hbm, wd_hbm, y_ref, wg_buf, wu_buf, wd_buf, sem):
    j = pl.program_id(0)
    valid = tvalid_ref[j] == 1
    slot = tslot_ref[j]

    def weight_copies(e, s):
        return (pltpu.make_async_copy(wg_hbm.at[e], wg_buf.at[s], sem.at[s, 0]),
                pltpu.make_async_copy(wu_hbm.at[e], wu_buf.at[s], sem.at[s, 1]),
                pltpu.make_async_copy(wd_hbm.at[e], wd_buf.at[s], sem.at[s, 2]))

    @pl.when(valid & (tfirst_ref[j] == 1))
    def _():
        @pl.when(j == 0)
        def _():
            for cp in weight_copies(texp_ref[j], slot):
                cp.start()

        for cp in weight_copies(texp_ref[j], slot):
            cp.wait()

        @pl.when(tnext_ref[j] >= 0)
        def _():
            for cp in weight_copies(tnext_ref[j], 1 - slot):
                cp.start()

    @pl.when(valid)
    def _():
        lo, hi = _unpack_rows(xs_ref[...])
        a = _dot(lo, wg_buf[slot, 0:PACKED, :]) + _dot(hi, wg_buf[slot, PACKED:D_MODEL, :])
        u = _dot(lo, wu_buf[slot, 0:PACKED, :]) + _dot(hi, wu_buf[slot, PACKED:D_MODEL, :])
        hid = (a * jax.nn.sigmoid(a)) * u
        y_ref[...] = _pack_rows(_dot(hid, wd_buf[slot]))

    @pl.when(jnp.logical_not(valid))
    def _():
        y_ref[...] = jnp.zeros_like(y_ref)


def _expert_ffn(tables, xs, w_gate, w_up, w_down):
    nt = N_EXPERT_TILES
    xmap = lambda j, te, tv, ts, tf, tn, nv: (jnp.minimum(j, nv[0] - 1), 0)
    grid_spec = pltpu.PrefetchScalarGridSpec(
        num_scalar_prefetch=6,
        grid=(nt,),
        in_specs=[pl.BlockSpec((EXPERT_TILE, PACKED), xmap),
                  pl.BlockSpec(memory_space=pl.ANY),
                  pl.BlockSpec(memory_space=pl.ANY),
                  pl.BlockSpec(memory_space=pl.ANY)],
        out_specs=pl.BlockSpec((EXPERT_TILE, PACKED), lambda j, *_: (j, 0)),
        scratch_shapes=[pltpu.VMEM((2, D_MODEL, D_EXPERT), jnp.float32),
                        pltpu.VMEM((2, D_MODEL, D_EXPERT), jnp.float32),
                        pltpu.VMEM((2, D_EXPERT, D_MODEL), jnp.float32),
                        pltpu.SemaphoreType.DMA((2, 3))],
    )
    return pl.pallas_call(
        _ffn_kernel,
        grid_spec=grid_spec,
        out_shape=jax.ShapeDtypeStruct((nt * EXPERT_TILE, PACKED), jnp.int32),
        compiler_params=_params(("arbitrary",)),
        name="expert_ffn",
    )(*tables, xs, w_gate, w_up, w_down)


def _combine_kernel(pos_cur_ref, pos_next_ref, r_ref, x1_ref, y_hbm, g_ref, b_ref, o_ref, ybuf, sem):
    j = pl.program_id(0)
    nt = pl.num_programs(0)
    slot = lax.rem(j, 2)
    groups = COMBINE_TILE // SUBLANES

    def gather(pos_ref, dst_slot):
        def body(gi, c):
            for s in range(SUBLANES):
                for k in range(2):
                    p = pos_ref[0, 0, (gi * SUBLANES + s) * 2 + k]
                    pltpu.make_async_copy(y_hbm.at[pl.ds(p, 1)], ybuf.at[dst_slot, k, gi, pl.ds(s, 1)],
                                          sem.at[dst_slot]).start(priority=k)
            return c
        lax.fori_loop(0, groups, body, 0)

    @pl.when(j == 0)
    def _():
        gather(pos_cur_ref, 0)

    @pl.when(j + 1 < nt)
    def _():
        gather(pos_next_ref, 1 - slot)

    for _ in range(2 * groups):
        pltpu.make_async_copy(y_hbm.at[pl.ds(0, SUBLANES)], ybuf.at[slot, 0, 0], sem.at[slot]).wait()
    r = r_ref[...]
    unpack = lambda k: jnp.concatenate(_unpack_rows(ybuf[slot, k].reshape(COMBINE_TILE, PACKED)), axis=1)
    ya, yb = unpack(0), unpack(1)
    hres = ALPHA * x1_ref[...] + r[:, 4:5] * ya + r[:, 5:6] * yb
    o_ref[...] = _layer_norm(hres, g_ref[...], b_ref[...])


def _combine(pos3, routing, x1, y_sorted, g, b):
    nt = TOKENS // COMBINE_TILE
    groups = COMBINE_TILE // SUBLANES
    return pl.pallas_call(
        _combine_kernel,
        grid=(nt,),
        in_specs=[pl.BlockSpec((1, 1, 2 * COMBINE_TILE), lambda j: (j, 0, 0), memory_space=pltpu.SMEM),
                  pl.BlockSpec((1, 1, 2 * COMBINE_TILE), lambda j: (jnp.minimum(j + 1, nt - 1), 0, 0),
                               memory_space=pltpu.SMEM),
                  pl.BlockSpec((COMBINE_TILE, LANES), lambda j: (j, 0)),
                  pl.BlockSpec((COMBINE_TILE, D_MODEL), lambda j: (j, 0)),
                  pl.BlockSpec(memory_space=pl.ANY),
                  pl.BlockSpec((1, D_MODEL), lambda j: (0, 0)),
                  pl.BlockSpec((1, D_MODEL), lambda j: (0, 0))],
        out_specs=pl.BlockSpec((COMBINE_TILE, D_MODEL), lambda j: (j, 0)),
        out_shape=jax.ShapeDtypeStruct((TOKENS, D_MODEL), jnp.float32),
        scratch_shapes=[pltpu.VMEM((2, 2, groups, SUBLANES, PACKED), jnp.int32),
                        pltpu.SemaphoreType.DMA((2,))],
        compiler_params=_params(("arbitrary",)),
        name="combine_ln2",
    )(pos3, pos3, routing, x1, y_sorted, g, b)


def _alibi_slopes():
    n = N_DIL_GROUPS * HEADS_PER_GROUP
    return jnp.asarray(2.0 ** (-ALIBI_MAX * np.arange(1, n + 1, dtype=np.float32) / n), jnp.float32)


def kernel(x, mem, ln_mem_g, ln_mem_b, w_in, b_in, w_conv, w_conv_out, w_dil_out, w_mem_kv, w_mem_out, w_o, ln1_g, ln1_b, w_group, b_group, w_router, b_router, w_gate, w_up, w_down, ln2_g, ln2_b):
    assert x.shape == (BATCH, SEQ, D_MODEL) and w_in.shape == (1, D_MODEL, IN_DIM)
    bf16 = jnp.bfloat16
    row = lambda v: v.reshape(1, -1)
    w_in2 = w_in[0]
    b_in2 = b_in

    kv = _memkv(mem, row(ln_mem_g), row(ln_mem_b), w_mem_kv[0])
    s_conv = _conv_branch(x, w_in2, b_in2, w_conv[0])
    qkv = _qkv_proj(x, w_in2, b_in2)
    o_mem = _mem_branch(x, w_in2, b_in2, kv)
    o_dil = _dil_branch(qkv, _alibi_slopes())

    x2 = x.reshape(TOKENS, D_MODEL)
    merged = _merge(x2, w_in2, b_in2,
                    s_conv.reshape(TOKENS, CONV_DIM), o_dil.reshape(TOKENS, DIL_OUT_DIM),
                    o_mem.reshape(TOKENS, MEM_DIM),
                    w_conv_out[0].astype(bf16), w_dil_out[0].astype(bf16), w_mem_out[0].astype(bf16))

    w_route = jnp.concatenate(
        [w_group[0], jnp.transpose(w_router[0], (1, 0, 2)).reshape(D_MODEL, N_EXPERTS),
         jnp.zeros((D_MODEL, LANES - N_GROUPS - N_EXPERTS), jnp.float32)], axis=1)
    b_route = jnp.concatenate(
        [b_group[0], b_router[0].reshape(N_EXPERTS),
         jnp.zeros((LANES - N_GROUPS - N_EXPERTS,), jnp.float32)]).reshape(1, LANES)
    x1, logits = _oproj(x2, merged, w_o[0].astype(bf16), ln1_g, ln1_b, w_route, b_route)

    routing, counts8 = _route(logits)
    pos = _slots(routing, counts8)[:, 0:2]

    i32 = jnp.int32
    counts = counts8[0, ROUTER_LANE0:ROUTER_LANE0 + N_EXPERTS].astype(i32)
    padded = ((counts + EXPERT_TILE - 1) // EXPERT_TILE) * EXPERT_TILE
    ends = jnp.cumsum(padded)
    starts = ends - padded
    tile_start = jnp.arange(N_EXPERT_TILES, dtype=i32) * EXPERT_TILE
    tile_expert = jnp.minimum(jnp.sum((ends[None, :] <= tile_start[:, None]).astype(i32), axis=1),
                              N_EXPERTS - 1)
    tile_valid = tile_start < ends[-1]
    prev_expert = jnp.concatenate([jnp.full((1,), -1, i32), tile_expert[:-1]])
    tile_first = tile_valid & (tile_expert != prev_expert)
    tile_slot = (jnp.cumsum(tile_first.astype(i32)) - 1) & 1
    big = N_EXPERTS
    idx = jnp.where(counts > 0, jnp.arange(N_EXPERTS, dtype=i32), big)
    later = jnp.concatenate([lax.cummin(idx[::-1])[::-1][1:], jnp.full((1,), big, i32)])
    next_used = jnp.where(later == big, -1, later)
    n_valid_tiles = (ends[-1:] // EXPERT_TILE).astype(i32)
    tables = (tile_expert, tile_valid.astype(i32), tile_slot, tile_first.astype(i32),
              next_used[tile_expert], n_valid_tiles)

    xs = _dispatch(starts + counts, padded - counts, n_valid_tiles,
                   pos.reshape(TOKENS // DISPATCH_TILE, 1, 2 * DISPATCH_TILE), x1)
    y_sorted = _expert_ffn(tables, xs,
                           w_gate.reshape(N_EXPERTS, D_MODEL, D_EXPERT),
                           w_up.reshape(N_EXPERTS, D_MODEL, D_EXPERT),
                           w_down.reshape(N_EXPERTS, D_EXPERT, D_MODEL))
    out = _combine(pos.reshape(TOKENS // COMBINE_TILE, 1, 2 * COMBINE_TILE), routing, x1, y_sorted,
                   ln2_g, ln2_b)
    return out.reshape(BATCH, SEQ, D_MODEL)
```

```python
import functools
import math

import numpy as np
import jax
import jax.numpy as jnp
from jax import lax
from jax.experimental import pallas as pl
from jax.experimental.pallas import tpu as pltpu

D_MODEL = 2048
BATCH = 8
SEQ = 2048
TOKENS = BATCH * SEQ
CONV_DIM = 1024
CONV_WIDTH = 3
DIL_PATTERNS = ((128, 1), (512, 4), (2048, 16))
N_DIL_GROUPS = 3
HEADS_PER_GROUP = 4
HEAD_DIM = 128
DIL_DIM = N_DIL_GROUPS * HEADS_PER_GROUP * HEAD_DIM
DIL_OUT_DIM = HEADS_PER_GROUP * HEAD_DIM
ATT_BLOCK = 128
ALIBI_MAX = 8.0
MEM_LEN = 256
MEM_HEADS = 4
MEM_HEAD_DIM = 256
MEM_DIM = MEM_HEADS * MEM_HEAD_DIM
N_BRANCHES = 3
IN_DIM = 3 * CONV_DIM + 3 * DIL_DIM + MEM_DIM + N_BRANCHES * D_MODEL
N_GROUPS = 4
EXPERTS_PER_GROUP = 8
N_EXPERTS = N_GROUPS * EXPERTS_PER_GROUP
D_EXPERT = 512
ALPHA = 2.0 ** 0.25
LN_EPS = 1e-5

OFF_CB = 0
OFF_CC = CONV_DIM
OFF_CH = 2 * CONV_DIM
OFF_Q = 3 * CONV_DIM
OFF_MQ = OFF_Q + 3 * DIL_DIM
OFF_GATE = OFF_MQ + MEM_DIM

LANES = 128
HALF = SEQ // 2
VMEM_LIMIT = 56 * 1024 * 1024

ROUTE_TILE = 1024
EXPERT_TILE = 256
N_EXPERT_TILES = 2 * TOKENS // EXPERT_TILE + N_EXPERTS
COMBINE_TILE = 256
ROUTER_LANE0 = N_GROUPS


def _params(sem, limit=VMEM_LIMIT):
    return pltpu.CompilerParams(dimension_semantics=sem, vmem_limit_bytes=limit)


def _layer_norm(x, g, b):
    mu = jnp.mean(x, axis=-1, keepdims=True)
    xc = x - mu
    var = jnp.mean(xc * xc, axis=-1, keepdims=True)
    return xc * lax.rsqrt(var + LN_EPS) * g + b


def _dot(a, b):
    return jnp.dot(a, b, preferred_element_type=jnp.float32)


def _dot_t(a, b):
    return lax.dot_general(a, b, (((1,), (1,)), ((), ())), preferred_element_type=jnp.float32)


def _memkv_kernel(mem_ref, g_ref, b_ref, w_ref, kv_ref):
    y = _layer_norm(mem_ref[...], g_ref[...], b_ref[...])
    kv_ref[...] = _dot(y, w_ref[...]).astype(kv_ref.dtype)


def _memkv(mem, g, b, w):
    return pl.pallas_call(
        _memkv_kernel,
        grid=(BATCH, 2),
        in_specs=[pl.BlockSpec((None, MEM_LEN, D_MODEL), lambda i, j: (i, 0, 0)),
                  pl.BlockSpec((1, D_MODEL), lambda i, j: (0, 0)),
                  pl.BlockSpec((1, D_MODEL), lambda i, j: (0, 0)),
                  pl.BlockSpec((D_MODEL, MEM_DIM), lambda i, j: (0, j))],
        out_specs=pl.BlockSpec((None, MEM_LEN, MEM_DIM), lambda i, j: (i, 0, j)),
        out_shape=jax.ShapeDtypeStruct((BATCH, MEM_LEN, 2 * MEM_DIM), jnp.bfloat16),
        compiler_params=_params(("parallel", "arbitrary")),
        name="mem_kv",
    )(mem, g, b, w)


CONV_TC = 512


def _conv_kernel(x_ref, wb_ref, wc_ref, wh_ref, bb_ref, bc_ref, bh_ref, wconv_ref, s_ref, xb_ref,
                 u_sc, carry_sc):
    half = pl.program_id(1)
    c = pl.program_id(2)

    @pl.when(c == 0)
    def _():
        xb_ref[...] = x_ref[...].astype(xb_ref.dtype)

    x = xb_ref[...]
    cb = _dot(x, wb_ref[...]) + bb_ref[...]
    cc = _dot(x, wc_ref[...]) + bc_ref[...]
    ch = _dot(x, wh_ref[...]) + bh_ref[...]
    u = cc * ch
    u_sc[0:8, :] = jnp.where(half == 0, 0.0, carry_sc[c])
    u_sc[8:8 + HALF, :] = u
    carry_sc[c] = u[HALF - 8:HALF, :]
    wconv = wconv_ref[...]
    y = (wconv[2:3, :] * u
         + wconv[1:2, :] * u_sc[7:7 + HALF, :]
         + wconv[0:1, :] * u_sc[6:6 + HALF, :])
    s_ref[...] = (cb * y).astype(s_ref.dtype)


def _conv_branch(x, w_in, b_in, w_conv):
    nb = lambda off: off // CONV_TC
    wspec = lambda off: pl.BlockSpec((D_MODEL, CONV_TC), lambda b, h, c, o=nb(off): (0, o + c))
    bspec = lambda off: pl.BlockSpec((1, CONV_TC), lambda b, h, c, o=nb(off): (0, o + c))
    return pl.pallas_call(
        _conv_kernel,
        grid=(BATCH, 2, CONV_DIM // CONV_TC),
        in_specs=[pl.BlockSpec((None, HALF, D_MODEL), lambda b, h, c: (b, h, 0)),
                  wspec(OFF_CB), wspec(OFF_CC), wspec(OFF_CH),
                  bspec(OFF_CB), bspec(OFF_CC), bspec(OFF_CH),
                  pl.BlockSpec((CONV_WIDTH, CONV_TC), lambda b, h, c: (0, c))],
        out_specs=[pl.BlockSpec((None, HALF, CONV_TC), lambda b, h, c: (b, h, c)),
                   pl.BlockSpec((None, HALF, D_MODEL), lambda b, h, c: (b, h, 0))],
        out_shape=[jax.ShapeDtypeStruct((BATCH, SEQ, CONV_DIM), jnp.bfloat16),
                   jax.ShapeDtypeStruct((BATCH, SEQ, D_MODEL), jnp.bfloat16)],
        scratch_shapes=[pltpu.VMEM((HALF + 8, CONV_TC), jnp.float32),
                        pltpu.VMEM((CONV_DIM // CONV_TC, 8, CONV_TC), jnp.float32)],
        compiler_params=_params(("arbitrary", "arbitrary", "arbitrary")),
        name="conv_branch",
    )(x, w_in, w_in, w_in, b_in, b_in, b_in, w_conv)


QKV_TN = 512
QKV_CHUNKS = QKV_TN // LANES


def _qkv_kernel(x_ref, w_ref, b_ref, o_ref, sc_ref):
    x = x_ref[...]
    for gi in (2, 1, 0):
        cols = slice(gi * QKV_TN, (gi + 1) * QKV_TN)
        acc = _dot(x, w_ref[:, cols]) + b_ref[:, cols]
        d = DIL_PATTERNS[gi][1]
        if d == 1:
            o_ref[:, cols] = acc.astype(o_ref.dtype)
            continue
        rows = HALF // d
        for c in range(QKV_CHUNKS):
            sc_ref[gi - 1, c] = acc[:, c * LANES:(c + 1) * LANES]
        for c in range(QKV_CHUNKS):
            for r in range(d):
                lo = gi * QKV_TN + c * LANES
                o_ref[r * rows:(r + 1) * rows, lo:lo + LANES] = (
                    sc_ref[gi - 1, c, pl.ds(r, rows, stride=d), :].astype(o_ref.dtype))


def _qkv_proj(xb, w_in, b_in):
    n0 = OFF_Q // DIL_DIM
    return pl.pallas_call(
        _qkv_kernel,
        grid=(BATCH, 2, 3),
        in_specs=[pl.BlockSpec((None, HALF, D_MODEL), lambda b, h, n: (b, h, 0)),
                  pl.BlockSpec((D_MODEL, DIL_DIM), lambda b, h, n: (0, n0 + n)),
                  pl.BlockSpec((1, DIL_DIM), lambda b, h, n: (0, n0 + n))],
        out_specs=pl.BlockSpec((None, HALF, DIL_DIM), lambda b, h, n: (b, h, n)),
        out_shape=jax.ShapeDtypeStruct((BATCH, SEQ, 3 * DIL_DIM), jnp.bfloat16),
        scratch_shapes=[pltpu.VMEM((N_DIL_GROUPS - 1, QKV_CHUNKS, HALF, LANES), jnp.float32)],
        compiler_params=_params(("parallel", "parallel", "arbitrary")),
        name="qkv_proj",
    )(xb, w_in, b_in)


MEM_HEADS_PER_STEP = 2
MEM_TN = MEM_HEADS_PER_STEP * MEM_HEAD_DIM


def _memattn_kernel(x_ref, w_ref, b_ref, mk_ref, mv_ref, o_ref):
    mq_all = (_dot(x_ref[...], w_ref[...]) + b_ref[...]).astype(jnp.bfloat16)
    for hh in range(MEM_HEADS_PER_STEP):
        cols = slice(hh * MEM_HEAD_DIM, (hh + 1) * MEM_HEAD_DIM)
        s = _dot_t(mq_all[:, cols], mk_ref[:, cols]) * (MEM_HEAD_DIM ** -0.5)
        m = jnp.max(s, axis=-1, keepdims=True)
        p = jnp.exp(s - m)
        den = jnp.sum(p, axis=-1, keepdims=True)
        o = _dot(p.astype(jnp.bfloat16), mv_ref[:, cols]) / den
        o_ref[:, cols] = o.astype(o_ref.dtype)


def _mem_branch(xb, w_in, b_in, kv):
    n0 = OFF_MQ // MEM_TN
    nv = MEM_DIM // MEM_TN
    return pl.pallas_call(
        _memattn_kernel,
        grid=(BATCH, 2, MEM_DIM // MEM_TN),
        in_specs=[pl.BlockSpec((None, HALF, D_MODEL), lambda b, h, n: (b, h, 0)),
                  pl.BlockSpec((D_MODEL, MEM_TN), lambda b, h, n: (0, n0 + n)),
                  pl.BlockSpec((1, MEM_TN), lambda b, h, n: (0, n0 + n)),
                  pl.BlockSpec((None, MEM_LEN, MEM_TN), lambda b, h, n: (b, 0, n)),
                  pl.BlockSpec((None, MEM_LEN, MEM_TN), lambda b, h, n: (b, 0, nv + n))],
        out_specs=pl.BlockSpec((None, HALF, MEM_TN), lambda b, h, n: (b, h, n)),
        out_shape=jax.ShapeDtypeStruct((BATCH, SEQ, MEM_DIM), jnp.bfloat16),
        compiler_params=_params(("parallel", "parallel", "arbitrary")),
        name="mem_branch",
    )(xb, w_in, b_in, kv, kv)


ATT_UNROLL = 16


def _softmax_block(s, v):
    m = jnp.max(s, axis=-1, keepdims=True)
    p = jnp.exp(s - m)
    den = jnp.sum(p, axis=-1, keepdims=True)
    o = _dot(p.astype(jnp.bfloat16), v) / den
    return o, m + jnp.log(den)


def _dilattn_kernel(slopes_ref,
                    q0_ref, q1_ref, q2_ref, k0_ref, k1_ref, k2_ref, v0_ref, v1_ref, v2_ref,
                    o_ref, o_sc, l_sc):
    h = pl.program_id(1)
    blk = ATT_BLOCK
    scale = HEAD_DIM ** -0.5
    qi = lax.broadcasted_iota(jnp.int32, (blk, 2 * blk), 0) + blk
    kj = lax.broadcasted_iota(jnp.int32, (blk, 2 * blk), 1)
    jrel = qi - kj
    valid = (jrel >= 0) & (jrel <= blk)
    jrel_f = jrel.astype(jnp.float32)

    def bias_for(g):
        slope = slopes_ref[g * HEADS_PER_GROUP + h]
        d = float(DIL_PATTERNS[g][1])
        return jnp.where(valid, (-slope * d) * jrel_f, -jnp.inf)

    def put(g, row_slice, o, lse):
        o_sc[g, row_slice, :] = o
        l_sc[g, row_slice, :] = jnp.broadcast_to(lse, (blk, HEAD_DIM))

    def run_blocks(g, blocks):
        scores = [_dot_t(q, k) * scale + bias for q, k, _, bias, _ in blocks]
        stats = []
        for s in scores:
            m = jnp.max(s, axis=-1, keepdims=True)
            p = jnp.exp(s - m)
            stats.append((m, p, jnp.sum(p, axis=-1, keepdims=True)))
        outs = [_dot(p.astype(jnp.bfloat16), blkdef[2]) / den
                for (m, p, den), blkdef in zip(stats, blocks)]
        for o, (m, p, den), blkdef in zip(outs, stats, blocks):
            put(g, blkdef[4], o, m + jnp.log(den))

    bias0 = bias_for(0)
    prev_cols = kj < blk

    def g0_body(it, carry):
        blocks = []
        for k in range(ATT_UNROLL):
            n = it * ATT_UNROLL + k
            q0 = pl.multiple_of(n * blk, blk)
            k0 = pl.multiple_of(jnp.maximum(n - 1, 0) * blk, blk)
            bias = jnp.where(prev_cols & (n == 0), -jnp.inf, bias0)
            blocks.append((q0_ref[pl.ds(q0, blk), :], k0_ref[pl.ds(k0, 2 * blk), :],
                           v0_ref[pl.ds(k0, 2 * blk), :], bias, pl.ds(q0, blk)))
        run_blocks(0, blocks)
        return carry

    lax.fori_loop(0, SEQ // blk // ATT_UNROLL, g0_body, 0)

    d1 = DIL_PATTERNS[1][1]
    cls1 = HALF // d1
    per_half = cls1 // blk
    bias1 = bias_for(1)

    def row1(r, n):
        return (n // per_half) * HALF + r * cls1 + (n % per_half) * blk

    nblk1 = SEQ // d1 // blk
    cls_per_trip = ATT_UNROLL // nblk1

    def g1_body(it, carry):
        blocks = []
        for c in range(cls_per_trip):
            r = it * cls_per_trip + c
            for n in range(nblk1):
                cur = pl.multiple_of(row1(r, n), blk)
                q = q1_ref[pl.ds(cur, blk), :]
                dst = pl.ds(n * blk * d1 + r, blk, stride=d1)
                if n == 0:
                    blocks.append((q, k1_ref[pl.ds(cur, blk), :], v1_ref[pl.ds(cur, blk), :],
                                   bias1[:, blk:], dst))
                else:
                    prev = pl.multiple_of(row1(r, n - 1), blk)
                    kc = jnp.concatenate([k1_ref[pl.ds(prev, blk), :], k1_ref[pl.ds(cur, blk), :]], axis=0)
                    vc = jnp.concatenate([v1_ref[pl.ds(prev, blk), :], v1_ref[pl.ds(cur, blk), :]], axis=0)
                    blocks.append((q, kc, vc, bias1, dst))
        run_blocks(1, blocks)
        return carry

    lax.fori_loop(0, d1 // cls_per_trip, g1_body, 0)

    d2 = DIL_PATTERNS[2][1]
    cls2 = HALF // d2
    bias2 = bias_for(2)

    def g2_body(it, carry):
        blocks = []
        for k in range(ATT_UNROLL):
            r = it * ATT_UNROLL + k
            a = pl.multiple_of(r * cls2, cls2)
            b = pl.multiple_of(HALF + r * cls2, cls2)
            cat = lambda ref, a=a, b=b: jnp.concatenate(
                [ref[pl.ds(a, cls2), :], ref[pl.ds(b, cls2), :]], axis=0)
            blocks.append((cat(q2_ref), cat(k2_ref), cat(v2_ref), bias2[:, blk:],
                           pl.ds(r, blk, stride=d2)))
        run_blocks(2, blocks)
        return carry

    lax.fori_loop(0, d2 // ATT_UNROLL, g2_body, 0)

    rows = 256
    for t in range(SEQ // rows):
        sl = pl.ds(t * rows, rows)
        l0, l1, l2 = l_sc[0, sl, :], l_sc[1, sl, :], l_sc[2, sl, :]
        m = jnp.maximum(jnp.maximum(l0, l1), l2)
        e0, e1, e2 = jnp.exp(l0 - m), jnp.exp(l1 - m), jnp.exp(l2 - m)
        mix = (e0 * o_sc[0, sl, :] + e1 * o_sc[1, sl, :] + e2 * o_sc[2, sl, :]) / (e0 + e1 + e2)
        o_ref[sl, :] = mix.astype(o_ref.dtype)


def _dil_branch(qkv, slopes):
    nq = DIL_DIM // HEAD_DIM

    def spec(section, g):
        return pl.BlockSpec((None, SEQ, HEAD_DIM),
                            lambda b, h, s_ref, o=section * nq + g * HEADS_PER_GROUP: (b, 0, o + h))

    grid_spec = pltpu.PrefetchScalarGridSpec(
        num_scalar_prefetch=1,
        grid=(BATCH, HEADS_PER_GROUP),
        in_specs=[spec(sec, g) for sec in range(3) for g in range(N_DIL_GROUPS)],
        out_specs=pl.BlockSpec((None, SEQ, HEAD_DIM), lambda b, h, s_ref: (b, 0, h)),
        scratch_shapes=[pltpu.VMEM((N_DIL_GROUPS, SEQ, HEAD_DIM), jnp.float32),
                        pltpu.VMEM((N_DIL_GROUPS, SEQ, HEAD_DIM), jnp.float32)],
    )
    return pl.pallas_call(
        _dilattn_kernel,
        grid_spec=grid_spec,
        out_shape=jax.ShapeDtypeStruct((BATCH, SEQ, DIL_OUT_DIM), jnp.bfloat16),
        compiler_params=_params(("parallel", "arbitrary")),
        name="dil_attn",
    )(slopes, *([qkv] * 9))


MERGE_TM = 1024
MERGE_TN = 512


def _merge_kernel(x_ref, wg0_ref, wg1_ref, wg2_ref, bg0_ref, bg1_ref, bg2_ref,
                  sc_ref, od_ref, om_ref, wco_ref, wdo_ref, wmo_ref, o_ref):
    x = x_ref[...]
    g0 = jax.nn.sigmoid(_dot(x, wg0_ref[...]) + bg0_ref[...])
    acc = g0 * _dot(sc_ref[...], wco_ref[...])
    g1 = jax.nn.sigmoid(_dot(x, wg1_ref[...]) + bg1_ref[...])
    acc = acc + g1 * _dot(od_ref[...], wdo_ref[...])
    g2 = jax.nn.sigmoid(_dot(x, wg2_ref[...]) + bg2_ref[...])
    acc = acc + g2 * _dot(om_ref[...], wmo_ref[...])
    o_ref[...] = acc.astype(o_ref.dtype)


def _merge(x2, w_in, b_in, s_conv, o_dil, o_mem, w_co, w_do, w_mo):
    nb = lambda br: (OFF_GATE + br * D_MODEL) // MERGE_TN
    gspec = lambda br: pl.BlockSpec((D_MODEL, MERGE_TN), lambda i, n, o=nb(br): (0, o + n))
    bspec = lambda br: pl.BlockSpec((1, MERGE_TN), lambda i, n, o=nb(br): (0, o + n))
    act = lambda width: pl.BlockSpec((MERGE_TM, width), lambda i, n: (i, 0))
    wout = lambda width: pl.BlockSpec((width, MERGE_TN), lambda i, n: (0, n))
    return pl.pallas_call(
        _merge_kernel,
        grid=(TOKENS // MERGE_TM, D_MODEL // MERGE_TN),
        in_specs=[act(D_MODEL), gspec(0), gspec(1), gspec(2), bspec(0), bspec(1), bspec(2),
                  act(CONV_DIM), act(DIL_OUT_DIM), act(MEM_DIM),
                  wout(CONV_DIM), wout(DIL_OUT_DIM), wout(MEM_DIM)],
        out_specs=pl.BlockSpec((MERGE_TM, MERGE_TN), lambda i, n: (i, n)),
        out_shape=jax.ShapeDtypeStruct((TOKENS, D_MODEL), jnp.bfloat16),
        compiler_params=_params(("parallel", "arbitrary")),
        name="gated_merge",
    )(x2, w_in, w_in, w_in, b_in, b_in, b_in, s_conv, o_dil, o_mem, w_co, w_do, w_mo)


OPROJ_TM = 512


def _oproj_kernel(x_ref, m_ref, wo_ref, g_ref, b_ref, wr_ref, br_ref, x1_ref, logit_ref):
    hres = ALPHA * x_ref[...] + _dot(m_ref[...], wo_ref[...])
    x1 = _layer_norm(hres, g_ref[...], b_ref[...])
    x1_ref[...] = x1
    logit_ref[...] = _dot(x1, wr_ref[...]) + br_ref[...]


def _oproj(x2, merged, w_o, g, b, w_route, b_route):
    row = lambda width: pl.BlockSpec((OPROJ_TM, width), lambda i: (i, 0))
    full = lambda r, c: pl.BlockSpec((r, c), lambda i: (0, 0))
    return pl.pallas_call(
        _oproj_kernel,
        grid=(TOKENS // OPROJ_TM,),
        in_specs=[row(D_MODEL), row(D_MODEL), full(D_MODEL, D_MODEL), full(1, D_MODEL), full(1, D_MODEL),
                  full(D_MODEL, LANES), full(1, LANES)],
        out_specs=[row(D_MODEL), row(LANES)],
        out_shape=[jax.ShapeDtypeStruct((TOKENS, D_MODEL), jnp.float32),
                   jax.ShapeDtypeStruct((TOKENS, LANES), jnp.float32)],
        compiler_params=_params(("parallel",)),
        name="oproj_ln1",
    )(x2, merged, w_o, g, b, w_route, b_route)


def _route_kernel(logit_ref, o_ref, cnt_ref, carry_sc):
    @pl.when(pl.program_id(0) == 0)
    def _():
        carry_sc[...] = jnp.zeros_like(carry_sc)

    tm = ROUTE_TILE
    z = logit_ref[...]
    lane = lax.broadcasted_iota(jnp.int32, (tm, LANES), 1)
    neg = -jnp.inf
    first = lambda hit: jnp.min(jnp.where(hit, lane, LANES), axis=-1, keepdims=True)

    glog = jnp.where(lane < N_GROUPS, z, neg)
    gmax = jnp.max(glog, axis=-1, keepdims=True)
    gsel = first(glog == gmax)
    gw = 1.0 / jnp.sum(jnp.exp(glog - gmax), axis=-1, keepdims=True)

    e_lane = lane - ROUTER_LANE0
    in_group = (e_lane >= 0) & (e_lane < N_EXPERTS) & ((e_lane // EXPERTS_PER_GROUP) == gsel)
    v = jnp.where(in_group, z, neg)
    top1 = jnp.max(v, axis=-1, keepdims=True)
    i1 = first(v == top1)
    v2 = jnp.where(lane == i1, neg, v)
    top2 = jnp.max(v2, axis=-1, keepdims=True)
    i2 = first(v2 == top2)
    t = jnp.exp(top2 - top1)
    w1 = gw / (1.0 + t)
    w2 = w1 * t

    onehot = ((lane == i1) | (lane == i2))
    ri = lax.broadcasted_iota(jnp.int32, (tm, tm), 0)
    ci = lax.broadcasted_iota(jnp.int32, (tm, tm), 1)
    tri = jnp.where(ci <= ri, 1.0, 0.0).astype(jnp.bfloat16)
    cum = _dot(tri, jnp.where(onehot, 1.0, 0.0).astype(jnp.bfloat16))
    carry = carry_sc[0:1, :]
    before = cum + carry - 1.0
    rank1 = jnp.sum(jnp.where(lane == i1, before, 0.0), axis=-1, keepdims=True)
    rank2 = jnp.sum(jnp.where(lane == i2, before, 0.0), axis=-1, keepdims=True)
    new_carry = carry + cum[tm - 1:tm, :]
    carry_sc[...] = jnp.broadcast_to(new_carry, carry_sc.shape)
    cnt_ref[...] = jnp.broadcast_to(new_carry, cnt_ref.shape)

    cols = [(i1 - ROUTER_LANE0).astype(jnp.float32), (i2 - ROUTER_LANE0).astype(jnp.float32),
            rank1, rank2, w1, w2]
    out = jnp.zeros((tm, LANES), jnp.float32)
    for k, col in enumerate(cols):
        out = jnp.where(lane == k, col, out)
    o_ref[...] = out


def _route(logits):
    return pl.pallas_call(
        _route_kernel,
        grid=(TOKENS // ROUTE_TILE,),
        in_specs=[pl.BlockSpec((ROUTE_TILE, LANES), lambda i: (i, 0))],
        out_specs=[pl.BlockSpec((ROUTE_TILE, LANES), lambda i: (i, 0)),
                   pl.BlockSpec((8, LANES), lambda i: (0, 0))],
        out_shape=[jax.ShapeDtypeStruct((TOKENS, LANES), jnp.float32),
                   jax.ShapeDtypeStruct((8, LANES), jnp.float32)],
        scratch_shapes=[pltpu.VMEM((8, LANES), jnp.float32)],
        compiler_params=_params(("arbitrary",)),
        name="route",
    )(logits)


def _slots_kernel(route_ref, cnt_ref, pos_ref):
    lane8 = lax.broadcasted_iota(jnp.int32, (8, LANES), 1)
    cnt = cnt_ref[...]
    padded = jnp.ceil(cnt * (1.0 / EXPERT_TILE)) * EXPERT_TILE
    incl = padded
    for sh in (1, 2, 4, 8, 16, 32, 64):
        incl = incl + jnp.where(lane8 >= sh, pltpu.roll(incl, sh, axis=1), 0.0)
    start = (incl - padded)[0:1, :]

    r = route_ref[...]
    lane = lax.broadcasted_iota(jnp.int32, r.shape, 1).astype(jnp.float32)
    pick = lambda e: jnp.sum(jnp.where(lane == e + float(ROUTER_LANE0), start, 0.0), axis=-1, keepdims=True)
    pos_a = pick(r[:, 0:1]) + r[:, 2:3]
    pos_b = pick(r[:, 1:2]) + r[:, 3:4]
    pos_ref[...] = jnp.where(lane == 0.0, pos_a, jnp.where(lane == 1.0, pos_b, 0.0)).astype(jnp.int32)


def _slots(routing, counts8):
    return pl.pallas_call(
        _slots_kernel,
        grid=(TOKENS // ROUTE_TILE,),
        in_specs=[pl.BlockSpec((ROUTE_TILE, LANES), lambda i: (i, 0)),
                  pl.BlockSpec((8, LANES), lambda i: (0, 0))],
        out_specs=pl.BlockSpec((ROUTE_TILE, LANES), lambda i: (i, 0)),
        out_shape=jax.ShapeDtypeStruct((TOKENS, LANES), jnp.int32),
        compiler_params=_params(("parallel",)),
        name="slots",
    )(routing, counts8)


SUBLANES = 8
DISPATCH_TILE = 512
SORTED_ROWS = N_EXPERT_TILES * EXPERT_TILE


PACKED = D_MODEL // 2


def _pack_rows(v):
    return pltpu.pack_elementwise([v[:, :PACKED], v[:, PACKED:]], packed_dtype=jnp.bfloat16)


def _unpack_rows(w):
    half = lambda i: pltpu.unpack_elementwise(w, index=i, packed_dtype=jnp.bfloat16,
                                              unpacked_dtype=jnp.float32)
    return half(0), half(1)


def _dispatch_kernel(pad_start_ref, pad_n_ref, nvalid_ref, pos_ref, x_ref, xs_hbm, zero_sc, pk_sc, sem,
                     row_sem):
    groups = DISPATCH_TILE // SUBLANES

    @pl.when(pl.program_id(0) == 0)
    def _():
        zero_sc[...] = jnp.zeros_like(zero_sc)

        def tile_copy(t):
            return pltpu.make_async_copy(zero_sc, xs_hbm.at[pl.ds(pl.multiple_of(t * EXPERT_TILE, EXPERT_TILE),
                                                                  EXPERT_TILE)], sem.at[2])

        def start_tile(t, c):
            tile_copy(t).start()
            return c

        def wait_tile(t, c):
            tile_copy(t).wait()
            return c
        lax.fori_loop(nvalid_ref[0], N_EXPERT_TILES, start_tile, 0)
        lax.fori_loop(nvalid_ref[0], N_EXPERT_TILES, wait_tile, 0)

        def per_expert(e, total):
            n = pad_n_ref[e]

            def per_row(i, c):
                pltpu.make_async_copy(zero_sc.at[pl.ds(0, 1)], xs_hbm.at[pl.ds(pad_start_ref[e] + i, 1)],
                                      sem.at[1]).start()
                return c
            lax.fori_loop(0, n, per_row, 0)
            return total + n
        total = lax.fori_loop(0, N_EXPERTS, per_expert, 0)

        def wait_row(i, c):
            pltpu.make_async_copy(zero_sc.at[pl.ds(0, 1)], xs_hbm.at[pl.ds(0, 1)], sem.at[1]).wait()
            return c
        lax.fori_loop(0, total, wait_row, 0)

    pk_sc[...] = _pack_rows(x_ref[...]).reshape(groups, SUBLANES, PACKED)

    def body(gi, c):
        for s in range(SUBLANES):
            for k in range(2):
                p = pos_ref[0, 0, (gi * SUBLANES + s) * 2 + k]
                pltpu.make_async_copy(pk_sc.at[gi, pl.ds(s, 1)], xs_hbm.at[pl.ds(p, 1)],
                                      row_sem.at[s]).start(priority=k)
        return c
    lax.fori_loop(0, groups, body, 0)
    for s in range(SUBLANES):
        for _ in range(2 * groups // SUBLANES):
            pltpu.make_async_copy(pk_sc.at[0], xs_hbm.at[pl.ds(0, SUBLANES)], row_sem.at[s]).wait()


def _dispatch(pad_start, pad_n, n_valid_tiles, pos3, x1):
    nt = TOKENS // DISPATCH_TILE
    groups = DISPATCH_TILE // SUBLANES
    grid_spec = pltpu.PrefetchScalarGridSpec(
        num_scalar_prefetch=3,
        grid=(nt,),
        in_specs=[pl.BlockSpec((1, 1, 2 * DISPATCH_TILE), lambda j, *_: (j, 0, 0), memory_space=pltpu.SMEM),
                  pl.BlockSpec((DISPATCH_TILE, D_MODEL), lambda j, *_: (j, 0))],
        out_specs=pl.BlockSpec(memory_space=pl.ANY),
        scratch_shapes=[pltpu.VMEM((EXPERT_TILE, PACKED), jnp.int32),
                        pltpu.VMEM((groups, SUBLANES, PACKED), jnp.int32),
                        pltpu.SemaphoreType.DMA((3,)),
                        pltpu.SemaphoreType.DMA((SUBLANES,))],
    )
    return pl.pallas_call(
        _dispatch_kernel,
        grid_spec=grid_spec,
        out_shape=jax.ShapeDtypeStruct((SORTED_ROWS, PACKED), jnp.int32),
        compiler_params=_params(("arbitrary",)),
        name="dispatch",
    )(pad_start, pad_n, n_valid_tiles, pos3, x1)


def _ffn_kernel(texp_ref, tvalid_ref, tslot_ref, tfirst_ref, tnext_ref, nvalid_ref,
                xs_ref, wg_hbm, wu_hbm, wd_hbm, y_ref, wg_buf, wu_buf, wd_buf, sem):
    j = pl.program_id(0)
    valid = tvalid_ref[j] == 1
    slot = tslot_ref[j]

    def weight_copies(e, s):
        return (pltpu.make_async_copy(wg_hbm.at[e], wg_buf.at[s], sem.at[s, 0]),
                pltpu.make_async_copy(wu_hbm.at[e], wu_buf.at[s], sem.at[s, 1]),
                pltpu.make_async_copy(wd_hbm.at[e], wd_buf.at[s], sem.at[s, 2]))

    @pl.when(valid & (tfirst_ref[j] == 1))
    def _():
        @pl.when(j == 0)
        def _():
            for cp in weight_copies(texp_ref[j], slot):
                cp.start()

        for cp in weight_copies(texp_ref[j], slot):
            cp.wait()

        @pl.when(tnext_ref[j] >= 0)
        def _():
            for cp in weight_copies(tnext_ref[j], 1 - slot):
                cp.start()

    @pl.when(valid)
    def _():
        lo, hi = _unpack_rows(xs_ref[...])
        a = _dot(lo, wg_buf[slot, 0:PACKED, :]) + _dot(hi, wg_buf[slot, PACKED:D_MODEL, :])
        u = _dot(lo, wu_buf[slot, 0:PACKED, :]) + _dot(hi, wu_buf[slot, PACKED:D_MODEL, :])
        hid = (a * jax.nn.sigmoid(a)) * u
        y_ref[...] = _pack_rows(_dot(hid, wd_buf[slot]))

    @pl.when(jnp.logical_not(valid))
    def _():
        y_ref[...] = jnp.zeros_like(y_ref)


def _expert_ffn(tables, xs, w_gate, w_up, w_down):
    nt = N_EXPERT_TILES
    xmap = lambda j, te, tv, ts, tf, tn, nv: (jnp.minimum(j, nv[0] - 1), 0)
    grid_spec = pltpu.PrefetchScalarGridSpec(
        num_scalar_prefetch=6,
        grid=(nt,),
        in_specs=[pl.BlockSpec((EXPERT_TILE, PACKED), xmap),
                  pl.BlockSpec(memory_space=pl.ANY),
                  pl.BlockSpec(memory_space=pl.ANY),
                  pl.BlockSpec(memory_space=pl.ANY)],
        out_specs=pl.BlockSpec((EXPERT_TILE, PACKED), lambda j, *_: (j, 0)),
        scratch_shapes=[pltpu.VMEM((2, D_MODEL, D_EXPERT), jnp.float32),
                        pltpu.VMEM((2, D_MODEL, D_EXPERT), jnp.float32),
                        pltpu.VMEM((2, D_EXPERT, D_MODEL), jnp.float32),
                        pltpu.SemaphoreType.DMA((2, 3))],
    )
    return pl.pallas_call(
        _ffn_kernel,
        grid_spec=grid_spec,
        out_shape=jax.ShapeDtypeStruct((nt * EXPERT_TILE, PACKED), jnp.int32),
        compiler_params=_params(("arbitrary",)),
        name="expert_ffn",
    )(*tables, xs, w_gate, w_up, w_down)


def _combine_kernel(pos_cur_ref, pos_next_ref, r_ref, x1_ref, y_hbm, g_ref, b_ref, o_ref, ybuf, sem):
    j = pl.program_id(0)
    nt = pl.num_programs(0)
    slot = lax.rem(j, 2)
    groups = COMBINE_TILE // SUBLANES

    def gather(pos_ref, dst_slot):
        def body(gi, c):
            for s in range(SUBLANES):
                for k in range(2):
                    p = pos_ref[0, 0, (gi * SUBLANES + s) * 2 + k]
                    pltpu.make_async_copy(y_hbm.at[pl.ds(p, 1)], ybuf.at[dst_slot, k, gi, pl.ds(s, 1)],
                                          sem.at[dst_slot, s]).start(priority=k)
            return c
        lax.fori_loop(0, groups, body, 0)

    @pl.when(j == 0)
    def _():
        gather(pos_cur_ref, 0)

    @pl.when(j + 1 < nt)
    def _():
        gather(pos_next_ref, 1 - slot)

    for s in range(SUBLANES):
        for _ in range(2 * groups // SUBLANES):
            pltpu.make_async_copy(y_hbm.at[pl.ds(0, SUBLANES)], ybuf.at[slot, 0, 0], sem.at[slot, s]).wait()
    r = r_ref[...]
    unpack = lambda k: jnp.concatenate(_unpack_rows(ybuf[slot, k].reshape(COMBINE_TILE, PACKED)), axis=1)
    ya, yb = unpack(0), unpack(1)
    hres = ALPHA * x1_ref[...] + r[:, 4:5] * ya + r[:, 5:6] * yb
    o_ref[...] = _layer_norm(hres, g_ref[...], b_ref[...])


def _combine(pos3, routing, x1, y_sorted, g, b):
    nt = TOKENS // COMBINE_TILE
    groups = COMBINE_TILE // SUBLANES
    return pl.pallas_call(
        _combine_kernel,
        grid=(nt,),
        in_specs=[pl.BlockSpec((1, 1, 2 * COMBINE_TILE), lambda j: (j, 0, 0), memory_space=pltpu.SMEM),
                  pl.BlockSpec((1, 1, 2 * COMBINE_TILE), lambda j: (jnp.minimum(j + 1, nt - 1), 0, 0),
                               memory_space=pltpu.SMEM),
                  pl.BlockSpec((COMBINE_TILE, LANES), lambda j: (j, 0)),
                  pl.BlockSpec((COMBINE_TILE, D_MODEL), lambda j: (j, 0)),
                  pl.BlockSpec(memory_space=pl.ANY),
                  pl.BlockSpec((1, D_MODEL), lambda j: (0, 0)),
                  pl.BlockSpec((1, D_MODEL), lambda j: (0, 0))],
        out_specs=pl.BlockSpec((COMBINE_TILE, D_MODEL), lambda j: (j, 0)),
        out_shape=jax.ShapeDtypeStruct((TOKENS, D_MODEL), jnp.float32),
        scratch_shapes=[pltpu.VMEM((2, 2, groups, SUBLANES, PACKED), jnp.int32),
                        pltpu.SemaphoreType.DMA((2, SUBLANES))],
        compiler_params=_params(("arbitrary",)),
        name="combine_ln2",
    )(pos3, pos3, routing, x1, y_sorted, g, b)


def _alibi_slopes():
    n = N_DIL_GROUPS * HEADS_PER_GROUP
    return jnp.asarray(2.0 ** (-ALIBI_MAX * np.arange(1, n + 1, dtype=np.float32) / n), jnp.float32)


def kernel(x, mem, ln_mem_g, ln_mem_b, w_in, b_in, w_conv, w_conv_out, w_dil_out, w_mem_kv, w_mem_out, w_o, ln1_g, ln1_b, w_group, b_group, w_router, b_router, w_gate, w_up, w_down, ln2_g, ln2_b):
    assert x.shape == (BATCH, SEQ, D_MODEL) and w_in.shape == (1, D_MODEL, IN_DIM)
    bf16 = jnp.bfloat16
    row = lambda v: v.reshape(1, -1)
    w_in2 = w_in[0].astype(bf16)
    b_in2 = b_in

    kv = _memkv(mem, row(ln_mem_g), row(ln_mem_b), w_mem_kv[0])
    s_conv, xb = _conv_branch(x, w_in2, b_in2, w_conv[0])
    qkv = _qkv_proj(xb, w_in2, b_in2)
    o_mem = _mem_branch(xb, w_in2, b_in2, kv)
    o_dil = _dil_branch(qkv, _alibi_slopes())

    x2 = x.reshape(TOKENS, D_MODEL)
    merged = _merge(xb.reshape(TOKENS, D_MODEL), w_in2, b_in2,
                    s_conv.reshape(TOKENS, CONV_DIM), o_dil.reshape(TOKENS, DIL_OUT_DIM),
                    o_mem.reshape(TOKENS, MEM_DIM),
                    w_conv_out[0].astype(bf16), w_dil_out[0].astype(bf16), w_mem_out[0].astype(bf16))

    w_route = jnp.concatenate(
        [w_group[0], jnp.transpose(w_router[0], (1, 0, 2)).reshape(D_MODEL, N_EXPERTS),
         jnp.zeros((D_MODEL, LANES - N_GROUPS - N_EXPERTS), jnp.float32)], axis=1)
    b_route = jnp.concatenate(
        [b_group[0], b_router[0].reshape(N_EXPERTS),
         jnp.zeros((LANES - N_GROUPS - N_EXPERTS,), jnp.float32)]).reshape(1, LANES)
    x1, logits = _oproj(x2, merged, w_o[0].astype(bf16), ln1_g, ln1_b, w_route, b_route)

    routing, counts8 = _route(logits)
    pos = _slots(routing, counts8)[:, 0:2]

    i32 = jnp.int32
    counts = counts8[0, ROUTER_LANE0:ROUTER_LANE0 + N_EXPERTS].astype(i32)
    padded = ((counts + EXPERT_TILE - 1) // EXPERT_TILE) * EXPERT_TILE
    ends = jnp.cumsum(padded)
    starts = ends - padded
    tile_start = jnp.arange(N_EXPERT_TILES, dtype=i32) * EXPERT_TILE
    tile_expert = jnp.minimum(jnp.sum((ends[None, :] <= tile_start[:, None]).astype(i32), axis=1),
                              N_EXPERTS - 1)
    tile_valid = tile_start < ends[-1]
    prev_expert = jnp.concatenate([jnp.full((1,), -1, i32), tile_expert[:-1]])
    tile_first = tile_valid & (tile_expert != prev_expert)
    tile_slot = (jnp.cumsum(tile_first.astype(i32)) - 1) & 1
    big = N_EXPERTS
    idx = jnp.where(counts > 0, jnp.arange(N_EXPERTS, dtype=i32), big)
    later = jnp.concatenate([lax.cummin(idx[::-1])[::-1][1:], jnp.full((1,), big, i32)])
    next_used = jnp.where(later == big, -1, later)
    n_valid_tiles = (ends[-1:] // EXPERT_TILE).astype(i32)
    tables = (tile_expert, tile_valid.astype(i32), tile_slot, tile_first.astype(i32),
              next_used[tile_expert], n_valid_tiles)

    xs = _dispatch(starts + counts, padded - counts, n_valid_tiles,
                   pos.reshape(TOKENS // DISPATCH_TILE, 1, 2 * DISPATCH_TILE), x1)
    y_sorted = _expert_ffn(tables, xs,
                           w_gate.reshape(N_EXPERTS, D_MODEL, D_EXPERT),
                           w_up.reshape(N_EXPERTS, D_MODEL, D_EXPERT),
                           w_down.reshape(N_EXPERTS, D_EXPERT, D_MODEL))
    out = _combine(pos.reshape(TOKENS // COMBINE_TILE, 1, 2 * COMBINE_TILE), routing, x1, y_sorted,
                   ln2_g, ln2_b)
    return out.reshape(BATCH, SEQ, D_MODEL)
```

```python
import functools
import math

import numpy as np
import jax
import jax.numpy as jnp
from jax import lax
from jax.experimental import pallas as pl
from jax.experimental.pallas import tpu as pltpu

D_MODEL = 2048
BATCH = 8
SEQ = 2048
TOKENS = BATCH * SEQ
CONV_DIM = 1024
CONV_WIDTH = 3
DIL_PATTERNS = ((128, 1), (512, 4), (2048, 16))
N_DIL_GROUPS = 3
HEADS_PER_GROUP = 4
HEAD_DIM = 128
DIL_DIM = N_DIL_GROUPS * HEADS_PER_GROUP * HEAD_DIM
DIL_OUT_DIM = HEADS_PER_GROUP * HEAD_DIM
ATT_BLOCK = 128
ALIBI_MAX = 8.0
MEM_LEN = 256
MEM_HEADS = 4
MEM_HEAD_DIM = 256
MEM_DIM = MEM_HEADS * MEM_HEAD_DIM
N_BRANCHES = 3
IN_DIM = 3 * CONV_DIM + 3 * DIL_DIM + MEM_DIM + N_BRANCHES * D_MODEL
N_GROUPS = 4
EXPERTS_PER_GROUP = 8
N_EXPERTS = N_GROUPS * EXPERTS_PER_GROUP
D_EXPERT = 512
ALPHA = 2.0 ** 0.25
LN_EPS = 1e-5

OFF_CB = 0
OFF_CC = CONV_DIM
OFF_CH = 2 * CONV_DIM
OFF_Q = 3 * CONV_DIM
OFF_MQ = OFF_Q + 3 * DIL_DIM
OFF_GATE = OFF_MQ + MEM_DIM

LANES = 128
HALF = SEQ // 2
VMEM_LIMIT = 56 * 1024 * 1024

ROUTE_TILE = 1024
EXPERT_TILE = 256
N_EXPERT_TILES = 2 * TOKENS // EXPERT_TILE + N_EXPERTS
COMBINE_TILE = 256
ROUTER_LANE0 = N_GROUPS


def _params(sem, limit=VMEM_LIMIT):
    return pltpu.CompilerParams(dimension_semantics=sem, vmem_limit_bytes=limit)


def _layer_norm(x, g, b):
    mu = jnp.mean(x, axis=-1, keepdims=True)
    xc = x - mu
    var = jnp.mean(xc * xc, axis=-1, keepdims=True)
    return xc * lax.rsqrt(var + LN_EPS) * g + b


def _dot(a, b):
    return jnp.dot(a, b, preferred_element_type=jnp.float32)


def _dot_t(a, b):
    return lax.dot_general(a, b, (((1,), (1,)), ((), ())), preferred_element_type=jnp.float32)


MEMKV_BATCHES = 2


def _memkv_kernel(mem_ref, g_ref, b_ref, w_ref, kv_ref):
    rows = MEMKV_BATCHES * MEM_LEN
    y = _layer_norm(mem_ref[...].reshape(rows, D_MODEL), g_ref[...], b_ref[...])
    kv = _dot(y.astype(jnp.bfloat16), w_ref[...])
    kv_ref[...] = kv.astype(kv_ref.dtype).reshape(MEMKV_BATCHES, MEM_LEN, 2 * MEM_DIM)


def _memkv(mem, g, b, w):
    return pl.pallas_call(
        _memkv_kernel,
        grid=(BATCH // MEMKV_BATCHES,),
        in_specs=[pl.BlockSpec((MEMKV_BATCHES, MEM_LEN, D_MODEL), lambda i: (i, 0, 0)),
                  pl.BlockSpec((1, D_MODEL), lambda i: (0, 0)),
                  pl.BlockSpec((1, D_MODEL), lambda i: (0, 0)),
                  pl.BlockSpec((D_MODEL, 2 * MEM_DIM), lambda i: (0, 0))],
        out_specs=pl.BlockSpec((MEMKV_BATCHES, MEM_LEN, 2 * MEM_DIM), lambda i: (i, 0, 0)),
        out_shape=jax.ShapeDtypeStruct((BATCH, MEM_LEN, 2 * MEM_DIM), jnp.bfloat16),
        compiler_params=_params(("parallel",)),
        name="mem_kv",
    )(mem, g, b, w)


CONV_TC = 512


def _conv_kernel(x_ref, wb_ref, wc_ref, wh_ref, bb_ref, bc_ref, bh_ref, wconv_ref, s_ref, xb_ref,
                 u_sc, carry_sc):
    half = pl.program_id(1)
    c = pl.program_id(2)

    @pl.when(c == 0)
    def _():
        xb_ref[...] = x_ref[...].astype(xb_ref.dtype)

    x = xb_ref[...]
    cb = _dot(x, wb_ref[...]) + bb_ref[...]
    cc = _dot(x, wc_ref[...]) + bc_ref[...]
    ch = _dot(x, wh_ref[...]) + bh_ref[...]
    u = cc * ch
    u_sc[0:8, :] = jnp.where(half == 0, 0.0, carry_sc[c])
    u_sc[8:8 + HALF, :] = u
    carry_sc[c] = u[HALF - 8:HALF, :]
    wconv = wconv_ref[...]
    y = (wconv[2:3, :] * u
         + wconv[1:2, :] * u_sc[7:7 + HALF, :]
         + wconv[0:1, :] * u_sc[6:6 + HALF, :])
    s_ref[...] = (cb * y).astype(s_ref.dtype)


def _conv_branch(x, w_in, b_in, w_conv):
    nb = lambda off: off // CONV_TC
    wspec = lambda off: pl.BlockSpec((D_MODEL, CONV_TC), lambda b, h, c, o=nb(off): (0, o + c))
    bspec = lambda off: pl.BlockSpec((1, CONV_TC), lambda b, h, c, o=nb(off): (0, o + c))
    return pl.pallas_call(
        _conv_kernel,
        grid=(BATCH, 2, CONV_DIM // CONV_TC),
        in_specs=[pl.BlockSpec((None, HALF, D_MODEL), lambda b, h, c: (b, h, 0)),
                  wspec(OFF_CB), wspec(OFF_CC), wspec(OFF_CH),
                  bspec(OFF_CB), bspec(OFF_CC), bspec(OFF_CH),
                  pl.BlockSpec((CONV_WIDTH, CONV_TC), lambda b, h, c: (0, c))],
        out_specs=[pl.BlockSpec((None, HALF, CONV_TC), lambda b, h, c: (b, h, c)),
                   pl.BlockSpec((None, HALF, D_MODEL), lambda b, h, c: (b, h, 0))],
        out_shape=[jax.ShapeDtypeStruct((BATCH, SEQ, CONV_DIM), jnp.bfloat16),
                   jax.ShapeDtypeStruct((BATCH, SEQ, D_MODEL), jnp.bfloat16)],
        scratch_shapes=[pltpu.VMEM((HALF + 8, CONV_TC), jnp.float32),
                        pltpu.VMEM((CONV_DIM // CONV_TC, 8, CONV_TC), jnp.float32)],
        compiler_params=_params(("arbitrary", "arbitrary", "arbitrary")),
        name="conv_branch",
    )(x, w_in, w_in, w_in, b_in, b_in, b_in, w_conv)


QKV_TN = 512
QKV_CHUNKS = QKV_TN // LANES


def _qkv_kernel(x_ref, w_ref, b_ref, o_ref, sc_ref):
    x = x_ref[...]
    for gi in (2, 1, 0):
        cols = slice(gi * QKV_TN, (gi + 1) * QKV_TN)
        acc = _dot(x, w_ref[:, cols]) + b_ref[:, cols]
        d = DIL_PATTERNS[gi][1]
        if d == 1:
            o_ref[:, cols] = acc.astype(o_ref.dtype)
            continue
        rows = HALF // d
        for c in range(QKV_CHUNKS):
            sc_ref[gi - 1, c] = acc[:, c * LANES:(c + 1) * LANES]
        for c in range(QKV_CHUNKS):
            for r in range(d):
                lo = gi * QKV_TN + c * LANES
                o_ref[r * rows:(r + 1) * rows, lo:lo + LANES] = (
                    sc_ref[gi - 1, c, pl.ds(r, rows, stride=d), :].astype(o_ref.dtype))


def _qkv_proj(xb, w_in, b_in):
    n0 = OFF_Q // DIL_DIM
    return pl.pallas_call(
        _qkv_kernel,
        grid=(BATCH, 2, 3),
        in_specs=[pl.BlockSpec((None, HALF, D_MODEL), lambda b, h, n: (b, h, 0)),
                  pl.BlockSpec((D_MODEL, DIL_DIM), lambda b, h, n: (0, n0 + n)),
                  pl.BlockSpec((1, DIL_DIM), lambda b, h, n: (0, n0 + n))],
        out_specs=pl.BlockSpec((None, HALF, DIL_DIM), lambda b, h, n: (b, h, n)),
        out_shape=jax.ShapeDtypeStruct((BATCH, SEQ, 3 * DIL_DIM), jnp.bfloat16),
        scratch_shapes=[pltpu.VMEM((N_DIL_GROUPS - 1, QKV_CHUNKS, HALF, LANES), jnp.float32)],
        compiler_params=_params(("parallel", "parallel", "arbitrary")),
        name="qkv_proj",
    )(xb, w_in, b_in)


MEM_HEADS_PER_STEP = 2
MEM_TN = MEM_HEADS_PER_STEP * MEM_HEAD_DIM


def _memattn_kernel(x_ref, w_ref, b_ref, mk_ref, mv_ref, o_ref):
    mq_all = (_dot(x_ref[...], w_ref[...]) + b_ref[...]).astype(jnp.bfloat16)
    for hh in range(MEM_HEADS_PER_STEP):
        cols = slice(hh * MEM_HEAD_DIM, (hh + 1) * MEM_HEAD_DIM)
        s = _dot_t(mq_all[:, cols], mk_ref[:, cols]) * (MEM_HEAD_DIM ** -0.5)
        m = jnp.max(s, axis=-1, keepdims=True)
        p = jnp.exp(s - m)
        den = jnp.sum(p, axis=-1, keepdims=True)
        o = _dot(p.astype(jnp.bfloat16), mv_ref[:, cols]) / den
        o_ref[:, cols] = o.astype(o_ref.dtype)


def _mem_branch(xb, w_in, b_in, kv):
    n0 = OFF_MQ // MEM_TN
    nv = MEM_DIM // MEM_TN
    return pl.pallas_call(
        _memattn_kernel,
        grid=(BATCH, 2, MEM_DIM // MEM_TN),
        in_specs=[pl.BlockSpec((None, HALF, D_MODEL), lambda b, h, n: (b, h, 0)),
                  pl.BlockSpec((D_MODEL, MEM_TN), lambda b, h, n: (0, n0 + n)),
                  pl.BlockSpec((1, MEM_TN), lambda b, h, n: (0, n0 + n)),
                  pl.BlockSpec((None, MEM_LEN, MEM_TN), lambda b, h, n: (b, 0, n)),
                  pl.BlockSpec((None, MEM_LEN, MEM_TN), lambda b, h, n: (b, 0, nv + n))],
        out_specs=pl.BlockSpec((None, HALF, MEM_TN), lambda b, h, n: (b, h, n)),
        out_shape=jax.ShapeDtypeStruct((BATCH, SEQ, MEM_DIM), jnp.bfloat16),
        compiler_params=_params(("parallel", "parallel", "arbitrary")),
        name="mem_branch",
    )(xb, w_in, b_in, kv, kv)


ATT_UNROLL = 16


def _softmax_block(s, v):
    m = jnp.max(s, axis=-1, keepdims=True)
    p = jnp.exp(s - m)
    den = jnp.sum(p, axis=-1, keepdims=True)
    o = _dot(p.astype(jnp.bfloat16), v) / den
    return o, m + jnp.log(den)


def _dilattn_kernel(slopes_ref,
                    q0_ref, q1_ref, q2_ref, k0_ref, k1_ref, k2_ref, v0_ref, v1_ref, v2_ref,
                    o_ref, o_sc, l_sc):
    h = pl.program_id(1)
    blk = ATT_BLOCK
    scale = HEAD_DIM ** -0.5
    qi = lax.broadcasted_iota(jnp.int32, (blk, 2 * blk), 0) + blk
    kj = lax.broadcasted_iota(jnp.int32, (blk, 2 * blk), 1)
    jrel = qi - kj
    valid = (jrel >= 0) & (jrel <= blk)
    jrel_f = jrel.astype(jnp.float32)

    def bias_for(g):
        slope = slopes_ref[g * HEADS_PER_GROUP + h]
        d = float(DIL_PATTERNS[g][1])
        return jnp.where(valid, (-slope * d) * jrel_f, -jnp.inf)

    def put(g, row_slice, o, lse):
        o_sc[g, row_slice, :] = o
        l_sc[g, row_slice, :] = jnp.broadcast_to(lse, (blk, HEAD_DIM))

    def run_blocks(g, blocks):
        scores = [_dot_t(q, k) * scale + bias for q, k, _, bias, _ in blocks]
        stats = []
        for s in scores:
            m = jnp.max(s, axis=-1, keepdims=True)
            p = jnp.exp(s - m)
            stats.append((m, p, jnp.sum(p, axis=-1, keepdims=True)))
        outs = [_dot(p.astype(jnp.bfloat16), blkdef[2]) / den
                for (m, p, den), blkdef in zip(stats, blocks)]
        for o, (m, p, den), blkdef in zip(outs, stats, blocks):
            put(g, blkdef[4], o, m + jnp.log(den))

    bias0 = bias_for(0)
    prev_cols = kj < blk

    def g0_body(it, carry):
        blocks = []
        for k in range(ATT_UNROLL):
            n = it * ATT_UNROLL + k
            q0 = pl.multiple_of(n * blk, blk)
            k0 = pl.multiple_of(jnp.maximum(n - 1, 0) * blk, blk)
            bias = jnp.where(prev_cols & (n == 0), -jnp.inf, bias0)
            blocks.append((q0_ref[pl.ds(q0, blk), :], k0_ref[pl.ds(k0, 2 * blk), :],
                           v0_ref[pl.ds(k0, 2 * blk), :], bias, pl.ds(q0, blk)))
        run_blocks(0, blocks)
        return carry

    lax.fori_loop(0, SEQ // blk // ATT_UNROLL, g0_body, 0)

    d1 = DIL_PATTERNS[1][1]
    cls1 = HALF // d1
    per_half = cls1 // blk
    bias1 = bias_for(1)

    def row1(r, n):
        return (n // per_half) * HALF + r * cls1 + (n % per_half) * blk

    nblk1 = SEQ // d1 // blk
    cls_per_trip = ATT_UNROLL // nblk1

    def g1_body(it, carry):
        blocks = []
        for c in range(cls_per_trip):
            r = it * cls_per_trip + c
            for n in range(nblk1):
                cur = pl.multiple_of(row1(r, n), blk)
                q = q1_ref[pl.ds(cur, blk), :]
                dst = pl.ds(n * blk * d1 + r, blk, stride=d1)
                if n == 0:
                    blocks.append((q, k1_ref[pl.ds(cur, blk), :], v1_ref[pl.ds(cur, blk), :],
                                   bias1[:, blk:], dst))
                else:
                    prev = pl.multiple_of(row1(r, n - 1), blk)
                    kc = jnp.concatenate([k1_ref[pl.ds(prev, blk), :], k1_ref[pl.ds(cur, blk), :]], axis=0)
                    vc = jnp.concatenate([v1_ref[pl.ds(prev, blk), :], v1_ref[pl.ds(cur, blk), :]], axis=0)
                    blocks.append((q, kc, vc, bias1, dst))
        run_blocks(1, blocks)
        return carry

    lax.fori_loop(0, d1 // cls_per_trip, g1_body, 0)

    d2 = DIL_PATTERNS[2][1]
    cls2 = HALF // d2
    bias2 = bias_for(2)

    def g2_body(it, carry):
        blocks = []
        for k in range(ATT_UNROLL):
            r = it * ATT_UNROLL + k
            a = pl.multiple_of(r * cls2, cls2)
            b = pl.multiple_of(HALF + r * cls2, cls2)
            cat = lambda ref, a=a, b=b: jnp.concatenate(
                [ref[pl.ds(a, cls2), :], ref[pl.ds(b, cls2), :]], axis=0)
            blocks.append((cat(q2_ref), cat(k2_ref), cat(v2_ref), bias2[:, blk:],
                           pl.ds(r, blk, stride=d2)))
        run_blocks(2, blocks)
        return carry

    lax.fori_loop(0, d2 // ATT_UNROLL, g2_body, 0)

    rows = 256
    for t in range(SEQ // rows):
        sl = pl.ds(t * rows, rows)
        l0, l1, l2 = l_sc[0, sl, :], l_sc[1, sl, :], l_sc[2, sl, :]
        m = jnp.maximum(jnp.maximum(l0, l1), l2)
        e0, e1, e2 = jnp.exp(l0 - m), jnp.exp(l1 - m), jnp.exp(l2 - m)
        mix = (e0 * o_sc[0, sl, :] + e1 * o_sc[1, sl, :] + e2 * o_sc[2, sl, :]) / (e0 + e1 + e2)
        o_ref[sl, :] = mix.astype(o_ref.dtype)


def _dil_branch(qkv, slopes):
    nq = DIL_DIM // HEAD_DIM

    def spec(section, g):
        return pl.BlockSpec((None, SEQ, HEAD_DIM),
                            lambda b, h, s_ref, o=section * nq + g * HEADS_PER_GROUP: (b, 0, o + h))

    grid_spec = pltpu.PrefetchScalarGridSpec(
        num_scalar_prefetch=1,
        grid=(BATCH, HEADS_PER_GROUP),
        in_specs=[spec(sec, g) for sec in range(3) for g in range(N_DIL_GROUPS)],
        out_specs=pl.BlockSpec((None, SEQ, HEAD_DIM), lambda b, h, s_ref: (b, 0, h)),
        scratch_shapes=[pltpu.VMEM((N_DIL_GROUPS, SEQ, HEAD_DIM), jnp.float32),
                        pltpu.VMEM((N_DIL_GROUPS, SEQ, HEAD_DIM), jnp.float32)],
    )
    return pl.pallas_call(
        _dilattn_kernel,
        grid_spec=grid_spec,
        out_shape=jax.ShapeDtypeStruct((BATCH, SEQ, DIL_OUT_DIM), jnp.bfloat16),
        compiler_params=_params(("parallel", "arbitrary")),
        name="dil_attn",
    )(slopes, *([qkv] * 9))


MERGE_TM = 1024
MERGE_TN = 512


def _merge_kernel(x_ref, wg0_ref, wg1_ref, wg2_ref, bg0_ref, bg1_ref, bg2_ref,
                  sc_ref, od_ref, om_ref, wco_ref, wdo_ref, wmo_ref, o_ref):
    x = x_ref[...]
    g0 = jax.nn.sigmoid(_dot(x, wg0_ref[...]) + bg0_ref[...])
    acc = g0 * _dot(sc_ref[...], wco_ref[...])
    g1 = jax.nn.sigmoid(_dot(x, wg1_ref[...]) + bg1_ref[...])
    acc = acc + g1 * _dot(od_ref[...], wdo_ref[...])
    g2 = jax.nn.sigmoid(_dot(x, wg2_ref[...]) + bg2_ref[...])
    acc = acc + g2 * _dot(om_ref[...], wmo_ref[...])
    o_ref[...] = acc.astype(o_ref.dtype)


def _merge(x2, w_in, b_in, s_conv, o_dil, o_mem, w_co, w_do, w_mo):
    nb = lambda br: (OFF_GATE + br * D_MODEL) // MERGE_TN
    gspec = lambda br: pl.BlockSpec((D_MODEL, MERGE_TN), lambda i, n, o=nb(br): (0, o + n))
    bspec = lambda br: pl.BlockSpec((1, MERGE_TN), lambda i, n, o=nb(br): (0, o + n))
    act = lambda width: pl.BlockSpec((MERGE_TM, width), lambda i, n: (i, 0))
    wout = lambda width: pl.BlockSpec((width, MERGE_TN), lambda i, n: (0, n))
    return pl.pallas_call(
        _merge_kernel,
        grid=(TOKENS // MERGE_TM, D_MODEL // MERGE_TN),
        in_specs=[act(D_MODEL), gspec(0), gspec(1), gspec(2), bspec(0), bspec(1), bspec(2),
                  act(CONV_DIM), act(DIL_OUT_DIM), act(MEM_DIM),
                  wout(CONV_DIM), wout(DIL_OUT_DIM), wout(MEM_DIM)],
        out_specs=pl.BlockSpec((MERGE_TM, MERGE_TN), lambda i, n: (i, n)),
        out_shape=jax.ShapeDtypeStruct((TOKENS, D_MODEL), jnp.bfloat16),
        compiler_params=_params(("parallel", "arbitrary")),
        name="gated_merge",
    )(x2, w_in, w_in, w_in, b_in, b_in, b_in, s_conv, o_dil, o_mem, w_co, w_do, w_mo)


OPROJ_TM = 512


OPROJ_PARTS = 2


def _oproj_kernel(x_ref, m_ref, wo_ref, g_ref, b_ref, wr_ref, br_ref, x1_ref, logit_ref):
    rows = OPROJ_TM // OPROJ_PARTS
    parts = [pl.ds(i * rows, rows) for i in range(OPROJ_PARTS)]
    proj = [_dot(m_ref[p, :], wo_ref[...]) for p in parts]
    for p, y in zip(parts, proj):
        x1 = _layer_norm(ALPHA * x_ref[p, :] + y, g_ref[...], b_ref[...])
        x1_ref[p, :] = x1
        logit_ref[p, :] = _dot(x1, wr_ref[...]) + br_ref[...]


def _oproj(x2, merged, w_o, g, b, w_route, b_route):
    row = lambda width: pl.BlockSpec((OPROJ_TM, width), lambda i: (i, 0))
    full = lambda r, c: pl.BlockSpec((r, c), lambda i: (0, 0))
    return pl.pallas_call(
        _oproj_kernel,
        grid=(TOKENS // OPROJ_TM,),
        in_specs=[row(D_MODEL), row(D_MODEL), full(D_MODEL, D_MODEL), full(1, D_MODEL), full(1, D_MODEL),
                  full(D_MODEL, LANES), full(1, LANES)],
        out_specs=[row(D_MODEL), row(LANES)],
        out_shape=[jax.ShapeDtypeStruct((TOKENS, D_MODEL), jnp.float32),
                   jax.ShapeDtypeStruct((TOKENS, LANES), jnp.float32)],
        compiler_params=_params(("parallel",)),
        name="oproj_ln1",
    )(x2, merged, w_o, g, b, w_route, b_route)


def _route_kernel(logit_ref, o_ref, cnt_ref, carry_sc):
    @pl.when(pl.program_id(0) == 0)
    def _():
        carry_sc[...] = jnp.zeros_like(carry_sc)

    tm = ROUTE_TILE
    z = logit_ref[...]
    lane = lax.broadcasted_iota(jnp.int32, (tm, LANES), 1)
    neg = -jnp.inf
    first = lambda hit: jnp.min(jnp.where(hit, lane, LANES), axis=-1, keepdims=True)

    glog = jnp.where(lane < N_GROUPS, z, neg)
    gmax = jnp.max(glog, axis=-1, keepdims=True)
    gsel = first(glog == gmax)
    gw = 1.0 / jnp.sum(jnp.exp(glog - gmax), axis=-1, keepdims=True)

    e_lane = lane - ROUTER_LANE0
    in_group = (e_lane >= 0) & (e_lane < N_EXPERTS) & ((e_lane // EXPERTS_PER_GROUP) == gsel)
    v = jnp.where(in_group, z, neg)
    top1 = jnp.max(v, axis=-1, keepdims=True)
    i1 = first(v == top1)
    v2 = jnp.where(lane == i1, neg, v)
    top2 = jnp.max(v2, axis=-1, keepdims=True)
    i2 = first(v2 == top2)
    t = jnp.exp(top2 - top1)
    w1 = gw / (1.0 + t)
    w2 = w1 * t

    onehot = ((lane == i1) | (lane == i2))
    ri = lax.broadcasted_iota(jnp.int32, (tm, tm), 0)
    ci = lax.broadcasted_iota(jnp.int32, (tm, tm), 1)
    tri = jnp.where(ci <= ri, 1.0, 0.0).astype(jnp.bfloat16)
    cum = _dot(tri, jnp.where(onehot, 1.0, 0.0).astype(jnp.bfloat16))
    carry = carry_sc[0:1, :]
    before = cum + carry - 1.0
    rank1 = jnp.sum(jnp.where(lane == i1, before, 0.0), axis=-1, keepdims=True)
    rank2 = jnp.sum(jnp.where(lane == i2, before, 0.0), axis=-1, keepdims=True)
    new_carry = carry + cum[tm - 1:tm, :]
    carry_sc[...] = jnp.broadcast_to(new_carry, carry_sc.shape)
    cnt_ref[...] = jnp.broadcast_to(new_carry, cnt_ref.shape)

    cols = [(i1 - ROUTER_LANE0).astype(jnp.float32), (i2 - ROUTER_LANE0).astype(jnp.float32),
            rank1, rank2, w1, w2]
    out = jnp.zeros((tm, LANES), jnp.float32)
    for k, col in enumerate(cols):
        out = jnp.where(lane == k, col, out)
    o_ref[...] = out


def _route(logits):
    return pl.pallas_call(
        _route_kernel,
        grid=(TOKENS // ROUTE_TILE,),
        in_specs=[pl.BlockSpec((ROUTE_TILE, LANES), lambda i: (i, 0))],
        out_specs=[pl.BlockSpec((ROUTE_TILE, LANES), lambda i: (i, 0)),
                   pl.BlockSpec((8, LANES), lambda i: (0, 0))],
        out_shape=[jax.ShapeDtypeStruct((TOKENS, LANES), jnp.float32),
                   jax.ShapeDtypeStruct((8, LANES), jnp.float32)],
        scratch_shapes=[pltpu.VMEM((8, LANES), jnp.float32)],
        compiler_params=_params(("arbitrary",)),
        name="route",
    )(logits)


def _slots_kernel(route_ref, cnt_ref, pos_ref):
    lane8 = lax.broadcasted_iota(jnp.int32, (8, LANES), 1)
    cnt = cnt_ref[...]
    padded = jnp.ceil(cnt * (1.0 / EXPERT_TILE)) * EXPERT_TILE
    incl = padded
    for sh in (1, 2, 4, 8, 16, 32, 64):
        incl = incl + jnp.where(lane8 >= sh, pltpu.roll(incl, sh, axis=1), 0.0)
    start = (incl - padded)[0:1, :]

    r = route_ref[...]
    lane = lax.broadcasted_iota(jnp.int32, r.shape, 1).astype(jnp.float32)
    pick = lambda e: jnp.sum(jnp.where(lane == e + float(ROUTER_LANE0), start, 0.0), axis=-1, keepdims=True)
    pos_a = pick(r[:, 0:1]) + r[:, 2:3]
    pos_b = pick(r[:, 1:2]) + r[:, 3:4]
    pos_ref[...] = jnp.where(lane == 0.0, pos_a, jnp.where(lane == 1.0, pos_b, 0.0)).astype(jnp.int32)


def _slots(routing, counts8):
    return pl.pallas_call(
        _slots_kernel,
        grid=(TOKENS // ROUTE_TILE,),
        in_specs=[pl.BlockSpec((ROUTE_TILE, LANES), lambda i: (i, 0)),
                  pl.BlockSpec((8, LANES), lambda i: (0, 0))],
        out_specs=pl.BlockSpec((ROUTE_TILE, LANES), lambda i: (i, 0)),
        out_shape=jax.ShapeDtypeStruct((TOKENS, LANES), jnp.int32),
        compiler_params=_params(("parallel",)),
        name="slots",
    )(routing, counts8)


SUBLANES = 8
DISPATCH_TILE = 512
SORTED_ROWS = N_EXPERT_TILES * EXPERT_TILE


PACKED = D_MODEL // 2


def _pack_rows(v):
    return pltpu.pack_elementwise([v[:, :PACKED], v[:, PACKED:]], packed_dtype=jnp.bfloat16)


def _unpack_rows(w):
    half = lambda i: pltpu.unpack_elementwise(w, index=i, packed_dtype=jnp.bfloat16,
                                              unpacked_dtype=jnp.float32)
    return half(0), half(1)


def _dispatch_kernel(pad_start_ref, pad_n_ref, nvalid_ref, pos_ref, x_ref, xs_hbm, zero_sc, pk_sc, sem,
                     row_sem):
    groups = DISPATCH_TILE // SUBLANES

    @pl.when(pl.program_id(0) == 0)
    def _():
        zero_sc[...] = jnp.zeros_like(zero_sc)

        def tile_copy(t):
            return pltpu.make_async_copy(zero_sc, xs_hbm.at[pl.ds(pl.multiple_of(t * EXPERT_TILE, EXPERT_TILE),
                                                                  EXPERT_TILE)], sem.at[2])

        def start_tile(t, c):
            tile_copy(t).start()
            return c

        def wait_tile(t, c):
            tile_copy(t).wait()
            return c
        lax.fori_loop(nvalid_ref[0], N_EXPERT_TILES, start_tile, 0)
        lax.fori_loop(nvalid_ref[0], N_EXPERT_TILES, wait_tile, 0)

        def per_expert(e, total):
            n = pad_n_ref[e]

            def per_row(i, c):
                pltpu.make_async_copy(zero_sc.at[pl.ds(0, 1)], xs_hbm.at[pl.ds(pad_start_ref[e] + i, 1)],
                                      sem.at[1]).start()
                return c
            lax.fori_loop(0, n, per_row, 0)
            return total + n
        total = lax.fori_loop(0, N_EXPERTS, per_expert, 0)

        def wait_row(i, c):
            pltpu.make_async_copy(zero_sc.at[pl.ds(0, 1)], xs_hbm.at[pl.ds(0, 1)], sem.at[1]).wait()
            return c
        lax.fori_loop(0, total, wait_row, 0)

    j = pl.program_id(0)
    slot = lax.rem(j, 2)
    pk_sc[slot] = _pack_rows(x_ref[...]).reshape(groups, SUBLANES, PACKED)

    def body(gi, c):
        for s in range(SUBLANES):
            for k in range(2):
                p = pos_ref[0, 0, (gi * SUBLANES + s) * 2 + k]
                pltpu.make_async_copy(pk_sc.at[slot, gi, pl.ds(s, 1)], xs_hbm.at[pl.ds(p, 1)],
                                      row_sem.at[slot]).start(priority=k)
        return c
    lax.fori_loop(0, groups, body, 0)

    def wait_slot(sl):
        for _ in range(2 * groups):
            pltpu.make_async_copy(pk_sc.at[sl, 0], xs_hbm.at[pl.ds(0, SUBLANES)], row_sem.at[sl]).wait()

    @pl.when(j > 0)
    def _():
        wait_slot(1 - slot)

    @pl.when(j == pl.num_programs(0) - 1)
    def _():
        wait_slot(slot)


def _dispatch(pad_start, pad_n, n_valid_tiles, pos3, x1):
    nt = TOKENS // DISPATCH_TILE
    groups = DISPATCH_TILE // SUBLANES
    grid_spec = pltpu.PrefetchScalarGridSpec(
        num_scalar_prefetch=3,
        grid=(nt,),
        in_specs=[pl.BlockSpec((1, 1, 2 * DISPATCH_TILE), lambda j, *_: (j, 0, 0), memory_space=pltpu.SMEM),
                  pl.BlockSpec((DISPATCH_TILE, D_MODEL), lambda j, *_: (j, 0))],
        out_specs=pl.BlockSpec(memory_space=pl.ANY),
        scratch_shapes=[pltpu.VMEM((EXPERT_TILE, PACKED), jnp.int32),
                        pltpu.VMEM((2, groups, SUBLANES, PACKED), jnp.int32),
                        pltpu.SemaphoreType.DMA((3,)),
                        pltpu.SemaphoreType.DMA((2,))],
    )
    return pl.pallas_call(
        _dispatch_kernel,
        grid_spec=grid_spec,
        out_shape=jax.ShapeDtypeStruct((SORTED_ROWS, PACKED), jnp.int32),
        compiler_params=_params(("arbitrary",)),
        name="dispatch",
    )(pad_start, pad_n, n_valid_tiles, pos3, x1)


def _ffn_kernel(texp_ref, tvalid_ref, tslot_ref, tfirst_ref, tnext_ref, nvalid_ref,
                xs_ref, wg_hbm, wu_hbm, wd_hbm, y_ref, wg_buf, wu_buf, wd_buf, sem):
    j = pl.program_id(0)
    valid = tvalid_ref[j] == 1
    slot = tslot_ref[j]

    def weight_copies(e, s):
        return (pltpu.make_async_copy(wg_hbm.at[e], wg_buf.at[s], sem.at[s, 0]),
                pltpu.make_async_copy(wu_hbm.at[e], wu_buf.at[s], sem.at[s, 1]),
                pltpu.make_async_copy(wd_hbm.at[e], wd_buf.at[s], sem.at[s, 2]))

    @pl.when(valid & (tfirst_ref[j] == 1))
    def _():
        @pl.when(j == 0)
        def _():
            for cp in weight_copies(texp_ref[j], slot):
                cp.start()

        for cp in weight_copies(texp_ref[j], slot):
            cp.wait()

        @pl.when(tnext_ref[j] >= 0)
        def _():
            for cp in weight_copies(tnext_ref[j], 1 - slot):
                cp.start()

    @pl.when(valid)
    def _():
        lo, hi = _unpack_rows(xs_ref[...])
        a = _dot(lo, wg_buf[slot, 0:PACKED, :]) + _dot(hi, wg_buf[slot, PACKED:D_MODEL, :])
        u = _dot(lo, wu_buf[slot, 0:PACKED, :]) + _dot(hi, wu_buf[slot, PACKED:D_MODEL, :])
        hid = (a * jax.nn.sigmoid(a)) * u
        y_ref[...] = _pack_rows(_dot(hid, wd_buf[slot]))

    @pl.when(jnp.logical_not(valid))
    def _():
        y_ref[...] = jnp.zeros_like(y_ref)


def _expert_ffn(tables, xs, w_gate, w_up, w_down):
    nt = N_EXPERT_TILES
    xmap = lambda j, te, tv, ts, tf, tn, nv: (jnp.minimum(j, nv[0] - 1), 0)
    grid_spec = pltpu.PrefetchScalarGridSpec(
        num_scalar_prefetch=6,
        grid=(nt,),
        in_specs=[pl.BlockSpec((EXPERT_TILE, PACKED), xmap),
                  pl.BlockSpec(memory_space=pl.ANY),
                  pl.BlockSpec(memory_space=pl.ANY),
                  pl.BlockSpec(memory_space=pl.ANY)],
        out_specs=pl.BlockSpec((EXPERT_TILE, PACKED), lambda j, *_: (j, 0)),
        scratch_shapes=[pltpu.VMEM((2, D_MODEL, D_EXPERT), jnp.float32),
                        pltpu.VMEM((2, D_MODEL, D_EXPERT), jnp.float32),
                        pltpu.VMEM((2, D_EXPERT, D_MODEL), jnp.float32),
                        pltpu.SemaphoreType.DMA((2, 3))],
    )
    return pl.pallas_call(
        _ffn_kernel,
        grid_spec=grid_spec,
        out_shape=jax.ShapeDtypeStruct((nt * EXPERT_TILE, PACKED), jnp.int32),
        compiler_params=_params(("arbitrary",)),
        name="expert_ffn",
    )(*tables, xs, w_gate, w_up, w_down)


def _combine_kernel(pos_cur_ref, pos_next_ref, r_ref, x1_ref, y_hbm, g_ref, b_ref, o_ref, ybuf, sem):
    j = pl.program_id(0)
    nt = pl.num_programs(0)
    slot = lax.rem(j, 2)
    groups = COMBINE_TILE // SUBLANES

    def gather(pos_ref, dst_slot):
        def body(gi, c):
            for s in range(SUBLANES):
                for k in range(2):
                    p = pos_ref[0, 0, (gi * SUBLANES + s) * 2 + k]
                    pltpu.make_async_copy(y_hbm.at[pl.ds(p, 1)], ybuf.at[dst_slot, k, gi, pl.ds(s, 1)],
                                          sem.at[dst_slot]).start(priority=k)
            return c
        lax.fori_loop(0, groups, body, 0)

    @pl.when(j == 0)
    def _():
        gather(pos_cur_ref, 0)

    @pl.when(j + 1 < nt)
    def _():
        gather(pos_next_ref, 1 - slot)

    for _ in range(2 * groups):
        pltpu.make_async_copy(y_hbm.at[pl.ds(0, SUBLANES)], ybuf.at[slot, 0, 0], sem.at[slot]).wait()
    r = r_ref[...]
    unpack = lambda k: jnp.concatenate(_unpack_rows(ybuf[slot, k].reshape(COMBINE_TILE, PACKED)), axis=1)
    ya, yb = unpack(0), unpack(1)
    hres = ALPHA * x1_ref[...] + r[:, 4:5] * ya + r[:, 5:6] * yb
    o_ref[...] = _layer_norm(hres, g_ref[...], b_ref[...])


def _combine(pos3, routing, x1, y_sorted, g, b):
    nt = TOKENS // COMBINE_TILE
    groups = COMBINE_TILE // SUBLANES
    return pl.pallas_call(
        _combine_kernel,
        grid=(nt,),
        in_specs=[pl.BlockSpec((1, 1, 2 * COMBINE_TILE), lambda j: (j, 0, 0), memory_space=pltpu.SMEM),
                  pl.BlockSpec((1, 1, 2 * COMBINE_TILE), lambda j: (jnp.minimum(j + 1, nt - 1), 0, 0),
                               memory_space=pltpu.SMEM),
                  pl.BlockSpec((COMBINE_TILE, LANES), lambda j: (j, 0)),
                  pl.BlockSpec((COMBINE_TILE, D_MODEL), lambda j: (j, 0)),
                  pl.BlockSpec(memory_space=pl.ANY),
                  pl.BlockSpec((1, D_MODEL), lambda j: (0, 0)),
                  pl.BlockSpec((1, D_MODEL), lambda j: (0, 0))],
        out_specs=pl.BlockSpec((COMBINE_TILE, D_MODEL), lambda j: (j, 0)),
        out_shape=jax.ShapeDtypeStruct((TOKENS, D_MODEL), jnp.float32),
        scratch_shapes=[pltpu.VMEM((2, 2, groups, SUBLANES, PACKED), jnp.int32),
                        pltpu.SemaphoreType.DMA((2,))],
        compiler_params=_params(("arbitrary",)),
        name="combine_ln2",
    )(pos3, pos3, routing, x1, y_sorted, g, b)


def _alibi_slopes():
    n = N_DIL_GROUPS * HEADS_PER_GROUP
    return jnp.asarray(2.0 ** (-ALIBI_MAX * np.arange(1, n + 1, dtype=np.float32) / n), jnp.float32)


def kernel(x, mem, ln_mem_g, ln_mem_b, w_in, b_in, w_conv, w_conv_out, w_dil_out, w_mem_kv, w_mem_out, w_o, ln1_g, ln1_b, w_group, b_group, w_router, b_router, w_gate, w_up, w_down, ln2_g, ln2_b):
    assert x.shape == (BATCH, SEQ, D_MODEL) and w_in.shape == (1, D_MODEL, IN_DIM)
    bf16 = jnp.bfloat16
    row = lambda v: v.reshape(1, -1)
    w_in2 = w_in[0].astype(bf16)
    b_in2 = b_in

    kv = _memkv(mem, row(ln_mem_g), row(ln_mem_b), w_mem_kv[0].astype(bf16))
    s_conv, xb = _conv_branch(x, w_in2, b_in2, w_conv[0])
    qkv = _qkv_proj(xb, w_in2, b_in2)
    o_mem = _mem_branch(xb, w_in2, b_in2, kv)
    o_dil = _dil_branch(qkv, _alibi_slopes())

    x2 = x.reshape(TOKENS, D_MODEL)
    merged = _merge(xb.reshape(TOKENS, D_MODEL), w_in2, b_in2,
                    s_conv.reshape(TOKENS, CONV_DIM), o_dil.reshape(TOKENS, DIL_OUT_DIM),
                    o_mem.reshape(TOKENS, MEM_DIM),
                    w_conv_out[0].astype(bf16), w_dil_out[0].astype(bf16), w_mem_out[0].astype(bf16))

    w_route = jnp.concatenate(
        [w_group[0], jnp.transpose(w_router[0], (1, 0, 2)).reshape(D_MODEL, N_EXPERTS),
         jnp.zeros((D_MODEL, LANES - N_GROUPS - N_EXPERTS), jnp.float32)], axis=1)
    b_route = jnp.concatenate(
        [b_group[0], b_router[0].reshape(N_EXPERTS),
         jnp.zeros((LANES - N_GROUPS - N_EXPERTS,), jnp.float32)]).reshape(1, LANES)
    x1, logits = _oproj(x2, merged, w_o[0].astype(bf16), ln1_g, ln1_b, w_route, b_route)

    routing, counts8 = _route(logits)
    pos = _slots(routing, counts8)[:, 0:2]

    i32 = jnp.int32
    counts = counts8[0, ROUTER_LANE0:ROUTER_LANE0 + N_EXPERTS].astype(i32)
    padded = ((counts + EXPERT_TILE - 1) // EXPERT_TILE) * EXPERT_TILE
    ends = jnp.cumsum(padded)
    starts = ends - padded
    tile_start = jnp.arange(N_EXPERT_TILES, dtype=i32) * EXPERT_TILE
    tile_expert = jnp.minimum(jnp.sum((ends[None, :] <= tile_start[:, None]).astype(i32), axis=1),
                              N_EXPERTS - 1)
    tile_valid = tile_start < ends[-1]
    prev_expert = jnp.concatenate([jnp.full((1,), -1, i32), tile_expert[:-1]])
    tile_first = tile_valid & (tile_expert != prev_expert)
    tile_slot = (jnp.cumsum(tile_first.astype(i32)) - 1) & 1
    big = N_EXPERTS
    idx = jnp.where(counts > 0, jnp.arange(N_EXPERTS, dtype=i32), big)
    later = jnp.concatenate([lax.cummin(idx[::-1])[::-1][1:], jnp.full((1,), big, i32)])
    next_used = jnp.where(later == big, -1, later)
    n_valid_tiles = (ends[-1:] // EXPERT_TILE).astype(i32)
    tables = (tile_expert, tile_valid.astype(i32), tile_slot, tile_first.astype(i32),
              next_used[tile_expert], n_valid_tiles)

    xs = _dispatch(starts + counts, padded - counts, n_valid_tiles,
                   pos.reshape(TOKENS // DISPATCH_TILE, 1, 2 * DISPATCH_TILE), x1)
    y_sorted = _expert_ffn(tables, xs,
                           w_gate.reshape(N_EXPERTS, D_MODEL, D_EXPERT),
                           w_up.reshape(N_EXPERTS, D_MODEL, D_EXPERT),
                           w_down.reshape(N_EXPERTS, D_EXPERT, D_MODEL))
    out = _combine(pos.reshape(TOKENS // COMBINE_TILE, 1, 2 * COMBINE_TILE), routing, x1, y_sorted,
                   ln2_g, ln2_b)
    return out.reshape(BATCH, SEQ, D_MODEL)
```

```python
import functools
import math

import numpy as np
import jax
import jax.numpy as jnp
from jax import lax
from jax.experimental import pallas as pl
from jax.experimental.pallas import tpu as pltpu

D_MODEL = 2048
BATCH = 8
SEQ = 2048
TOKENS = BATCH * SEQ
CONV_DIM = 1024
CONV_WIDTH = 3
DIL_PATTERNS = ((128, 1), (512, 4), (2048, 16))
N_DIL_GROUPS = 3
HEADS_PER_GROUP = 4
HEAD_DIM = 128
DIL_DIM = N_DIL_GROUPS * HEADS_PER_GROUP * HEAD_DIM
DIL_OUT_DIM = HEADS_PER_GROUP * HEAD_DIM
ATT_BLOCK = 128
ALIBI_MAX = 8.0
MEM_LEN = 256
MEM_HEADS = 4
MEM_HEAD_DIM = 256
MEM_DIM = MEM_HEADS * MEM_HEAD_DIM
N_BRANCHES = 3
IN_DIM = 3 * CONV_DIM + 3 * DIL_DIM + MEM_DIM + N_BRANCHES * D_MODEL
N_GROUPS = 4
EXPERTS_PER_GROUP = 8
N_EXPERTS = N_GROUPS * EXPERTS_PER_GROUP
D_EXPERT = 512
ALPHA = 2.0 ** 0.25
LN_EPS = 1e-5

OFF_CB = 0
OFF_CC = CONV_DIM
OFF_CH = 2 * CONV_DIM
OFF_Q = 3 * CONV_DIM
OFF_MQ = OFF_Q + 3 * DIL_DIM
OFF_GATE = OFF_MQ + MEM_DIM

LANES = 128
HALF = SEQ // 2
VMEM_LIMIT = 56 * 1024 * 1024

ROUTE_TILE = 1024
EXPERT_TILE = 512
N_EXPERT_TILES = 2 * TOKENS // EXPERT_TILE + N_EXPERTS
COMBINE_TILE = 256
ROUTER_LANE0 = N_GROUPS


def _params(sem, limit=VMEM_LIMIT):
    return pltpu.CompilerParams(dimension_semantics=sem, vmem_limit_bytes=limit)


def _layer_norm(x, g, b):
    mu = jnp.mean(x, axis=-1, keepdims=True)
    xc = x - mu
    var = jnp.mean(xc * xc, axis=-1, keepdims=True)
    return xc * lax.rsqrt(var + LN_EPS) * g + b


def _dot(a, b):
    return jnp.dot(a, b, preferred_element_type=jnp.float32)


def _dot_t(a, b):
    return lax.dot_general(a, b, (((1,), (1,)), ((), ())), preferred_element_type=jnp.float32)


MEMKV_BATCHES = 2


def _memkv_kernel(mem_ref, g_ref, b_ref, w_ref, kv_ref):
    rows = MEMKV_BATCHES * MEM_LEN
    y = _layer_norm(mem_ref[...].reshape(rows, D_MODEL), g_ref[...], b_ref[...])
    kv = _dot(y.astype(jnp.bfloat16), w_ref[...])
    kv_ref[...] = kv.astype(kv_ref.dtype).reshape(MEMKV_BATCHES, MEM_LEN, 2 * MEM_DIM)


def _memkv(mem, g, b, w):
    return pl.pallas_call(
        _memkv_kernel,
        grid=(BATCH // MEMKV_BATCHES,),
        in_specs=[pl.BlockSpec((MEMKV_BATCHES, MEM_LEN, D_MODEL), lambda i: (i, 0, 0)),
                  pl.BlockSpec((1, D_MODEL), lambda i: (0, 0)),
                  pl.BlockSpec((1, D_MODEL), lambda i: (0, 0)),
                  pl.BlockSpec((D_MODEL, 2 * MEM_DIM), lambda i: (0, 0))],
        out_specs=pl.BlockSpec((MEMKV_BATCHES, MEM_LEN, 2 * MEM_DIM), lambda i: (i, 0, 0)),
        out_shape=jax.ShapeDtypeStruct((BATCH, MEM_LEN, 2 * MEM_DIM), jnp.bfloat16),
        compiler_params=_params(("parallel",)),
        name="mem_kv",
    )(mem, g, b, w)


CONV_TC = 512


def _conv_kernel(x_ref, wb_ref, wc_ref, wh_ref, bb_ref, bc_ref, bh_ref, wconv_ref, s_ref, xb_ref,
                 u_sc, carry_sc):
    half = pl.program_id(1)
    c = pl.program_id(2)

    @pl.when(c == 0)
    def _():
        xb_ref[...] = x_ref[...].astype(xb_ref.dtype)

    x = xb_ref[...]
    cb = _dot(x, wb_ref[...]) + bb_ref[...]
    cc = _dot(x, wc_ref[...]) + bc_ref[...]
    ch = _dot(x, wh_ref[...]) + bh_ref[...]
    u = cc * ch
    u_sc[0:8, :] = jnp.where(half == 0, 0.0, carry_sc[c])
    u_sc[8:8 + HALF, :] = u
    carry_sc[c] = u[HALF - 8:HALF, :]
    wconv = wconv_ref[...]
    y = (wconv[2:3, :] * u
         + wconv[1:2, :] * u_sc[7:7 + HALF, :]
         + wconv[0:1, :] * u_sc[6:6 + HALF, :])
    s_ref[...] = (cb * y).astype(s_ref.dtype)


def _conv_branch(x, w_in, b_in, w_conv):
    nb = lambda off: off // CONV_TC
    wspec = lambda off: pl.BlockSpec((D_MODEL, CONV_TC), lambda b, h, c, o=nb(off): (0, o + c))
    bspec = lambda off: pl.BlockSpec((1, CONV_TC), lambda b, h, c, o=nb(off): (0, o + c))
    return pl.pallas_call(
        _conv_kernel,
        grid=(BATCH, 2, CONV_DIM // CONV_TC),
        in_specs=[pl.BlockSpec((None, HALF, D_MODEL), lambda b, h, c: (b, h, 0)),
                  wspec(OFF_CB), wspec(OFF_CC), wspec(OFF_CH),
                  bspec(OFF_CB), bspec(OFF_CC), bspec(OFF_CH),
                  pl.BlockSpec((CONV_WIDTH, CONV_TC), lambda b, h, c: (0, c))],
        out_specs=[pl.BlockSpec((None, HALF, CONV_TC), lambda b, h, c: (b, h, c)),
                   pl.BlockSpec((None, HALF, D_MODEL), lambda b, h, c: (b, h, 0))],
        out_shape=[jax.ShapeDtypeStruct((BATCH, SEQ, CONV_DIM), jnp.bfloat16),
                   jax.ShapeDtypeStruct((BATCH, SEQ, D_MODEL), jnp.bfloat16)],
        scratch_shapes=[pltpu.VMEM((HALF + 8, CONV_TC), jnp.float32),
                        pltpu.VMEM((CONV_DIM // CONV_TC, 8, CONV_TC), jnp.float32)],
        compiler_params=_params(("arbitrary", "arbitrary", "arbitrary")),
        name="conv_branch",
    )(x, w_in, w_in, w_in, b_in, b_in, b_in, w_conv)


QKV_TN = 512
QKV_CHUNKS = QKV_TN // LANES


def _qkv_kernel(x_ref, w_ref, b_ref, o_ref, sc_ref):
    x = x_ref[...]
    for gi in (2, 1, 0):
        cols = slice(gi * QKV_TN, (gi + 1) * QKV_TN)
        acc = _dot(x, w_ref[:, cols]) + b_ref[:, cols]
        d = DIL_PATTERNS[gi][1]
        if d == 1:
            o_ref[:, cols] = acc.astype(o_ref.dtype)
            continue
        rows = HALF // d
        for c in range(QKV_CHUNKS):
            sc_ref[gi - 1, c] = acc[:, c * LANES:(c + 1) * LANES]
        for c in range(QKV_CHUNKS):
            for r in range(d):
                lo = gi * QKV_TN + c * LANES
                o_ref[r * rows:(r + 1) * rows, lo:lo + LANES] = (
                    sc_ref[gi - 1, c, pl.ds(r, rows, stride=d), :].astype(o_ref.dtype))


def _qkv_proj(xb, w_in, b_in):
    n0 = OFF_Q // DIL_DIM
    return pl.pallas_call(
        _qkv_kernel,
        grid=(BATCH, 2, 3),
        in_specs=[pl.BlockSpec((None, HALF, D_MODEL), lambda b, h, n: (b, h, 0)),
                  pl.BlockSpec((D_MODEL, DIL_DIM), lambda b, h, n: (0, n0 + n)),
                  pl.BlockSpec((1, DIL_DIM), lambda b, h, n: (0, n0 + n))],
        out_specs=pl.BlockSpec((None, HALF, DIL_DIM), lambda b, h, n: (b, h, n)),
        out_shape=jax.ShapeDtypeStruct((BATCH, SEQ, 3 * DIL_DIM), jnp.bfloat16),
        scratch_shapes=[pltpu.VMEM((N_DIL_GROUPS - 1, QKV_CHUNKS, HALF, LANES), jnp.float32)],
        compiler_params=_params(("parallel", "parallel", "arbitrary")),
        name="qkv_proj",
    )(xb, w_in, b_in)


MEM_HEADS_PER_STEP = 2
MEM_TN = MEM_HEADS_PER_STEP * MEM_HEAD_DIM


def _memattn_kernel(x_ref, w_ref, b_ref, mk_ref, mv_ref, o_ref):
    mq_all = (_dot(x_ref[...], w_ref[...]) + b_ref[...]).astype(jnp.bfloat16)
    for hh in range(MEM_HEADS_PER_STEP):
        cols = slice(hh * MEM_HEAD_DIM, (hh + 1) * MEM_HEAD_DIM)
        s = _dot_t(mq_all[:, cols], mk_ref[:, cols]) * (MEM_HEAD_DIM ** -0.5)
        m = jnp.max(s, axis=-1, keepdims=True)
        p = jnp.exp(s - m)
        den = jnp.sum(p, axis=-1, keepdims=True)
        o = _dot(p.astype(jnp.bfloat16), mv_ref[:, cols]) / den
        o_ref[:, cols] = o.astype(o_ref.dtype)


def _mem_branch(xb, w_in, b_in, kv):
    n0 = OFF_MQ // MEM_TN
    nv = MEM_DIM // MEM_TN
    return pl.pallas_call(
        _memattn_kernel,
        grid=(BATCH, 2, MEM_DIM // MEM_TN),
        in_specs=[pl.BlockSpec((None, HALF, D_MODEL), lambda b, h, n: (b, h, 0)),
                  pl.BlockSpec((D_MODEL, MEM_TN), lambda b, h, n: (0, n0 + n)),
                  pl.BlockSpec((1, MEM_TN), lambda b, h, n: (0, n0 + n)),
                  pl.BlockSpec((None, MEM_LEN, MEM_TN), lambda b, h, n: (b, 0, n)),
                  pl.BlockSpec((None, MEM_LEN, MEM_TN), lambda b, h, n: (b, 0, nv + n))],
        out_specs=pl.BlockSpec((None, HALF, MEM_TN), lambda b, h, n: (b, h, n)),
        out_shape=jax.ShapeDtypeStruct((BATCH, SEQ, MEM_DIM), jnp.bfloat16),
        compiler_params=_params(("parallel", "parallel", "arbitrary")),
        name="mem_branch",
    )(xb, w_in, b_in, kv, kv)


ATT_UNROLL = 16


def _softmax_block(s, v):
    m = jnp.max(s, axis=-1, keepdims=True)
    p = jnp.exp(s - m)
    den = jnp.sum(p, axis=-1, keepdims=True)
    o = _dot(p.astype(jnp.bfloat16), v) / den
    return o, m + jnp.log(den)


def _dilattn_kernel(slopes_ref,
                    q0_ref, q1_ref, q2_ref, k0_ref, k1_ref, k2_ref, v0_ref, v1_ref, v2_ref,
                    o_ref, o_sc, l_sc):
    h = pl.program_id(1)
    blk = ATT_BLOCK
    scale = HEAD_DIM ** -0.5
    qi = lax.broadcasted_iota(jnp.int32, (blk, 2 * blk), 0) + blk
    kj = lax.broadcasted_iota(jnp.int32, (blk, 2 * blk), 1)
    jrel = qi - kj
    valid = (jrel >= 0) & (jrel <= blk)
    jrel_f = jrel.astype(jnp.float32)

    def bias_for(g):
        slope = slopes_ref[g * HEADS_PER_GROUP + h]
        d = float(DIL_PATTERNS[g][1])
        return jnp.where(valid, (-slope * d) * jrel_f, -jnp.inf)

    def put(g, row_slice, o, lse):
        o_sc[g, row_slice, :] = o
        l_sc[g, row_slice, :] = jnp.broadcast_to(lse, (blk, HEAD_DIM))

    def run_blocks(g, blocks):
        scores = [_dot_t(q, k) * scale + bias for q, k, _, bias, _ in blocks]
        stats = []
        for s in scores:
            m = jnp.max(s, axis=-1, keepdims=True)
            p = jnp.exp(s - m)
            stats.append((m, p, jnp.sum(p, axis=-1, keepdims=True)))
        outs = [_dot(p.astype(jnp.bfloat16), blkdef[2]) / den
                for (m, p, den), blkdef in zip(stats, blocks)]
        for o, (m, p, den), blkdef in zip(outs, stats, blocks):
            put(g, blkdef[4], o, m + jnp.log(den))

    bias0 = bias_for(0)
    prev_cols = kj < blk

    def g0_body(it, carry):
        blocks = []
        for k in range(ATT_UNROLL):
            n = it * ATT_UNROLL + k
            q0 = pl.multiple_of(n * blk, blk)
            k0 = pl.multiple_of(jnp.maximum(n - 1, 0) * blk, blk)
            bias = jnp.where(prev_cols & (n == 0), -jnp.inf, bias0)
            blocks.append((q0_ref[pl.ds(q0, blk), :], k0_ref[pl.ds(k0, 2 * blk), :],
                           v0_ref[pl.ds(k0, 2 * blk), :], bias, pl.ds(q0, blk)))
        run_blocks(0, blocks)
        return carry

    lax.fori_loop(0, SEQ // blk // ATT_UNROLL, g0_body, 0)

    d1 = DIL_PATTERNS[1][1]
    cls1 = HALF // d1
    per_half = cls1 // blk
    bias1 = bias_for(1)

    def row1(r, n):
        return (n // per_half) * HALF + r * cls1 + (n % per_half) * blk

    nblk1 = SEQ // d1 // blk
    cls_per_trip = ATT_UNROLL // nblk1

    def g1_body(it, carry):
        blocks = []
        for c in range(cls_per_trip):
            r = it * cls_per_trip + c
            for n in range(nblk1):
                cur = pl.multiple_of(row1(r, n), blk)
                q = q1_ref[pl.ds(cur, blk), :]
                dst = pl.ds(n * blk * d1 + r, blk, stride=d1)
                if n == 0:
                    blocks.append((q, k1_ref[pl.ds(cur, blk), :], v1_ref[pl.ds(cur, blk), :],
                                   bias1[:, blk:], dst))
                else:
                    prev = pl.multiple_of(row1(r, n - 1), blk)
                    kc = jnp.concatenate([k1_ref[pl.ds(prev, blk), :], k1_ref[pl.ds(cur, blk), :]], axis=0)
                    vc = jnp.concatenate([v1_ref[pl.ds(prev, blk), :], v1_ref[pl.ds(cur, blk), :]], axis=0)
                    blocks.append((q, kc, vc, bias1, dst))
        run_blocks(1, blocks)
        return carry

    lax.fori_loop(0, d1 // cls_per_trip, g1_body, 0)

    d2 = DIL_PATTERNS[2][1]
    cls2 = HALF // d2
    bias2 = bias_for(2)

    def g2_body(it, carry):
        blocks = []
        for k in range(ATT_UNROLL):
            r = it * ATT_UNROLL + k
            a = pl.multiple_of(r * cls2, cls2)
            b = pl.multiple_of(HALF + r * cls2, cls2)
            cat = lambda ref, a=a, b=b: jnp.concatenate(
                [ref[pl.ds(a, cls2), :], ref[pl.ds(b, cls2), :]], axis=0)
            blocks.append((cat(q2_ref), cat(k2_ref), cat(v2_ref), bias2[:, blk:],
                           pl.ds(r, blk, stride=d2)))
        run_blocks(2, blocks)
        return carry

    lax.fori_loop(0, d2 // ATT_UNROLL, g2_body, 0)

    rows = 256
    for t in range(SEQ // rows):
        sl = pl.ds(t * rows, rows)
        l0, l1, l2 = l_sc[0, sl, :], l_sc[1, sl, :], l_sc[2, sl, :]
        m = jnp.maximum(jnp.maximum(l0, l1), l2)
        e0, e1, e2 = jnp.exp(l0 - m), jnp.exp(l1 - m), jnp.exp(l2 - m)
        mix = (e0 * o_sc[0, sl, :] + e1 * o_sc[1, sl, :] + e2 * o_sc[2, sl, :]) / (e0 + e1 + e2)
        o_ref[sl, :] = mix.astype(o_ref.dtype)


def _dil_branch(qkv, slopes):
    nq = DIL_DIM // HEAD_DIM

    def spec(section, g):
        return pl.BlockSpec((None, SEQ, HEAD_DIM),
                            lambda b, h, s_ref, o=section * nq + g * HEADS_PER_GROUP: (b, 0, o + h))

    grid_spec = pltpu.PrefetchScalarGridSpec(
        num_scalar_prefetch=1,
        grid=(BATCH, HEADS_PER_GROUP),
        in_specs=[spec(sec, g) for sec in range(3) for g in range(N_DIL_GROUPS)],
        out_specs=pl.BlockSpec((None, SEQ, HEAD_DIM), lambda b, h, s_ref: (b, 0, h)),
        scratch_shapes=[pltpu.VMEM((N_DIL_GROUPS, SEQ, HEAD_DIM), jnp.float32),
                        pltpu.VMEM((N_DIL_GROUPS, SEQ, HEAD_DIM), jnp.float32)],
    )
    return pl.pallas_call(
        _dilattn_kernel,
        grid_spec=grid_spec,
        out_shape=jax.ShapeDtypeStruct((BATCH, SEQ, DIL_OUT_DIM), jnp.bfloat16),
        compiler_params=_params(("parallel", "arbitrary")),
        name="dil_attn",
    )(slopes, *([qkv] * 9))


MERGE_TM = 1024
MERGE_TN = 512


def _merge_kernel(x_ref, wg0_ref, wg1_ref, wg2_ref, bg0_ref, bg1_ref, bg2_ref,
                  sc_ref, od_ref, om_ref, wco_ref, wdo_ref, wmo_ref, o_ref):
    x = x_ref[...]
    g0 = jax.nn.sigmoid(_dot(x, wg0_ref[...]) + bg0_ref[...])
    acc = g0 * _dot(sc_ref[...], wco_ref[...])
    g1 = jax.nn.sigmoid(_dot(x, wg1_ref[...]) + bg1_ref[...])
    acc = acc + g1 * _dot(od_ref[...], wdo_ref[...])
    g2 = jax.nn.sigmoid(_dot(x, wg2_ref[...]) + bg2_ref[...])
    acc = acc + g2 * _dot(om_ref[...], wmo_ref[...])
    o_ref[...] = acc.astype(o_ref.dtype)


def _merge(x2, w_in, b_in, s_conv, o_dil, o_mem, w_co, w_do, w_mo):
    nb = lambda br: (OFF_GATE + br * D_MODEL) // MERGE_TN
    gspec = lambda br: pl.BlockSpec((D_MODEL, MERGE_TN), lambda i, n, o=nb(br): (0, o + n))
    bspec = lambda br: pl.BlockSpec((1, MERGE_TN), lambda i, n, o=nb(br): (0, o + n))
    act = lambda width: pl.BlockSpec((MERGE_TM, width), lambda i, n: (i, 0))
    wout = lambda width: pl.BlockSpec((width, MERGE_TN), lambda i, n: (0, n))
    return pl.pallas_call(
        _merge_kernel,
        grid=(TOKENS // MERGE_TM, D_MODEL // MERGE_TN),
        in_specs=[act(D_MODEL), gspec(0), gspec(1), gspec(2), bspec(0), bspec(1), bspec(2),
                  act(CONV_DIM), act(DIL_OUT_DIM), act(MEM_DIM),
                  wout(CONV_DIM), wout(DIL_OUT_DIM), wout(MEM_DIM)],
        out_specs=pl.BlockSpec((MERGE_TM, MERGE_TN), lambda i, n: (i, n)),
        out_shape=jax.ShapeDtypeStruct((TOKENS, D_MODEL), jnp.bfloat16),
        compiler_params=_params(("parallel", "arbitrary")),
        name="gated_merge",
    )(x2, w_in, w_in, w_in, b_in, b_in, b_in, s_conv, o_dil, o_mem, w_co, w_do, w_mo)


OPROJ_TM = 512


OPROJ_PARTS = 2


def _oproj_kernel(x_ref, m_ref, wo_ref, g_ref, b_ref, wr_ref, br_ref, x1_ref, logit_ref):
    rows = OPROJ_TM // OPROJ_PARTS
    parts = [pl.ds(i * rows, rows) for i in range(OPROJ_PARTS)]
    proj = [_dot(m_ref[p, :], wo_ref[...]) for p in parts]
    for p, y in zip(parts, proj):
        x1 = _layer_norm(ALPHA * x_ref[p, :] + y, g_ref[...], b_ref[...])
        x1_ref[p, :] = x1
        logit_ref[p, :] = _dot(x1, wr_ref[...]) + br_ref[...]


def _oproj(x2, merged, w_o, g, b, w_route, b_route):
    row = lambda width: pl.BlockSpec((OPROJ_TM, width), lambda i: (i, 0))
    full = lambda r, c: pl.BlockSpec((r, c), lambda i: (0, 0))
    return pl.pallas_call(
        _oproj_kernel,
        grid=(TOKENS // OPROJ_TM,),
        in_specs=[row(D_MODEL), row(D_MODEL), full(D_MODEL, D_MODEL), full(1, D_MODEL), full(1, D_MODEL),
                  full(D_MODEL, LANES), full(1, LANES)],
        out_specs=[row(D_MODEL), row(LANES)],
        out_shape=[jax.ShapeDtypeStruct((TOKENS, D_MODEL), jnp.float32),
                   jax.ShapeDtypeStruct((TOKENS, LANES), jnp.float32)],
        compiler_params=_params(("parallel",)),
        name="oproj_ln1",
    )(x2, merged, w_o, g, b, w_route, b_route)


def _route_kernel(logit_ref, o_ref, cnt_ref, carry_sc):
    @pl.when(pl.program_id(0) == 0)
    def _():
        carry_sc[...] = jnp.zeros_like(carry_sc)

    tm = ROUTE_TILE
    z = logit_ref[...]
    lane = lax.broadcasted_iota(jnp.int32, (tm, LANES), 1)
    neg = -jnp.inf
    first = lambda hit: jnp.min(jnp.where(hit, lane, LANES), axis=-1, keepdims=True)

    glog = jnp.where(lane < N_GROUPS, z, neg)
    gmax = jnp.max(glog, axis=-1, keepdims=True)
    gsel = first(glog == gmax)
    gw = 1.0 / jnp.sum(jnp.exp(glog - gmax), axis=-1, keepdims=True)

    e_lane = lane - ROUTER_LANE0
    in_group = (e_lane >= 0) & (e_lane < N_EXPERTS) & ((e_lane // EXPERTS_PER_GROUP) == gsel)
    v = jnp.where(in_group, z, neg)
    top1 = jnp.max(v, axis=-1, keepdims=True)
    i1 = first(v == top1)
    v2 = jnp.where(lane == i1, neg, v)
    top2 = jnp.max(v2, axis=-1, keepdims=True)
    i2 = first(v2 == top2)
    t = jnp.exp(top2 - top1)
    w1 = gw / (1.0 + t)
    w2 = w1 * t

    onehot = ((lane == i1) | (lane == i2))
    ri = lax.broadcasted_iota(jnp.int32, (tm, tm), 0)
    ci = lax.broadcasted_iota(jnp.int32, (tm, tm), 1)
    tri = jnp.where(ci <= ri, 1.0, 0.0).astype(jnp.bfloat16)
    cum = _dot(tri, jnp.where(onehot, 1.0, 0.0).astype(jnp.bfloat16))
    carry = carry_sc[0:1, :]
    before = cum + carry - 1.0
    rank1 = jnp.sum(jnp.where(lane == i1, before, 0.0), axis=-1, keepdims=True)
    rank2 = jnp.sum(jnp.where(lane == i2, before, 0.0), axis=-1, keepdims=True)
    new_carry = carry + cum[tm - 1:tm, :]
    carry_sc[...] = jnp.broadcast_to(new_carry, carry_sc.shape)
    cnt_ref[...] = jnp.broadcast_to(new_carry, cnt_ref.shape)

    cols = [(i1 - ROUTER_LANE0).astype(jnp.float32), (i2 - ROUTER_LANE0).astype(jnp.float32),
            rank1, rank2, w1, w2]
    out = jnp.zeros((tm, LANES), jnp.float32)
    for k, col in enumerate(cols):
        out = jnp.where(lane == k, col, out)
    o_ref[...] = out


def _route(logits):
    return pl.pallas_call(
        _route_kernel,
        grid=(TOKENS // ROUTE_TILE,),
        in_specs=[pl.BlockSpec((ROUTE_TILE, LANES), lambda i: (i, 0))],
        out_specs=[pl.BlockSpec((ROUTE_TILE, LANES), lambda i: (i, 0)),
                   pl.BlockSpec((8, LANES), lambda i: (0, 0))],
        out_shape=[jax.ShapeDtypeStruct((TOKENS, LANES), jnp.float32),
                   jax.ShapeDtypeStruct((8, LANES), jnp.float32)],
        scratch_shapes=[pltpu.VMEM((8, LANES), jnp.float32)],
        compiler_params=_params(("arbitrary",)),
        name="route",
    )(logits)


def _slots_kernel(route_ref, cnt_ref, pos_ref):
    lane8 = lax.broadcasted_iota(jnp.int32, (8, LANES), 1)
    cnt = cnt_ref[...]
    padded = jnp.ceil(cnt * (1.0 / EXPERT_TILE)) * EXPERT_TILE
    incl = padded
    for sh in (1, 2, 4, 8, 16, 32, 64):
        incl = incl + jnp.where(lane8 >= sh, pltpu.roll(incl, sh, axis=1), 0.0)
    start = (incl - padded)[0:1, :]

    r = route_ref[...]
    lane = lax.broadcasted_iota(jnp.int32, r.shape, 1).astype(jnp.float32)
    pick = lambda e: jnp.sum(jnp.where(lane == e + float(ROUTER_LANE0), start, 0.0), axis=-1, keepdims=True)
    pos_a = pick(r[:, 0:1]) + r[:, 2:3]
    pos_b = pick(r[:, 1:2]) + r[:, 3:4]
    pos_ref[...] = jnp.where(lane == 0.0, pos_a, jnp.where(lane == 1.0, pos_b, 0.0)).astype(jnp.int32)


def _slots(routing, counts8):
    return pl.pallas_call(
        _slots_kernel,
        grid=(TOKENS // ROUTE_TILE,),
        in_specs=[pl.BlockSpec((ROUTE_TILE, LANES), lambda i: (i, 0)),
                  pl.BlockSpec((8, LANES), lambda i: (0, 0))],
        out_specs=pl.BlockSpec((ROUTE_TILE, LANES), lambda i: (i, 0)),
        out_shape=jax.ShapeDtypeStruct((TOKENS, LANES), jnp.int32),
        compiler_params=_params(("parallel",)),
        name="slots",
    )(routing, counts8)


SUBLANES = 8
DISPATCH_TILE = 512
SORTED_ROWS = N_EXPERT_TILES * EXPERT_TILE


PACKED = D_MODEL // 2


def _pack_rows(v):
    words = pltpu.pack_elementwise([v[:, :PACKED], v[:, PACKED:]], packed_dtype=jnp.bfloat16)
    return lax.bitcast_convert_type(words, jnp.uint32)


def _unpack_rows(w):
    half = lambda i: pltpu.unpack_elementwise(w, index=i, packed_dtype=jnp.bfloat16,
                                              unpacked_dtype=jnp.float32)
    return half(0), half(1)


def _dispatch_kernel(pad_start_ref, pad_n_ref, nvalid_ref, pos_ref, x_ref, xs_hbm, zero_sc, pk_sc, sem,
                     row_sem):
    groups = DISPATCH_TILE // SUBLANES

    @pl.when(pl.program_id(0) == 0)
    def _():
        zero_sc[...] = jnp.zeros_like(zero_sc)

        def tile_copy(t):
            return pltpu.make_async_copy(zero_sc, xs_hbm.at[pl.ds(pl.multiple_of(t * EXPERT_TILE, EXPERT_TILE),
                                                                  EXPERT_TILE)], sem.at[2])

        def start_tile(t, c):
            tile_copy(t).start()
            return c

        def wait_tile(t, c):
            tile_copy(t).wait()
            return c
        lax.fori_loop(nvalid_ref[0], N_EXPERT_TILES, start_tile, 0)
        lax.fori_loop(nvalid_ref[0], N_EXPERT_TILES, wait_tile, 0)

        def pad_copies(e, act):
            n = pad_n_ref[e]
            cur = pad_start_ref[e]
            for bit in range(EXPERT_TILE.bit_length() - 1):
                size = 1 << bit
                has = lax.bitwise_and(n, size)

                @pl.when(has != 0)
                def _(cur=cur, size=size):
                    act(pltpu.make_async_copy(zero_sc.at[pl.ds(0, size)],
                                              xs_hbm.at[pl.ds(pl.multiple_of(cur, size), size)], sem.at[1]))
                cur = cur + has

        def start_pads(e, c):
            pad_copies(e, lambda cp: cp.start())
            return c

        def wait_pads(e, c):
            pad_copies(e, lambda cp: cp.wait())
            return c
        lax.fori_loop(0, N_EXPERTS, start_pads, 0)
        lax.fori_loop(0, N_EXPERTS, wait_pads, 0)

    j = pl.program_id(0)
    slot = lax.rem(j, 2)
    pk_sc[slot] = _pack_rows(x_ref[...]).reshape(groups, SUBLANES, PACKED)

    def body(gi, c):
        for s in range(SUBLANES):
            for k in range(2):
                p = pos_ref[0, 0, (gi * SUBLANES + s) * 2 + k]
                pltpu.make_async_copy(pk_sc.at[slot, gi, pl.ds(s, 1)], xs_hbm.at[pl.ds(p, 1)],
                                      row_sem.at[slot]).start(priority=k)
        return c
    lax.fori_loop(0, groups, body, 0)

    def wait_slot(sl):
        for _ in range(2 * groups):
            pltpu.make_async_copy(pk_sc.at[sl, 0], xs_hbm.at[pl.ds(0, SUBLANES)], row_sem.at[sl]).wait()

    @pl.when(j > 0)
    def _():
        wait_slot(1 - slot)

    @pl.when(j == pl.num_programs(0) - 1)
    def _():
        wait_slot(slot)


def _dispatch(pad_start, pad_n, n_valid_tiles, pos3, x1):
    nt = TOKENS // DISPATCH_TILE
    groups = DISPATCH_TILE // SUBLANES
    grid_spec = pltpu.PrefetchScalarGridSpec(
        num_scalar_prefetch=3,
        grid=(nt,),
        in_specs=[pl.BlockSpec((1, 1, 2 * DISPATCH_TILE), lambda j, *_: (j, 0, 0), memory_space=pltpu.SMEM),
                  pl.BlockSpec((DISPATCH_TILE, D_MODEL), lambda j, *_: (j, 0))],
        out_specs=pl.BlockSpec(memory_space=pl.ANY),
        scratch_shapes=[pltpu.VMEM((EXPERT_TILE, PACKED), jnp.uint32),
                        pltpu.VMEM((2, groups, SUBLANES, PACKED), jnp.uint32),
                        pltpu.SemaphoreType.DMA((3,)),
                        pltpu.SemaphoreType.DMA((2,))],
    )
    return pl.pallas_call(
        _dispatch_kernel,
        grid_spec=grid_spec,
        out_shape=jax.ShapeDtypeStruct((SORTED_ROWS, PACKED), jnp.uint32),
        compiler_params=_params(("arbitrary",)),
        name="dispatch",
    )(pad_start, pad_n, n_valid_tiles, pos3, x1)


def _ffn_kernel(texp_ref, tvalid_ref, tslot_ref, tfirst_ref, tnext_ref, nvalid_ref,
                xs_ref, wg_hbm, wu_hbm, wd_hbm, y_ref, wg_buf, wu_buf, wd_buf, sem):
    j = pl.program_id(0)
    valid = tvalid_ref[j] == 1
    slot = tslot_ref[j]

    def weight_copies(e, s):
        return (pltpu.make_async_copy(wg_hbm.at[e], wg_buf.at[s], sem.at[s, 0]),
                pltpu.make_async_copy(wu_hbm.at[e], wu_buf.at[s], sem.at[s, 1]),
                pltpu.make_async_copy(wd_hbm.at[e], wd_buf.at[s], sem.at[s, 2]))

    @pl.when(valid & (tfirst_ref[j] == 1))
    def _():
        @pl.when(j == 0)
        def _():
            for cp in weight_copies(texp_ref[j], slot):
                cp.start()

        for cp in weight_copies(texp_ref[j], slot):
            cp.wait()

        @pl.when(tnext_ref[j] >= 0)
        def _():
            for cp in weight_copies(tnext_ref[j], 1 - slot):
                cp.start()

    @pl.when(valid)
    def _():
        lo, hi = _unpack_rows(xs_ref[...])
        a = _dot(lo, wg_buf[slot, 0:PACKED, :]) + _dot(hi, wg_buf[slot, PACKED:D_MODEL, :])
        u = _dot(lo, wu_buf[slot, 0:PACKED, :]) + _dot(hi, wu_buf[slot, PACKED:D_MODEL, :])
        hid = (a * jax.nn.sigmoid(a)) * u
        y_ref[...] = _pack_rows(_dot(hid, wd_buf[slot]))

    @pl.when(jnp.logical_not(valid))
    def _():
        y_ref[...] = jnp.zeros_like(y_ref)


def _expert_ffn(tables, xs, w_gate, w_up, w_down):
    nt = N_EXPERT_TILES
    xmap = lambda j, te, tv, ts, tf, tn, nv: (jnp.minimum(j, nv[0] - 1), 0)
    grid_spec = pltpu.PrefetchScalarGridSpec(
        num_scalar_prefetch=6,
        grid=(nt,),
        in_specs=[pl.BlockSpec((EXPERT_TILE, PACKED), xmap),
                  pl.BlockSpec(memory_space=pl.ANY),
                  pl.BlockSpec(memory_space=pl.ANY),
                  pl.BlockSpec(memory_space=pl.ANY)],
        out_specs=pl.BlockSpec((EXPERT_TILE, PACKED), lambda j, *_: (j, 0)),
        scratch_shapes=[pltpu.VMEM((2, D_MODEL, D_EXPERT), jnp.float32),
                        pltpu.VMEM((2, D_MODEL, D_EXPERT), jnp.float32),
                        pltpu.VMEM((2, D_EXPERT, D_MODEL), jnp.float32),
                        pltpu.SemaphoreType.DMA((2, 3))],
    )
    return pl.pallas_call(
        _ffn_kernel,
        grid_spec=grid_spec,
        out_shape=jax.ShapeDtypeStruct((nt * EXPERT_TILE, PACKED), jnp.uint32),
        compiler_params=_params(("arbitrary",)),
        name="expert_ffn",
    )(*tables, xs, w_gate, w_up, w_down)


def _combine_kernel(pos_cur_ref, pos_next_ref, r_ref, x1_ref, y_hbm, g_ref, b_ref, o_ref, ybuf, sem):
    j = pl.program_id(0)
    nt = pl.num_programs(0)
    slot = lax.rem(j, 2)
    groups = COMBINE_TILE // SUBLANES

    def gather(pos_ref, dst_slot):
        def body(gi, c):
            for s in range(SUBLANES):
                for k in range(2):
                    p = pos_ref[0, 0, (gi * SUBLANES + s) * 2 + k]
                    pltpu.make_async_copy(y_hbm.at[pl.ds(p, 1)], ybuf.at[dst_slot, k, gi, pl.ds(s, 1)],
                                          sem.at[dst_slot]).start(priority=k)
            return c
        lax.fori_loop(0, groups, body, 0)

    @pl.when(j == 0)
    def _():
        gather(pos_cur_ref, 0)

    @pl.when(j + 1 < nt)
    def _():
        gather(pos_next_ref, 1 - slot)

    for _ in range(2 * groups):
        pltpu.make_async_copy(y_hbm.at[pl.ds(0, SUBLANES)], ybuf.at[slot, 0, 0], sem.at[slot]).wait()
    r = r_ref[...]
    unpack = lambda k: jnp.concatenate(_unpack_rows(ybuf[slot, k].reshape(COMBINE_TILE, PACKED)), axis=1)
    ya, yb = unpack(0), unpack(1)
    hres = ALPHA * x1_ref[...] + r[:, 4:5] * ya + r[:, 5:6] * yb
    o_ref[...] = _layer_norm(hres, g_ref[...], b_ref[...])


def _combine(pos3, routing, x1, y_sorted, g, b):
    nt = TOKENS // COMBINE_TILE
    groups = COMBINE_TILE // SUBLANES
    return pl.pallas_call(
        _combine_kernel,
        grid=(nt,),
        in_specs=[pl.BlockSpec((1, 1, 2 * COMBINE_TILE), lambda j: (j, 0, 0), memory_space=pltpu.SMEM),
                  pl.BlockSpec((1, 1, 2 * COMBINE_TILE), lambda j: (jnp.minimum(j + 1, nt - 1), 0, 0),
                               memory_space=pltpu.SMEM),
                  pl.BlockSpec((COMBINE_TILE, LANES), lambda j: (j, 0)),
                  pl.BlockSpec((COMBINE_TILE, D_MODEL), lambda j: (j, 0)),
                  pl.BlockSpec(memory_space=pl.ANY),
                  pl.BlockSpec((1, D_MODEL), lambda j: (0, 0)),
                  pl.BlockSpec((1, D_MODEL), lambda j: (0, 0))],
        out_specs=pl.BlockSpec((COMBINE_TILE, D_MODEL), lambda j: (j, 0)),
        out_shape=jax.ShapeDtypeStruct((TOKENS, D_MODEL), jnp.float32),
        scratch_shapes=[pltpu.VMEM((2, 2, groups, SUBLANES, PACKED), jnp.uint32),
                        pltpu.SemaphoreType.DMA((2,))],
        compiler_params=_params(("arbitrary",)),
        name="combine_ln2",
    )(pos3, pos3, routing, x1, y_sorted, g, b)


def _alibi_slopes():
    n = N_DIL_GROUPS * HEADS_PER_GROUP
    return jnp.asarray(2.0 ** (-ALIBI_MAX * np.arange(1, n + 1, dtype=np.float32) / n), jnp.float32)


def kernel(x, mem, ln_mem_g, ln_mem_b, w_in, b_in, w_conv, w_conv_out, w_dil_out, w_mem_kv, w_mem_out, w_o, ln1_g, ln1_b, w_group, b_group, w_router, b_router, w_gate, w_up, w_down, ln2_g, ln2_b):
    assert x.shape == (BATCH, SEQ, D_MODEL) and w_in.shape == (1, D_MODEL, IN_DIM)
    bf16 = jnp.bfloat16
    row = lambda v: v.reshape(1, -1)
    w_in2 = w_in[0].astype(bf16)
    b_in2 = b_in

    kv = _memkv(mem, row(ln_mem_g), row(ln_mem_b), w_mem_kv[0].astype(bf16))
    s_conv, xb = _conv_branch(x, w_in2, b_in2, w_conv[0])
    qkv = _qkv_proj(xb, w_in2, b_in2)
    o_mem = _mem_branch(xb, w_in2, b_in2, kv)
    o_dil = _dil_branch(qkv, _alibi_slopes())

    x2 = x.reshape(TOKENS, D_MODEL)
    merged = _merge(xb.reshape(TOKENS, D_MODEL), w_in2, b_in2,
                    s_conv.reshape(TOKENS, CONV_DIM), o_dil.reshape(TOKENS, DIL_OUT_DIM),
                    o_mem.reshape(TOKENS, MEM_DIM),
                    w_conv_out[0].astype(bf16), w_dil_out[0].astype(bf16), w_mem_out[0].astype(bf16))

    w_route = jnp.concatenate(
        [w_group[0], jnp.transpose(w_router[0], (1, 0, 2)).reshape(D_MODEL, N_EXPERTS),
         jnp.zeros((D_MODEL, LANES - N_GROUPS - N_EXPERTS), jnp.float32)], axis=1)
    b_route = jnp.concatenate(
        [b_group[0], b_router[0].reshape(N_EXPERTS),
         jnp.zeros((LANES - N_GROUPS - N_EXPERTS,), jnp.float32)]).reshape(1, LANES)
    x1, logits = _oproj(x2, merged, w_o[0].astype(bf16), ln1_g, ln1_b, w_route, b_route)

    routing, counts8 = _route(logits)
    pos = _slots(routing, counts8)[:, 0:2]

    i32 = jnp.int32
    counts = counts8[0, ROUTER_LANE0:ROUTER_LANE0 + N_EXPERTS].astype(i32)
    padded = ((counts + EXPERT_TILE - 1) // EXPERT_TILE) * EXPERT_TILE
    ends = jnp.cumsum(padded)
    starts = ends - padded
    tile_start = jnp.arange(N_EXPERT_TILES, dtype=i32) * EXPERT_TILE
    tile_expert = jnp.minimum(jnp.sum((ends[None, :] <= tile_start[:, None]).astype(i32), axis=1),
                              N_EXPERTS - 1)
    tile_valid = tile_start < ends[-1]
    prev_expert = jnp.concatenate([jnp.full((1,), -1, i32), tile_expert[:-1]])
    tile_first = tile_valid & (tile_expert != prev_expert)
    tile_slot = (jnp.cumsum(tile_first.astype(i32)) - 1) & 1
    big = N_EXPERTS
    idx = jnp.where(counts > 0, jnp.arange(N_EXPERTS, dtype=i32), big)
    later = jnp.concatenate([lax.cummin(idx[::-1])[::-1][1:], jnp.full((1,), big, i32)])
    next_used = jnp.where(later == big, -1, later)
    n_valid_tiles = (ends[-1:] // EXPERT_TILE).astype(i32)
    tables = (tile_expert, tile_valid.astype(i32), tile_slot, tile_first.astype(i32),
              next_used[tile_expert], n_valid_tiles)

    xs = _dispatch(starts + counts, padded - counts, n_valid_tiles,
                   pos.reshape(TOKENS // DISPATCH_TILE, 1, 2 * DISPATCH_TILE), x1)
    y_sorted = _expert_ffn(tables, xs,
                           w_gate.reshape(N_EXPERTS, D_MODEL, D_EXPERT),
                           w_up.reshape(N_EXPERTS, D_MODEL, D_EXPERT),
                           w_down.reshape(N_EXPERTS, D_EXPERT, D_MODEL))
    out = _combine(pos.reshape(TOKENS // COMBINE_TILE, 1, 2 * COMBINE_TILE), routing, x1, y_sorted,
                   ln2_g, ln2_b)
    return out.reshape(BATCH, SEQ, D_MODEL)
```

```python
import functools
import math

import numpy as np
import jax
import jax.numpy as jnp
from jax import lax
from jax.experimental import pallas as pl
from jax.experimental.pallas import tpu as pltpu

D_MODEL = 2048
BATCH = 8
SEQ = 2048
TOKENS = BATCH * SEQ
CONV_DIM = 1024
CONV_WIDTH = 3
DIL_PATTERNS = ((128, 1), (512, 4), (2048, 16))
N_DIL_GROUPS = 3
HEADS_PER_GROUP = 4
HEAD_DIM = 128
DIL_DIM = N_DIL_GROUPS * HEADS_PER_GROUP * HEAD_DIM
DIL_OUT_DIM = HEADS_PER_GROUP * HEAD_DIM
ATT_BLOCK = 128
ALIBI_MAX = 8.0
MEM_LEN = 256
MEM_HEADS = 4
MEM_HEAD_DIM = 256
MEM_DIM = MEM_HEADS * MEM_HEAD_DIM
N_BRANCHES = 3
IN_DIM = 3 * CONV_DIM + 3 * DIL_DIM + MEM_DIM + N_BRANCHES * D_MODEL
N_GROUPS = 4
EXPERTS_PER_GROUP = 8
N_EXPERTS = N_GROUPS * EXPERTS_PER_GROUP
D_EXPERT = 512
ALPHA = 2.0 ** 0.25
LN_EPS = 1e-5

OFF_CB = 0
OFF_CC = CONV_DIM
OFF_CH = 2 * CONV_DIM
OFF_Q = 3 * CONV_DIM
OFF_MQ = OFF_Q + 3 * DIL_DIM
OFF_GATE = OFF_MQ + MEM_DIM

LANES = 128
HALF = SEQ // 2
VMEM_LIMIT = 56 * 1024 * 1024

ROUTE_TILE = 1024
EXPERT_TILE = 512
N_EXPERT_TILES = 2 * TOKENS // EXPERT_TILE + N_EXPERTS
COMBINE_TILE = 256
ROUTER_LANE0 = N_GROUPS


def _params(sem, limit=VMEM_LIMIT):
    return pltpu.CompilerParams(dimension_semantics=sem, vmem_limit_bytes=limit)


def _layer_norm(x, g, b):
    mu = jnp.mean(x, axis=-1, keepdims=True)
    xc = x - mu
    var = jnp.mean(xc * xc, axis=-1, keepdims=True)
    return xc * lax.rsqrt(var + LN_EPS) * g + b


def _dot(a, b):
    return jnp.dot(a, b, preferred_element_type=jnp.float32)


def _dot_t(a, b):
    return lax.dot_general(a, b, (((1,), (1,)), ((), ())), preferred_element_type=jnp.float32)


MEMKV_BATCHES = 2


def _memkv_kernel(mem_ref, g_ref, b_ref, w_ref, kv_ref):
    rows = MEMKV_BATCHES * MEM_LEN
    y = _layer_norm(mem_ref[...].reshape(rows, D_MODEL), g_ref[...], b_ref[...])
    kv = _dot(y.astype(jnp.bfloat16), w_ref[...])
    kv_ref[...] = kv.astype(kv_ref.dtype).reshape(MEMKV_BATCHES, MEM_LEN, 2 * MEM_DIM)


def _memkv(mem, g, b, w):
    return pl.pallas_call(
        _memkv_kernel,
        grid=(BATCH // MEMKV_BATCHES,),
        in_specs=[pl.BlockSpec((MEMKV_BATCHES, MEM_LEN, D_MODEL), lambda i: (i, 0, 0)),
                  pl.BlockSpec((1, D_MODEL), lambda i: (0, 0)),
                  pl.BlockSpec((1, D_MODEL), lambda i: (0, 0)),
                  pl.BlockSpec((D_MODEL, 2 * MEM_DIM), lambda i: (0, 0))],
        out_specs=pl.BlockSpec((MEMKV_BATCHES, MEM_LEN, 2 * MEM_DIM), lambda i: (i, 0, 0)),
        out_shape=jax.ShapeDtypeStruct((BATCH, MEM_LEN, 2 * MEM_DIM), jnp.bfloat16),
        compiler_params=_params(("parallel",)),
        name="mem_kv",
    )(mem, g, b, w)


CONV_TC = 512


def _conv_kernel(x_ref, wb_ref, wc_ref, wh_ref, bb_ref, bc_ref, bh_ref, wconv_ref, s_ref, xb_ref,
                 u_sc, carry_sc):
    half = pl.program_id(1)
    c = pl.program_id(2)

    @pl.when(c == 0)
    def _():
        xb_ref[...] = x_ref[...].astype(xb_ref.dtype)

    x = xb_ref[...]
    cc = _dot(x, wc_ref[...]) + bc_ref[...]
    ch = _dot(x, wh_ref[...]) + bh_ref[...]
    u = cc * ch
    u_sc[0:8, :] = jnp.where(half == 0, 0.0, carry_sc[c])
    u_sc[8:8 + HALF, :] = u
    carry_sc[c] = u[HALF - 8:HALF, :]
    wconv = wconv_ref[...]
    y = (wconv[2:3, :] * u
         + wconv[1:2, :] * u_sc[7:7 + HALF, :]
         + wconv[0:1, :] * u_sc[6:6 + HALF, :])
    cb = _dot(x, wb_ref[...]) + bb_ref[...]
    s_ref[...] = (cb * y).astype(s_ref.dtype)


def _conv_branch(x, w_in, b_in, w_conv):
    nb = lambda off: off // CONV_TC
    wspec = lambda off: pl.BlockSpec((D_MODEL, CONV_TC), lambda b, h, c, o=nb(off): (0, o + c))
    bspec = lambda off: pl.BlockSpec((1, CONV_TC), lambda b, h, c, o=nb(off): (0, o + c))
    return pl.pallas_call(
        _conv_kernel,
        grid=(BATCH, 2, CONV_DIM // CONV_TC),
        in_specs=[pl.BlockSpec((None, HALF, D_MODEL), lambda b, h, c: (b, h, 0)),
                  wspec(OFF_CB), wspec(OFF_CC), wspec(OFF_CH),
                  bspec(OFF_CB), bspec(OFF_CC), bspec(OFF_CH),
                  pl.BlockSpec((CONV_WIDTH, CONV_TC), lambda b, h, c: (0, c))],
        out_specs=[pl.BlockSpec((None, HALF, CONV_TC), lambda b, h, c: (b, h, c)),
                   pl.BlockSpec((None, HALF, D_MODEL), lambda b, h, c: (b, h, 0))],
        out_shape=[jax.ShapeDtypeStruct((BATCH, SEQ, CONV_DIM), jnp.bfloat16),
                   jax.ShapeDtypeStruct((BATCH, SEQ, D_MODEL), jnp.bfloat16)],
        scratch_shapes=[pltpu.VMEM((HALF + 8, CONV_TC), jnp.float32),
                        pltpu.VMEM((CONV_DIM // CONV_TC, 8, CONV_TC), jnp.float32)],
        compiler_params=_params(("arbitrary", "arbitrary", "arbitrary")),
        name="conv_branch",
    )(x, w_in, w_in, w_in, b_in, b_in, b_in, w_conv)


QKV_TN = 512
QKV_CHUNKS = QKV_TN // LANES


def _qkv_kernel(x_ref, w_ref, b_ref, o_ref, sc_ref, sc2_ref):
    x = x_ref[...]
    for gi in (2, 1, 0):
        cols = slice(gi * QKV_TN, (gi + 1) * QKV_TN)
        acc = _dot(x, w_ref[:, cols]) + b_ref[:, cols]
        d = DIL_PATTERNS[gi][1]
        if d == 1:
            o_ref[:, cols] = acc.astype(o_ref.dtype)
            continue
        rows = HALF // d
        for c in range(QKV_CHUNKS):
            sc_ref[gi - 1, c] = acc[:, c * LANES:(c + 1) * LANES]
        if d == 16:
            q4 = HALF // 4
            for c in range(QKV_CHUNKS):
                for r in range(4):
                    sc2_ref[c, r * q4:(r + 1) * q4, :] = sc_ref[gi - 1, c, pl.ds(r, q4, stride=4), :]
            for c in range(QKV_CHUNKS):
                lo = gi * QKV_TN + c * LANES
                for r in range(d):
                    r_lo, r_hi = r % 4, r // 4
                    o_ref[r * rows:(r + 1) * rows, lo:lo + LANES] = (
                        sc2_ref[c, pl.ds(r_lo * q4 + r_hi, rows, stride=4), :].astype(o_ref.dtype))
            continue
        for c in range(QKV_CHUNKS):
            for r in range(d):
                lo = gi * QKV_TN + c * LANES
                o_ref[r * rows:(r + 1) * rows, lo:lo + LANES] = (
                    sc_ref[gi - 1, c, pl.ds(r, rows, stride=d), :].astype(o_ref.dtype))


def _qkv_proj(xb, w_in, b_in):
    n0 = OFF_Q // DIL_DIM
    return pl.pallas_call(
        _qkv_kernel,
        grid=(BATCH, 2, 3),
        in_specs=[pl.BlockSpec((None, HALF, D_MODEL), lambda b, h, n: (b, h, 0)),
                  pl.BlockSpec((D_MODEL, DIL_DIM), lambda b, h, n: (0, n0 + n)),
                  pl.BlockSpec((1, DIL_DIM), lambda b, h, n: (0, n0 + n))],
        out_specs=pl.BlockSpec((None, HALF, DIL_DIM), lambda b, h, n: (b, h, n)),
        out_shape=jax.ShapeDtypeStruct((BATCH, SEQ, 3 * DIL_DIM), jnp.bfloat16),
        scratch_shapes=[pltpu.VMEM((N_DIL_GROUPS - 1, QKV_CHUNKS, HALF, LANES), jnp.float32),
                        pltpu.VMEM((QKV_CHUNKS, HALF, LANES), jnp.float32)],
        compiler_params=_params(("parallel", "parallel", "arbitrary")),
        name="qkv_proj",
    )(xb, w_in, b_in)


MEM_HEADS_PER_STEP = 2
MEM_TN = MEM_HEADS_PER_STEP * MEM_HEAD_DIM


def _memattn_kernel(x_ref, w_ref, b_ref, mk_ref, mv_ref, o_ref):
    mq_all = (_dot(x_ref[...], w_ref[...]) + b_ref[...]).astype(jnp.bfloat16)
    for hh in range(MEM_HEADS_PER_STEP):
        cols = slice(hh * MEM_HEAD_DIM, (hh + 1) * MEM_HEAD_DIM)
        s = _dot_t(mq_all[:, cols], mk_ref[:, cols]) * (MEM_HEAD_DIM ** -0.5)
        m = jnp.max(s, axis=-1, keepdims=True)
        p = jnp.exp(s - m)
        den = jnp.sum(p, axis=-1, keepdims=True)
        o = _dot(p.astype(jnp.bfloat16), mv_ref[:, cols]) / den
        o_ref[:, cols] = o.astype(o_ref.dtype)


def _mem_branch(xb, w_in, b_in, kv):
    n0 = OFF_MQ // MEM_TN
    nv = MEM_DIM // MEM_TN
    return pl.pallas_call(
        _memattn_kernel,
        grid=(BATCH, 2, MEM_DIM // MEM_TN),
        in_specs=[pl.BlockSpec((None, HALF, D_MODEL), lambda b, h, n: (b, h, 0)),
                  pl.BlockSpec((D_MODEL, MEM_TN), lambda b, h, n: (0, n0 + n)),
                  pl.BlockSpec((1, MEM_TN), lambda b, h, n: (0, n0 + n)),
                  pl.BlockSpec((None, MEM_LEN, MEM_TN), lambda b, h, n: (b, 0, n)),
                  pl.BlockSpec((None, MEM_LEN, MEM_TN), lambda b, h, n: (b, 0, nv + n))],
        out_specs=pl.BlockSpec((None, HALF, MEM_TN), lambda b, h, n: (b, h, n)),
        out_shape=jax.ShapeDtypeStruct((BATCH, SEQ, MEM_DIM), jnp.bfloat16),
        compiler_params=_params(("parallel", "parallel", "arbitrary")),
        name="mem_branch",
    )(xb, w_in, b_in, kv, kv)


ATT_UNROLL = 16


def _softmax_block(s, v):
    m = jnp.max(s, axis=-1, keepdims=True)
    p = jnp.exp(s - m)
    den = jnp.sum(p, axis=-1, keepdims=True)
    o = _dot(p.astype(jnp.bfloat16), v) / den
    return o, m + jnp.log(den)


def _dilattn_kernel(slopes_ref,
                    q0_ref, q1_ref, q2_ref, k0_ref, k1_ref, k2_ref, v0_ref, v1_ref, v2_ref,
                    o_ref, o_sc, l_sc):
    h = pl.program_id(1)
    blk = ATT_BLOCK
    scale = HEAD_DIM ** -0.5
    qi = lax.broadcasted_iota(jnp.int32, (blk, 2 * blk), 0) + blk
    kj = lax.broadcasted_iota(jnp.int32, (blk, 2 * blk), 1)
    jrel = qi - kj
    valid = (jrel >= 0) & (jrel <= blk)
    jrel_f = jrel.astype(jnp.float32)

    def bias_for(g):
        slope = slopes_ref[g * HEADS_PER_GROUP + h]
        d = float(DIL_PATTERNS[g][1])
        return jnp.where(valid, (-slope * d) * jrel_f, -jnp.inf)

    def put(g, row_slice, o, lse):
        o_sc[g, row_slice, :] = o
        l_sc[g, row_slice, :] = jnp.broadcast_to(lse, (blk, HEAD_DIM))

    def run_blocks(g, blocks):
        scores = [_dot_t(q, k) * scale + bias for q, k, _, bias, _ in blocks]
        stats = []
        for s in scores:
            m = jnp.max(s, axis=-1, keepdims=True)
            p = jnp.exp(s - m)
            stats.append((m, p, jnp.sum(p, axis=-1, keepdims=True)))
        outs = [_dot(p.astype(jnp.bfloat16), blkdef[2]) / den
                for (m, p, den), blkdef in zip(stats, blocks)]
        for o, (m, p, den), blkdef in zip(outs, stats, blocks):
            put(g, blkdef[4], o, m + jnp.log(den))

    bias0 = bias_for(0)
    prev_cols = kj < blk

    def g0_body(it, carry):
        blocks = []
        for k in range(ATT_UNROLL):
            n = it * ATT_UNROLL + k
            q0 = pl.multiple_of(n * blk, blk)
            k0 = pl.multiple_of(jnp.maximum(n - 1, 0) * blk, blk)
            bias = jnp.where(prev_cols & (n == 0), -jnp.inf, bias0)
            blocks.append((q0_ref[pl.ds(q0, blk), :], k0_ref[pl.ds(k0, 2 * blk), :],
                           v0_ref[pl.ds(k0, 2 * blk), :], bias, pl.ds(q0, blk)))
        run_blocks(0, blocks)
        return carry

    lax.fori_loop(0, SEQ // blk // ATT_UNROLL, g0_body, 0)

    d1 = DIL_PATTERNS[1][1]
    cls1 = HALF // d1
    per_half = cls1 // blk
    bias1 = bias_for(1)

    def row1(r, n):
        return (n // per_half) * HALF + r * cls1 + (n % per_half) * blk

    nblk1 = SEQ // d1 // blk
    cls_per_trip = ATT_UNROLL // nblk1

    def g1_body(it, carry):
        blocks = []
        for c in range(cls_per_trip):
            r = it * cls_per_trip + c
            for n in range(nblk1):
                cur = pl.multiple_of(row1(r, n), blk)
                q = q1_ref[pl.ds(cur, blk), :]
                dst = pl.ds(n * blk * d1 + r, blk, stride=d1)
                if n == 0:
                    blocks.append((q, k1_ref[pl.ds(cur, blk), :], v1_ref[pl.ds(cur, blk), :],
                                   bias1[:, blk:], dst))
                else:
                    prev = pl.multiple_of(row1(r, n - 1), blk)
                    kc = jnp.concatenate([k1_ref[pl.ds(prev, blk), :], k1_ref[pl.ds(cur, blk), :]], axis=0)
                    vc = jnp.concatenate([v1_ref[pl.ds(prev, blk), :], v1_ref[pl.ds(cur, blk), :]], axis=0)
                    blocks.append((q, kc, vc, bias1, dst))
        run_blocks(1, blocks)
        return carry

    lax.fori_loop(0, d1 // cls_per_trip, g1_body, 0)

    d2 = DIL_PATTERNS[2][1]
    cls2 = HALF // d2
    bias2 = bias_for(2)

    def g2_body(it, carry):
        blocks = []
        for k in range(ATT_UNROLL):
            r = it * ATT_UNROLL + k
            a = pl.multiple_of(r * cls2, cls2)
            b = pl.multiple_of(HALF + r * cls2, cls2)
            cat = lambda ref, a=a, b=b: jnp.concatenate(
                [ref[pl.ds(a, cls2), :], ref[pl.ds(b, cls2), :]], axis=0)
            blocks.append((cat(q2_ref), cat(k2_ref), cat(v2_ref), bias2[:, blk:],
                           pl.ds(r, blk, stride=d2)))
        run_blocks(2, blocks)
        return carry

    lax.fori_loop(0, d2 // ATT_UNROLL, g2_body, 0)

    rows = 256
    for t in range(SEQ // rows):
        sl = pl.ds(t * rows, rows)
        l0, l1, l2 = l_sc[0, sl, :], l_sc[1, sl, :], l_sc[2, sl, :]
        m = jnp.maximum(jnp.maximum(l0, l1), l2)
        e0, e1, e2 = jnp.exp(l0 - m), jnp.exp(l1 - m), jnp.exp(l2 - m)
        mix = (e0 * o_sc[0, sl, :] + e1 * o_sc[1, sl, :] + e2 * o_sc[2, sl, :]) / (e0 + e1 + e2)
        o_ref[sl, :] = mix.astype(o_ref.dtype)


def _dil_branch(qkv, slopes):
    nq = DIL_DIM // HEAD_DIM

    def spec(section, g):
        return pl.BlockSpec((None, SEQ, HEAD_DIM),
                            lambda b, h, s_ref, o=section * nq + g * HEADS_PER_GROUP: (b, 0, o + h))

    grid_spec = pltpu.PrefetchScalarGridSpec(
        num_scalar_prefetch=1,
        grid=(BATCH, HEADS_PER_GROUP),
        in_specs=[spec(sec, g) for sec in range(3) for g in range(N_DIL_GROUPS)],
        out_specs=pl.BlockSpec((None, SEQ, HEAD_DIM), lambda b, h, s_ref: (b, 0, h)),
        scratch_shapes=[pltpu.VMEM((N_DIL_GROUPS, SEQ, HEAD_DIM), jnp.float32),
                        pltpu.VMEM((N_DIL_GROUPS, SEQ, HEAD_DIM), jnp.float32)],
    )
    return pl.pallas_call(
        _dilattn_kernel,
        grid_spec=grid_spec,
        out_shape=jax.ShapeDtypeStruct((BATCH, SEQ, DIL_OUT_DIM), jnp.bfloat16),
        compiler_params=_params(("parallel", "arbitrary")),
        name="dil_attn",
    )(slopes, *([qkv] * 9))


MERGE_TM = 1024
MERGE_TN = 512


def _merge_kernel(x_ref, wg0_ref, wg1_ref, wg2_ref, bg0_ref, bg1_ref, bg2_ref,
                  sc_ref, od_ref, om_ref, wout_ref, o_ref):
    r1, r2 = CONV_DIM, CONV_DIM + DIL_OUT_DIM
    x = x_ref[...]
    g0 = jax.nn.sigmoid(_dot(x, wg0_ref[...]) + bg0_ref[...])
    acc = g0 * _dot(sc_ref[...], wout_ref[0:r1, :])
    g1 = jax.nn.sigmoid(_dot(x, wg1_ref[...]) + bg1_ref[...])
    acc = acc + g1 * _dot(od_ref[...], wout_ref[r1:r2, :])
    g2 = jax.nn.sigmoid(_dot(x, wg2_ref[...]) + bg2_ref[...])
    acc = acc + g2 * _dot(om_ref[...], wout_ref[r2:r2 + MEM_DIM, :])
    o_ref[...] = acc.astype(o_ref.dtype)


def _merge(x2, w_in, b_in, s_conv, o_dil, o_mem, w_out):
    nb = lambda br: (OFF_GATE + br * D_MODEL) // MERGE_TN
    gspec = lambda br: pl.BlockSpec((D_MODEL, MERGE_TN), lambda i, n, o=nb(br): (0, o + n))
    bspec = lambda br: pl.BlockSpec((1, MERGE_TN), lambda i, n, o=nb(br): (0, o + n))
    act = lambda width: pl.BlockSpec((MERGE_TM, width), lambda i, n: (i, 0))
    wout = lambda width: pl.BlockSpec((width, MERGE_TN), lambda i, n: (0, n))
    return pl.pallas_call(
        _merge_kernel,
        grid=(TOKENS // MERGE_TM, D_MODEL // MERGE_TN),
        in_specs=[act(D_MODEL), gspec(0), gspec(1), gspec(2), bspec(0), bspec(1), bspec(2),
                  act(CONV_DIM), act(DIL_OUT_DIM), act(MEM_DIM),
                  wout(CONV_DIM + DIL_OUT_DIM + MEM_DIM)],
        out_specs=pl.BlockSpec((MERGE_TM, MERGE_TN), lambda i, n: (i, n)),
        out_shape=jax.ShapeDtypeStruct((TOKENS, D_MODEL), jnp.bfloat16),
        compiler_params=_params(("parallel", "arbitrary")),
        name="gated_merge",
    )(x2, w_in, w_in, w_in, b_in, b_in, b_in, s_conv, o_dil, o_mem, w_out)


OPROJ_TM = 512


OPROJ_PARTS = 2


def _oproj_kernel(x_ref, m_ref, wo_ref, g_ref, b_ref, wr_ref, br_ref, x1_ref, logit_ref):
    rows = OPROJ_TM // OPROJ_PARTS
    parts = [pl.ds(i * rows, rows) for i in range(OPROJ_PARTS)]
    proj = [_dot(m_ref[p, :], wo_ref[...]) for p in parts]
    for p, y in zip(parts, proj):
        x1 = _layer_norm(ALPHA * x_ref[p, :] + y, g_ref[...], b_ref[...])
        x1_ref[p, :] = x1
        logit_ref[p, :] = _dot(x1, wr_ref[...]) + br_ref[...]


def _oproj(x2, merged, w_o, g, b, w_route, b_route):
    row = lambda width: pl.BlockSpec((OPROJ_TM, width), lambda i: (i, 0))
    full = lambda r, c: pl.BlockSpec((r, c), lambda i: (0, 0))
    return pl.pallas_call(
        _oproj_kernel,
        grid=(TOKENS // OPROJ_TM,),
        in_specs=[row(D_MODEL), row(D_MODEL), full(D_MODEL, D_MODEL), full(1, D_MODEL), full(1, D_MODEL),
                  full(D_MODEL, LANES), full(1, LANES)],
        out_specs=[row(D_MODEL), row(LANES)],
        out_shape=[jax.ShapeDtypeStruct((TOKENS, D_MODEL), jnp.float32),
                   jax.ShapeDtypeStruct((TOKENS, LANES), jnp.float32)],
        compiler_params=_params(("parallel",)),
        name="oproj_ln1",
    )(x2, merged, w_o, g, b, w_route, b_route)


def _route_kernel(logit_ref, o_ref, cnt_ref, carry_sc):
    @pl.when(pl.program_id(0) == 0)
    def _():
        carry_sc[...] = jnp.zeros_like(carry_sc)

    tm = ROUTE_TILE
    z = logit_ref[...]
    lane = lax.broadcasted_iota(jnp.int32, (tm, LANES), 1)
    neg = -jnp.inf
    first = lambda hit: jnp.min(jnp.where(hit, lane, LANES), axis=-1, keepdims=True)

    glog = jnp.where(lane < N_GROUPS, z, neg)
    gmax = jnp.max(glog, axis=-1, keepdims=True)
    gsel = first(glog == gmax)
    gw = 1.0 / jnp.sum(jnp.exp(glog - gmax), axis=-1, keepdims=True)

    e_lane = lane - ROUTER_LANE0
    in_group = (e_lane >= 0) & (e_lane < N_EXPERTS) & ((e_lane // EXPERTS_PER_GROUP) == gsel)
    v = jnp.where(in_group, z, neg)
    top1 = jnp.max(v, axis=-1, keepdims=True)
    i1 = first(v == top1)
    v2 = jnp.where(lane == i1, neg, v)
    top2 = jnp.max(v2, axis=-1, keepdims=True)
    i2 = first(v2 == top2)
    t = jnp.exp(top2 - top1)
    w1 = gw / (1.0 + t)
    w2 = w1 * t

    onehot = ((lane == i1) | (lane == i2))
    ri = lax.broadcasted_iota(jnp.int32, (tm, tm), 0)
    ci = lax.broadcasted_iota(jnp.int32, (tm, tm), 1)
    tri = jnp.where(ci <= ri, 1.0, 0.0).astype(jnp.bfloat16)
    cum = _dot(tri, jnp.where(onehot, 1.0, 0.0).astype(jnp.bfloat16))
    carry = carry_sc[0:1, :]
    before = cum + carry - 1.0
    rank1 = jnp.sum(jnp.where(lane == i1, before, 0.0), axis=-1, keepdims=True)
    rank2 = jnp.sum(jnp.where(lane == i2, before, 0.0), axis=-1, keepdims=True)
    new_carry = carry + cum[tm - 1:tm, :]
    carry_sc[...] = jnp.broadcast_to(new_carry, carry_sc.shape)
    cnt_ref[...] = jnp.broadcast_to(new_carry, cnt_ref.shape)

    cols = [(i1 - ROUTER_LANE0).astype(jnp.float32), (i2 - ROUTER_LANE0).astype(jnp.float32),
            rank1, rank2, w1, w2]
    out = jnp.zeros((tm, LANES), jnp.float32)
    for k, col in enumerate(cols):
        out = jnp.where(lane == k, col, out)
    o_ref[...] = out


def _route(logits):
    return pl.pallas_call(
        _route_kernel,
        grid=(TOKENS // ROUTE_TILE,),
        in_specs=[pl.BlockSpec((ROUTE_TILE, LANES), lambda i: (i, 0))],
        out_specs=[pl.BlockSpec((ROUTE_TILE, LANES), lambda i: (i, 0)),
                   pl.BlockSpec((8, LANES), lambda i: (0, 0))],
        out_shape=[jax.ShapeDtypeStruct((TOKENS, LANES), jnp.float32),
                   jax.ShapeDtypeStruct((8, LANES), jnp.float32)],
        scratch_shapes=[pltpu.VMEM((8, LANES), jnp.float32)],
        compiler_params=_params(("arbitrary",)),
        name="route",
    )(logits)


def _slots_kernel(route_ref, cnt_ref, pos_ref):
    lane8 = lax.broadcasted_iota(jnp.int32, (8, LANES), 1)
    cnt = cnt_ref[...]
    padded = jnp.ceil(cnt * (1.0 / EXPERT_TILE)) * EXPERT_TILE
    incl = padded
    for sh in (1, 2, 4, 8, 16, 32, 64):
        incl = incl + jnp.where(lane8 >= sh, pltpu.roll(incl, sh, axis=1), 0.0)
    start = (incl - padded)[0:1, :]

    r = route_ref[...]
    lane = lax.broadcasted_iota(jnp.int32, r.shape, 1).astype(jnp.float32)
    pick = lambda e: jnp.sum(jnp.where(lane == e + float(ROUTER_LANE0), start, 0.0), axis=-1, keepdims=True)
    pos_a = pick(r[:, 0:1]) + r[:, 2:3]
    pos_b = pick(r[:, 1:2]) + r[:, 3:4]
    both = jnp.where(lane == 0.0, pos_a, jnp.where(lane == 1.0, pos_b, 0.0))
    pos_ref[...] = jnp.transpose(both)[0:SUBLANES, :].astype(jnp.int32)


def _slots(routing, counts8):
    return pl.pallas_call(
        _slots_kernel,
        grid=(TOKENS // ROUTE_TILE,),
        in_specs=[pl.BlockSpec((ROUTE_TILE, LANES), lambda i: (i, 0)),
                  pl.BlockSpec((8, LANES), lambda i: (0, 0))],
        out_specs=pl.BlockSpec((SUBLANES, ROUTE_TILE), lambda i: (0, i)),
        out_shape=jax.ShapeDtypeStruct((SUBLANES, TOKENS), jnp.int32),
        compiler_params=_params(("parallel",)),
        name="slots",
    )(routing, counts8)


SUBLANES = 8
DISPATCH_TILE = 512
SORTED_ROWS = N_EXPERT_TILES * EXPERT_TILE


PACKED = D_MODEL // 2


def _pack_rows(v):
    words = pltpu.pack_elementwise([v[:, :PACKED], v[:, PACKED:]], packed_dtype=jnp.bfloat16)
    return lax.bitcast_convert_type(words, jnp.uint32)


def _unpack_rows(w):
    half = lambda i: pltpu.unpack_elementwise(w, index=i, packed_dtype=jnp.bfloat16,
                                              unpacked_dtype=jnp.float32)
    return half(0), half(1)


def _dispatch_kernel(pad_start_ref, pad_n_ref, nvalid_ref, pos_ref, x_ref, xs_hbm, zero_sc, pk_sc, sem,
                     row_sem):
    groups = DISPATCH_TILE // SUBLANES

    @pl.when(pl.program_id(0) == 0)
    def _():
        zero_sc[...] = jnp.zeros_like(zero_sc)

        def tile_copy(t):
            return pltpu.make_async_copy(zero_sc, xs_hbm.at[pl.ds(pl.multiple_of(t * EXPERT_TILE, EXPERT_TILE),
                                                                  EXPERT_TILE)], sem.at[2])

        def start_tile(t, c):
            tile_copy(t).start()
            return c

        def wait_tile(t, c):
            tile_copy(t).wait()
            return c
        lax.fori_loop(nvalid_ref[0], N_EXPERT_TILES, start_tile, 0)
        lax.fori_loop(nvalid_ref[0], N_EXPERT_TILES, wait_tile, 0)

        def pad_copies(e, act):
            n = pad_n_ref[e]
            cur = pad_start_ref[e]
            for bit in range(EXPERT_TILE.bit_length() - 1):
                size = 1 << bit
                has = lax.bitwise_and(n, size)

                @pl.when(has != 0)
                def _(cur=cur, size=size):
                    act(pltpu.make_async_copy(zero_sc.at[pl.ds(0, size)],
                                              xs_hbm.at[pl.ds(pl.multiple_of(cur, size), size)], sem.at[1]))
                cur = cur + has

        def start_pads(e, c):
            pad_copies(e, lambda cp: cp.start())
            return c

        def wait_pads(e, c):
            pad_copies(e, lambda cp: cp.wait())
            return c
        lax.fori_loop(0, N_EXPERTS, start_pads, 0)
        lax.fori_loop(0, N_EXPERTS, wait_pads, 0)

    j = pl.program_id(0)
    slot = lax.rem(j, 2)
    pk_sc[slot] = _pack_rows(x_ref[...]).reshape(groups, SUBLANES, PACKED)

    def body(gi, c):
        for s in range(SUBLANES):
            for k in range(2):
                p = pos_ref[0, 0, k * DISPATCH_TILE + gi * SUBLANES + s]
                pltpu.make_async_copy(pk_sc.at[slot, gi, pl.ds(s, 1)], xs_hbm.at[pl.ds(p, 1)],
                                      row_sem.at[slot]).start(priority=k)
        return c
    lax.fori_loop(0, groups, body, 0)

    def wait_slot(sl):
        for _ in range(2 * groups):
            pltpu.make_async_copy(pk_sc.at[sl, 0], xs_hbm.at[pl.ds(0, SUBLANES)], row_sem.at[sl]).wait()

    @pl.when(j > 0)
    def _():
        wait_slot(1 - slot)

    @pl.when(j == pl.num_programs(0) - 1)
    def _():
        wait_slot(slot)


def _dispatch(pad_start, pad_n, n_valid_tiles, pos3, x1):
    nt = TOKENS // DISPATCH_TILE
    groups = DISPATCH_TILE // SUBLANES
    grid_spec = pltpu.PrefetchScalarGridSpec(
        num_scalar_prefetch=3,
        grid=(nt,),
        in_specs=[pl.BlockSpec((1, 1, 2 * DISPATCH_TILE), lambda j, *_: (j, 0, 0), memory_space=pltpu.SMEM),
                  pl.BlockSpec((DISPATCH_TILE, D_MODEL), lambda j, *_: (j, 0))],
        out_specs=pl.BlockSpec(memory_space=pl.ANY),
        scratch_shapes=[pltpu.VMEM((EXPERT_TILE, PACKED), jnp.uint32),
                        pltpu.VMEM((2, groups, SUBLANES, PACKED), jnp.uint32),
                        pltpu.SemaphoreType.DMA((3,)),
                        pltpu.SemaphoreType.DMA((2,))],
    )
    return pl.pallas_call(
        _dispatch_kernel,
        grid_spec=grid_spec,
        out_shape=jax.ShapeDtypeStruct((SORTED_ROWS, PACKED), jnp.uint32),
        compiler_params=_params(("arbitrary",)),
        name="dispatch",
    )(pad_start, pad_n, n_valid_tiles, pos3, x1)


def _ffn_kernel(texp_ref, tvalid_ref, tslot_ref, tfirst_ref, tnext_ref, nvalid_ref,
                xs_ref, wg_hbm, wu_hbm, wd_hbm, y_ref, wg_buf, wu_buf, wd_buf, sem):
    j = pl.program_id(0)
    valid = tvalid_ref[j] == 1
    slot = tslot_ref[j]

    def weight_copies(e, s):
        return (pltpu.make_async_copy(wg_hbm.at[e], wg_buf.at[s], sem.at[s, 0]),
                pltpu.make_async_copy(wu_hbm.at[e], wu_buf.at[s], sem.at[s, 1]),
                pltpu.make_async_copy(wd_hbm.at[e], wd_buf.at[s], sem.at[s, 2]))

    @pl.when(valid & (tfirst_ref[j] == 1))
    def _():
        @pl.when(j == 0)
        def _():
            for cp in weight_copies(texp_ref[j], slot):
                cp.start()

        for cp in weight_copies(texp_ref[j], slot):
            cp.wait()

        @pl.when(tnext_ref[j] >= 0)
        def _():
            for cp in weight_copies(tnext_ref[j], 1 - slot):
                cp.start()

    @pl.when(valid)
    def _():
        lo, hi = _unpack_rows(xs_ref[...])
        a = _dot(lo, wg_buf[slot, 0:PACKED, :]) + _dot(hi, wg_buf[slot, PACKED:D_MODEL, :])
        u = _dot(lo, wu_buf[slot, 0:PACKED, :]) + _dot(hi, wu_buf[slot, PACKED:D_MODEL, :])
        hid = (a * jax.nn.sigmoid(a)) * u
        y_ref[...] = _pack_rows(_dot(hid, wd_buf[slot]))

    @pl.when(jnp.logical_not(valid))
    def _():
        y_ref[...] = jnp.zeros_like(y_ref)


def _expert_ffn(tables, xs, w_gate, w_up, w_down):
    nt = N_EXPERT_TILES
    xmap = lambda j, te, tv, ts, tf, tn, nv: (jnp.minimum(j, nv[0] - 1), 0)
    grid_spec = pltpu.PrefetchScalarGridSpec(
        num_scalar_prefetch=6,
        grid=(nt,),
        in_specs=[pl.BlockSpec((EXPERT_TILE, PACKED), xmap),
                  pl.BlockSpec(memory_space=pl.ANY),
                  pl.BlockSpec(memory_space=pl.ANY),
                  pl.BlockSpec(memory_space=pl.ANY)],
        out_specs=pl.BlockSpec((EXPERT_TILE, PACKED), lambda j, *_: (j, 0)),
        scratch_shapes=[pltpu.VMEM((2, D_MODEL, D_EXPERT), jnp.float32),
                        pltpu.VMEM((2, D_MODEL, D_EXPERT), jnp.float32),
                        pltpu.VMEM((2, D_EXPERT, D_MODEL), jnp.float32),
                        pltpu.SemaphoreType.DMA((2, 3))],
    )
    return pl.pallas_call(
        _ffn_kernel,
        grid_spec=grid_spec,
        out_shape=jax.ShapeDtypeStruct((nt * EXPERT_TILE, PACKED), jnp.uint32),
        compiler_params=_params(("arbitrary",)),
        name="expert_ffn",
    )(*tables, xs, w_gate, w_up, w_down)


def _combine_kernel(pos_cur_ref, pos_next_ref, r_ref, x1_ref, y_hbm, g_ref, b_ref, o_ref, ybuf, sem):
    j = pl.program_id(0)
    nt = pl.num_programs(0)
    slot = lax.rem(j, 2)
    groups = COMBINE_TILE // SUBLANES

    def gather(pos_ref, dst_slot):
        def body(gi, c):
            for s in range(SUBLANES):
                for k in range(2):
                    p = pos_ref[0, 0, k * COMBINE_TILE + gi * SUBLANES + s]
                    pltpu.make_async_copy(y_hbm.at[pl.ds(p, 1)], ybuf.at[dst_slot, k, gi, pl.ds(s, 1)],
                                          sem.at[dst_slot]).start(priority=k)
            return c
        lax.fori_loop(0, groups, body, 0)

    @pl.when(j == 0)
    def _():
        gather(pos_cur_ref, 0)

    @pl.when(j + 1 < nt)
    def _():
        gather(pos_next_ref, 1 - slot)

    for _ in range(2 * groups):
        pltpu.make_async_copy(y_hbm.at[pl.ds(0, SUBLANES)], ybuf.at[slot, 0, 0], sem.at[slot]).wait()
    r = r_ref[...]
    unpack = lambda k: jnp.concatenate(_unpack_rows(ybuf[slot, k].reshape(COMBINE_TILE, PACKED)), axis=1)
    ya, yb = unpack(0), unpack(1)
    hres = ALPHA * x1_ref[...] + r[:, 4:5] * ya + r[:, 5:6] * yb
    o_ref[...] = _layer_norm(hres, g_ref[...], b_ref[...])


def _combine(pos3, routing, x1, y_sorted, g, b):
    nt = TOKENS // COMBINE_TILE
    groups = COMBINE_TILE // SUBLANES
    return pl.pallas_call(
        _combine_kernel,
        grid=(nt,),
        in_specs=[pl.BlockSpec((1, 1, 2 * COMBINE_TILE), lambda j: (j, 0, 0), memory_space=pltpu.SMEM),
                  pl.BlockSpec((1, 1, 2 * COMBINE_TILE), lambda j: (jnp.minimum(j + 1, nt - 1), 0, 0),
                               memory_space=pltpu.SMEM),
                  pl.BlockSpec((COMBINE_TILE, LANES), lambda j: (j, 0)),
                  pl.BlockSpec((COMBINE_TILE, D_MODEL), lambda j: (j, 0)),
                  pl.BlockSpec(memory_space=pl.ANY),
                  pl.BlockSpec((1, D_MODEL), lambda j: (0, 0)),
                  pl.BlockSpec((1, D_MODEL), lambda j: (0, 0))],
        out_specs=pl.BlockSpec((COMBINE_TILE, D_MODEL), lambda j: (j, 0)),
        out_shape=jax.ShapeDtypeStruct((TOKENS, D_MODEL), jnp.float32),
        scratch_shapes=[pltpu.VMEM((2, 2, groups, SUBLANES, PACKED), jnp.uint32),
                        pltpu.SemaphoreType.DMA((2,))],
        compiler_params=_params(("arbitrary",)),
        name="combine_ln2",
    )(pos3, pos3, routing, x1, y_sorted, g, b)


def _alibi_slopes():
    n = N_DIL_GROUPS * HEADS_PER_GROUP
    return jnp.asarray(2.0 ** (-ALIBI_MAX * np.arange(1, n + 1, dtype=np.float32) / n), jnp.float32)


def kernel(x, mem, ln_mem_g, ln_mem_b, w_in, b_in, w_conv, w_conv_out, w_dil_out, w_mem_kv, w_mem_out, w_o, ln1_g, ln1_b, w_group, b_group, w_router, b_router, w_gate, w_up, w_down, ln2_g, ln2_b):
    assert x.shape == (BATCH, SEQ, D_MODEL) and w_in.shape == (1, D_MODEL, IN_DIM)
    bf16 = jnp.bfloat16
    row = lambda v: v.reshape(1, -1)
    w_in2 = w_in[0].astype(bf16)
    b_in2 = b_in

    kv = _memkv(mem, row(ln_mem_g), row(ln_mem_b), w_mem_kv[0].astype(bf16))
    s_conv, xb = _conv_branch(x, w_in2, b_in2, w_conv[0])
    qkv = _qkv_proj(xb, w_in2, b_in2)
    o_mem = _mem_branch(xb, w_in2, b_in2, kv)
    o_dil = _dil_branch(qkv, _alibi_slopes())

    x2 = x.reshape(TOKENS, D_MODEL)
    merged = _merge(xb.reshape(TOKENS, D_MODEL), w_in2, b_in2,
                    s_conv.reshape(TOKENS, CONV_DIM), o_dil.reshape(TOKENS, DIL_OUT_DIM),
                    o_mem.reshape(TOKENS, MEM_DIM),
                    jnp.concatenate([w_conv_out[0], w_dil_out[0], w_mem_out[0]], axis=0).astype(bf16))

    w_route = jnp.concatenate(
        [w_group[0], jnp.transpose(w_router[0], (1, 0, 2)).reshape(D_MODEL, N_EXPERTS),
         jnp.zeros((D_MODEL, LANES - N_GROUPS - N_EXPERTS), jnp.float32)], axis=1)
    b_route = jnp.concatenate(
        [b_group[0], b_router[0].reshape(N_EXPERTS),
         jnp.zeros((LANES - N_GROUPS - N_EXPERTS,), jnp.float32)]).reshape(1, LANES)
    x1, logits = _oproj(x2, merged, w_o[0].astype(bf16), ln1_g, ln1_b, w_route, b_route)

    routing, counts8 = _route(logits)
    pos = _slots(routing, counts8)[0:2]

    def pos_tiles(tile):
        return jnp.transpose(pos.reshape(2, TOKENS // tile, tile), (1, 0, 2)).reshape(TOKENS // tile, 1, 2 * tile)

    i32 = jnp.int32
    counts = counts8[0, ROUTER_LANE0:ROUTER_LANE0 + N_EXPERTS].astype(i32)
    padded = ((counts + EXPERT_TILE - 1) // EXPERT_TILE) * EXPERT_TILE
    ends = jnp.cumsum(padded)
    starts = ends - padded
    tile_start = jnp.arange(N_EXPERT_TILES, dtype=i32) * EXPERT_TILE
    tile_expert = jnp.minimum(jnp.sum((ends[None, :] <= tile_start[:, None]).astype(i32), axis=1),
                              N_EXPERTS - 1)
    tile_valid = tile_start < ends[-1]
    prev_expert = jnp.concatenate([jnp.full((1,), -1, i32), tile_expert[:-1]])
    tile_first = tile_valid & (tile_expert != prev_expert)
    tile_slot = (jnp.cumsum(tile_first.astype(i32)) - 1) & 1
    big = N_EXPERTS
    idx = jnp.where(counts > 0, jnp.arange(N_EXPERTS, dtype=i32), big)
    later = jnp.concatenate([lax.cummin(idx[::-1])[::-1][1:], jnp.full((1,), big, i32)])
    next_used = jnp.where(later == big, -1, later)
    n_valid_tiles = (ends[-1:] // EXPERT_TILE).astype(i32)
    tables = (tile_expert, tile_valid.astype(i32), tile_slot, tile_first.astype(i32),
              next_used[tile_expert], n_valid_tiles)

    xs = _dispatch(starts + counts, padded - counts, n_valid_tiles,
                   pos_tiles(DISPATCH_TILE), x1)
    y_sorted = _expert_ffn(tables, xs,
                           w_gate.reshape(N_EXPERTS, D_MODEL, D_EXPERT),
                           w_up.reshape(N_EXPERTS, D_MODEL, D_EXPERT),
                           w_down.reshape(N_EXPERTS, D_EXPERT, D_MODEL))
    out = _combine(pos_tiles(COMBINE_TILE), routing, x1, y_sorted,
                   ln2_g, ln2_b)
    return out.reshape(BATCH, SEQ, D_MODEL)
```

```python
import functools
import math

import numpy as np
import jax
import jax.numpy as jnp
from jax import lax
from jax.experimental import pallas as pl
from jax.experimental.pallas import tpu as pltpu

D_MODEL = 2048
BATCH = 8
SEQ = 2048
TOKENS = BATCH * SEQ
CONV_DIM = 1024
CONV_WIDTH = 3
DIL_PATTERNS = ((128, 1), (512, 4), (2048, 16))
N_DIL_GROUPS = 3
HEADS_PER_GROUP = 4
HEAD_DIM = 128
DIL_DIM = N_DIL_GROUPS * HEADS_PER_GROUP * HEAD_DIM
DIL_OUT_DIM = HEADS_PER_GROUP * HEAD_DIM
ATT_BLOCK = 128
ALIBI_MAX = 8.0
MEM_LEN = 256
MEM_HEADS = 4
MEM_HEAD_DIM = 256
MEM_DIM = MEM_HEADS * MEM_HEAD_DIM
N_BRANCHES = 3
IN_DIM = 3 * CONV_DIM + 3 * DIL_DIM + MEM_DIM + N_BRANCHES * D_MODEL
N_GROUPS = 4
EXPERTS_PER_GROUP = 8
N_EXPERTS = N_GROUPS * EXPERTS_PER_GROUP
D_EXPERT = 512
ALPHA = 2.0 ** 0.25
LN_EPS = 1e-5

OFF_CB = 0
OFF_CC = CONV_DIM
OFF_CH = 2 * CONV_DIM
OFF_Q = 3 * CONV_DIM
OFF_MQ = OFF_Q + 3 * DIL_DIM
OFF_GATE = OFF_MQ + MEM_DIM

LANES = 128
HALF = SEQ // 2
VMEM_LIMIT = 56 * 1024 * 1024

ROUTE_TILE = 1024
EXPERT_TILE = 512
N_EXPERT_TILES = 2 * TOKENS // EXPERT_TILE + N_EXPERTS
COMBINE_TILE = 512
COMBINE_CHUNK = 128
ROUTER_LANE0 = N_GROUPS


def _params(sem, limit=VMEM_LIMIT):
    return pltpu.CompilerParams(dimension_semantics=sem, vmem_limit_bytes=limit)


def _layer_norm(x, g, b):
    mu = jnp.mean(x, axis=-1, keepdims=True)
    xc = x - mu
    var = jnp.mean(xc * xc, axis=-1, keepdims=True)
    return xc * lax.rsqrt(var + LN_EPS) * g + b


def _dot(a, b):
    return jnp.dot(a, b, preferred_element_type=jnp.float32)


def _dot_t(a, b):
    return lax.dot_general(a, b, (((1,), (1,)), ((), ())), preferred_element_type=jnp.float32)


MEMKV_BATCHES = 2


def _memkv_kernel(mem_ref, g_ref, b_ref, w_ref, kv_ref):
    rows = MEMKV_BATCHES * MEM_LEN
    y = _layer_norm(mem_ref[...].reshape(rows, D_MODEL), g_ref[...], b_ref[...])
    kv = _dot(y.astype(jnp.bfloat16), w_ref[...])
    kv_ref[...] = kv.astype(kv_ref.dtype).reshape(MEMKV_BATCHES, MEM_LEN, 2 * MEM_DIM)


def _memkv(mem, g, b, w):
    return pl.pallas_call(
        _memkv_kernel,
        grid=(BATCH // MEMKV_BATCHES,),
        in_specs=[pl.BlockSpec((MEMKV_BATCHES, MEM_LEN, D_MODEL), lambda i: (i, 0, 0)),
                  pl.BlockSpec((1, D_MODEL), lambda i: (0, 0)),
                  pl.BlockSpec((1, D_MODEL), lambda i: (0, 0)),
                  pl.BlockSpec((D_MODEL, 2 * MEM_DIM), lambda i: (0, 0))],
        out_specs=pl.BlockSpec((MEMKV_BATCHES, MEM_LEN, 2 * MEM_DIM), lambda i: (i, 0, 0)),
        out_shape=jax.ShapeDtypeStruct((BATCH, MEM_LEN, 2 * MEM_DIM), jnp.bfloat16),
        compiler_params=_params(("parallel",)),
        name="mem_kv",
    )(mem, g, b, w)


CONV_TC = 512


def _conv_kernel(x_ref, wb_ref, wc_ref, wh_ref, bb_ref, bc_ref, bh_ref, wconv_ref, s_ref, xb_ref,
                 u_sc, carry_sc):
    half = pl.program_id(1)
    c = pl.program_id(2)

    @pl.when(c == 0)
    def _():
        xb_ref[...] = x_ref[...].astype(xb_ref.dtype)

    x = xb_ref[...]
    cc = _dot(x, wc_ref[...]) + bc_ref[...]
    ch = _dot(x, wh_ref[...]) + bh_ref[...]
    u = cc * ch
    u_sc[0:8, :] = jnp.where(half == 0, 0.0, carry_sc[c])
    u_sc[8:8 + HALF, :] = u
    carry_sc[c] = u[HALF - 8:HALF, :]
    wconv = wconv_ref[...]
    y = (wconv[2:3, :] * u
         + wconv[1:2, :] * u_sc[7:7 + HALF, :]
         + wconv[0:1, :] * u_sc[6:6 + HALF, :])
    cb = _dot(x, wb_ref[...]) + bb_ref[...]
    s_ref[...] = (cb * y).astype(s_ref.dtype)


def _conv_branch(x, w_in, b_in, w_conv):
    nb = lambda off: off // CONV_TC
    wspec = lambda off: pl.BlockSpec((D_MODEL, CONV_TC), lambda b, h, c, o=nb(off): (0, o + c))
    bspec = lambda off: pl.BlockSpec((1, CONV_TC), lambda b, h, c, o=nb(off): (0, o + c))
    return pl.pallas_call(
        _conv_kernel,
        grid=(BATCH, 2, CONV_DIM // CONV_TC),
        in_specs=[pl.BlockSpec((None, HALF, D_MODEL), lambda b, h, c: (b, h, 0)),
                  wspec(OFF_CB), wspec(OFF_CC), wspec(OFF_CH),
                  bspec(OFF_CB), bspec(OFF_CC), bspec(OFF_CH),
                  pl.BlockSpec((CONV_WIDTH, CONV_TC), lambda b, h, c: (0, c))],
        out_specs=[pl.BlockSpec((None, HALF, CONV_TC), lambda b, h, c: (b, h, c)),
                   pl.BlockSpec((None, HALF, D_MODEL), lambda b, h, c: (b, h, 0))],
        out_shape=[jax.ShapeDtypeStruct((BATCH, SEQ, CONV_DIM), jnp.bfloat16),
                   jax.ShapeDtypeStruct((BATCH, SEQ, D_MODEL), jnp.bfloat16)],
        scratch_shapes=[pltpu.VMEM((HALF + 8, CONV_TC), jnp.float32),
                        pltpu.VMEM((CONV_DIM // CONV_TC, 8, CONV_TC), jnp.float32)],
        compiler_params=_params(("arbitrary", "arbitrary", "arbitrary")),
        name="conv_branch",
    )(x, w_in, w_in, w_in, b_in, b_in, b_in, w_conv)


QKV_TN = 512
QKV_CHUNKS = QKV_TN // LANES


def _qkv_kernel(x_ref, w_ref, b_ref, o_ref, sc_ref, sc2_ref):
    x = x_ref[...]
    for gi in (2, 1, 0):
        cols = slice(gi * QKV_TN, (gi + 1) * QKV_TN)
        acc = _dot(x, w_ref[:, cols]) + b_ref[:, cols]
        d = DIL_PATTERNS[gi][1]
        if d == 1:
            o_ref[:, cols] = acc.astype(o_ref.dtype)
            continue
        rows = HALF // d
        for c in range(QKV_CHUNKS):
            sc_ref[gi - 1, c] = acc[:, c * LANES:(c + 1) * LANES]
        if d == 16:
            q4 = HALF // 4
            for c in range(QKV_CHUNKS):
                for r in range(4):
                    sc2_ref[c, r * q4:(r + 1) * q4, :] = sc_ref[gi - 1, c, pl.ds(r, q4, stride=4), :]
            for c in range(QKV_CHUNKS):
                lo = gi * QKV_TN + c * LANES
                for r in range(d):
                    r_lo, r_hi = r % 4, r // 4
                    o_ref[r * rows:(r + 1) * rows, lo:lo + LANES] = (
                        sc2_ref[c, pl.ds(r_lo * q4 + r_hi, rows, stride=4), :].astype(o_ref.dtype))
            continue
        for c in range(QKV_CHUNKS):
            for r in range(d):
                lo = gi * QKV_TN + c * LANES
                o_ref[r * rows:(r + 1) * rows, lo:lo + LANES] = (
                    sc_ref[gi - 1, c, pl.ds(r, rows, stride=d), :].astype(o_ref.dtype))


def _qkv_proj(xb, w_in, b_in):
    n0 = OFF_Q // DIL_DIM
    return pl.pallas_call(
        _qkv_kernel,
        grid=(BATCH, 2, 3),
        in_specs=[pl.BlockSpec((None, HALF, D_MODEL), lambda b, h, n: (b, h, 0)),
                  pl.BlockSpec((D_MODEL, DIL_DIM), lambda b, h, n: (0, n0 + n)),
                  pl.BlockSpec((1, DIL_DIM), lambda b, h, n: (0, n0 + n))],
        out_specs=pl.BlockSpec((None, HALF, DIL_DIM), lambda b, h, n: (b, h, n)),
        out_shape=jax.ShapeDtypeStruct((BATCH, SEQ, 3 * DIL_DIM), jnp.bfloat16),
        scratch_shapes=[pltpu.VMEM((N_DIL_GROUPS - 1, QKV_CHUNKS, HALF, LANES), jnp.float32),
                        pltpu.VMEM((QKV_CHUNKS, HALF, LANES), jnp.float32)],
        compiler_params=_params(("parallel", "parallel", "arbitrary")),
        name="qkv_proj",
    )(xb, w_in, b_in)


MEM_HEADS_PER_STEP = 2
MEM_TN = MEM_HEADS_PER_STEP * MEM_HEAD_DIM


def _memattn_kernel(x_ref, w_ref, b_ref, mk_ref, mv_ref, o_ref):
    mq_all = (_dot(x_ref[...], w_ref[...]) + b_ref[...]).astype(jnp.bfloat16)
    for hh in range(MEM_HEADS_PER_STEP):
        cols = slice(hh * MEM_HEAD_DIM, (hh + 1) * MEM_HEAD_DIM)
        s = _dot_t(mq_all[:, cols], mk_ref[:, cols]) * (MEM_HEAD_DIM ** -0.5)
        m = jnp.max(s, axis=-1, keepdims=True)
        p = jnp.exp(s - m)
        den = jnp.sum(p, axis=-1, keepdims=True)
        o = _dot(p.astype(jnp.bfloat16), mv_ref[:, cols]) / den
        o_ref[:, cols] = o.astype(o_ref.dtype)


def _mem_branch(xb, w_in, b_in, kv):
    n0 = OFF_MQ // MEM_TN
    nv = MEM_DIM // MEM_TN
    return pl.pallas_call(
        _memattn_kernel,
        grid=(BATCH, 2, MEM_DIM // MEM_TN),
        in_specs=[pl.BlockSpec((None, HALF, D_MODEL), lambda b, h, n: (b, h, 0)),
                  pl.BlockSpec((D_MODEL, MEM_TN), lambda b, h, n: (0, n0 + n)),
                  pl.BlockSpec((1, MEM_TN), lambda b, h, n: (0, n0 + n)),
                  pl.BlockSpec((None, MEM_LEN, MEM_TN), lambda b, h, n: (b, 0, n)),
                  pl.BlockSpec((None, MEM_LEN, MEM_TN), lambda b, h, n: (b, 0, nv + n))],
        out_specs=pl.BlockSpec((None, HALF, MEM_TN), lambda b, h, n: (b, h, n)),
        out_shape=jax.ShapeDtypeStruct((BATCH, SEQ, MEM_DIM), jnp.bfloat16),
        compiler_params=_params(("parallel", "parallel", "arbitrary")),
        name="mem_branch",
    )(xb, w_in, b_in, kv, kv)


ATT_UNROLL = 8


def _softmax_block(s, v):
    m = jnp.max(s, axis=-1, keepdims=True)
    p = jnp.exp(s - m)
    den = jnp.sum(p, axis=-1, keepdims=True)
    o = _dot(p.astype(jnp.bfloat16), v) / den
    return o, m + jnp.log(den)


def _dilattn_kernel(slopes_ref,
                    q0_ref, q1_ref, q2_ref, k0_ref, k1_ref, k2_ref, v0_ref, v1_ref, v2_ref,
                    o_ref, o_sc, l_sc):
    h = pl.program_id(1)
    blk = ATT_BLOCK
    scale = HEAD_DIM ** -0.5
    qi = lax.broadcasted_iota(jnp.int32, (blk, 2 * blk), 0) + blk
    kj = lax.broadcasted_iota(jnp.int32, (blk, 2 * blk), 1)
    jrel = qi - kj
    valid = (jrel >= 0) & (jrel <= blk)
    jrel_f = jrel.astype(jnp.float32)

    def bias_for(g):
        slope = slopes_ref[g * HEADS_PER_GROUP + h]
        d = float(DIL_PATTERNS[g][1])
        return jnp.where(valid, (-slope * d) * jrel_f, -jnp.inf)

    def put(g, row_slice, o, lse):
        o_sc[g, row_slice, :] = o
        l_sc[g, row_slice, :] = jnp.broadcast_to(lse, (blk, HEAD_DIM))

    def run_blocks(g, blocks):
        scores = [_dot_t(q, k) * scale + bias for q, k, _, bias, _ in blocks]
        stats = []
        for s in scores:
            m = jnp.max(s, axis=-1, keepdims=True)
            p = jnp.exp(s - m)
            stats.append((m, p, jnp.sum(p, axis=-1, keepdims=True)))
        outs = [_dot(p.astype(jnp.bfloat16), blkdef[2]) / den
                for (m, p, den), blkdef in zip(stats, blocks)]
        for o, (m, p, den), blkdef in zip(outs, stats, blocks):
            put(g, blkdef[4], o, m + jnp.log(den))

    bias0 = bias_for(0)
    prev_cols = kj < blk

    def g0_body(it, carry):
        blocks = []
        for k in range(ATT_UNROLL):
            n = it * ATT_UNROLL + k
            q0 = pl.multiple_of(n * blk, blk)
            k0 = pl.multiple_of(jnp.maximum(n - 1, 0) * blk, blk)
            bias = jnp.where(prev_cols & (n == 0), -jnp.inf, bias0)
            blocks.append((q0_ref[pl.ds(q0, blk), :], k0_ref[pl.ds(k0, 2 * blk), :],
                           v0_ref[pl.ds(k0, 2 * blk), :], bias, pl.ds(q0, blk)))
        run_blocks(0, blocks)
        return carry

    lax.fori_loop(0, SEQ // blk // ATT_UNROLL, g0_body, 0)

    d1 = DIL_PATTERNS[1][1]
    cls1 = HALF // d1
    per_half = cls1 // blk
    bias1 = bias_for(1)

    def row1(r, n):
        return (n // per_half) * HALF + r * cls1 + (n % per_half) * blk

    nblk1 = SEQ // d1 // blk
    cls_per_trip = ATT_UNROLL // nblk1

    def g1_body(it, carry):
        blocks = []
        for c in range(cls_per_trip):
            r = it * cls_per_trip + c
            for n in range(nblk1):
                cur = pl.multiple_of(row1(r, n), blk)
                q = q1_ref[pl.ds(cur, blk), :]
                dst = pl.ds(n * blk * d1 + r, blk, stride=d1)
                if n == 0:
                    blocks.append((q, k1_ref[pl.ds(cur, blk), :], v1_ref[pl.ds(cur, blk), :],
                                   bias1[:, blk:], dst))
                else:
                    prev = pl.multiple_of(row1(r, n - 1), blk)
                    kc = jnp.concatenate([k1_ref[pl.ds(prev, blk), :], k1_ref[pl.ds(cur, blk), :]], axis=0)
                    vc = jnp.concatenate([v1_ref[pl.ds(prev, blk), :], v1_ref[pl.ds(cur, blk), :]], axis=0)
                    blocks.append((q, kc, vc, bias1, dst))
        run_blocks(1, blocks)
        return carry

    lax.fori_loop(0, d1 // cls_per_trip, g1_body, 0)

    d2 = DIL_PATTERNS[2][1]
    cls2 = HALF // d2
    bias2 = bias_for(2)

    def g2_body(it, carry):
        blocks = []
        for k in range(ATT_UNROLL):
            r = it * ATT_UNROLL + k
            a = pl.multiple_of(r * cls2, cls2)
            b = pl.multiple_of(HALF + r * cls2, cls2)
            cat = lambda ref, a=a, b=b: jnp.concatenate(
                [ref[pl.ds(a, cls2), :], ref[pl.ds(b, cls2), :]], axis=0)
            blocks.append((cat(q2_ref), cat(k2_ref), cat(v2_ref), bias2[:, blk:],
                           pl.ds(r, blk, stride=d2)))
        run_blocks(2, blocks)
        return carry

    lax.fori_loop(0, d2 // ATT_UNROLL, g2_body, 0)

    rows = 256
    for t in range(SEQ // rows):
        sl = pl.ds(t * rows, rows)
        l0, l1, l2 = l_sc[0, sl, :], l_sc[1, sl, :], l_sc[2, sl, :]
        m = jnp.maximum(jnp.maximum(l0, l1), l2)
        e0, e1, e2 = jnp.exp(l0 - m), jnp.exp(l1 - m), jnp.exp(l2 - m)
        mix = (e0 * o_sc[0, sl, :] + e1 * o_sc[1, sl, :] + e2 * o_sc[2, sl, :]) / (e0 + e1 + e2)
        o_ref[sl, :] = mix.astype(o_ref.dtype)


def _dil_branch(qkv, slopes):
    nq = DIL_DIM // HEAD_DIM

    def spec(section, g):
        return pl.BlockSpec((None, SEQ, HEAD_DIM),
                            lambda b, h, s_ref, o=section * nq + g * HEADS_PER_GROUP: (b, 0, o + h))

    grid_spec = pltpu.PrefetchScalarGridSpec(
        num_scalar_prefetch=1,
        grid=(BATCH, HEADS_PER_GROUP),
        in_specs=[spec(sec, g) for sec in range(3) for g in range(N_DIL_GROUPS)],
        out_specs=pl.BlockSpec((None, SEQ, HEAD_DIM), lambda b, h, s_ref: (b, 0, h)),
        scratch_shapes=[pltpu.VMEM((N_DIL_GROUPS, SEQ, HEAD_DIM), jnp.float32),
                        pltpu.VMEM((N_DIL_GROUPS, SEQ, HEAD_DIM), jnp.float32)],
    )
    return pl.pallas_call(
        _dilattn_kernel,
        grid_spec=grid_spec,
        out_shape=jax.ShapeDtypeStruct((BATCH, SEQ, DIL_OUT_DIM), jnp.bfloat16),
        compiler_params=_params(("parallel", "arbitrary")),
        name="dil_attn",
    )(slopes, *([qkv] * 9))


MERGE_TM = 1024
MERGE_TN = 512


def _merge_kernel(x_ref, wg0_ref, wg1_ref, wg2_ref, bg0_ref, bg1_ref, bg2_ref,
                  sc_ref, od_ref, om_ref, wout_ref, o_ref):
    r1, r2 = CONV_DIM, CONV_DIM + DIL_OUT_DIM
    x = x_ref[...]
    g0 = jax.nn.sigmoid(_dot(x, wg0_ref[...]) + bg0_ref[...])
    acc = g0 * _dot(sc_ref[...], wout_ref[0:r1, :])
    g1 = jax.nn.sigmoid(_dot(x, wg1_ref[...]) + bg1_ref[...])
    acc = acc + g1 * _dot(od_ref[...], wout_ref[r1:r2, :])
    g2 = jax.nn.sigmoid(_dot(x, wg2_ref[...]) + bg2_ref[...])
    acc = acc + g2 * _dot(om_ref[...], wout_ref[r2:r2 + MEM_DIM, :])
    o_ref[...] = acc.astype(o_ref.dtype)


def _merge(x2, w_in, b_in, s_conv, o_dil, o_mem, w_out):
    nb = lambda br: (OFF_GATE + br * D_MODEL) // MERGE_TN
    gspec = lambda br: pl.BlockSpec((D_MODEL, MERGE_TN), lambda i, n, o=nb(br): (0, o + n))
    bspec = lambda br: pl.BlockSpec((1, MERGE_TN), lambda i, n, o=nb(br): (0, o + n))
    act = lambda width: pl.BlockSpec((MERGE_TM, width), lambda i, n: (i, 0))
    wout = lambda width: pl.BlockSpec((width, MERGE_TN), lambda i, n: (0, n))
    return pl.pallas_call(
        _merge_kernel,
        grid=(TOKENS // MERGE_TM, D_MODEL // MERGE_TN),
        in_specs=[act(D_MODEL), gspec(0), gspec(1), gspec(2), bspec(0), bspec(1), bspec(2),
                  act(CONV_DIM), act(DIL_OUT_DIM), act(MEM_DIM),
                  wout(CONV_DIM + DIL_OUT_DIM + MEM_DIM)],
        out_specs=pl.BlockSpec((MERGE_TM, MERGE_TN), lambda i, n: (i, n)),
        out_shape=jax.ShapeDtypeStruct((TOKENS, D_MODEL), jnp.bfloat16),
        compiler_params=_params(("parallel", "arbitrary")),
        name="gated_merge",
    )(x2, w_in, w_in, w_in, b_in, b_in, b_in, s_conv, o_dil, o_mem, w_out)


OPROJ_TM = 512


OPROJ_PARTS = 2


def _oproj_kernel(x_ref, m_ref, wo_ref, g_ref, b_ref, wr_ref, br_ref, x1_ref, logit_ref):
    rows = OPROJ_TM // OPROJ_PARTS
    parts = [pl.ds(i * rows, rows) for i in range(OPROJ_PARTS)]
    proj = [_dot(m_ref[p, :], wo_ref[...]) for p in parts]
    for p, y in zip(parts, proj):
        x1 = _layer_norm(ALPHA * x_ref[p, :] + y, g_ref[...], b_ref[...])
        x1_ref[p, :] = x1
        logit_ref[p, :] = _dot(x1, wr_ref[...]) + br_ref[...]


def _oproj(x2, merged, w_o, g, b, w_route, b_route):
    row = lambda width: pl.BlockSpec((OPROJ_TM, width), lambda i: (i, 0))
    full = lambda r, c: pl.BlockSpec((r, c), lambda i: (0, 0))
    return pl.pallas_call(
        _oproj_kernel,
        grid=(TOKENS // OPROJ_TM,),
        in_specs=[row(D_MODEL), row(D_MODEL), full(D_MODEL, D_MODEL), full(1, D_MODEL), full(1, D_MODEL),
                  full(D_MODEL, LANES), full(1, LANES)],
        out_specs=[row(D_MODEL), row(LANES)],
        out_shape=[jax.ShapeDtypeStruct((TOKENS, D_MODEL), jnp.float32),
                   jax.ShapeDtypeStruct((TOKENS, LANES), jnp.float32)],
        compiler_params=_params(("parallel",)),
        name="oproj_ln1",
    )(x2, merged, w_o, g, b, w_route, b_route)


def _route_kernel(logit_ref, o_ref, cnt_ref, carry_sc):
    @pl.when(pl.program_id(0) == 0)
    def _():
        carry_sc[...] = jnp.zeros_like(carry_sc)

    tm = ROUTE_TILE
    z = logit_ref[...]
    lane = lax.broadcasted_iota(jnp.int32, (tm, LANES), 1)
    neg = -jnp.inf
    first = lambda hit: jnp.min(jnp.where(hit, lane, LANES), axis=-1, keepdims=True)

    glog = jnp.where(lane < N_GROUPS, z, neg)
    gmax = jnp.max(glog, axis=-1, keepdims=True)
    gsel = first(glog == gmax)
    gw = 1.0 / jnp.sum(jnp.exp(glog - gmax), axis=-1, keepdims=True)

    e_lane = lane - ROUTER_LANE0
    in_group = (e_lane >= 0) & (e_lane < N_EXPERTS) & ((e_lane // EXPERTS_PER_GROUP) == gsel)
    v = jnp.where(in_group, z, neg)
    top1 = jnp.max(v, axis=-1, keepdims=True)
    i1 = first(v == top1)
    v2 = jnp.where(lane == i1, neg, v)
    top2 = jnp.max(v2, axis=-1, keepdims=True)
    i2 = first(v2 == top2)
    t = jnp.exp(top2 - top1)
    w1 = gw / (1.0 + t)
    w2 = w1 * t

    onehot = ((lane == i1) | (lane == i2))
    ri = lax.broadcasted_iota(jnp.int32, (tm, tm), 0)
    ci = lax.broadcasted_iota(jnp.int32, (tm, tm), 1)
    tri = jnp.where(ci <= ri, 1.0, 0.0).astype(jnp.bfloat16)
    cum = _dot(tri, jnp.where(onehot, 1.0, 0.0).astype(jnp.bfloat16))
    carry = carry_sc[0:1, :]
    before = cum + carry - 1.0
    rank1 = jnp.sum(jnp.where(lane == i1, before, 0.0), axis=-1, keepdims=True)
    rank2 = jnp.sum(jnp.where(lane == i2, before, 0.0), axis=-1, keepdims=True)
    new_carry = carry + cum[tm - 1:tm, :]
    carry_sc[...] = jnp.broadcast_to(new_carry, carry_sc.shape)
    cnt_ref[...] = jnp.broadcast_to(new_carry, cnt_ref.shape)

    cols = [(i1 - ROUTER_LANE0).astype(jnp.float32), (i2 - ROUTER_LANE0).astype(jnp.float32),
            rank1, rank2, w1, w2]
    out = jnp.zeros((tm, LANES), jnp.float32)
    for k, col in enumerate(cols):
        out = jnp.where(lane == k, col, out)
    o_ref[...] = out


def _route(logits):
    return pl.pallas_call(
        _route_kernel,
        grid=(TOKENS // ROUTE_TILE,),
        in_specs=[pl.BlockSpec((ROUTE_TILE, LANES), lambda i: (i, 0))],
        out_specs=[pl.BlockSpec((ROUTE_TILE, LANES), lambda i: (i, 0)),
                   pl.BlockSpec((8, LANES), lambda i: (0, 0))],
        out_shape=[jax.ShapeDtypeStruct((TOKENS, LANES), jnp.float32),
                   jax.ShapeDtypeStruct((8, LANES), jnp.float32)],
        scratch_shapes=[pltpu.VMEM((8, LANES), jnp.float32)],
        compiler_params=_params(("arbitrary",)),
        name="route",
    )(logits)


def _slots_kernel(route_ref, cnt_ref, pos_ref):
    lane8 = lax.broadcasted_iota(jnp.int32, (8, LANES), 1)
    cnt = cnt_ref[...]
    padded = jnp.ceil(cnt * (1.0 / EXPERT_TILE)) * EXPERT_TILE
    incl = padded
    for sh in (1, 2, 4, 8, 16, 32, 64):
        incl = incl + jnp.where(lane8 >= sh, pltpu.roll(incl, sh, axis=1), 0.0)
    start = (incl - padded)[0:1, :]

    r = route_ref[...]
    lane = lax.broadcasted_iota(jnp.int32, r.shape, 1).astype(jnp.float32)
    pick = lambda e: jnp.sum(jnp.where(lane == e + float(ROUTER_LANE0), start, 0.0), axis=-1, keepdims=True)
    pos_a = pick(r[:, 0:1]) + r[:, 2:3]
    pos_b = pick(r[:, 1:2]) + r[:, 3:4]
    both = jnp.where(lane == 0.0, pos_a, jnp.where(lane == 1.0, pos_b, 0.0))
    pos_ref[...] = jnp.transpose(both)[0:SUBLANES, :].astype(jnp.int32)


def _slots(routing, counts8):
    return pl.pallas_call(
        _slots_kernel,
        grid=(TOKENS // ROUTE_TILE,),
        in_specs=[pl.BlockSpec((ROUTE_TILE, LANES), lambda i: (i, 0)),
                  pl.BlockSpec((8, LANES), lambda i: (0, 0))],
        out_specs=pl.BlockSpec((SUBLANES, ROUTE_TILE), lambda i: (0, i)),
        out_shape=jax.ShapeDtypeStruct((SUBLANES, TOKENS), jnp.int32),
        compiler_params=_params(("parallel",)),
        name="slots",
    )(routing, counts8)


SUBLANES = 8
DISPATCH_TILE = 512
SORTED_ROWS = N_EXPERT_TILES * EXPERT_TILE


PACKED = D_MODEL // 2


def _pack_rows(v):
    words = pltpu.pack_elementwise([v[:, :PACKED], v[:, PACKED:]], packed_dtype=jnp.bfloat16)
    return lax.bitcast_convert_type(words, jnp.uint32)


def _unpack_rows(w):
    half = lambda i: pltpu.unpack_elementwise(w, index=i, packed_dtype=jnp.bfloat16,
                                              unpacked_dtype=jnp.float32)
    return half(0), half(1)


def _dispatch_kernel(pad_start_ref, pad_n_ref, nvalid_ref, pos_ref, x_ref, xs_hbm, zero_sc, pk_sc, sem,
                     row_sem):
    groups = DISPATCH_TILE // SUBLANES

    @pl.when(pl.program_id(0) == 0)
    def _():
        zero_sc[...] = jnp.zeros_like(zero_sc)

        def tile_copy(t):
            return pltpu.make_async_copy(zero_sc, xs_hbm.at[pl.ds(pl.multiple_of(t * EXPERT_TILE, EXPERT_TILE),
                                                                  EXPERT_TILE)], sem.at[2])

        def start_tile(t, c):
            tile_copy(t).start()
            return c

        def wait_tile(t, c):
            tile_copy(t).wait()
            return c
        lax.fori_loop(nvalid_ref[0], N_EXPERT_TILES, start_tile, 0)
        lax.fori_loop(nvalid_ref[0], N_EXPERT_TILES, wait_tile, 0)

        def pad_copies(e, act):
            n = pad_n_ref[e]
            cur = pad_start_ref[e]
            for bit in range(EXPERT_TILE.bit_length() - 1):
                size = 1 << bit
                has = lax.bitwise_and(n, size)

                @pl.when(has != 0)
                def _(cur=cur, size=size):
                    act(pltpu.make_async_copy(zero_sc.at[pl.ds(0, size)],
                                              xs_hbm.at[pl.ds(pl.multiple_of(cur, size), size)], sem.at[1]))
                cur = cur + has

        def start_pads(e, c):
            pad_copies(e, lambda cp: cp.start())
            return c

        def wait_pads(e, c):
            pad_copies(e, lambda cp: cp.wait())
            return c
        lax.fori_loop(0, N_EXPERTS, start_pads, 0)
        lax.fori_loop(0, N_EXPERTS, wait_pads, 0)

    j = pl.program_id(0)
    slot = lax.rem(j, 2)
    pk_sc[slot] = _pack_rows(x_ref[...]).reshape(groups, SUBLANES, PACKED)

    def body(gi, c):
        for s in range(SUBLANES):
            for k in range(2):
                p = pos_ref[0, 0, k * DISPATCH_TILE + gi * SUBLANES + s]
                pltpu.make_async_copy(pk_sc.at[slot, gi, pl.ds(s, 1)], xs_hbm.at[pl.ds(p, 1)],
                                      row_sem.at[slot]).start(priority=k)
        return c
    lax.fori_loop(0, groups, body, 0)

    def wait_slot(sl):
        for _ in range(2 * groups):
            pltpu.make_async_copy(pk_sc.at[sl, 0], xs_hbm.at[pl.ds(0, SUBLANES)], row_sem.at[sl]).wait()

    @pl.when(j > 0)
    def _():
        wait_slot(1 - slot)

    @pl.when(j == pl.num_programs(0) - 1)
    def _():
        wait_slot(slot)


def _dispatch(pad_start, pad_n, n_valid_tiles, pos3, x1):
    nt = TOKENS // DISPATCH_TILE
    groups = DISPATCH_TILE // SUBLANES
    grid_spec = pltpu.PrefetchScalarGridSpec(
        num_scalar_prefetch=3,
        grid=(nt,),
        in_specs=[pl.BlockSpec((1, 1, 2 * DISPATCH_TILE), lambda j, *_: (j, 0, 0), memory_space=pltpu.SMEM),
                  pl.BlockSpec((DISPATCH_TILE, D_MODEL), lambda j, *_: (j, 0))],
        out_specs=pl.BlockSpec(memory_space=pl.ANY),
        scratch_shapes=[pltpu.VMEM((EXPERT_TILE, PACKED), jnp.uint32),
                        pltpu.VMEM((2, groups, SUBLANES, PACKED), jnp.uint32),
                        pltpu.SemaphoreType.DMA((3,)),
                        pltpu.SemaphoreType.DMA((2,))],
    )
    return pl.pallas_call(
        _dispatch_kernel,
        grid_spec=grid_spec,
        out_shape=jax.ShapeDtypeStruct((SORTED_ROWS, PACKED), jnp.uint32),
        compiler_params=_params(("arbitrary",)),
        name="dispatch",
    )(pad_start, pad_n, n_valid_tiles, pos3, x1)


def _ffn_kernel(texp_ref, tvalid_ref, tslot_ref, tfirst_ref, tnext_ref, nvalid_ref,
                xs_ref, wg_hbm, wu_hbm, wd_hbm, y_ref, wg_buf, wu_buf, wd_buf, sem):
    j = pl.program_id(0)
    valid = tvalid_ref[j] == 1
    slot = tslot_ref[j]

    def weight_copies(e, s):
        return (pltpu.make_async_copy(wg_hbm.at[e], wg_buf.at[s], sem.at[s, 0]),
                pltpu.make_async_copy(wu_hbm.at[e], wu_buf.at[s], sem.at[s, 1]),
                pltpu.make_async_copy(wd_hbm.at[e], wd_buf.at[s], sem.at[s, 2]))

    @pl.when(valid & (tfirst_ref[j] == 1))
    def _():
        @pl.when(j == 0)
        def _():
            for cp in weight_copies(texp_ref[j], slot):
                cp.start()

        for cp in weight_copies(texp_ref[j], slot):
            cp.wait()

        @pl.when(tnext_ref[j] >= 0)
        def _():
            for cp in weight_copies(tnext_ref[j], 1 - slot):
                cp.start()

    @pl.when(valid)
    def _():
        lo, hi = _unpack_rows(xs_ref[...])
        a = _dot(lo, wg_buf[slot, 0:PACKED, :]) + _dot(hi, wg_buf[slot, PACKED:D_MODEL, :])
        u = _dot(lo, wu_buf[slot, 0:PACKED, :]) + _dot(hi, wu_buf[slot, PACKED:D_MODEL, :])
        hid = (a * jax.nn.sigmoid(a)) * u
        y_ref[...] = _pack_rows(_dot(hid, wd_buf[slot]))

    @pl.when(jnp.logical_not(valid))
    def _():
        y_ref[...] = jnp.zeros_like(y_ref)


def _expert_ffn(tables, xs, w_gate, w_up, w_down):
    nt = N_EXPERT_TILES
    xmap = lambda j, te, tv, ts, tf, tn, nv: (jnp.minimum(j, nv[0] - 1), 0)
    grid_spec = pltpu.PrefetchScalarGridSpec(
        num_scalar_prefetch=6,
        grid=(nt,),
        in_specs=[pl.BlockSpec((EXPERT_TILE, PACKED), xmap),
                  pl.BlockSpec(memory_space=pl.ANY),
                  pl.BlockSpec(memory_space=pl.ANY),
                  pl.BlockSpec(memory_space=pl.ANY)],
        out_specs=pl.BlockSpec((EXPERT_TILE, PACKED), lambda j, *_: (j, 0)),
        scratch_shapes=[pltpu.VMEM((2, D_MODEL, D_EXPERT), jnp.float32),
                        pltpu.VMEM((2, D_MODEL, D_EXPERT), jnp.float32),
                        pltpu.VMEM((2, D_EXPERT, D_MODEL), jnp.float32),
                        pltpu.SemaphoreType.DMA((2, 3))],
    )
    return pl.pallas_call(
        _ffn_kernel,
        grid_spec=grid_spec,
        out_shape=jax.ShapeDtypeStruct((nt * EXPERT_TILE, PACKED), jnp.uint32),
        compiler_params=_params(("arbitrary",)),
        name="expert_ffn",
    )(*tables, xs, w_gate, w_up, w_down)


def _combine_kernel(pos_cur_ref, pos_next_ref, r_ref, x1_ref, y_hbm, g_ref, b_ref, o_ref, ybuf, sem):
    j = pl.program_id(0)
    nt = pl.num_programs(0)
    slot = lax.rem(j, 2)
    groups = COMBINE_TILE // SUBLANES

    def gather(pos_ref, dst_slot):
        def body(gi, c):
            for s in range(SUBLANES):
                for k in range(2):
                    p = pos_ref[0, 0, k * COMBINE_TILE + gi * SUBLANES + s]
                    pltpu.make_async_copy(y_hbm.at[pl.ds(p, 1)], ybuf.at[dst_slot, k, gi, pl.ds(s, 1)],
                                          sem.at[dst_slot]).start(priority=k)
            return c
        lax.fori_loop(0, groups, body, 0)

    @pl.when(j == 0)
    def _():
        gather(pos_cur_ref, 0)

    @pl.when(j + 1 < nt)
    def _():
        gather(pos_next_ref, 1 - slot)

    for _ in range(2 * groups):
        pltpu.make_async_copy(y_hbm.at[pl.ds(0, SUBLANES)], ybuf.at[slot, 0, 0], sem.at[slot]).wait()
    def chunk(c, carry):
        rows = pl.ds(pl.multiple_of(c * COMBINE_CHUNK, COMBINE_CHUNK), COMBINE_CHUNK)
        grp = pl.ds(c * (COMBINE_CHUNK // SUBLANES), COMBINE_CHUNK // SUBLANES)
        unpack = lambda k: jnp.concatenate(
            _unpack_rows(ybuf[slot, k, grp].reshape(COMBINE_CHUNK, PACKED)), axis=1)
        r = r_ref[rows, :]
        hres = ALPHA * x1_ref[rows, :] + r[:, 4:5] * unpack(0) + r[:, 5:6] * unpack(1)
        o_ref[rows, :] = _layer_norm(hres, g_ref[...], b_ref[...])
        return carry
    lax.fori_loop(0, COMBINE_TILE // COMBINE_CHUNK, chunk, 0)


def _combine(pos3, routing, x1, y_sorted, g, b):
    nt = TOKENS // COMBINE_TILE
    groups = COMBINE_TILE // SUBLANES
    return pl.pallas_call(
        _combine_kernel,
        grid=(nt,),
        in_specs=[pl.BlockSpec((1, 1, 2 * COMBINE_TILE), lambda j: (j, 0, 0), memory_space=pltpu.SMEM),
                  pl.BlockSpec((1, 1, 2 * COMBINE_TILE), lambda j: (jnp.minimum(j + 1, nt - 1), 0, 0),
                               memory_space=pltpu.SMEM),
                  pl.BlockSpec((COMBINE_TILE, LANES), lambda j: (j, 0)),
                  pl.BlockSpec((COMBINE_TILE, D_MODEL), lambda j: (j, 0)),
                  pl.BlockSpec(memory_space=pl.ANY),
                  pl.BlockSpec((1, D_MODEL), lambda j: (0, 0)),
                  pl.BlockSpec((1, D_MODEL), lambda j: (0, 0))],
        out_specs=pl.BlockSpec((COMBINE_TILE, D_MODEL), lambda j: (j, 0)),
        out_shape=jax.ShapeDtypeStruct((TOKENS, D_MODEL), jnp.float32),
        scratch_shapes=[pltpu.VMEM((2, 2, groups, SUBLANES, PACKED), jnp.uint32),
                        pltpu.SemaphoreType.DMA((2,))],
        compiler_params=_params(("arbitrary",)),
        name="combine_ln2",
    )(pos3, pos3, routing, x1, y_sorted, g, b)


def _alibi_slopes():
    n = N_DIL_GROUPS * HEADS_PER_GROUP
    return jnp.asarray(2.0 ** (-ALIBI_MAX * np.arange(1, n + 1, dtype=np.float32) / n), jnp.float32)


def kernel(x, mem, ln_mem_g, ln_mem_b, w_in, b_in, w_conv, w_conv_out, w_dil_out, w_mem_kv, w_mem_out, w_o, ln1_g, ln1_b, w_group, b_group, w_router, b_router, w_gate, w_up, w_down, ln2_g, ln2_b):
    assert x.shape == (BATCH, SEQ, D_MODEL) and w_in.shape == (1, D_MODEL, IN_DIM)
    bf16 = jnp.bfloat16
    row = lambda v: v.reshape(1, -1)
    w_in2 = w_in[0].astype(bf16)
    b_in2 = b_in

    kv = _memkv(mem, row(ln_mem_g), row(ln_mem_b), w_mem_kv[0].astype(bf16))
    s_conv, xb = _conv_branch(x, w_in2, b_in2, w_conv[0])
    qkv = _qkv_proj(xb, w_in2, b_in2)
    o_mem = _mem_branch(xb, w_in2, b_in2, kv)
    o_dil = _dil_branch(qkv, _alibi_slopes())

    x2 = x.reshape(TOKENS, D_MODEL)
    merged = _merge(xb.reshape(TOKENS, D_MODEL), w_in2, b_in2,
                    s_conv.reshape(TOKENS, CONV_DIM), o_dil.reshape(TOKENS, DIL_OUT_DIM),
                    o_mem.reshape(TOKENS, MEM_DIM),
                    jnp.concatenate([w_conv_out[0], w_dil_out[0], w_mem_out[0]], axis=0).astype(bf16))

    w_route = jnp.concatenate(
        [w_group[0], jnp.transpose(w_router[0], (1, 0, 2)).reshape(D_MODEL, N_EXPERTS),
         jnp.zeros((D_MODEL, LANES - N_GROUPS - N_EXPERTS), jnp.float32)], axis=1)
    b_route = jnp.concatenate(
        [b_group[0], b_router[0].reshape(N_EXPERTS),
         jnp.zeros((LANES - N_GROUPS - N_EXPERTS,), jnp.float32)]).reshape(1, LANES)
    x1, logits = _oproj(x2, merged, w_o[0].astype(bf16), ln1_g, ln1_b, w_route, b_route)

    routing, counts8 = _route(logits)
    pos = _slots(routing, counts8)[0:2]

    def pos_tiles(tile):
        return jnp.transpose(pos.reshape(2, TOKENS // tile, tile), (1, 0, 2)).reshape(TOKENS // tile, 1, 2 * tile)

    i32 = jnp.int32
    counts = counts8[0, ROUTER_LANE0:ROUTER_LANE0 + N_EXPERTS].astype(i32)
    padded = ((counts + EXPERT_TILE - 1) // EXPERT_TILE) * EXPERT_TILE
    ends = jnp.cumsum(padded)
    starts = ends - padded
    tile_start = jnp.arange(N_EXPERT_TILES, dtype=i32) * EXPERT_TILE
    tile_expert = jnp.minimum(jnp.sum((ends[None, :] <= tile_start[:, None]).astype(i32), axis=1),
                              N_EXPERTS - 1)
    tile_valid = tile_start < ends[-1]
    prev_expert = jnp.concatenate([jnp.full((1,), -1, i32), tile_expert[:-1]])
    tile_first = tile_valid & (tile_expert != prev_expert)
    tile_slot = (jnp.cumsum(tile_first.astype(i32)) - 1) & 1
    big = N_EXPERTS
    idx = jnp.where(counts > 0, jnp.arange(N_EXPERTS, dtype=i32), big)
    later = jnp.concatenate([lax.cummin(idx[::-1])[::-1][1:], jnp.full((1,), big, i32)])
    next_used = jnp.where(later == big, -1, later)
    n_valid_tiles = (ends[-1:] // EXPERT_TILE).astype(i32)
    tables = (tile_expert, tile_valid.astype(i32), tile_slot, tile_first.astype(i32),
              next_used[tile_expert], n_valid_tiles)

    xs = _dispatch(starts + counts, padded - counts, n_valid_tiles,
                   pos_tiles(DISPATCH_TILE), x1)
    y_sorted = _expert_ffn(tables, xs,
                           w_gate.reshape(N_EXPERTS, D_MODEL, D_EXPERT),
                           w_up.reshape(N_EXPERTS, D_MODEL, D_EXPERT),
                           w_down.reshape(N_EXPERTS, D_EXPERT, D_MODEL))
    out = _combine(pos_tiles(COMBINE_TILE), routing, x1, y_sorted,
                   ln2_g, ln2_b)
    return out.reshape(BATCH, SEQ, D_MODEL)
```

```python
import functools
import math

import numpy as np
import jax
import jax.numpy as jnp
from jax import lax
from jax.experimental import pallas as pl
from jax.experimental.pallas import tpu as pltpu

D_MODEL = 2048
BATCH = 8
SEQ = 2048
TOKENS = BATCH * SEQ
CONV_DIM = 1024
CONV_WIDTH = 3
DIL_PATTERNS = ((128, 1), (512, 4), (2048, 16))
N_DIL_GROUPS = 3
HEADS_PER_GROUP = 4
HEAD_DIM = 128
DIL_DIM = N_DIL_GROUPS * HEADS_PER_GROUP * HEAD_DIM
DIL_OUT_DIM = HEADS_PER_GROUP * HEAD_DIM
ATT_BLOCK = 128
ALIBI_MAX = 8.0
MEM_LEN = 256
MEM_HEADS = 4
MEM_HEAD_DIM = 256
MEM_DIM = MEM_HEADS * MEM_HEAD_DIM
N_BRANCHES = 3
IN_DIM = 3 * CONV_DIM + 3 * DIL_DIM + MEM_DIM + N_BRANCHES * D_MODEL
N_GROUPS = 4
EXPERTS_PER_GROUP = 8
N_EXPERTS = N_GROUPS * EXPERTS_PER_GROUP
D_EXPERT = 512
ALPHA = 2.0 ** 0.25
LN_EPS = 1e-5

OFF_CB = 0
OFF_CC = CONV_DIM
OFF_CH = 2 * CONV_DIM
OFF_Q = 3 * CONV_DIM
OFF_MQ = OFF_Q + 3 * DIL_DIM
OFF_GATE = OFF_MQ + MEM_DIM

LANES = 128
HALF = SEQ // 2
VMEM_LIMIT = 56 * 1024 * 1024

ROUTE_TILE = 1024
EXPERT_TILE = 384
N_EXPERT_TILES = 2 * TOKENS // EXPERT_TILE + N_EXPERTS
COMBINE_TILE = 256
ROUTER_LANE0 = N_GROUPS


def _params(sem, limit=VMEM_LIMIT):
    return pltpu.CompilerParams(dimension_semantics=sem, vmem_limit_bytes=limit)


def _layer_norm(x, g, b):
    mu = jnp.mean(x, axis=-1, keepdims=True)
    xc = x - mu
    var = jnp.mean(xc * xc, axis=-1, keepdims=True)
    return xc * lax.rsqrt(var + LN_EPS) * g + b


def _dot(a, b):
    return jnp.dot(a, b, preferred_element_type=jnp.float32)


def _dot_t(a, b):
    return lax.dot_general(a, b, (((1,), (1,)), ((), ())), preferred_element_type=jnp.float32)


MEMKV_BATCHES = 2


def _memkv_kernel(mem_ref, g_ref, b_ref, w_ref, kv_ref):
    rows = MEMKV_BATCHES * MEM_LEN
    y = _layer_norm(mem_ref[...].reshape(rows, D_MODEL), g_ref[...], b_ref[...])
    kv = _dot(y.astype(jnp.bfloat16), w_ref[...])
    kv_ref[...] = kv.astype(kv_ref.dtype).reshape(MEMKV_BATCHES, MEM_LEN, 2 * MEM_DIM)


def _memkv(mem, g, b, w):
    return pl.pallas_call(
        _memkv_kernel,
        grid=(BATCH // MEMKV_BATCHES,),
        in_specs=[pl.BlockSpec((MEMKV_BATCHES, MEM_LEN, D_MODEL), lambda i: (i, 0, 0)),
                  pl.BlockSpec((1, D_MODEL), lambda i: (0, 0)),
                  pl.BlockSpec((1, D_MODEL), lambda i: (0, 0)),
                  pl.BlockSpec((D_MODEL, 2 * MEM_DIM), lambda i: (0, 0))],
        out_specs=pl.BlockSpec((MEMKV_BATCHES, MEM_LEN, 2 * MEM_DIM), lambda i: (i, 0, 0)),
        out_shape=jax.ShapeDtypeStruct((BATCH, MEM_LEN, 2 * MEM_DIM), jnp.bfloat16),
        compiler_params=_params(("parallel",)),
        name="mem_kv",
    )(mem, g, b, w)


CONV_TC = 512


def _conv_kernel(x_ref, wb_ref, wc_ref, wh_ref, bb_ref, bc_ref, bh_ref, wconv_ref, s_ref, xb_ref,
                 u_sc, carry_sc):
    half = pl.program_id(1)
    c = pl.program_id(2)

    @pl.when(c == 0)
    def _():
        xb_ref[...] = x_ref[...].astype(xb_ref.dtype)

    x = xb_ref[...]
    cc = _dot(x, wc_ref[...]) + bc_ref[...]
    ch = _dot(x, wh_ref[...]) + bh_ref[...]
    u = cc * ch
    u_sc[0:8, :] = jnp.where(half == 0, 0.0, carry_sc[c])
    u_sc[8:8 + HALF, :] = u
    carry_sc[c] = u[HALF - 8:HALF, :]
    wconv = wconv_ref[...]
    y = (wconv[2:3, :] * u
         + wconv[1:2, :] * u_sc[7:7 + HALF, :]
         + wconv[0:1, :] * u_sc[6:6 + HALF, :])
    cb = _dot(x, wb_ref[...]) + bb_ref[...]
    s_ref[...] = (cb * y).astype(s_ref.dtype)


def _conv_branch(x, w_in, b_in, w_conv):
    nb = lambda off: off // CONV_TC
    wspec = lambda off: pl.BlockSpec((D_MODEL, CONV_TC), lambda b, h, c, o=nb(off): (0, o + c))
    bspec = lambda off: pl.BlockSpec((1, CONV_TC), lambda b, h, c, o=nb(off): (0, o + c))
    return pl.pallas_call(
        _conv_kernel,
        grid=(BATCH, 2, CONV_DIM // CONV_TC),
        in_specs=[pl.BlockSpec((None, HALF, D_MODEL), lambda b, h, c: (b, h, 0)),
                  wspec(OFF_CB), wspec(OFF_CC), wspec(OFF_CH),
                  bspec(OFF_CB), bspec(OFF_CC), bspec(OFF_CH),
                  pl.BlockSpec((CONV_WIDTH, CONV_TC), lambda b, h, c: (0, c))],
        out_specs=[pl.BlockSpec((None, HALF, CONV_TC), lambda b, h, c: (b, h, c)),
                   pl.BlockSpec((None, HALF, D_MODEL), lambda b, h, c: (b, h, 0))],
        out_shape=[jax.ShapeDtypeStruct((BATCH, SEQ, CONV_DIM), jnp.bfloat16),
                   jax.ShapeDtypeStruct((BATCH, SEQ, D_MODEL), jnp.bfloat16)],
        scratch_shapes=[pltpu.VMEM((HALF + 8, CONV_TC), jnp.float32),
                        pltpu.VMEM((CONV_DIM // CONV_TC, 8, CONV_TC), jnp.float32)],
        compiler_params=_params(("arbitrary", "arbitrary", "arbitrary")),
        name="conv_branch",
    )(x, w_in, w_in, w_in, b_in, b_in, b_in, w_conv)


QKV_TN = 512
QKV_CHUNKS = QKV_TN // LANES


def _qkv_kernel(x_ref, w_ref, b_ref, o_ref, sc_ref, sc2_ref):
    x = x_ref[...]
    for gi in (2, 1, 0):
        cols = slice(gi * QKV_TN, (gi + 1) * QKV_TN)
        acc = _dot(x, w_ref[:, cols]) + b_ref[:, cols]
        d = DIL_PATTERNS[gi][1]
        if d == 1:
            o_ref[:, cols] = acc.astype(o_ref.dtype)
            continue
        rows = HALF // d
        for c in range(QKV_CHUNKS):
            sc_ref[gi - 1, c] = acc[:, c * LANES:(c + 1) * LANES]
        if d == 16:
            q4 = HALF // 4
            for c in range(QKV_CHUNKS):
                for r in range(4):
                    sc2_ref[c, r * q4:(r + 1) * q4, :] = sc_ref[gi - 1, c, pl.ds(r, q4, stride=4), :]
            for c in range(QKV_CHUNKS):
                lo = gi * QKV_TN + c * LANES
                for r in range(d):
                    r_lo, r_hi = r % 4, r // 4
                    o_ref[r * rows:(r + 1) * rows, lo:lo + LANES] = (
                        sc2_ref[c, pl.ds(r_lo * q4 + r_hi, rows, stride=4), :].astype(o_ref.dtype))
            continue
        for c in range(QKV_CHUNKS):
            for r in range(d):
                lo = gi * QKV_TN + c * LANES
                o_ref[r * rows:(r + 1) * rows, lo:lo + LANES] = (
                    sc_ref[gi - 1, c, pl.ds(r, rows, stride=d), :].astype(o_ref.dtype))


def _qkv_proj(xb, w_in, b_in):
    n0 = OFF_Q // DIL_DIM
    return pl.pallas_call(
        _qkv_kernel,
        grid=(BATCH, 2, 3),
        in_specs=[pl.BlockSpec((None, HALF, D_MODEL), lambda b, h, n: (b, h, 0)),
                  pl.BlockSpec((D_MODEL, DIL_DIM), lambda b, h, n: (0, n0 + n)),
                  pl.BlockSpec((1, DIL_DIM), lambda b, h, n: (0, n0 + n))],
        out_specs=pl.BlockSpec((None, HALF, DIL_DIM), lambda b, h, n: (b, h, n)),
        out_shape=jax.ShapeDtypeStruct((BATCH, SEQ, 3 * DIL_DIM), jnp.bfloat16),
        scratch_shapes=[pltpu.VMEM((N_DIL_GROUPS - 1, QKV_CHUNKS, HALF, LANES), jnp.float32),
                        pltpu.VMEM((QKV_CHUNKS, HALF, LANES), jnp.float32)],
        compiler_params=_params(("parallel", "parallel", "arbitrary")),
        name="qkv_proj",
    )(xb, w_in, b_in)


MEM_HEADS_PER_STEP = 2
MEM_TN = MEM_HEADS_PER_STEP * MEM_HEAD_DIM


def _memattn_kernel(x_ref, w_ref, b_ref, mk_ref, mv_ref, o_ref):
    mq_all = (_dot(x_ref[...], w_ref[...]) + b_ref[...]).astype(jnp.bfloat16)
    for hh in range(MEM_HEADS_PER_STEP):
        cols = slice(hh * MEM_HEAD_DIM, (hh + 1) * MEM_HEAD_DIM)
        s = _dot_t(mq_all[:, cols], mk_ref[:, cols]) * (MEM_HEAD_DIM ** -0.5)
        m = jnp.max(s, axis=-1, keepdims=True)
        p = jnp.exp(s - m)
        den = jnp.sum(p, axis=-1, keepdims=True)
        o = _dot(p.astype(jnp.bfloat16), mv_ref[:, cols]) / den
        o_ref[:, cols] = o.astype(o_ref.dtype)


def _mem_branch(xb, w_in, b_in, kv):
    n0 = OFF_MQ // MEM_TN
    nv = MEM_DIM // MEM_TN
    return pl.pallas_call(
        _memattn_kernel,
        grid=(BATCH, 2, MEM_DIM // MEM_TN),
        in_specs=[pl.BlockSpec((None, HALF, D_MODEL), lambda b, h, n: (b, h, 0)),
                  pl.BlockSpec((D_MODEL, MEM_TN), lambda b, h, n: (0, n0 + n)),
                  pl.BlockSpec((1, MEM_TN), lambda b, h, n: (0, n0 + n)),
                  pl.BlockSpec((None, MEM_LEN, MEM_TN), lambda b, h, n: (b, 0, n)),
                  pl.BlockSpec((None, MEM_LEN, MEM_TN), lambda b, h, n: (b, 0, nv + n))],
        out_specs=pl.BlockSpec((None, HALF, MEM_TN), lambda b, h, n: (b, h, n)),
        out_shape=jax.ShapeDtypeStruct((BATCH, SEQ, MEM_DIM), jnp.bfloat16),
        compiler_params=_params(("parallel", "parallel", "arbitrary")),
        name="mem_branch",
    )(xb, w_in, b_in, kv, kv)


ATT_UNROLL = 16


def _softmax_block(s, v):
    m = jnp.max(s, axis=-1, keepdims=True)
    p = jnp.exp(s - m)
    den = jnp.sum(p, axis=-1, keepdims=True)
    o = _dot(p.astype(jnp.bfloat16), v) / den
    return o, m + jnp.log(den)


def _dilattn_kernel(slopes_ref,
                    q0_ref, q1_ref, q2_ref, k0_ref, k1_ref, k2_ref, v0_ref, v1_ref, v2_ref,
                    o_ref, o_sc, l_sc):
    h = pl.program_id(1)
    blk = ATT_BLOCK
    scale = HEAD_DIM ** -0.5
    qi = lax.broadcasted_iota(jnp.int32, (blk, 2 * blk), 0) + blk
    kj = lax.broadcasted_iota(jnp.int32, (blk, 2 * blk), 1)
    jrel = qi - kj
    valid = (jrel >= 0) & (jrel <= blk)
    jrel_f = jrel.astype(jnp.float32)

    def bias_for(g):
        slope = slopes_ref[g * HEADS_PER_GROUP + h]
        d = float(DIL_PATTERNS[g][1])
        return jnp.where(valid, (-slope * d) * jrel_f, -jnp.inf)

    def put(g, row_slice, o, lse):
        o_sc[g, row_slice, :] = o
        l_sc[g, row_slice, :] = jnp.broadcast_to(lse, (blk, HEAD_DIM))

    def run_blocks(g, blocks):
        scores = [_dot_t(q, k) * scale + bias for q, k, _, bias, _ in blocks]
        stats = []
        for s in scores:
            m = jnp.max(s, axis=-1, keepdims=True)
            p = jnp.exp(s - m)
            stats.append((m, p, jnp.sum(p, axis=-1, keepdims=True)))
        outs = [_dot(p.astype(jnp.bfloat16), blkdef[2]) / den
                for (m, p, den), blkdef in zip(stats, blocks)]
        for o, (m, p, den), blkdef in zip(outs, stats, blocks):
            put(g, blkdef[4], o, m + jnp.log(den))

    bias0 = bias_for(0)
    prev_cols = kj < blk

    def g0_body(it, carry):
        blocks = []
        for k in range(ATT_UNROLL):
            n = it * ATT_UNROLL + k
            q0 = pl.multiple_of(n * blk, blk)
            k0 = pl.multiple_of(jnp.maximum(n - 1, 0) * blk, blk)
            bias = jnp.where(prev_cols & (n == 0), -jnp.inf, bias0)
            blocks.append((q0_ref[pl.ds(q0, blk), :], k0_ref[pl.ds(k0, 2 * blk), :],
                           v0_ref[pl.ds(k0, 2 * blk), :], bias, pl.ds(q0, blk)))
        run_blocks(0, blocks)
        return carry

    lax.fori_loop(0, SEQ // blk // ATT_UNROLL, g0_body, 0)

    d1 = DIL_PATTERNS[1][1]
    cls1 = HALF // d1
    per_half = cls1 // blk
    bias1 = bias_for(1)

    def row1(r, n):
        return (n // per_half) * HALF + r * cls1 + (n % per_half) * blk

    nblk1 = SEQ // d1 // blk
    cls_per_trip = ATT_UNROLL // nblk1

    def g1_body(it, carry):
        blocks = []
        for c in range(cls_per_trip):
            r = it * cls_per_trip + c
            for n in range(nblk1):
                cur = pl.multiple_of(row1(r, n), blk)
                q = q1_ref[pl.ds(cur, blk), :]
                dst = pl.ds(n * blk * d1 + r, blk, stride=d1)
                if n == 0:
                    blocks.append((q, k1_ref[pl.ds(cur, blk), :], v1_ref[pl.ds(cur, blk), :],
                                   bias1[:, blk:], dst))
                else:
                    prev = pl.multiple_of(row1(r, n - 1), blk)
                    kc = jnp.concatenate([k1_ref[pl.ds(prev, blk), :], k1_ref[pl.ds(cur, blk), :]], axis=0)
                    vc = jnp.concatenate([v1_ref[pl.ds(prev, blk), :], v1_ref[pl.ds(cur, blk), :]], axis=0)
                    blocks.append((q, kc, vc, bias1, dst))
        run_blocks(1, blocks)
        return carry

    lax.fori_loop(0, d1 // cls_per_trip, g1_body, 0)

    d2 = DIL_PATTERNS[2][1]
    cls2 = HALF // d2
    bias2 = bias_for(2)

    def g2_body(it, carry):
        blocks = []
        for k in range(ATT_UNROLL):
            r = it * ATT_UNROLL + k
            a = pl.multiple_of(r * cls2, cls2)
            b = pl.multiple_of(HALF + r * cls2, cls2)
            cat = lambda ref, a=a, b=b: jnp.concatenate(
                [ref[pl.ds(a, cls2), :], ref[pl.ds(b, cls2), :]], axis=0)
            blocks.append((cat(q2_ref), cat(k2_ref), cat(v2_ref), bias2[:, blk:],
                           pl.ds(r, blk, stride=d2)))
        run_blocks(2, blocks)
        return carry

    lax.fori_loop(0, d2 // ATT_UNROLL, g2_body, 0)

    rows = 256
    for t in range(SEQ // rows):
        sl = pl.ds(t * rows, rows)
        l0, l1, l2 = l_sc[0, sl, :], l_sc[1, sl, :], l_sc[2, sl, :]
        m = jnp.maximum(jnp.maximum(l0, l1), l2)
        e0, e1, e2 = jnp.exp(l0 - m), jnp.exp(l1 - m), jnp.exp(l2 - m)
        mix = (e0 * o_sc[0, sl, :] + e1 * o_sc[1, sl, :] + e2 * o_sc[2, sl, :]) / (e0 + e1 + e2)
        o_ref[sl, :] = mix.astype(o_ref.dtype)


def _dil_branch(qkv, slopes):
    nq = DIL_DIM // HEAD_DIM

    def spec(section, g):
        return pl.BlockSpec((None, SEQ, HEAD_DIM),
                            lambda b, h, s_ref, o=section * nq + g * HEADS_PER_GROUP: (b, 0, o + h))

    grid_spec = pltpu.PrefetchScalarGridSpec(
        num_scalar_prefetch=1,
        grid=(BATCH, HEADS_PER_GROUP),
        in_specs=[spec(sec, g) for sec in range(3) for g in range(N_DIL_GROUPS)],
        out_specs=pl.BlockSpec((None, SEQ, HEAD_DIM), lambda b, h, s_ref: (b, 0, h)),
        scratch_shapes=[pltpu.VMEM((N_DIL_GROUPS, SEQ, HEAD_DIM), jnp.float32),
                        pltpu.VMEM((N_DIL_GROUPS, SEQ, HEAD_DIM), jnp.float32)],
    )
    return pl.pallas_call(
        _dilattn_kernel,
        grid_spec=grid_spec,
        out_shape=jax.ShapeDtypeStruct((BATCH, SEQ, DIL_OUT_DIM), jnp.bfloat16),
        compiler_params=_params(("parallel", "arbitrary")),
        name="dil_attn",
    )(slopes, *([qkv] * 9))


MERGE_TM = 1024
MERGE_TN = 512


def _merge_kernel(x_ref, wg0_ref, wg1_ref, wg2_ref, bg0_ref, bg1_ref, bg2_ref,
                  sc_ref, od_ref, om_ref, wout_ref, o_ref):
    r1, r2 = CONV_DIM, CONV_DIM + DIL_OUT_DIM
    x = x_ref[...]
    g0 = jax.nn.sigmoid(_dot(x, wg0_ref[...]) + bg0_ref[...])
    acc = g0 * _dot(sc_ref[...], wout_ref[0:r1, :])
    g1 = jax.nn.sigmoid(_dot(x, wg1_ref[...]) + bg1_ref[...])
    acc = acc + g1 * _dot(od_ref[...], wout_ref[r1:r2, :])
    g2 = jax.nn.sigmoid(_dot(x, wg2_ref[...]) + bg2_ref[...])
    acc = acc + g2 * _dot(om_ref[...], wout_ref[r2:r2 + MEM_DIM, :])
    o_ref[...] = acc.astype(o_ref.dtype)


def _merge(x2, w_in, b_in, s_conv, o_dil, o_mem, w_out):
    nb = lambda br: (OFF_GATE + br * D_MODEL) // MERGE_TN
    gspec = lambda br: pl.BlockSpec((D_MODEL, MERGE_TN), lambda i, n, o=nb(br): (0, o + n))
    bspec = lambda br: pl.BlockSpec((1, MERGE_TN), lambda i, n, o=nb(br): (0, o + n))
    act = lambda width: pl.BlockSpec((MERGE_TM, width), lambda i, n: (i, 0))
    wout = lambda width: pl.BlockSpec((width, MERGE_TN), lambda i, n: (0, n))
    return pl.pallas_call(
        _merge_kernel,
        grid=(TOKENS // MERGE_TM, D_MODEL // MERGE_TN),
        in_specs=[act(D_MODEL), gspec(0), gspec(1), gspec(2), bspec(0), bspec(1), bspec(2),
                  act(CONV_DIM), act(DIL_OUT_DIM), act(MEM_DIM),
                  wout(CONV_DIM + DIL_OUT_DIM + MEM_DIM)],
        out_specs=pl.BlockSpec((MERGE_TM, MERGE_TN), lambda i, n: (i, n)),
        out_shape=jax.ShapeDtypeStruct((TOKENS, D_MODEL), jnp.bfloat16),
        compiler_params=_params(("parallel", "arbitrary")),
        name="gated_merge",
    )(x2, w_in, w_in, w_in, b_in, b_in, b_in, s_conv, o_dil, o_mem, w_out)


OPROJ_TM = 512


OPROJ_PARTS = 4


def _oproj_kernel(x_ref, m_ref, wo_ref, g_ref, b_ref, wr_ref, br_ref, x1_ref, logit_ref):
    rows = OPROJ_TM // OPROJ_PARTS
    parts = [pl.ds(i * rows, rows) for i in range(OPROJ_PARTS)]
    proj = [_dot(m_ref[p, :], wo_ref[...]) for p in parts]
    for p, y in zip(parts, proj):
        x1 = _layer_norm(ALPHA * x_ref[p, :] + y, g_ref[...], b_ref[...])
        x1_ref[p, :] = x1
        logit_ref[p, :] = _dot(x1, wr_ref[...]) + br_ref[...]


def _oproj(x2, merged, w_o, g, b, w_route, b_route):
    row = lambda width: pl.BlockSpec((OPROJ_TM, width), lambda i: (i, 0))
    full = lambda r, c: pl.BlockSpec((r, c), lambda i: (0, 0))
    return pl.pallas_call(
        _oproj_kernel,
        grid=(TOKENS // OPROJ_TM,),
        in_specs=[row(D_MODEL), row(D_MODEL), full(D_MODEL, D_MODEL), full(1, D_MODEL), full(1, D_MODEL),
                  full(D_MODEL, LANES), full(1, LANES)],
        out_specs=[row(D_MODEL), row(LANES)],
        out_shape=[jax.ShapeDtypeStruct((TOKENS, D_MODEL), jnp.float32),
                   jax.ShapeDtypeStruct((TOKENS, LANES), jnp.float32)],
        compiler_params=_params(("parallel",)),
        name="oproj_ln1",
    )(x2, merged, w_o, g, b, w_route, b_route)


def _route_kernel(logit_ref, o_ref, cnt_ref, carry_sc):
    @pl.when(pl.program_id(0) == 0)
    def _():
        carry_sc[...] = jnp.zeros_like(carry_sc)

    tm = ROUTE_TILE
    z = logit_ref[...]
    lane = lax.broadcasted_iota(jnp.int32, (tm, LANES), 1)
    neg = -jnp.inf
    first = lambda hit: jnp.min(jnp.where(hit, lane, LANES), axis=-1, keepdims=True)

    glog = jnp.where(lane < N_GROUPS, z, neg)
    gmax = jnp.max(glog, axis=-1, keepdims=True)
    gsel = first(glog == gmax)
    gw = 1.0 / jnp.sum(jnp.exp(glog - gmax), axis=-1, keepdims=True)

    e_lane = lane - ROUTER_LANE0
    in_group = (e_lane >= 0) & (e_lane < N_EXPERTS) & ((e_lane // EXPERTS_PER_GROUP) == gsel)
    v = jnp.where(in_group, z, neg)
    top1 = jnp.max(v, axis=-1, keepdims=True)
    i1 = first(v == top1)
    v2 = jnp.where(lane == i1, neg, v)
    top2 = jnp.max(v2, axis=-1, keepdims=True)
    i2 = first(v2 == top2)
    t = jnp.exp(top2 - top1)
    w1 = gw / (1.0 + t)
    w2 = w1 * t

    onehot = ((lane == i1) | (lane == i2))
    ri = lax.broadcasted_iota(jnp.int32, (tm, tm), 0)
    ci = lax.broadcasted_iota(jnp.int32, (tm, tm), 1)
    tri = jnp.where(ci <= ri, 1.0, 0.0).astype(jnp.bfloat16)
    cum = _dot(tri, jnp.where(onehot, 1.0, 0.0).astype(jnp.bfloat16))
    carry = carry_sc[0:1, :]
    before = cum + carry - 1.0
    rank1 = jnp.sum(jnp.where(lane == i1, before, 0.0), axis=-1, keepdims=True)
    rank2 = jnp.sum(jnp.where(lane == i2, before, 0.0), axis=-1, keepdims=True)
    new_carry = carry + cum[tm - 1:tm, :]
    carry_sc[...] = jnp.broadcast_to(new_carry, carry_sc.shape)
    cnt_ref[...] = jnp.broadcast_to(new_carry, cnt_ref.shape)

    cols = [(i1 - ROUTER_LANE0).astype(jnp.float32), (i2 - ROUTER_LANE0).astype(jnp.float32),
            rank1, rank2, w1, w2]
    out = jnp.zeros((tm, LANES), jnp.float32)
    for k, col in enumerate(cols):
        out = jnp.where(lane == k, col, out)
    o_ref[...] = out


def _route(logits):
    return pl.pallas_call(
        _route_kernel,
        grid=(TOKENS // ROUTE_TILE,),
        in_specs=[pl.BlockSpec((ROUTE_TILE, LANES), lambda i: (i, 0))],
        out_specs=[pl.BlockSpec((ROUTE_TILE, LANES), lambda i: (i, 0)),
                   pl.BlockSpec((8, LANES), lambda i: (0, 0))],
        out_shape=[jax.ShapeDtypeStruct((TOKENS, LANES), jnp.float32),
                   jax.ShapeDtypeStruct((8, LANES), jnp.float32)],
        scratch_shapes=[pltpu.VMEM((8, LANES), jnp.float32)],
        compiler_params=_params(("arbitrary",)),
        name="route",
    )(logits)


def _slots_kernel(route_ref, cnt_ref, pos_ref):
    lane8 = lax.broadcasted_iota(jnp.int32, (8, LANES), 1)
    cnt = cnt_ref[...]
    padded = jnp.floor((cnt + (EXPERT_TILE - 1)) / EXPERT_TILE) * EXPERT_TILE
    incl = padded
    for sh in (1, 2, 4, 8, 16, 32, 64):
        incl = incl + jnp.where(lane8 >= sh, pltpu.roll(incl, sh, axis=1), 0.0)
    start = (incl - padded)[0:1, :]

    r = route_ref[...]
    lane = lax.broadcasted_iota(jnp.int32, r.shape, 1).astype(jnp.float32)
    pick = lambda e: jnp.sum(jnp.where(lane == e + float(ROUTER_LANE0), start, 0.0), axis=-1, keepdims=True)
    pos_a = pick(r[:, 0:1]) + r[:, 2:3]
    pos_b = pick(r[:, 1:2]) + r[:, 3:4]
    both = jnp.where(lane == 0.0, pos_a, jnp.where(lane == 1.0, pos_b, 0.0))
    pos_ref[...] = jnp.transpose(both)[0:SUBLANES, :].astype(jnp.int32)


def _slots(routing, counts8):
    return pl.pallas_call(
        _slots_kernel,
        grid=(TOKENS // ROUTE_TILE,),
        in_specs=[pl.BlockSpec((ROUTE_TILE, LANES), lambda i: (i, 0)),
                  pl.BlockSpec((8, LANES), lambda i: (0, 0))],
        out_specs=pl.BlockSpec((SUBLANES, ROUTE_TILE), lambda i: (0, i)),
        out_shape=jax.ShapeDtypeStruct((SUBLANES, TOKENS), jnp.int32),
        compiler_params=_params(("parallel",)),
        name="slots",
    )(routing, counts8)


SUBLANES = 8
DISPATCH_TILE = 512
SORTED_ROWS = N_EXPERT_TILES * EXPERT_TILE


PACKED = D_MODEL // 2


def _pack_rows(v):
    words = pltpu.pack_elementwise([v[:, :PACKED], v[:, PACKED:]], packed_dtype=jnp.bfloat16)
    return lax.bitcast_convert_type(words, jnp.uint32)


def _unpack_rows(w):
    half = lambda i: pltpu.unpack_elementwise(w, index=i, packed_dtype=jnp.bfloat16,
                                              unpacked_dtype=jnp.float32)
    return half(0), half(1)


def _dispatch_kernel(pad_start_ref, pad_n_ref, nvalid_ref, pos_ref, x_ref, xs_hbm, zero_sc, pk_sc, sem,
                     row_sem):
    groups = DISPATCH_TILE // SUBLANES

    @pl.when(pl.program_id(0) == 0)
    def _():
        zero_sc[...] = jnp.zeros_like(zero_sc)

        def tile_copy(t):
            return pltpu.make_async_copy(zero_sc, xs_hbm.at[pl.ds(pl.multiple_of(t * EXPERT_TILE, EXPERT_TILE),
                                                                  EXPERT_TILE)], sem.at[2])

        def start_tile(t, c):
            tile_copy(t).start()
            return c

        def wait_tile(t, c):
            tile_copy(t).wait()
            return c
        lax.fori_loop(nvalid_ref[0], N_EXPERT_TILES, start_tile, 0)
        lax.fori_loop(nvalid_ref[0], N_EXPERT_TILES, wait_tile, 0)

        def pad_copies(e, act):
            n = pad_n_ref[e]
            cur = pad_start_ref[e]
            for bit in range((EXPERT_TILE - 1).bit_length()):
                size = 1 << bit
                has = lax.bitwise_and(n, size)

                @pl.when(has != 0)
                def _(cur=cur, size=size):
                    at = pl.multiple_of(cur, math.gcd(size, EXPERT_TILE))
                    act(pltpu.make_async_copy(zero_sc.at[pl.ds(0, size)], xs_hbm.at[pl.ds(at, size)],
                                              sem.at[1]))
                cur = cur + has

        def start_pads(e, c):
            pad_copies(e, lambda cp: cp.start())
            return c

        def wait_pads(e, c):
            pad_copies(e, lambda cp: cp.wait())
            return c
        lax.fori_loop(0, N_EXPERTS, start_pads, 0)
        lax.fori_loop(0, N_EXPERTS, wait_pads, 0)

    j = pl.program_id(0)
    slot = lax.rem(j, 2)
    pk_sc[slot] = _pack_rows(x_ref[...]).reshape(groups, SUBLANES, PACKED)

    def body(gi, c):
        for s in range(SUBLANES):
            for k in range(2):
                p = pos_ref[0, 0, k * DISPATCH_TILE + gi * SUBLANES + s]
                pltpu.make_async_copy(pk_sc.at[slot, gi, pl.ds(s, 1)], xs_hbm.at[pl.ds(p, 1)],
                                      row_sem.at[slot]).start(priority=k)
        return c
    lax.fori_loop(0, groups, body, 0)

    def wait_slot(sl):
        for _ in range(2 * groups):
            pltpu.make_async_copy(pk_sc.at[sl, 0], xs_hbm.at[pl.ds(0, SUBLANES)], row_sem.at[sl]).wait()

    @pl.when(j > 0)
    def _():
        wait_slot(1 - slot)

    @pl.when(j == pl.num_programs(0) - 1)
    def _():
        wait_slot(slot)


def _dispatch(pad_start, pad_n, n_valid_tiles, pos3, x1):
    nt = TOKENS // DISPATCH_TILE
    groups = DISPATCH_TILE // SUBLANES
    grid_spec = pltpu.PrefetchScalarGridSpec(
        num_scalar_prefetch=3,
        grid=(nt,),
        in_specs=[pl.BlockSpec((1, 1, 2 * DISPATCH_TILE), lambda j, *_: (j, 0, 0), memory_space=pltpu.SMEM),
                  pl.BlockSpec((DISPATCH_TILE, D_MODEL), lambda j, *_: (j, 0))],
        out_specs=pl.BlockSpec(memory_space=pl.ANY),
        scratch_shapes=[pltpu.VMEM((EXPERT_TILE, PACKED), jnp.uint32),
                        pltpu.VMEM((2, groups, SUBLANES, PACKED), jnp.uint32),
                        pltpu.SemaphoreType.DMA((3,)),
                        pltpu.SemaphoreType.DMA((2,))],
    )
    return pl.pallas_call(
        _dispatch_kernel,
        grid_spec=grid_spec,
        out_shape=jax.ShapeDtypeStruct((SORTED_ROWS, PACKED), jnp.uint32),
        compiler_params=_params(("arbitrary",)),
        name="dispatch",
    )(pad_start, pad_n, n_valid_tiles, pos3, x1)


def _ffn_kernel(texp_ref, tvalid_ref, tslot_ref, tfirst_ref, tnext_ref, nvalid_ref,
                xs_ref, wg_hbm, wu_hbm, wd_hbm, y_ref, wg_buf, wu_buf, wd_buf, sem):
    j = pl.program_id(0)
    valid = tvalid_ref[j] == 1
    slot = tslot_ref[j]

    def weight_copies(e, s):
        return (pltpu.make_async_copy(wg_hbm.at[e], wg_buf.at[s], sem.at[s, 0]),
                pltpu.make_async_copy(wu_hbm.at[e], wu_buf.at[s], sem.at[s, 1]),
                pltpu.make_async_copy(wd_hbm.at[e], wd_buf.at[s], sem.at[s, 2]))

    @pl.when(valid & (tfirst_ref[j] == 1))
    def _():
        @pl.when(j == 0)
        def _():
            for cp in weight_copies(texp_ref[j], slot):
                cp.start()

        for cp in weight_copies(texp_ref[j], slot):
            cp.wait()

        @pl.when(tnext_ref[j] >= 0)
        def _():
            for cp in weight_copies(tnext_ref[j], 1 - slot):
                cp.start()

    @pl.when(valid)
    def _():
        lo, hi = _unpack_rows(xs_ref[...])
        a = _dot(lo, wg_buf[slot, 0:PACKED, :]) + _dot(hi, wg_buf[slot, PACKED:D_MODEL, :])
        u = _dot(lo, wu_buf[slot, 0:PACKED, :]) + _dot(hi, wu_buf[slot, PACKED:D_MODEL, :])
        hid = (a * jax.nn.sigmoid(a)) * u
        y_ref[...] = _pack_rows(_dot(hid, wd_buf[slot]))

    @pl.when(jnp.logical_not(valid))
    def _():
        y_ref[...] = jnp.zeros_like(y_ref)


def _expert_ffn(tables, xs, w_gate, w_up, w_down):
    nt = N_EXPERT_TILES
    xmap = lambda j, te, tv, ts, tf, tn, nv: (jnp.minimum(j, nv[0] - 1), 0)
    grid_spec = pltpu.PrefetchScalarGridSpec(
        num_scalar_prefetch=6,
        grid=(nt,),
        in_specs=[pl.BlockSpec((EXPERT_TILE, PACKED), xmap),
                  pl.BlockSpec(memory_space=pl.ANY),
                  pl.BlockSpec(memory_space=pl.ANY),
                  pl.BlockSpec(memory_space=pl.ANY)],
        out_specs=pl.BlockSpec((EXPERT_TILE, PACKED), lambda j, *_: (j, 0)),
        scratch_shapes=[pltpu.VMEM((2, D_MODEL, D_EXPERT), jnp.float32),
                        pltpu.VMEM((2, D_MODEL, D_EXPERT), jnp.float32),
                        pltpu.VMEM((2, D_EXPERT, D_MODEL), jnp.float32),
                        pltpu.SemaphoreType.DMA((2, 3))],
    )
    return pl.pallas_call(
        _ffn_kernel,
        grid_spec=grid_spec,
        out_shape=jax.ShapeDtypeStruct((nt * EXPERT_TILE, PACKED), jnp.uint32),
        compiler_params=_params(("arbitrary",)),
        name="expert_ffn",
    )(*tables, xs, w_gate, w_up, w_down)


def _combine_kernel(pos_cur_ref, pos_next_ref, r_ref, x1_ref, y_hbm, g_ref, b_ref, o_ref, ybuf, sem):
    j = pl.program_id(0)
    nt = pl.num_programs(0)
    slot = lax.rem(j, 2)
    groups = COMBINE_TILE // SUBLANES

    def gather(pos_ref, dst_slot):
        def body(gi, c):
            for s in range(SUBLANES):
                for k in range(2):
                    p = pos_ref[0, 0, k * COMBINE_TILE + gi * SUBLANES + s]
                    pltpu.make_async_copy(y_hbm.at[pl.ds(p, 1)], ybuf.at[dst_slot, k, gi, pl.ds(s, 1)],
                                          sem.at[dst_slot]).start(priority=k)
            return c
        lax.fori_loop(0, groups, body, 0)

    @pl.when(j == 0)
    def _():
        gather(pos_cur_ref, 0)

    @pl.when(j + 1 < nt)
    def _():
        gather(pos_next_ref, 1 - slot)

    for _ in range(2 * groups):
        pltpu.make_async_copy(y_hbm.at[pl.ds(0, SUBLANES)], ybuf.at[slot, 0, 0], sem.at[slot]).wait()
    r = r_ref[...]
    unpack = lambda k: jnp.concatenate(_unpack_rows(ybuf[slot, k].reshape(COMBINE_TILE, PACKED)), axis=1)
    ya, yb = unpack(0), unpack(1)
    hres = ALPHA * x1_ref[...] + r[:, 4:5] * ya + r[:, 5:6] * yb
    o_ref[...] = _layer_norm(hres, g_ref[...], b_ref[...])


def _combine(pos3, routing, x1, y_sorted, g, b):
    nt = TOKENS // COMBINE_TILE
    groups = COMBINE_TILE // SUBLANES
    return pl.pallas_call(
        _combine_kernel,
        grid=(nt,),
        in_specs=[pl.BlockSpec((1, 1, 2 * COMBINE_TILE), lambda j: (j, 0, 0), memory_space=pltpu.SMEM),
                  pl.BlockSpec((1, 1, 2 * COMBINE_TILE), lambda j: (jnp.minimum(j + 1, nt - 1), 0, 0),
                               memory_space=pltpu.SMEM),
                  pl.BlockSpec((COMBINE_TILE, LANES), lambda j: (j, 0)),
                  pl.BlockSpec((COMBINE_TILE, D_MODEL), lambda j: (j, 0)),
                  pl.BlockSpec(memory_space=pl.ANY),
                  pl.BlockSpec((1, D_MODEL), lambda j: (0, 0)),
                  pl.BlockSpec((1, D_MODEL), lambda j: (0, 0))],
        out_specs=pl.BlockSpec((COMBINE_TILE, D_MODEL), lambda j: (j, 0)),
        out_shape=jax.ShapeDtypeStruct((TOKENS, D_MODEL), jnp.float32),
        scratch_shapes=[pltpu.VMEM((2, 2, groups, SUBLANES, PACKED), jnp.uint32),
                        pltpu.SemaphoreType.DMA((2,))],
        compiler_params=_params(("arbitrary",)),
        name="combine_ln2",
    )(pos3, pos3, routing, x1, y_sorted, g, b)


def _alibi_slopes():
    n = N_DIL_GROUPS * HEADS_PER_GROUP
    return jnp.asarray(2.0 ** (-ALIBI_MAX * np.arange(1, n + 1, dtype=np.float32) / n), jnp.float32)


def kernel(x, mem, ln_mem_g, ln_mem_b, w_in, b_in, w_conv, w_conv_out, w_dil_out, w_mem_kv, w_mem_out, w_o, ln1_g, ln1_b, w_group, b_group, w_router, b_router, w_gate, w_up, w_down, ln2_g, ln2_b):
    assert x.shape == (BATCH, SEQ, D_MODEL) and w_in.shape == (1, D_MODEL, IN_DIM)
    bf16 = jnp.bfloat16
    row = lambda v: v.reshape(1, -1)
    w_in2 = w_in[0].astype(bf16)
    b_in2 = b_in

    kv = _memkv(mem, row(ln_mem_g), row(ln_mem_b), w_mem_kv[0].astype(bf16))
    s_conv, xb = _conv_branch(x, w_in2, b_in2, w_conv[0])
    qkv = _qkv_proj(xb, w_in2, b_in2)
    o_mem = _mem_branch(xb, w_in2, b_in2, kv)
    o_dil = _dil_branch(qkv, _alibi_slopes())

    x2 = x.reshape(TOKENS, D_MODEL)
    merged = _merge(xb.reshape(TOKENS, D_MODEL), w_in2, b_in2,
                    s_conv.reshape(TOKENS, CONV_DIM), o_dil.reshape(TOKENS, DIL_OUT_DIM),
                    o_mem.reshape(TOKENS, MEM_DIM),
                    jnp.concatenate([w_conv_out[0], w_dil_out[0], w_mem_out[0]], axis=0).astype(bf16))

    w_route = jnp.concatenate(
        [w_group[0], jnp.transpose(w_router[0], (1, 0, 2)).reshape(D_MODEL, N_EXPERTS),
         jnp.zeros((D_MODEL, LANES - N_GROUPS - N_EXPERTS), jnp.float32)], axis=1)
    b_route = jnp.concatenate(
        [b_group[0], b_router[0].reshape(N_EXPERTS),
         jnp.zeros((LANES - N_GROUPS - N_EXPERTS,), jnp.float32)]).reshape(1, LANES)
    x1, logits = _oproj(x2, merged, w_o[0].astype(bf16), ln1_g, ln1_b, w_route, b_route)

    routing, counts8 = _route(logits)
    pos = _slots(routing, counts8)[0:2]

    def pos_tiles(tile):
        return jnp.transpose(pos.reshape(2, TOKENS // tile, tile), (1, 0, 2)).reshape(TOKENS // tile, 1, 2 * tile)

    i32 = jnp.int32
    counts = counts8[0, ROUTER_LANE0:ROUTER_LANE0 + N_EXPERTS].astype(i32)
    padded = ((counts + EXPERT_TILE - 1) // EXPERT_TILE) * EXPERT_TILE
    ends = jnp.cumsum(padded)
    starts = ends - padded
    tile_start = jnp.arange(N_EXPERT_TILES, dtype=i32) * EXPERT_TILE
    tile_expert = jnp.minimum(jnp.sum((ends[None, :] <= tile_start[:, None]).astype(i32), axis=1),
                              N_EXPERTS - 1)
    tile_valid = tile_start < ends[-1]
    prev_expert = jnp.concatenate([jnp.full((1,), -1, i32), tile_expert[:-1]])
    tile_first = tile_valid & (tile_expert != prev_expert)
    tile_slot = (jnp.cumsum(tile_first.astype(i32)) - 1) & 1
    big = N_EXPERTS
    idx = jnp.where(counts > 0, jnp.arange(N_EXPERTS, dtype=i32), big)
    later = jnp.concatenate([lax.cummin(idx[::-1])[::-1][1:], jnp.full((1,), big, i32)])
    next_used = jnp.where(later == big, -1, later)
    n_valid_tiles = (ends[-1:] // EXPERT_TILE).astype(i32)
    tables = (tile_expert, tile_valid.astype(i32), tile_slot, tile_first.astype(i32),
              next_used[tile_expert], n_valid_tiles)

    xs = _dispatch(starts + counts, padded - counts, n_valid_tiles,
                   pos_tiles(DISPATCH_TILE), x1)
    y_sorted = _expert_ffn(tables, xs,
                           w_gate.reshape(N_EXPERTS, D_MODEL, D_EXPERT),
                           w_up.reshape(N_EXPERTS, D_MODEL, D_EXPERT),
                           w_down.reshape(N_EXPERTS, D_EXPERT, D_MODEL))
    out = _combine(pos_tiles(COMBINE_TILE), routing, x1, y_sorted,
                   ln2_g, ln2_b)
    return out.reshape(BATCH, SEQ, D_MODEL)
```

```python
import math

import numpy as np
import jax
import jax.numpy as jnp
from jax import lax
from jax.experimental import pallas as pl
from jax.experimental.pallas import tpu as pltpu

D_MODEL = 2048
BATCH = 8
SEQ = 2048
TOKENS = BATCH * SEQ
CONV_DIM = 1024
CONV_WIDTH = 3
DIL_PATTERNS = ((128, 1), (512, 4), (2048, 16))
N_DIL_GROUPS = 3
HEADS_PER_GROUP = 4
HEAD_DIM = 128
DIL_DIM = N_DIL_GROUPS * HEADS_PER_GROUP * HEAD_DIM
DIL_OUT_DIM = HEADS_PER_GROUP * HEAD_DIM
ATT_BLOCK = 128
ALIBI_MAX = 8.0
MEM_LEN = 256
MEM_HEADS = 4
MEM_HEAD_DIM = 256
MEM_DIM = MEM_HEADS * MEM_HEAD_DIM
N_BRANCHES = 3
IN_DIM = 3 * CONV_DIM + 3 * DIL_DIM + MEM_DIM + N_BRANCHES * D_MODEL
N_GROUPS = 4
EXPERTS_PER_GROUP = 8
N_EXPERTS = N_GROUPS * EXPERTS_PER_GROUP
D_EXPERT = 512
ALPHA = 2.0 ** 0.25
LN_EPS = 1e-5

OFF_CB = 0
OFF_CC = CONV_DIM
OFF_CH = 2 * CONV_DIM
OFF_Q = 3 * CONV_DIM
OFF_MQ = OFF_Q + 3 * DIL_DIM
OFF_GATE = OFF_MQ + MEM_DIM

LANES = 128
SUBLANES = 8
HALF = SEQ // 2
VMEM_LIMIT = 56 * 1024 * 1024

ROUTE_TILE = 1024
EXPERT_TILE = 512
N_EXPERT_TILES = 2 * TOKENS // EXPERT_TILE + N_EXPERTS
COMBINE_TILE = 256
ROUTER_LANE0 = 8


def _params(sem, limit=VMEM_LIMIT):
    return pltpu.CompilerParams(dimension_semantics=sem, vmem_limit_bytes=limit)


def _layer_norm(x, g, b):
    mu = jnp.mean(x, axis=-1, keepdims=True)
    xc = x - mu
    var = jnp.mean(xc * xc, axis=-1, keepdims=True)
    return xc * lax.rsqrt(var + LN_EPS) * g + b


def _dot(a, b):
    return jnp.dot(a, b, preferred_element_type=jnp.float32)


def _dot_t(a, b):
    return lax.dot_general(a, b, (((1,), (1,)), ((), ())), preferred_element_type=jnp.float32)


MEMKV_BATCHES = 2


def _memkv_kernel(mem_ref, g_ref, b_ref, w_ref, kv_ref):
    rows = MEMKV_BATCHES * MEM_LEN
    y = _layer_norm(mem_ref[...].reshape(rows, D_MODEL), g_ref[...], b_ref[...])
    kv = _dot(y, w_ref[...])
    kv_ref[...] = kv.astype(kv_ref.dtype).reshape(MEMKV_BATCHES, MEM_LEN, 2 * MEM_DIM)


def _memkv(mem, g, b, w):
    return pl.pallas_call(
        _memkv_kernel,
        grid=(BATCH // MEMKV_BATCHES,),
        in_specs=[pl.BlockSpec((MEMKV_BATCHES, MEM_LEN, D_MODEL), lambda i: (i, 0, 0)),
                  pl.BlockSpec((1, D_MODEL), lambda i: (0, 0)),
                  pl.BlockSpec((1, D_MODEL), lambda i: (0, 0)),
                  pl.BlockSpec((D_MODEL, 2 * MEM_DIM), lambda i: (0, 0))],
        out_specs=pl.BlockSpec((MEMKV_BATCHES, MEM_LEN, 2 * MEM_DIM), lambda i: (i, 0, 0)),
        out_shape=jax.ShapeDtypeStruct((BATCH, MEM_LEN, 2 * MEM_DIM), jnp.bfloat16),
        compiler_params=_params(("parallel",)),
        name="mem_kv",
    )(mem, g, b, w)


CONV_TC = 512


def _conv_kernel(x_ref, wb_ref, wc_ref, wh_ref, bb_ref, bc_ref, bh_ref, wconv_ref, s_ref, xb_ref,
                 u_sc, carry_sc):
    half = pl.program_id(1)
    c = pl.program_id(2)

    @pl.when(c == 0)
    def _():
        xb_ref[...] = x_ref[...].astype(xb_ref.dtype)

    x = xb_ref[...]
    cc = _dot(x, wc_ref[...]) + bc_ref[...]
    ch = _dot(x, wh_ref[...]) + bh_ref[...]
    u = cc * ch
    u_sc[0:8, :] = jnp.where(half == 0, 0.0, carry_sc[c])
    u_sc[8:8 + HALF, :] = u
    carry_sc[c] = u[HALF - 8:HALF, :]
    wconv = wconv_ref[...]
    y = (wconv[2:3, :] * u
         + wconv[1:2, :] * u_sc[7:7 + HALF, :]
         + wconv[0:1, :] * u_sc[6:6 + HALF, :])
    cb = _dot(x, wb_ref[...]) + bb_ref[...]
    s_ref[...] = (cb * y).astype(s_ref.dtype)


def _conv_branch(x, w_in, b_in, w_conv):
    nb = lambda off: off // CONV_TC
    wspec = lambda off: pl.BlockSpec((D_MODEL, CONV_TC), lambda b, h, c, o=nb(off): (0, o + c))
    bspec = lambda off: pl.BlockSpec((1, CONV_TC), lambda b, h, c, o=nb(off): (0, o + c))
    return pl.pallas_call(
        _conv_kernel,
        grid=(BATCH, 2, CONV_DIM // CONV_TC),
        in_specs=[pl.BlockSpec((None, HALF, D_MODEL), lambda b, h, c: (b, h, 0)),
                  wspec(OFF_CB), wspec(OFF_CC), wspec(OFF_CH),
                  bspec(OFF_CB), bspec(OFF_CC), bspec(OFF_CH),
                  pl.BlockSpec((CONV_WIDTH, CONV_TC), lambda b, h, c: (0, c))],
        out_specs=[pl.BlockSpec((None, HALF, CONV_TC), lambda b, h, c: (b, h, c)),
                   pl.BlockSpec((None, HALF, D_MODEL), lambda b, h, c: (b, h, 0))],
        out_shape=[jax.ShapeDtypeStruct((BATCH, SEQ, CONV_DIM), jnp.bfloat16),
                   jax.ShapeDtypeStruct((BATCH, SEQ, D_MODEL), jnp.bfloat16)],
        scratch_shapes=[pltpu.VMEM((HALF + 8, CONV_TC), jnp.float32),
                        pltpu.VMEM((CONV_DIM // CONV_TC, 8, CONV_TC), jnp.float32)],
        compiler_params=_params(("arbitrary", "arbitrary", "arbitrary")),
        name="conv_branch",
    )(x, w_in, w_in, w_in, b_in, b_in, b_in, w_conv)


QKV_TN = 512
QKV_CHUNKS = QKV_TN // LANES


def _qkv_kernel(x_ref, w_ref, b_ref, o_ref, sc_ref, sc2_ref):
    x = x_ref[...]
    for gi in (2, 1, 0):
        cols = slice(gi * QKV_TN, (gi + 1) * QKV_TN)
        acc = _dot(x, w_ref[:, cols]) + b_ref[:, cols]
        d = DIL_PATTERNS[gi][1]
        if d == 1:
            o_ref[:, cols] = acc.astype(o_ref.dtype)
            continue
        rows = HALF // d
        for c in range(QKV_CHUNKS):
            sc_ref[gi - 1, c] = acc[:, c * LANES:(c + 1) * LANES]
        if d == 16:
            q4 = HALF // 4
            for c in range(QKV_CHUNKS):
                for r in range(4):
                    sc2_ref[c, r * q4:(r + 1) * q4, :] = sc_ref[gi - 1, c, pl.ds(r, q4, stride=4), :]
            for c in range(QKV_CHUNKS):
                lo = gi * QKV_TN + c * LANES
                for r in range(d):
                    r_lo, r_hi = r % 4, r // 4
                    o_ref[r * rows:(r + 1) * rows, lo:lo + LANES] = (
                        sc2_ref[c, pl.ds(r_lo * q4 + r_hi, rows, stride=4), :].astype(o_ref.dtype))
            continue
        for c in range(QKV_CHUNKS):
            for r in range(d):
                lo = gi * QKV_TN + c * LANES
                o_ref[r * rows:(r + 1) * rows, lo:lo + LANES] = (
                    sc_ref[gi - 1, c, pl.ds(r, rows, stride=d), :].astype(o_ref.dtype))


def _qkv_proj(xb, w_in, b_in):
    n0 = OFF_Q // DIL_DIM
    return pl.pallas_call(
        _qkv_kernel,
        grid=(BATCH, 2, 3),
        in_specs=[pl.BlockSpec((None, HALF, D_MODEL), lambda b, h, n: (b, h, 0)),
                  pl.BlockSpec((D_MODEL, DIL_DIM), lambda b, h, n: (0, n0 + n)),
                  pl.BlockSpec((1, DIL_DIM), lambda b, h, n: (0, n0 + n))],
        out_specs=pl.BlockSpec((None, HALF, DIL_DIM), lambda b, h, n: (b, h, n)),
        out_shape=jax.ShapeDtypeStruct((BATCH, SEQ, 3 * DIL_DIM), jnp.bfloat16),
        scratch_shapes=[pltpu.VMEM((N_DIL_GROUPS - 1, QKV_CHUNKS, HALF, LANES), jnp.float32),
                        pltpu.VMEM((QKV_CHUNKS, HALF, LANES), jnp.float32)],
        compiler_params=_params(("parallel", "parallel", "arbitrary")),
        name="qkv_proj",
    )(xb, w_in, b_in)


MEM_HEADS_PER_STEP = 2
MEM_TN = MEM_HEADS_PER_STEP * MEM_HEAD_DIM


def _memattn_kernel(x_ref, w_ref, b_ref, mk_ref, mv_ref, o_ref):
    mq_all = (_dot(x_ref[...], w_ref[...]) + b_ref[...]).astype(jnp.bfloat16)
    for hh in range(MEM_HEADS_PER_STEP):
        cols = slice(hh * MEM_HEAD_DIM, (hh + 1) * MEM_HEAD_DIM)
        s = _dot_t(mq_all[:, cols], mk_ref[:, cols]) * (MEM_HEAD_DIM ** -0.5)
        m = jnp.max(s, axis=-1, keepdims=True)
        p = jnp.exp(s - m)
        den = jnp.sum(p, axis=-1, keepdims=True)
        o = _dot(p.astype(jnp.bfloat16), mv_ref[:, cols]) / den
        o_ref[:, cols] = o.astype(o_ref.dtype)


def _mem_branch(xb, w_in, b_in, kv):
    n0 = OFF_MQ // MEM_TN
    nv = MEM_DIM // MEM_TN
    return pl.pallas_call(
        _memattn_kernel,
        grid=(BATCH, 2, MEM_DIM // MEM_TN),
        in_specs=[pl.BlockSpec((None, HALF, D_MODEL), lambda b, h, n: (b, h, 0)),
                  pl.BlockSpec((D_MODEL, MEM_TN), lambda b, h, n: (0, n0 + n)),
                  pl.BlockSpec((1, MEM_TN), lambda b, h, n: (0, n0 + n)),
                  pl.BlockSpec((None, MEM_LEN, MEM_TN), lambda b, h, n: (b, 0, n)),
                  pl.BlockSpec((None, MEM_LEN, MEM_TN), lambda b, h, n: (b, 0, nv + n))],
        out_specs=pl.BlockSpec((None, HALF, MEM_TN), lambda b, h, n: (b, h, n)),
        out_shape=jax.ShapeDtypeStruct((BATCH, SEQ, MEM_DIM), jnp.bfloat16),
        compiler_params=_params(("parallel", "parallel", "arbitrary")),
        name="mem_branch",
    )(xb, w_in, b_in, kv, kv)


ATT_UNROLL = 16


def _softmax_block(s, v):
    m = jnp.max(s, axis=-1, keepdims=True)
    p = jnp.exp(s - m)
    den = jnp.sum(p, axis=-1, keepdims=True)
    o = _dot(p.astype(jnp.bfloat16), v) / den
    return o, m + jnp.log(den)


def _dilattn_kernel(slopes_ref,
                    q0_ref, q1_ref, q2_ref, k0_ref, k1_ref, k2_ref, v0_ref, v1_ref, v2_ref,
                    o_ref, o_sc, l_sc):
    h = pl.program_id(1)
    blk = ATT_BLOCK
    scale = HEAD_DIM ** -0.5
    qi = lax.broadcasted_iota(jnp.int32, (blk, 2 * blk), 0) + blk
    kj = lax.broadcasted_iota(jnp.int32, (blk, 2 * blk), 1)
    jrel = qi - kj
    valid = (jrel >= 0) & (jrel <= blk)
    jrel_f = jrel.astype(jnp.float32)

    def bias_for(g):
        slope = slopes_ref[g * HEADS_PER_GROUP + h]
        d = float(DIL_PATTERNS[g][1])
        return jnp.where(valid, (-slope * d) * jrel_f, -jnp.inf)

    def put(g, row_slice, o, lse):
        o_sc[g, row_slice, :] = o
        l_sc[g, row_slice, :] = jnp.broadcast_to(lse, (blk, HEAD_DIM))

    def run_blocks(g, blocks):
        scores = [_dot_t(q, k) * scale + bias for q, k, _, bias, _ in blocks]
        stats = []
        for s in scores:
            m = jnp.max(s, axis=-1, keepdims=True)
            p = jnp.exp(s - m)
            stats.append((m, p, jnp.sum(p, axis=-1, keepdims=True)))
        outs = [_dot(p.astype(jnp.bfloat16), blkdef[2]) / den
                for (m, p, den), blkdef in zip(stats, blocks)]
        for o, (m, p, den), blkdef in zip(outs, stats, blocks):
            put(g, blkdef[4], o, m + jnp.log(den))

    bias0 = bias_for(0)
    prev_cols = kj < blk

    def g0_body(it, carry):
        blocks = []
        for k in range(ATT_UNROLL):
            n = it * ATT_UNROLL + k
            q0 = pl.multiple_of(n * blk, blk)
            k0 = pl.multiple_of(jnp.maximum(n - 1, 0) * blk, blk)
            bias = jnp.where(prev_cols & (n == 0), -jnp.inf, bias0)
            blocks.append((q0_ref[pl.ds(q0, blk), :], k0_ref[pl.ds(k0, 2 * blk), :],
                           v0_ref[pl.ds(k0, 2 * blk), :], bias, pl.ds(q0, blk)))
        run_blocks(0, blocks)
        return carry

    lax.fori_loop(0, SEQ // blk // ATT_UNROLL, g0_body, 0)

    d1 = DIL_PATTERNS[1][1]
    cls1 = HALF // d1
    per_half = cls1 // blk
    bias1 = bias_for(1)

    def row1(r, n):
        return (n // per_half) * HALF + r * cls1 + (n % per_half) * blk

    nblk1 = SEQ // d1 // blk
    cls_per_trip = ATT_UNROLL // nblk1

    def g1_body(it, carry):
        blocks = []
        for c in range(cls_per_trip):
            r = it * cls_per_trip + c
            for n in range(nblk1):
                cur = pl.multiple_of(row1(r, n), blk)
                q = q1_ref[pl.ds(cur, blk), :]
                dst = pl.ds(n * blk * d1 + r, blk, stride=d1)
                if n == 0:
                    blocks.append((q, k1_ref[pl.ds(cur, blk), :], v1_ref[pl.ds(cur, blk), :],
                                   bias1[:, blk:], dst))
                else:
                    prev = pl.multiple_of(row1(r, n - 1), blk)
                    kc = jnp.concatenate([k1_ref[pl.ds(prev, blk), :], k1_ref[pl.ds(cur, blk), :]], axis=0)
                    vc = jnp.concatenate([v1_ref[pl.ds(prev, blk), :], v1_ref[pl.ds(cur, blk), :]], axis=0)
                    blocks.append((q, kc, vc, bias1, dst))
        run_blocks(1, blocks)
        return carry

    lax.fori_loop(0, d1 // cls_per_trip, g1_body, 0)

    d2 = DIL_PATTERNS[2][1]
    cls2 = HALF // d2
    bias2 = bias_for(2)

    def g2_body(it, carry):
        blocks = []
        for k in range(ATT_UNROLL):
            r = it * ATT_UNROLL + k
            a = pl.multiple_of(r * cls2, cls2)
            b = pl.multiple_of(HALF + r * cls2, cls2)
            cat = lambda ref, a=a, b=b: jnp.concatenate(
                [ref[pl.ds(a, cls2), :], ref[pl.ds(b, cls2), :]], axis=0)
            blocks.append((cat(q2_ref), cat(k2_ref), cat(v2_ref), bias2[:, blk:],
                           pl.ds(r, blk, stride=d2)))
        run_blocks(2, blocks)
        return carry

    lax.fori_loop(0, d2 // ATT_UNROLL, g2_body, 0)

    rows = 256
    for t in range(SEQ // rows):
        sl = pl.ds(t * rows, rows)
        l0, l1, l2 = l_sc[0, sl, :], l_sc[1, sl, :], l_sc[2, sl, :]
        m = jnp.maximum(jnp.maximum(l0, l1), l2)
        e0, e1, e2 = jnp.exp(l0 - m), jnp.exp(l1 - m), jnp.exp(l2 - m)
        mix = (e0 * o_sc[0, sl, :] + e1 * o_sc[1, sl, :] + e2 * o_sc[2, sl, :]) / (e0 + e1 + e2)
        o_ref[sl, :] = mix.astype(o_ref.dtype)


def _dil_branch(qkv, slopes):
    nq = DIL_DIM // HEAD_DIM

    def spec(section, g):
        return pl.BlockSpec((None, SEQ, HEAD_DIM),
                            lambda b, h, s_ref, o=section * nq + g * HEADS_PER_GROUP: (b, 0, o + h))

    grid_spec = pltpu.PrefetchScalarGridSpec(
        num_scalar_prefetch=1,
        grid=(BATCH, HEADS_PER_GROUP),
        in_specs=[spec(sec, g) for sec in range(3) for g in range(N_DIL_GROUPS)],
        out_specs=pl.BlockSpec((None, SEQ, HEAD_DIM), lambda b, h, s_ref: (b, 0, h)),
        scratch_shapes=[pltpu.VMEM((N_DIL_GROUPS, SEQ, HEAD_DIM), jnp.float32),
                        pltpu.VMEM((N_DIL_GROUPS, SEQ, HEAD_DIM), jnp.float32)],
    )
    return pl.pallas_call(
        _dilattn_kernel,
        grid_spec=grid_spec,
        out_shape=jax.ShapeDtypeStruct((BATCH, SEQ, DIL_OUT_DIM), jnp.bfloat16),
        compiler_params=_params(("parallel", "arbitrary")),
        name="dil_attn",
    )(slopes, *([qkv] * 9))


MERGE_TM = 1024
MERGE_TN = 512


def _merge_kernel(x_ref, wg0_ref, wg1_ref, wg2_ref, bg0_ref, bg1_ref, bg2_ref,
                  sc_ref, od_ref, om_ref, wout_ref, o_ref):
    r1, r2 = CONV_DIM, CONV_DIM + DIL_OUT_DIM
    x = x_ref[...]
    g0 = jax.nn.sigmoid(_dot(x, wg0_ref[...]) + bg0_ref[...])
    acc = g0 * _dot(sc_ref[...], wout_ref[0:r1, :])
    g1 = jax.nn.sigmoid(_dot(x, wg1_ref[...]) + bg1_ref[...])
    acc = acc + g1 * _dot(od_ref[...], wout_ref[r1:r2, :])
    g2 = jax.nn.sigmoid(_dot(x, wg2_ref[...]) + bg2_ref[...])
    acc = acc + g2 * _dot(om_ref[...], wout_ref[r2:r2 + MEM_DIM, :])
    o_ref[...] = acc.astype(o_ref.dtype)


def _merge(x2, w_in, b_in, s_conv, o_dil, o_mem, w_out):
    nb = lambda br: (OFF_GATE + br * D_MODEL) // MERGE_TN
    gspec = lambda br: pl.BlockSpec((D_MODEL, MERGE_TN), lambda i, n, o=nb(br): (0, o + n))
    bspec = lambda br: pl.BlockSpec((1, MERGE_TN), lambda i, n, o=nb(br): (0, o + n))
    act = lambda width: pl.BlockSpec((MERGE_TM, width), lambda i, n: (i, 0))
    wout = lambda width: pl.BlockSpec((width, MERGE_TN), lambda i, n: (0, n))
    return pl.pallas_call(
        _merge_kernel,
        grid=(TOKENS // MERGE_TM, D_MODEL // MERGE_TN),
        in_specs=[act(D_MODEL), gspec(0), gspec(1), gspec(2), bspec(0), bspec(1), bspec(2),
                  act(CONV_DIM), act(DIL_OUT_DIM), act(MEM_DIM),
                  wout(CONV_DIM + DIL_OUT_DIM + MEM_DIM)],
        out_specs=pl.BlockSpec((MERGE_TM, MERGE_TN), lambda i, n: (i, n)),
        out_shape=jax.ShapeDtypeStruct((TOKENS, D_MODEL), jnp.bfloat16),
        compiler_params=_params(("parallel", "arbitrary")),
        name="gated_merge",
    )(x2, w_in, w_in, w_in, b_in, b_in, b_in, s_conv, o_dil, o_mem, w_out)


OPROJ_TM = 512


OPROJ_PARTS = 4


def _oproj_kernel(x_ref, m_ref, wo_ref, g_ref, b_ref, wr_ref, br_ref, x1_ref, logit_ref):
    rows = OPROJ_TM // OPROJ_PARTS
    parts = [pl.ds(i * rows, rows) for i in range(OPROJ_PARTS)]
    proj = [_dot(m_ref[p, :], wo_ref[...]) for p in parts]
    for p, y in zip(parts, proj):
        x1 = _layer_norm(ALPHA * x_ref[p, :] + y, g_ref[...], b_ref[...])
        x1_ref[p, :] = x1
        logit_ref[p, :] = _dot(x1, wr_ref[...]) + br_ref[...]


def _oproj(x2, merged, w_o, g, b, w_route, b_route):
    row = lambda width: pl.BlockSpec((OPROJ_TM, width), lambda i: (i, 0))
    full = lambda r, c: pl.BlockSpec((r, c), lambda i: (0, 0))
    return pl.pallas_call(
        _oproj_kernel,
        grid=(TOKENS // OPROJ_TM,),
        in_specs=[row(D_MODEL), row(D_MODEL), full(D_MODEL, D_MODEL), full(1, D_MODEL), full(1, D_MODEL),
                  full(D_MODEL, LANES), full(1, LANES)],
        out_specs=[row(D_MODEL), row(LANES)],
        out_shape=[jax.ShapeDtypeStruct((TOKENS, D_MODEL), jnp.float32),
                   jax.ShapeDtypeStruct((TOKENS, LANES), jnp.float32)],
        compiler_params=_params(("parallel",)),
        name="oproj_ln1",
    )(x2, merged, w_o, g, b, w_route, b_route)


def _route_kernel(logit_ref, rec_ref, rect_ref, cnt_ref, carry_sc):
    @pl.when(pl.program_id(0) == 0)
    def _():
        carry_sc[...] = jnp.zeros_like(carry_sc)

    tm = ROUTE_TILE
    zt = jnp.transpose(logit_ref[...])
    row = lax.broadcasted_iota(jnp.int32, (SUBLANES, tm), 0)
    neg = -jnp.inf
    colmax = lambda a: jnp.max(a, axis=0, keepdims=True)
    first = lambda hit: jnp.min(jnp.where(hit, row, SUBLANES), axis=0, keepdims=True)

    glog = jnp.where(row < N_GROUPS, zt[0:SUBLANES], neg)
    gmax = colmax(glog)
    gsel = first(glog == gmax)
    gw = 1.0 / jnp.sum(jnp.exp(glog - gmax), axis=0, keepdims=True)

    elog = zt[ROUTER_LANE0:ROUTER_LANE0 + EXPERTS_PER_GROUP]
    for g in range(1, N_GROUPS):
        lo = ROUTER_LANE0 + g * EXPERTS_PER_GROUP
        elog = jnp.where(gsel == g, zt[lo:lo + EXPERTS_PER_GROUP], elog)
    top1 = colmax(elog)
    i1 = first(elog == top1)
    rest = jnp.where(row == i1, neg, elog)
    top2 = colmax(rest)
    i2 = first(rest == top2)
    t = jnp.exp(top2 - top1)
    w1 = gw / (1.0 + t)
    w2 = w1 * t

    hit = (row == i1) | (row == i2)
    onehot = jnp.concatenate([jnp.where(hit & (gsel == g), 1.0, 0.0) for g in range(N_GROUPS)], axis=0)
    chunks = tm // LANES
    stacked = jnp.concatenate([onehot[:, c * LANES:(c + 1) * LANES] for c in range(chunks)], axis=0)
    ri = lax.broadcasted_iota(jnp.int32, (LANES, LANES), 0)
    ci = lax.broadcasted_iota(jnp.int32, (LANES, LANES), 1)
    upper = jnp.where(ri <= ci, 1.0, 0.0).astype(jnp.bfloat16)
    within = _dot(stacked.astype(jnp.bfloat16), upper)
    run = carry_sc[:, 0:1]
    pieces = []
    for c in range(chunks):
        cum = within[c * N_EXPERTS:(c + 1) * N_EXPERTS, :] + run
        pieces.append(cum)
        run = cum[:, LANES - 1:LANES]
    before = jnp.concatenate(pieces, axis=1) - 1.0
    carry_sc[...] = jnp.broadcast_to(run, carry_sc.shape)
    cnt_ref[...] = jnp.broadcast_to(run, cnt_ref.shape)

    rank_a = jnp.zeros((1, tm), jnp.float32)
    rank_b = jnp.zeros((1, tm), jnp.float32)
    for g in range(N_GROUPS):
        b8 = before[g * EXPERTS_PER_GROUP:(g + 1) * EXPERTS_PER_GROUP, :]
        mine = gsel == g
        rank_a = rank_a + jnp.sum(jnp.where(mine & (row == i1), b8, 0.0), axis=0, keepdims=True)
        rank_b = rank_b + jnp.sum(jnp.where(mine & (row == i2), b8, 0.0), axis=0, keepdims=True)

    base = (gsel * EXPERTS_PER_GROUP).astype(jnp.float32)
    vals = [base + i1.astype(jnp.float32), base + i2.astype(jnp.float32), rank_a, rank_b, w1, w2]
    rect = jnp.zeros((SUBLANES, tm), jnp.float32)
    for k, val in enumerate(vals):
        rect = jnp.where(row == k, val, rect)
    rect_ref[...] = rect
    padded = jnp.concatenate([rect, jnp.zeros((LANES - SUBLANES, tm), jnp.float32)], axis=0)
    rec_ref[...] = jnp.transpose(padded)


def _route(logits):
    return pl.pallas_call(
        _route_kernel,
        grid=(TOKENS // ROUTE_TILE,),
        in_specs=[pl.BlockSpec((ROUTE_TILE, LANES), lambda i: (i, 0))],
        out_specs=[pl.BlockSpec((ROUTE_TILE, LANES), lambda i: (i, 0)),
                   pl.BlockSpec((SUBLANES, ROUTE_TILE), lambda i: (0, i)),
                   pl.BlockSpec((N_EXPERTS, LANES), lambda i: (0, 0))],
        out_shape=[jax.ShapeDtypeStruct((TOKENS, LANES), jnp.float32),
                   jax.ShapeDtypeStruct((SUBLANES, TOKENS), jnp.float32),
                   jax.ShapeDtypeStruct((N_EXPERTS, LANES), jnp.float32)],
        scratch_shapes=[pltpu.VMEM((N_EXPERTS, LANES), jnp.float32)],
        compiler_params=_params(("arbitrary",)),
        name="route",
    )(logits)


def _slots_kernel(start_ref, rect_ref, pos_ref):
    r = rect_ref[...]
    row = lax.broadcasted_iota(jnp.int32, r.shape, 0)

    def pos_of(e, rank):
        e = e.astype(jnp.int32)
        start = jnp.zeros(e.shape, jnp.int32)
        for k in range(N_EXPERTS):
            start = jnp.where(e == k, start_ref[k], start)
        return start + rank.astype(jnp.int32)

    pos_a = pos_of(r[0:1], r[2:3])
    pos_b = pos_of(r[1:2], r[3:4])
    pos_ref[...] = jnp.where(row == 0, pos_a, jnp.where(row == 1, pos_b, 0))


def _slots(starts, rect):
    grid_spec = pltpu.PrefetchScalarGridSpec(
        num_scalar_prefetch=1,
        grid=(TOKENS // ROUTE_TILE,),
        in_specs=[pl.BlockSpec((SUBLANES, ROUTE_TILE), lambda i, st: (0, i))],
        out_specs=pl.BlockSpec((SUBLANES, ROUTE_TILE), lambda i, st: (0, i)),
    )
    return pl.pallas_call(
        _slots_kernel,
        grid_spec=grid_spec,
        out_shape=jax.ShapeDtypeStruct((SUBLANES, TOKENS), jnp.int32),
        compiler_params=_params(("parallel",)),
        name="slots",
    )(starts, rect)


DISPATCH_TILE = 512
SORTED_ROWS = N_EXPERT_TILES * EXPERT_TILE


PACKED = D_MODEL // 2


def _pack_rows(v):
    words = pltpu.pack_elementwise([v[:, :PACKED], v[:, PACKED:]], packed_dtype=jnp.bfloat16)
    return lax.bitcast_convert_type(words, jnp.uint32)


def _unpack_rows(w):
    half = lambda i: pltpu.unpack_elementwise(w, index=i, packed_dtype=jnp.bfloat16,
                                              unpacked_dtype=jnp.float32)
    return half(0), half(1)


def _dispatch_kernel(pad_start_ref, pad_n_ref, nvalid_ref, pos_ref, x_ref, xs_hbm, zero_sc, pk_sc, sem,
                     row_sem):
    groups = DISPATCH_TILE // SUBLANES

    @pl.when(pl.program_id(0) == 0)
    def _():
        zero_sc[...] = jnp.zeros_like(zero_sc)

        def tile_copy(t):
            return pltpu.make_async_copy(zero_sc, xs_hbm.at[pl.ds(pl.multiple_of(t * EXPERT_TILE, EXPERT_TILE),
                                                                  EXPERT_TILE)], sem.at[2])

        def start_tile(t, c):
            tile_copy(t).start()
            return c

        def wait_tile(t, c):
            tile_copy(t).wait()
            return c
        lax.fori_loop(nvalid_ref[0], N_EXPERT_TILES, start_tile, 0)
        lax.fori_loop(nvalid_ref[0], N_EXPERT_TILES, wait_tile, 0)

        def pad_copies(e, act):
            n = pad_n_ref[e]
            cur = pad_start_ref[e]
            for bit in range((EXPERT_TILE - 1).bit_length()):
                size = 1 << bit
                has = lax.bitwise_and(n, size)

                @pl.when(has != 0)
                def _(cur=cur, size=size):
                    at = pl.multiple_of(cur, math.gcd(size, EXPERT_TILE))
                    act(pltpu.make_async_copy(zero_sc.at[pl.ds(0, size)], xs_hbm.at[pl.ds(at, size)],
                                              sem.at[1]))
                cur = cur + has

        def start_pads(e, c):
            pad_copies(e, lambda cp: cp.start())
            return c

        def wait_pads(e, c):
            pad_copies(e, lambda cp: cp.wait())
            return c
        lax.fori_loop(0, N_EXPERTS, start_pads, 0)
        lax.fori_loop(0, N_EXPERTS, wait_pads, 0)

    j = pl.program_id(0)
    slot = lax.rem(j, 2)
    pk_sc[slot] = _pack_rows(x_ref[...]).reshape(groups, SUBLANES, PACKED)

    def body(gi, c):
        for s in range(SUBLANES):
            for k in range(2):
                p = pos_ref[0, 0, k * DISPATCH_TILE + gi * SUBLANES + s]
                pltpu.make_async_copy(pk_sc.at[slot, gi, pl.ds(s, 1)], xs_hbm.at[pl.ds(p, 1)],
                                      row_sem.at[slot]).start(priority=k)
        return c
    lax.fori_loop(0, groups, body, 0)

    def wait_slot(sl):
        for _ in range(2 * groups):
            pltpu.make_async_copy(pk_sc.at[sl, 0], xs_hbm.at[pl.ds(0, SUBLANES)], row_sem.at[sl]).wait()

    @pl.when(j > 0)
    def _():
        wait_slot(1 - slot)

    @pl.when(j == pl.num_programs(0) - 1)
    def _():
        wait_slot(slot)


def _dispatch(pad_start, pad_n, n_valid_tiles, pos3, x1):
    nt = TOKENS // DISPATCH_TILE
    groups = DISPATCH_TILE // SUBLANES
    grid_spec = pltpu.PrefetchScalarGridSpec(
        num_scalar_prefetch=3,
        grid=(nt,),
        in_specs=[pl.BlockSpec((1, 1, 2 * DISPATCH_TILE), lambda j, *_: (j, 0, 0), memory_space=pltpu.SMEM),
                  pl.BlockSpec((DISPATCH_TILE, D_MODEL), lambda j, *_: (j, 0))],
        out_specs=pl.BlockSpec(memory_space=pl.ANY),
        scratch_shapes=[pltpu.VMEM((EXPERT_TILE, PACKED), jnp.uint32),
                        pltpu.VMEM((2, groups, SUBLANES, PACKED), jnp.uint32),
                        pltpu.SemaphoreType.DMA((3,)),
                        pltpu.SemaphoreType.DMA((2,))],
    )
    return pl.pallas_call(
        _dispatch_kernel,
        grid_spec=grid_spec,
        out_shape=jax.ShapeDtypeStruct((SORTED_ROWS, PACKED), jnp.uint32),
        compiler_params=_params(("arbitrary",)),
        name="dispatch",
    )(pad_start, pad_n, n_valid_tiles, pos3, x1)


def _ffn_kernel(texp_ref, tvalid_ref, tslot_ref, tfirst_ref, tnext_ref, nvalid_ref,
                xs_ref, wg_hbm, wu_hbm, wd_hbm, y_ref, wg_buf, wu_buf, wd_buf, sem):
    j = pl.program_id(0)
    valid = tvalid_ref[j] == 1
    slot = tslot_ref[j]

    def weight_copies(e, s):
        return (pltpu.make_async_copy(wg_hbm.at[e], wg_buf.at[s], sem.at[s, 0]),
                pltpu.make_async_copy(wu_hbm.at[e], wu_buf.at[s], sem.at[s, 1]),
                pltpu.make_async_copy(wd_hbm.at[e], wd_buf.at[s], sem.at[s, 2]))

    @pl.when(valid & (tfirst_ref[j] == 1))
    def _():
        @pl.when(j == 0)
        def _():
            for cp in weight_copies(texp_ref[j], slot):
                cp.start()

        for cp in weight_copies(texp_ref[j], slot):
            cp.wait()

        @pl.when(tnext_ref[j] >= 0)
        def _():
            for cp in weight_copies(tnext_ref[j], 1 - slot):
                cp.start()

    @pl.when(valid)
    def _():
        lo, hi = _unpack_rows(xs_ref[...])
        a = _dot(lo, wg_buf[slot, 0:PACKED, :]) + _dot(hi, wg_buf[slot, PACKED:D_MODEL, :])
        u = _dot(lo, wu_buf[slot, 0:PACKED, :]) + _dot(hi, wu_buf[slot, PACKED:D_MODEL, :])
        hid = (a * jax.nn.sigmoid(a)) * u
        y_ref[...] = _pack_rows(_dot(hid, wd_buf[slot]))

    @pl.when(jnp.logical_not(valid))
    def _():
        y_ref[...] = jnp.zeros_like(y_ref)


def _expert_ffn(tables, xs, w_gate, w_up, w_down):
    nt = N_EXPERT_TILES
    xmap = lambda j, te, tv, ts, tf, tn, nv: (jnp.minimum(j, nv[0] - 1), 0)
    grid_spec = pltpu.PrefetchScalarGridSpec(
        num_scalar_prefetch=6,
        grid=(nt,),
        in_specs=[pl.BlockSpec((EXPERT_TILE, PACKED), xmap),
                  pl.BlockSpec(memory_space=pl.ANY),
                  pl.BlockSpec(memory_space=pl.ANY),
                  pl.BlockSpec(memory_space=pl.ANY)],
        out_specs=pl.BlockSpec((EXPERT_TILE, PACKED), lambda j, *_: (j, 0)),
        scratch_shapes=[pltpu.VMEM((2, D_MODEL, D_EXPERT), jnp.float32),
                        pltpu.VMEM((2, D_MODEL, D_EXPERT), jnp.float32),
                        pltpu.VMEM((2, D_EXPERT, D_MODEL), jnp.float32),
                        pltpu.SemaphoreType.DMA((2, 3))],
    )
    return pl.pallas_call(
        _ffn_kernel,
        grid_spec=grid_spec,
        out_shape=jax.ShapeDtypeStruct((nt * EXPERT_TILE, PACKED), jnp.uint32),
        compiler_params=_params(("arbitrary",)),
        name="expert_ffn",
    )(*tables, xs, w_gate, w_up, w_down)


def _combine_kernel(pos_cur_ref, pos_next_ref, r_ref, x1_ref, y_hbm, g_ref, b_ref, o_ref, ybuf, sem):
    j = pl.program_id(0)
    nt = pl.num_programs(0)
    slot = lax.rem(j, 2)
    groups = COMBINE_TILE // SUBLANES

    def gather(pos_ref, dst_slot):
        def body(gi, c):
            for s in range(SUBLANES):
                for k in range(2):
                    p = pos_ref[0, 0, k * COMBINE_TILE + gi * SUBLANES + s]
                    pltpu.make_async_copy(y_hbm.at[pl.ds(p, 1)], ybuf.at[dst_slot, k, gi, pl.ds(s, 1)],
                                          sem.at[dst_slot]).start(priority=k)
            return c
        lax.fori_loop(0, groups, body, 0)

    @pl.when(j == 0)
    def _():
        gather(pos_cur_ref, 0)

    @pl.when(j + 1 < nt)
    def _():
        gather(pos_next_ref, 1 - slot)

    for _ in range(2 * groups):
        pltpu.make_async_copy(y_hbm.at[pl.ds(0, SUBLANES)], ybuf.at[slot, 0, 0], sem.at[slot]).wait()
    r = r_ref[...]
    unpack = lambda k: jnp.concatenate(_unpack_rows(ybuf[slot, k].reshape(COMBINE_TILE, PACKED)), axis=1)
    ya, yb = unpack(0), unpack(1)
    hres = ALPHA * x1_ref[...] + r[:, 4:5] * ya + r[:, 5:6] * yb
    o_ref[...] = _layer_norm(hres, g_ref[...], b_ref[...])


def _combine(pos3, routing, x1, y_sorted, g, b):
    nt = TOKENS // COMBINE_TILE
    groups = COMBINE_TILE // SUBLANES
    return pl.pallas_call(
        _combine_kernel,
        grid=(nt,),
        in_specs=[pl.BlockSpec((1, 1, 2 * COMBINE_TILE), lambda j: (j, 0, 0), memory_space=pltpu.SMEM),
                  pl.BlockSpec((1, 1, 2 * COMBINE_TILE), lambda j: (jnp.minimum(j + 1, nt - 1), 0, 0),
                               memory_space=pltpu.SMEM),
                  pl.BlockSpec((COMBINE_TILE, LANES), lambda j: (j, 0)),
                  pl.BlockSpec((COMBINE_TILE, D_MODEL), lambda j: (j, 0)),
                  pl.BlockSpec(memory_space=pl.ANY),
                  pl.BlockSpec((1, D_MODEL), lambda j: (0, 0)),
                  pl.BlockSpec((1, D_MODEL), lambda j: (0, 0))],
        out_specs=pl.BlockSpec((COMBINE_TILE, D_MODEL), lambda j: (j, 0)),
        out_shape=jax.ShapeDtypeStruct((TOKENS, D_MODEL), jnp.float32),
        scratch_shapes=[pltpu.VMEM((2, 2, groups, SUBLANES, PACKED), jnp.uint32),
                        pltpu.SemaphoreType.DMA((2,))],
        compiler_params=_params(("arbitrary",)),
        name="combine_ln2",
    )(pos3, pos3, routing, x1, y_sorted, g, b)


def _alibi_slopes():
    n = N_DIL_GROUPS * HEADS_PER_GROUP
    return jnp.asarray(2.0 ** (-ALIBI_MAX * np.arange(1, n + 1, dtype=np.float32) / n), jnp.float32)


def kernel(x, mem, ln_mem_g, ln_mem_b, w_in, b_in, w_conv, w_conv_out, w_dil_out, w_mem_kv, w_mem_out, w_o, ln1_g, ln1_b, w_group, b_group, w_router, b_router, w_gate, w_up, w_down, ln2_g, ln2_b):
    assert x.shape == (BATCH, SEQ, D_MODEL) and w_in.shape == (1, D_MODEL, IN_DIM)
    bf16 = jnp.bfloat16
    row = lambda v: v.reshape(1, -1)
    w_in2 = w_in[0].astype(bf16)
    b_in2 = b_in

    kv = _memkv(mem, row(ln_mem_g), row(ln_mem_b), w_mem_kv[0])
    s_conv, xb = _conv_branch(x, w_in2, b_in2, w_conv[0])
    qkv = _qkv_proj(xb, w_in2, b_in2)
    o_mem = _mem_branch(xb, w_in2, b_in2, kv)
    o_dil = _dil_branch(qkv, _alibi_slopes())

    x2 = x.reshape(TOKENS, D_MODEL)
    merged = _merge(xb.reshape(TOKENS, D_MODEL), w_in2, b_in2,
                    s_conv.reshape(TOKENS, CONV_DIM), o_dil.reshape(TOKENS, DIL_OUT_DIM),
                    o_mem.reshape(TOKENS, MEM_DIM),
                    jnp.concatenate([w_conv_out[0], w_dil_out[0], w_mem_out[0]], axis=0).astype(bf16))

    gap = ROUTER_LANE0 - N_GROUPS
    tail = LANES - ROUTER_LANE0 - N_EXPERTS
    w_route = jnp.concatenate(
        [w_group[0], jnp.zeros((D_MODEL, gap), jnp.float32),
         jnp.transpose(w_router[0], (1, 0, 2)).reshape(D_MODEL, N_EXPERTS),
         jnp.zeros((D_MODEL, tail), jnp.float32)], axis=1)
    b_route = jnp.concatenate(
        [b_group[0], jnp.zeros((gap,), jnp.float32), b_router[0].reshape(N_EXPERTS),
         jnp.zeros((tail,), jnp.float32)]).reshape(1, LANES)
    x1, logits = _oproj(x2, merged, w_o[0].astype(bf16), ln1_g, ln1_b, w_route, b_route)

    routing, routing_t, counts_b = _route(logits)

    i32 = jnp.int32
    counts = counts_b[:, 0].astype(i32)
    padded = ((counts + EXPERT_TILE - 1) // EXPERT_TILE) * EXPERT_TILE
    ends = jnp.cumsum(padded)
    starts = ends - padded
    pos = _slots(starts.astype(i32), routing_t)[0:2]

    def pos_tiles(tile):
        return jnp.transpose(pos.reshape(2, TOKENS // tile, tile), (1, 0, 2)).reshape(TOKENS // tile, 1, 2 * tile)

    tile_start = jnp.arange(N_EXPERT_TILES, dtype=i32) * EXPERT_TILE
    tile_expert = jnp.minimum(jnp.sum((ends[None, :] <= tile_start[:, None]).astype(i32), axis=1),
                              N_EXPERTS - 1)
    tile_valid = tile_start < ends[-1]
    prev_expert = jnp.concatenate([jnp.full((1,), -1, i32), tile_expert[:-1]])
    tile_first = tile_valid & (tile_expert != prev_expert)
    tile_slot = (jnp.cumsum(tile_first.astype(i32)) - 1) & 1
    big = N_EXPERTS
    idx = jnp.where(counts > 0, jnp.arange(N_EXPERTS, dtype=i32), big)
    later = jnp.concatenate([lax.cummin(idx[::-1])[::-1][1:], jnp.full((1,), big, i32)])
    next_used = jnp.where(later == big, -1, later)
    n_valid_tiles = (ends[-1:] // EXPERT_TILE).astype(i32)
    tables = (tile_expert, tile_valid.astype(i32), tile_slot, tile_first.astype(i32),
              next_used[tile_expert], n_valid_tiles)

    xs = _dispatch(starts + counts, padded - counts, n_valid_tiles,
                   pos_tiles(DISPATCH_TILE), x1)
    y_sorted = _expert_ffn(tables, xs,
                           w_gate.reshape(N_EXPERTS, D_MODEL, D_EXPERT),
                           w_up.reshape(N_EXPERTS, D_MODEL, D_EXPERT),
                           w_down.reshape(N_EXPERTS, D_EXPERT, D_MODEL))
    out = _combine(pos_tiles(COMBINE_TILE), routing, x1, y_sorted,
                   ln2_g, ln2_b)
    return out.reshape(BATCH, SEQ, D_MODEL)
```

```python
import math

import numpy as np
import jax
import jax.numpy as jnp
from jax import lax
from jax.experimental import pallas as pl
from jax.experimental.pallas import tpu as pltpu

D_MODEL = 2048
BATCH = 8
SEQ = 2048
TOKENS = BATCH * SEQ
CONV_DIM = 1024
CONV_WIDTH = 3
DIL_PATTERNS = ((128, 1), (512, 4), (2048, 16))
N_DIL_GROUPS = 3
HEADS_PER_GROUP = 4
HEAD_DIM = 128
DIL_DIM = N_DIL_GROUPS * HEADS_PER_GROUP * HEAD_DIM
DIL_OUT_DIM = HEADS_PER_GROUP * HEAD_DIM
ATT_BLOCK = 128
ALIBI_MAX = 8.0
MEM_LEN = 256
MEM_HEADS = 4
MEM_HEAD_DIM = 256
MEM_DIM = MEM_HEADS * MEM_HEAD_DIM
N_BRANCHES = 3
IN_DIM = 3 * CONV_DIM + 3 * DIL_DIM + MEM_DIM + N_BRANCHES * D_MODEL
N_GROUPS = 4
EXPERTS_PER_GROUP = 8
N_EXPERTS = N_GROUPS * EXPERTS_PER_GROUP
D_EXPERT = 512
ALPHA = 2.0 ** 0.25
LN_EPS = 1e-5

OFF_CB = 0
OFF_CC = CONV_DIM
OFF_CH = 2 * CONV_DIM
OFF_Q = 3 * CONV_DIM
OFF_MQ = OFF_Q + 3 * DIL_DIM
OFF_GATE = OFF_MQ + MEM_DIM

LANES = 128
SUBLANES = 8
HALF = SEQ // 2
VMEM_LIMIT = 56 * 1024 * 1024

ROUTE_TILE = 1024
EXPERT_TILE = 512
N_EXPERT_TILES = 2 * TOKENS // EXPERT_TILE + N_EXPERTS
COMBINE_TILE = 256
ROUTER_LANE0 = 8


def _params(sem, limit=VMEM_LIMIT):
    return pltpu.CompilerParams(dimension_semantics=sem, vmem_limit_bytes=limit)


def _layer_norm(x, g, b):
    mu = jnp.mean(x, axis=-1, keepdims=True)
    xc = x - mu
    var = jnp.mean(xc * xc, axis=-1, keepdims=True)
    return xc * lax.rsqrt(var + LN_EPS) * g + b


def _dot(a, b):
    return jnp.dot(a, b, preferred_element_type=jnp.float32)


def _dot_t(a, b):
    return lax.dot_general(a, b, (((1,), (1,)), ((), ())), preferred_element_type=jnp.float32)


MEMKV_BATCHES = 2


def _memkv_kernel(mem_ref, g_ref, b_ref, w_ref, kv_ref):
    rows = MEMKV_BATCHES * MEM_LEN
    y = _layer_norm(mem_ref[...].reshape(rows, D_MODEL), g_ref[...], b_ref[...])
    kv = _dot(y, w_ref[...])
    kv_ref[...] = kv.astype(kv_ref.dtype).reshape(MEMKV_BATCHES, MEM_LEN, 2 * MEM_DIM)


def _memkv(mem, g, b, w):
    return pl.pallas_call(
        _memkv_kernel,
        grid=(BATCH // MEMKV_BATCHES,),
        in_specs=[pl.BlockSpec((MEMKV_BATCHES, MEM_LEN, D_MODEL), lambda i: (i, 0, 0)),
                  pl.BlockSpec((1, D_MODEL), lambda i: (0, 0)),
                  pl.BlockSpec((1, D_MODEL), lambda i: (0, 0)),
                  pl.BlockSpec((D_MODEL, 2 * MEM_DIM), lambda i: (0, 0))],
        out_specs=pl.BlockSpec((MEMKV_BATCHES, MEM_LEN, 2 * MEM_DIM), lambda i: (i, 0, 0)),
        out_shape=jax.ShapeDtypeStruct((BATCH, MEM_LEN, 2 * MEM_DIM), jnp.bfloat16),
        compiler_params=_params(("parallel",)),
        name="mem_kv",
    )(mem, g, b, w)


CONV_TC = 512


def _conv_kernel(x_ref, wb_ref, wc_ref, wh_ref, bb_ref, bc_ref, bh_ref, wconv_ref, s_ref, xb_ref,
                 u_sc, carry_sc):
    half = pl.program_id(1)
    c = pl.program_id(2)

    @pl.when(c == 0)
    def _():
        xb_ref[...] = x_ref[...].astype(xb_ref.dtype)

    x = xb_ref[...]
    cc = _dot(x, wc_ref[...]) + bc_ref[...]
    ch = _dot(x, wh_ref[...]) + bh_ref[...]
    u = cc * ch
    pre = SUBLANES
    u_sc[0:pre, :] = jnp.where(half == 0, 0.0, carry_sc[c])
    u_sc[pre:pre + HALF, :] = u
    carry_sc[c] = u[HALF - pre:HALF, :]
    wconv = wconv_ref[...]
    y = wconv[CONV_WIDTH - 1:CONV_WIDTH, :] * u
    for back in range(1, CONV_WIDTH):
        tap = CONV_WIDTH - 1 - back
        y = y + wconv[tap:tap + 1, :] * u_sc[pre - back:pre - back + HALF, :]
    cb = _dot(x, wb_ref[...]) + bb_ref[...]
    s_ref[...] = (cb * y).astype(s_ref.dtype)


def _conv_branch(x, w_in, b_in, w_conv):
    nb = lambda off: off // CONV_TC
    wspec = lambda off: pl.BlockSpec((D_MODEL, CONV_TC), lambda b, h, c, o=nb(off): (0, o + c))
    bspec = lambda off: pl.BlockSpec((1, CONV_TC), lambda b, h, c, o=nb(off): (0, o + c))
    return pl.pallas_call(
        _conv_kernel,
        grid=(BATCH, 2, CONV_DIM // CONV_TC),
        in_specs=[pl.BlockSpec((None, HALF, D_MODEL), lambda b, h, c: (b, h, 0)),
                  wspec(OFF_CB), wspec(OFF_CC), wspec(OFF_CH),
                  bspec(OFF_CB), bspec(OFF_CC), bspec(OFF_CH),
                  pl.BlockSpec((CONV_WIDTH, CONV_TC), lambda b, h, c: (0, c))],
        out_specs=[pl.BlockSpec((None, HALF, CONV_TC), lambda b, h, c: (b, h, c)),
                   pl.BlockSpec((None, HALF, D_MODEL), lambda b, h, c: (b, h, 0))],
        out_shape=[jax.ShapeDtypeStruct((BATCH, SEQ, CONV_DIM), jnp.bfloat16),
                   jax.ShapeDtypeStruct((BATCH, SEQ, D_MODEL), jnp.bfloat16)],
        scratch_shapes=[pltpu.VMEM((HALF + SUBLANES, CONV_TC), jnp.float32),
                        pltpu.VMEM((CONV_DIM // CONV_TC, SUBLANES, CONV_TC), jnp.float32)],
        compiler_params=_params(("arbitrary", "arbitrary", "arbitrary")),
        name="conv_branch",
    )(x, w_in, w_in, w_in, b_in, b_in, b_in, w_conv)


QKV_TN = 512
QKV_CHUNKS = QKV_TN // LANES


def _qkv_kernel(x_ref, w_ref, b_ref, o_ref, sc_ref, sc2_ref):
    x = x_ref[...]
    for gi in (2, 1, 0):
        cols = slice(gi * QKV_TN, (gi + 1) * QKV_TN)
        acc = _dot(x, w_ref[:, cols]) + b_ref[:, cols]
        d = DIL_PATTERNS[gi][1]
        if d == 1:
            o_ref[:, cols] = acc.astype(o_ref.dtype)
            continue
        rows = HALF // d
        for c in range(QKV_CHUNKS):
            sc_ref[gi - 1, c] = acc[:, c * LANES:(c + 1) * LANES]
        if d == 16:
            q4 = HALF // 4
            for c in range(QKV_CHUNKS):
                for r in range(4):
                    sc2_ref[c, r * q4:(r + 1) * q4, :] = sc_ref[gi - 1, c, pl.ds(r, q4, stride=4), :]
            for c in range(QKV_CHUNKS):
                lo = gi * QKV_TN + c * LANES
                for r in range(d):
                    r_lo, r_hi = r % 4, r // 4
                    o_ref[r * rows:(r + 1) * rows, lo:lo + LANES] = (
                        sc2_ref[c, pl.ds(r_lo * q4 + r_hi, rows, stride=4), :].astype(o_ref.dtype))
            continue
        for c in range(QKV_CHUNKS):
            for r in range(d):
                lo = gi * QKV_TN + c * LANES
                o_ref[r * rows:(r + 1) * rows, lo:lo + LANES] = (
                    sc_ref[gi - 1, c, pl.ds(r, rows, stride=d), :].astype(o_ref.dtype))


def _qkv_proj(xb, w_in, b_in):
    n0 = OFF_Q // DIL_DIM
    return pl.pallas_call(
        _qkv_kernel,
        grid=(BATCH, 2, 3),
        in_specs=[pl.BlockSpec((None, HALF, D_MODEL), lambda b, h, n: (b, h, 0)),
                  pl.BlockSpec((D_MODEL, DIL_DIM), lambda b, h, n: (0, n0 + n)),
                  pl.BlockSpec((1, DIL_DIM), lambda b, h, n: (0, n0 + n))],
        out_specs=pl.BlockSpec((None, HALF, DIL_DIM), lambda b, h, n: (b, h, n)),
        out_shape=jax.ShapeDtypeStruct((BATCH, SEQ, 3 * DIL_DIM), jnp.bfloat16),
        scratch_shapes=[pltpu.VMEM((N_DIL_GROUPS - 1, QKV_CHUNKS, HALF, LANES), jnp.float32),
                        pltpu.VMEM((QKV_CHUNKS, HALF, LANES), jnp.float32)],
        compiler_params=_params(("parallel", "parallel", "arbitrary")),
        name="qkv_proj",
    )(xb, w_in, b_in)


MEM_HEADS_PER_STEP = 2
MEM_TN = MEM_HEADS_PER_STEP * MEM_HEAD_DIM


def _memattn_kernel(x_ref, w_ref, b_ref, mk_ref, mv_ref, o_ref):
    mq_all = (_dot(x_ref[...], w_ref[...]) + b_ref[...]).astype(jnp.bfloat16)
    for hh in range(MEM_HEADS_PER_STEP):
        cols = slice(hh * MEM_HEAD_DIM, (hh + 1) * MEM_HEAD_DIM)
        s = _dot_t(mq_all[:, cols], mk_ref[:, cols]) * (MEM_HEAD_DIM ** -0.5 * math.log2(math.e))
        m = jnp.max(s, axis=-1, keepdims=True)
        p = jnp.exp2(s - m)
        den = jnp.sum(p, axis=-1, keepdims=True)
        o = _dot(p.astype(jnp.bfloat16), mv_ref[:, cols]) / den
        o_ref[:, cols] = o.astype(o_ref.dtype)


def _mem_branch(xb, w_in, b_in, kv):
    n0 = OFF_MQ // MEM_TN
    nv = MEM_DIM // MEM_TN
    return pl.pallas_call(
        _memattn_kernel,
        grid=(BATCH, 2, MEM_DIM // MEM_TN),
        in_specs=[pl.BlockSpec((None, HALF, D_MODEL), lambda b, h, n: (b, h, 0)),
                  pl.BlockSpec((D_MODEL, MEM_TN), lambda b, h, n: (0, n0 + n)),
                  pl.BlockSpec((1, MEM_TN), lambda b, h, n: (0, n0 + n)),
                  pl.BlockSpec((None, MEM_LEN, MEM_TN), lambda b, h, n: (b, 0, n)),
                  pl.BlockSpec((None, MEM_LEN, MEM_TN), lambda b, h, n: (b, 0, nv + n))],
        out_specs=pl.BlockSpec((None, HALF, MEM_TN), lambda b, h, n: (b, h, n)),
        out_shape=jax.ShapeDtypeStruct((BATCH, SEQ, MEM_DIM), jnp.bfloat16),
        compiler_params=_params(("parallel", "parallel", "arbitrary")),
        name="mem_branch",
    )(xb, w_in, b_in, kv, kv)


ATT_UNROLL = 16
MIX_ROWS = 256


def _dilattn_kernel(slopes_ref,
                    q0_ref, q1_ref, q2_ref, k0_ref, k1_ref, k2_ref, v0_ref, v1_ref, v2_ref,
                    o_ref, o_sc, l_sc):
    h = pl.program_id(1)
    blk = ATT_BLOCK
    log2e = math.log2(math.e)
    scale = HEAD_DIM ** -0.5 * log2e
    qi = lax.broadcasted_iota(jnp.int32, (blk, 2 * blk), 0) + blk
    kj = lax.broadcasted_iota(jnp.int32, (blk, 2 * blk), 1)
    jrel = qi - kj
    valid = (jrel >= 0) & (jrel <= blk)
    jrel_f = jrel.astype(jnp.float32)

    def bias_for(g):
        slope = slopes_ref[g * HEADS_PER_GROUP + h]
        d = float(DIL_PATTERNS[g][1])
        return jnp.where(valid, (-slope * d * log2e) * jrel_f, -jnp.inf)

    def put(g, row_slice, o, lse):
        o_sc[g, row_slice, :] = o
        l_sc[g, row_slice, :] = jnp.broadcast_to(lse, (blk, HEAD_DIM))

    def run_blocks(g, blocks):
        scores = [_dot_t(q, k) * scale + bias for q, k, _, bias, _ in blocks]
        stats = []
        for s in scores:
            m = jnp.max(s, axis=-1, keepdims=True)
            p = jnp.exp2(s - m)
            stats.append((m, p, jnp.sum(p, axis=-1, keepdims=True)))
        outs = [_dot(p.astype(jnp.bfloat16), blkdef[2]) / den
                for (m, p, den), blkdef in zip(stats, blocks)]
        for o, (m, p, den), blkdef in zip(outs, stats, blocks):
            put(g, blkdef[4], o, m + jnp.log2(den))

    bias0 = bias_for(0)
    prev_cols = kj < blk

    def g0_body(it, carry):
        blocks = []
        for k in range(ATT_UNROLL):
            n = it * ATT_UNROLL + k
            q0 = pl.multiple_of(n * blk, blk)
            p0 = pl.multiple_of(jnp.maximum(n - 1, 0) * blk, blk)
            ctx = lambda ref, p0=p0, q0=q0: jnp.concatenate(
                [ref[pl.ds(p0, blk), :], ref[pl.ds(q0, blk), :]], axis=0)
            bias = jnp.where(prev_cols & (n == 0), -jnp.inf, bias0)
            blocks.append((q0_ref[pl.ds(q0, blk), :], ctx(k0_ref), ctx(v0_ref), bias, pl.ds(q0, blk)))
        run_blocks(0, blocks)
        return carry

    lax.fori_loop(0, SEQ // blk // ATT_UNROLL, g0_body, 0)

    d1 = DIL_PATTERNS[1][1]
    cls1 = HALF // d1
    per_half = cls1 // blk
    bias1 = bias_for(1)

    def row1(r, n):
        return (n // per_half) * HALF + r * cls1 + (n % per_half) * blk

    nblk1 = SEQ // d1 // blk
    cls_per_trip = ATT_UNROLL // nblk1

    def g1_body(it, carry):
        blocks = []
        for c in range(cls_per_trip):
            r = it * cls_per_trip + c
            for n in range(nblk1):
                cur = pl.multiple_of(row1(r, n), blk)
                q = q1_ref[pl.ds(cur, blk), :]
                dst = pl.ds(n * blk * d1 + r, blk, stride=d1)
                if n == 0:
                    blocks.append((q, k1_ref[pl.ds(cur, blk), :], v1_ref[pl.ds(cur, blk), :],
                                   bias1[:, blk:], dst))
                else:
                    prev = pl.multiple_of(row1(r, n - 1), blk)
                    kc = jnp.concatenate([k1_ref[pl.ds(prev, blk), :], k1_ref[pl.ds(cur, blk), :]], axis=0)
                    vc = jnp.concatenate([v1_ref[pl.ds(prev, blk), :], v1_ref[pl.ds(cur, blk), :]], axis=0)
                    blocks.append((q, kc, vc, bias1, dst))
        run_blocks(1, blocks)
        return carry

    lax.fori_loop(0, d1 // cls_per_trip, g1_body, 0)

    d2 = DIL_PATTERNS[2][1]
    cls2 = HALF // d2
    bias2 = bias_for(2)

    def g2_body(it, carry):
        blocks = []
        for k in range(ATT_UNROLL):
            r = it * ATT_UNROLL + k
            a = pl.multiple_of(r * cls2, cls2)
            b = pl.multiple_of(HALF + r * cls2, cls2)
            cat = lambda ref, a=a, b=b: jnp.concatenate(
                [ref[pl.ds(a, cls2), :], ref[pl.ds(b, cls2), :]], axis=0)
            blocks.append((cat(q2_ref), cat(k2_ref), cat(v2_ref), bias2[:, blk:],
                           pl.ds(r, blk, stride=d2)))
        run_blocks(2, blocks)
        return carry

    lax.fori_loop(0, d2 // ATT_UNROLL, g2_body, 0)

    for t in range(SEQ // MIX_ROWS):
        sl = pl.ds(t * MIX_ROWS, MIX_ROWS)
        l0, l1, l2 = l_sc[0, sl, :], l_sc[1, sl, :], l_sc[2, sl, :]
        m = jnp.maximum(jnp.maximum(l0, l1), l2)
        e0, e1, e2 = jnp.exp2(l0 - m), jnp.exp2(l1 - m), jnp.exp2(l2 - m)
        mix = (e0 * o_sc[0, sl, :] + e1 * o_sc[1, sl, :] + e2 * o_sc[2, sl, :]) / (e0 + e1 + e2)
        o_ref[sl, :] = mix.astype(o_ref.dtype)


def _dil_branch(qkv, slopes):
    nq = DIL_DIM // HEAD_DIM

    def spec(section, g):
        return pl.BlockSpec((None, SEQ, HEAD_DIM),
                            lambda b, h, s_ref, o=section * nq + g * HEADS_PER_GROUP: (b, 0, o + h))

    grid_spec = pltpu.PrefetchScalarGridSpec(
        num_scalar_prefetch=1,
        grid=(BATCH, HEADS_PER_GROUP),
        in_specs=[spec(sec, g) for sec in range(3) for g in range(N_DIL_GROUPS)],
        out_specs=pl.BlockSpec((None, SEQ, HEAD_DIM), lambda b, h, s_ref: (b, 0, h)),
        scratch_shapes=[pltpu.VMEM((N_DIL_GROUPS, SEQ, HEAD_DIM), jnp.float32),
                        pltpu.VMEM((N_DIL_GROUPS, SEQ, HEAD_DIM), jnp.float32)],
    )
    return pl.pallas_call(
        _dilattn_kernel,
        grid_spec=grid_spec,
        out_shape=jax.ShapeDtypeStruct((BATCH, SEQ, DIL_OUT_DIM), jnp.bfloat16),
        compiler_params=_params(("parallel", "arbitrary")),
        name="dil_attn",
    )(slopes, *([qkv] * 9))


MERGE_TM = 1024
MERGE_TN = 512


def _merge_kernel(x_ref, wg0_ref, wg1_ref, wg2_ref, bg0_ref, bg1_ref, bg2_ref,
                  sc_ref, od_ref, om_ref, wout_ref, o_ref):
    r1, r2 = CONV_DIM, CONV_DIM + DIL_OUT_DIM
    x = x_ref[...]
    g0 = jax.nn.sigmoid(_dot(x, wg0_ref[...]) + bg0_ref[...])
    acc = g0 * _dot(sc_ref[...], wout_ref[0:r1, :])
    g1 = jax.nn.sigmoid(_dot(x, wg1_ref[...]) + bg1_ref[...])
    acc = acc + g1 * _dot(od_ref[...], wout_ref[r1:r2, :])
    g2 = jax.nn.sigmoid(_dot(x, wg2_ref[...]) + bg2_ref[...])
    acc = acc + g2 * _dot(om_ref[...], wout_ref[r2:r2 + MEM_DIM, :])
    o_ref[...] = acc.astype(o_ref.dtype)


def _merge(x2, w_in, b_in, s_conv, o_dil, o_mem, w_out):
    nb = lambda br: (OFF_GATE + br * D_MODEL) // MERGE_TN
    gspec = lambda br: pl.BlockSpec((D_MODEL, MERGE_TN), lambda i, n, o=nb(br): (0, o + n))
    bspec = lambda br: pl.BlockSpec((1, MERGE_TN), lambda i, n, o=nb(br): (0, o + n))
    act = lambda width: pl.BlockSpec((MERGE_TM, width), lambda i, n: (i, 0))
    wout = lambda width: pl.BlockSpec((width, MERGE_TN), lambda i, n: (0, n))
    return pl.pallas_call(
        _merge_kernel,
        grid=(TOKENS // MERGE_TM, D_MODEL // MERGE_TN),
        in_specs=[act(D_MODEL), gspec(0), gspec(1), gspec(2), bspec(0), bspec(1), bspec(2),
                  act(CONV_DIM), act(DIL_OUT_DIM), act(MEM_DIM),
                  wout(CONV_DIM + DIL_OUT_DIM + MEM_DIM)],
        out_specs=pl.BlockSpec((MERGE_TM, MERGE_TN), lambda i, n: (i, n)),
        out_shape=jax.ShapeDtypeStruct((TOKENS, D_MODEL), jnp.bfloat16),
        compiler_params=_params(("parallel", "arbitrary")),
        name="gated_merge",
    )(x2, w_in, w_in, w_in, b_in, b_in, b_in, s_conv, o_dil, o_mem, w_out)


OPROJ_TM = 512


OPROJ_PARTS = 4


def _oproj_kernel(x_ref, m_ref, wo_ref, g_ref, b_ref, wr_ref, br_ref, x1_ref, logit_ref):
    rows = OPROJ_TM // OPROJ_PARTS
    parts = [pl.ds(i * rows, rows) for i in range(OPROJ_PARTS)]
    proj = [_dot(m_ref[p, :], wo_ref[...]) for p in parts]
    for p, y in zip(parts, proj):
        x1 = _layer_norm(ALPHA * x_ref[p, :] + y, g_ref[...], b_ref[...])
        x1_ref[p, :] = x1
        logit_ref[p, :] = _dot(x1, wr_ref[...]) + br_ref[...]


def _oproj(x2, merged, w_o, g, b, w_route, b_route):
    row = lambda width: pl.BlockSpec((OPROJ_TM, width), lambda i: (i, 0))
    full = lambda r, c: pl.BlockSpec((r, c), lambda i: (0, 0))
    return pl.pallas_call(
        _oproj_kernel,
        grid=(TOKENS // OPROJ_TM,),
        in_specs=[row(D_MODEL), row(D_MODEL), full(D_MODEL, D_MODEL), full(1, D_MODEL), full(1, D_MODEL),
                  full(D_MODEL, LANES), full(1, LANES)],
        out_specs=[row(D_MODEL), row(LANES)],
        out_shape=[jax.ShapeDtypeStruct((TOKENS, D_MODEL), jnp.float32),
                   jax.ShapeDtypeStruct((TOKENS, LANES), jnp.float32)],
        compiler_params=_params(("parallel",)),
        name="oproj_ln1",
    )(x2, merged, w_o, g, b, w_route, b_route)


def _route_kernel(logit_ref, rec_ref, rect_ref, cnt_ref, carry_sc):
    @pl.when(pl.program_id(0) == 0)
    def _():
        carry_sc[...] = jnp.zeros_like(carry_sc)

    tm = ROUTE_TILE
    zt = jnp.transpose(logit_ref[...])
    row = lax.broadcasted_iota(jnp.int32, (SUBLANES, tm), 0)
    neg = -jnp.inf
    colmax = lambda a: jnp.max(a, axis=0, keepdims=True)
    first = lambda hit: jnp.min(jnp.where(hit, row, SUBLANES), axis=0, keepdims=True)

    glog = jnp.where(row < N_GROUPS, zt[0:SUBLANES], neg)
    gmax = colmax(glog)
    gsel = first(glog == gmax)
    gw = 1.0 / jnp.sum(jnp.exp(glog - gmax), axis=0, keepdims=True)

    elog = zt[ROUTER_LANE0:ROUTER_LANE0 + EXPERTS_PER_GROUP]
    for g in range(1, N_GROUPS):
        lo = ROUTER_LANE0 + g * EXPERTS_PER_GROUP
        elog = jnp.where(gsel == g, zt[lo:lo + EXPERTS_PER_GROUP], elog)
    top1 = colmax(elog)
    i1 = first(elog == top1)
    rest = jnp.where(row == i1, neg, elog)
    top2 = colmax(rest)
    i2 = first(rest == top2)
    t = jnp.exp(top2 - top1)
    w1 = gw / (1.0 + t)
    w2 = w1 * t

    hit = (row == i1) | (row == i2)
    onehot = jnp.concatenate([jnp.where(hit & (gsel == g), 1.0, 0.0) for g in range(N_GROUPS)], axis=0)
    chunks = tm // LANES
    stacked = jnp.concatenate([onehot[:, c * LANES:(c + 1) * LANES] for c in range(chunks)], axis=0)
    ri = lax.broadcasted_iota(jnp.int32, (LANES, LANES), 0)
    ci = lax.broadcasted_iota(jnp.int32, (LANES, LANES), 1)
    upper = jnp.where(ri <= ci, 1.0, 0.0).astype(jnp.bfloat16)
    within = _dot(stacked.astype(jnp.bfloat16), upper)
    run = carry_sc[:, 0:1]
    pieces = []
    for c in range(chunks):
        cum = within[c * N_EXPERTS:(c + 1) * N_EXPERTS, :] + run
        pieces.append(cum)
        run = cum[:, LANES - 1:LANES]
    before = jnp.concatenate(pieces, axis=1) - 1.0
    carry_sc[...] = jnp.broadcast_to(run, carry_sc.shape)
    cnt_ref[...] = jnp.broadcast_to(run, cnt_ref.shape)

    rank_a = jnp.zeros((1, tm), jnp.float32)
    rank_b = jnp.zeros((1, tm), jnp.float32)
    for g in range(N_GROUPS):
        b8 = before[g * EXPERTS_PER_GROUP:(g + 1) * EXPERTS_PER_GROUP, :]
        mine = gsel == g
        rank_a = rank_a + jnp.sum(jnp.where(mine & (row == i1), b8, 0.0), axis=0, keepdims=True)
        rank_b = rank_b + jnp.sum(jnp.where(mine & (row == i2), b8, 0.0), axis=0, keepdims=True)

    base = (gsel * EXPERTS_PER_GROUP).astype(jnp.float32)
    vals = [base + i1.astype(jnp.float32), base + i2.astype(jnp.float32), rank_a, rank_b, w1, w2]
    rect = jnp.zeros((SUBLANES, tm), jnp.float32)
    for k, val in enumerate(vals):
        rect = jnp.where(row == k, val, rect)
    rect_ref[...] = rect
    padded = jnp.concatenate([rect, jnp.zeros((LANES - SUBLANES, tm), jnp.float32)], axis=0)
    rec_ref[...] = jnp.transpose(padded)


def _route(logits):
    return pl.pallas_call(
        _route_kernel,
        grid=(TOKENS // ROUTE_TILE,),
        in_specs=[pl.BlockSpec((ROUTE_TILE, LANES), lambda i: (i, 0))],
        out_specs=[pl.BlockSpec((ROUTE_TILE, LANES), lambda i: (i, 0)),
                   pl.BlockSpec((SUBLANES, ROUTE_TILE), lambda i: (0, i)),
                   pl.BlockSpec((N_EXPERTS, LANES), lambda i: (0, 0))],
        out_shape=[jax.ShapeDtypeStruct((TOKENS, LANES), jnp.float32),
                   jax.ShapeDtypeStruct((SUBLANES, TOKENS), jnp.float32),
                   jax.ShapeDtypeStruct((N_EXPERTS, LANES), jnp.float32)],
        scratch_shapes=[pltpu.VMEM((N_EXPERTS, LANES), jnp.float32)],
        compiler_params=_params(("arbitrary",)),
        name="route",
    )(logits)


def _slots_kernel(start_ref, rect_ref, pos_ref):
    r = rect_ref[...]
    row = lax.broadcasted_iota(jnp.int32, r.shape, 0)

    def pos_of(e, rank):
        e = e.astype(jnp.int32)
        start = jnp.zeros(e.shape, jnp.int32)
        for k in range(N_EXPERTS):
            start = jnp.where(e == k, start_ref[k], start)
        return start + rank.astype(jnp.int32)

    pos_a = pos_of(r[0:1], r[2:3])
    pos_b = pos_of(r[1:2], r[3:4])
    pos_ref[...] = jnp.where(row == 0, pos_a, jnp.where(row == 1, pos_b, 0))


def _slots(starts, rect):
    grid_spec = pltpu.PrefetchScalarGridSpec(
        num_scalar_prefetch=1,
        grid=(TOKENS // ROUTE_TILE,),
        in_specs=[pl.BlockSpec((SUBLANES, ROUTE_TILE), lambda i, st: (0, i))],
        out_specs=pl.BlockSpec((SUBLANES, ROUTE_TILE), lambda i, st: (0, i)),
    )
    return pl.pallas_call(
        _slots_kernel,
        grid_spec=grid_spec,
        out_shape=jax.ShapeDtypeStruct((SUBLANES, TOKENS), jnp.int32),
        compiler_params=_params(("parallel",)),
        name="slots",
    )(starts, rect)


DISPATCH_TILE = 512
SORTED_ROWS = N_EXPERT_TILES * EXPERT_TILE


def _dispatch_kernel(pad_start_ref, pad_n_ref, nvalid_ref, pos_ref, x_ref, xs_hbm, zero_sc, pk_sc, sem,
                     row_sem):
    groups = DISPATCH_TILE // SUBLANES

    @pl.when(pl.program_id(0) == 0)
    def _():
        zero_sc[...] = jnp.zeros_like(zero_sc)

        def tile_copy(t):
            return pltpu.make_async_copy(zero_sc, xs_hbm.at[pl.ds(pl.multiple_of(t * EXPERT_TILE, EXPERT_TILE),
                                                                  EXPERT_TILE)], sem.at[2])

        def start_tile(t, c):
            tile_copy(t).start()
            return c

        def wait_tile(t, c):
            tile_copy(t).wait()
            return c
        lax.fori_loop(nvalid_ref[0], N_EXPERT_TILES, start_tile, 0)
        lax.fori_loop(nvalid_ref[0], N_EXPERT_TILES, wait_tile, 0)

        def pad_copies(e, act):
            n = pad_n_ref[e]
            cur = pad_start_ref[e]
            for bit in range((EXPERT_TILE - 1).bit_length()):
                size = 1 << bit
                has = lax.bitwise_and(n, size)

                @pl.when(has != 0)
                def _(cur=cur, size=size):
                    at = pl.multiple_of(cur, math.gcd(size, EXPERT_TILE))
                    act(pltpu.make_async_copy(zero_sc.at[pl.ds(0, size)], xs_hbm.at[pl.ds(at, size)],
                                              sem.at[1]))
                cur = cur + has

        def start_pads(e, c):
            pad_copies(e, lambda cp: cp.start())
            return c

        def wait_pads(e, c):
            pad_copies(e, lambda cp: cp.wait())
            return c
        lax.fori_loop(0, N_EXPERTS, start_pads, 0)
        lax.fori_loop(0, N_EXPERTS, wait_pads, 0)

    j = pl.program_id(0)
    slot = lax.rem(j, 2)
    pk_sc[slot] = x_ref[...].reshape(groups, SUBLANES, D_MODEL)

    def body(gi, c):
        for s in range(SUBLANES):
            for k in range(2):
                p = pos_ref[0, 0, k * DISPATCH_TILE + gi * SUBLANES + s]
                pltpu.make_async_copy(pk_sc.at[slot, gi, pl.ds(s, 1)], xs_hbm.at[pl.ds(p, 1)],
                                      row_sem.at[slot]).start(priority=k)
        return c
    lax.fori_loop(0, groups, body, 0)

    def wait_slot(sl):
        for _ in range(2 * groups):
            pltpu.make_async_copy(pk_sc.at[sl, 0], xs_hbm.at[pl.ds(0, SUBLANES)], row_sem.at[sl]).wait()

    @pl.when(j > 0)
    def _():
        wait_slot(1 - slot)

    @pl.when(j == pl.num_programs(0) - 1)
    def _():
        wait_slot(slot)


def _dispatch(pad_start, pad_n, n_valid_tiles, pos3, x1):
    nt = TOKENS // DISPATCH_TILE
    groups = DISPATCH_TILE // SUBLANES
    grid_spec = pltpu.PrefetchScalarGridSpec(
        num_scalar_prefetch=3,
        grid=(nt,),
        in_specs=[pl.BlockSpec((1, 1, 2 * DISPATCH_TILE), lambda j, *_: (j, 0, 0), memory_space=pltpu.SMEM),
                  pl.BlockSpec((DISPATCH_TILE, D_MODEL), lambda j, *_: (j, 0))],
        out_specs=pl.BlockSpec(memory_space=pl.ANY),
        scratch_shapes=[pltpu.VMEM((EXPERT_TILE, D_MODEL), jnp.float32),
                        pltpu.VMEM((2, groups, SUBLANES, D_MODEL), jnp.float32),
                        pltpu.SemaphoreType.DMA((3,)),
                        pltpu.SemaphoreType.DMA((2,))],
    )
    return pl.pallas_call(
        _dispatch_kernel,
        grid_spec=grid_spec,
        out_shape=jax.ShapeDtypeStruct((SORTED_ROWS, D_MODEL), jnp.float32),
        compiler_params=_params(("arbitrary",)),
        name="dispatch",
    )(pad_start, pad_n, n_valid_tiles, pos3, x1)


def _ffn_kernel(texp_ref, tvalid_ref, tslot_ref, tfirst_ref, tnext_ref, nvalid_ref,
                xs_ref, wg_hbm, wu_hbm, wd_hbm, y_ref, wg_buf, wu_buf, wd_buf, sem):
    j = pl.program_id(0)
    valid = tvalid_ref[j] == 1
    slot = tslot_ref[j]

    def weight_copies(e, s):
        return (pltpu.make_async_copy(wg_hbm.at[e], wg_buf.at[s], sem.at[s, 0]),
                pltpu.make_async_copy(wu_hbm.at[e], wu_buf.at[s], sem.at[s, 1]),
                pltpu.make_async_copy(wd_hbm.at[e], wd_buf.at[s], sem.at[s, 2]))

    @pl.when(valid & (tfirst_ref[j] == 1))
    def _():
        @pl.when(j == 0)
        def _():
            for cp in weight_copies(texp_ref[j], slot):
                cp.start()

        for cp in weight_copies(texp_ref[j], slot):
            cp.wait()

        @pl.when(tnext_ref[j] >= 0)
        def _():
            for cp in weight_copies(tnext_ref[j], 1 - slot):
                cp.start()

    @pl.when(valid)
    def _():
        xt = xs_ref[...]
        a = _dot(xt, wg_buf[slot])
        u = _dot(xt, wu_buf[slot])
        hid = (a * jax.nn.sigmoid(a)) * u
        y_ref[...] = _dot(hid, wd_buf[slot])

    @pl.when(jnp.logical_not(valid))
    def _():
        y_ref[...] = jnp.zeros_like(y_ref)


def _expert_ffn(tables, xs, w_gate, w_up, w_down):
    nt = N_EXPERT_TILES
    xmap = lambda j, te, tv, ts, tf, tn, nv: (jnp.minimum(j, nv[0] - 1), 0)
    grid_spec = pltpu.PrefetchScalarGridSpec(
        num_scalar_prefetch=6,
        grid=(nt,),
        in_specs=[pl.BlockSpec((EXPERT_TILE, D_MODEL), xmap),
                  pl.BlockSpec(memory_space=pl.ANY),
                  pl.BlockSpec(memory_space=pl.ANY),
                  pl.BlockSpec(memory_space=pl.ANY)],
        out_specs=pl.BlockSpec((EXPERT_TILE, D_MODEL), lambda j, *_: (j, 0)),
        scratch_shapes=[pltpu.VMEM((2, D_MODEL, D_EXPERT), jnp.float32),
                        pltpu.VMEM((2, D_MODEL, D_EXPERT), jnp.float32),
                        pltpu.VMEM((2, D_EXPERT, D_MODEL), jnp.float32),
                        pltpu.SemaphoreType.DMA((2, 3))],
    )
    return pl.pallas_call(
        _ffn_kernel,
        grid_spec=grid_spec,
        out_shape=jax.ShapeDtypeStruct((nt * EXPERT_TILE, D_MODEL), jnp.float32),
        compiler_params=_params(("arbitrary",)),
        name="expert_ffn",
    )(*tables, xs, w_gate, w_up, w_down)


def _combine_kernel(pos_cur_ref, pos_next_ref, r_ref, x1_ref, y_hbm, g_ref, b_ref, o_ref, ybuf, sem):
    j = pl.program_id(0)
    nt = pl.num_programs(0)
    slot = lax.rem(j, 2)
    groups = COMBINE_TILE // SUBLANES

    def gather(pos_ref, dst_slot):
        def body(gi, c):
            for s in range(SUBLANES):
                for k in range(2):
                    p = pos_ref[0, 0, k * COMBINE_TILE + gi * SUBLANES + s]
                    pltpu.make_async_copy(y_hbm.at[pl.ds(p, 1)], ybuf.at[dst_slot, k, gi, pl.ds(s, 1)],
                                          sem.at[dst_slot]).start(priority=k)
            return c
        lax.fori_loop(0, groups, body, 0)

    @pl.when(j == 0)
    def _():
        gather(pos_cur_ref, 0)

    @pl.when(j + 1 < nt)
    def _():
        gather(pos_next_ref, 1 - slot)

    for _ in range(2 * groups):
        pltpu.make_async_copy(y_hbm.at[pl.ds(0, SUBLANES)], ybuf.at[slot, 0, 0], sem.at[slot]).wait()
    r = r_ref[...]
    ya = ybuf[slot, 0].reshape(COMBINE_TILE, D_MODEL)
    yb = ybuf[slot, 1].reshape(COMBINE_TILE, D_MODEL)
    hres = ALPHA * x1_ref[...] + r[:, 4:5] * ya + r[:, 5:6] * yb
    o_ref[...] = _layer_norm(hres, g_ref[...], b_ref[...])


def _combine(pos3, routing, x1, y_sorted, g, b):
    nt = TOKENS // COMBINE_TILE
    groups = COMBINE_TILE // SUBLANES
    return pl.pallas_call(
        _combine_kernel,
        grid=(nt,),
        in_specs=[pl.BlockSpec((1, 1, 2 * COMBINE_TILE), lambda j: (j, 0, 0), memory_space=pltpu.SMEM),
                  pl.BlockSpec((1, 1, 2 * COMBINE_TILE), lambda j: (jnp.minimum(j + 1, nt - 1), 0, 0),
                               memory_space=pltpu.SMEM),
                  pl.BlockSpec((COMBINE_TILE, LANES), lambda j: (j, 0)),
                  pl.BlockSpec((COMBINE_TILE, D_MODEL), lambda j: (j, 0)),
                  pl.BlockSpec(memory_space=pl.ANY),
                  pl.BlockSpec((1, D_MODEL), lambda j: (0, 0)),
                  pl.BlockSpec((1, D_MODEL), lambda j: (0, 0))],
        out_specs=pl.BlockSpec((COMBINE_TILE, D_MODEL), lambda j: (j, 0)),
        out_shape=jax.ShapeDtypeStruct((TOKENS, D_MODEL), jnp.float32),
        scratch_shapes=[pltpu.VMEM((2, 2, groups, SUBLANES, D_MODEL), jnp.float32),
                        pltpu.SemaphoreType.DMA((2,))],
        compiler_params=_params(("arbitrary",)),
        name="combine_ln2",
    )(pos3, pos3, routing, x1, y_sorted, g, b)


def _alibi_slopes():
    n = N_DIL_GROUPS * HEADS_PER_GROUP
    return jnp.asarray(2.0 ** (-ALIBI_MAX * np.arange(1, n + 1, dtype=np.float32) / n), jnp.float32)


def kernel(x, mem, ln_mem_g, ln_mem_b, w_in, b_in, w_conv, w_conv_out, w_dil_out, w_mem_kv, w_mem_out, w_o, ln1_g, ln1_b, w_group, b_group, w_router, b_router, w_gate, w_up, w_down, ln2_g, ln2_b):
    assert x.shape == (BATCH, SEQ, D_MODEL) and w_in.shape == (1, D_MODEL, IN_DIM)
    bf16 = jnp.bfloat16
    row = lambda v: v.reshape(1, -1)
    w_in2 = w_in[0].astype(bf16)
    b_in2 = b_in

    kv = _memkv(mem, row(ln_mem_g), row(ln_mem_b), w_mem_kv[0])
    s_conv, xb = _conv_branch(x, w_in2, b_in2, w_conv[0])
    qkv = _qkv_proj(xb, w_in2, b_in2)
    o_mem = _mem_branch(xb, w_in2, b_in2, kv)
    o_dil = _dil_branch(qkv, _alibi_slopes())

    x2 = x.reshape(TOKENS, D_MODEL)
    merged = _merge(xb.reshape(TOKENS, D_MODEL), w_in2, b_in2,
                    s_conv.reshape(TOKENS, CONV_DIM), o_dil.reshape(TOKENS, DIL_OUT_DIM),
                    o_mem.reshape(TOKENS, MEM_DIM),
                    jnp.concatenate([w_conv_out[0], w_dil_out[0], w_mem_out[0]], axis=0).astype(bf16))

    gap = ROUTER_LANE0 - N_GROUPS
    tail = LANES - ROUTER_LANE0 - N_EXPERTS
    w_route = jnp.concatenate(
        [w_group[0], jnp.zeros((D_MODEL, gap), jnp.float32),
         jnp.transpose(w_router[0], (1, 0, 2)).reshape(D_MODEL, N_EXPERTS),
         jnp.zeros((D_MODEL, tail), jnp.float32)], axis=1)
    b_route = jnp.concatenate(
        [b_group[0], jnp.zeros((gap,), jnp.float32), b_router[0].reshape(N_EXPERTS),
         jnp.zeros((tail,), jnp.float32)]).reshape(1, LANES)
    x1, logits = _oproj(x2, merged, w_o[0].astype(bf16), ln1_g, ln1_b, w_route, b_route)

    routing, routing_t, counts_b = _route(logits)

    i32 = jnp.int32
    counts = counts_b[:, 0].astype(i32)
    padded = ((counts + EXPERT_TILE - 1) // EXPERT_TILE) * EXPERT_TILE
    ends = jnp.cumsum(padded)
    starts = ends - padded
    pos = _slots(starts.astype(i32), routing_t)[0:2]

    def pos_tiles(tile):
        return jnp.transpose(pos.reshape(2, TOKENS // tile, tile), (1, 0, 2)).reshape(TOKENS // tile, 1, 2 * tile)

    tile_start = jnp.arange(N_EXPERT_TILES, dtype=i32) * EXPERT_TILE
    tile_expert = jnp.minimum(jnp.sum((ends[None, :] <= tile_start[:, None]).astype(i32), axis=1),
                              N_EXPERTS - 1)
    tile_valid = tile_start < ends[-1]
    prev_expert = jnp.concatenate([jnp.full((1,), -1, i32), tile_expert[:-1]])
    tile_first = tile_valid & (tile_expert != prev_expert)
    tile_slot = (jnp.cumsum(tile_first.astype(i32)) - 1) & 1
    big = N_EXPERTS
    idx = jnp.where(counts > 0, jnp.arange(N_EXPERTS, dtype=i32), big)
    later = jnp.concatenate([lax.cummin(idx[::-1])[::-1][1:], jnp.full((1,), big, i32)])
    next_used = jnp.where(later == big, -1, later)
    n_valid_tiles = (ends[-1:] // EXPERT_TILE).astype(i32)
    tables = (tile_expert, tile_valid.astype(i32), tile_slot, tile_first.astype(i32),
              next_used[tile_expert], n_valid_tiles)

    xs = _dispatch(starts + counts, padded - counts, n_valid_tiles,
                   pos_tiles(DISPATCH_TILE), x1)
    y_sorted = _expert_ffn(tables, xs,
                           w_gate.reshape(N_EXPERTS, D_MODEL, D_EXPERT),
                           w_up.reshape(N_EXPERTS, D_MODEL, D_EXPERT),
                           w_down.reshape(N_EXPERTS, D_EXPERT, D_MODEL))
    out = _combine(pos_tiles(COMBINE_TILE), routing, x1, y_sorted,
                   ln2_g, ln2_b)
    return out.reshape(BATCH, SEQ, D_MODEL)
```

```python
import math

import numpy as np
import jax
import jax.numpy as jnp
from jax import lax
from jax.experimental import pallas as pl
from jax.experimental.pallas import tpu as pltpu

D_MODEL = 2048
BATCH = 8
SEQ = 2048
TOKENS = BATCH * SEQ
CONV_DIM = 1024
CONV_WIDTH = 3
DIL_PATTERNS = ((128, 1), (512, 4), (2048, 16))
N_DIL_GROUPS = 3
HEADS_PER_GROUP = 4
HEAD_DIM = 128
DIL_DIM = N_DIL_GROUPS * HEADS_PER_GROUP * HEAD_DIM
DIL_OUT_DIM = HEADS_PER_GROUP * HEAD_DIM
ATT_BLOCK = 128
ALIBI_MAX = 8.0
MEM_LEN = 256
MEM_HEADS = 4
MEM_HEAD_DIM = 256
MEM_DIM = MEM_HEADS * MEM_HEAD_DIM
N_BRANCHES = 3
IN_DIM = 3 * CONV_DIM + 3 * DIL_DIM + MEM_DIM + N_BRANCHES * D_MODEL
N_GROUPS = 4
EXPERTS_PER_GROUP = 8
N_EXPERTS = N_GROUPS * EXPERTS_PER_GROUP
D_EXPERT = 512
ALPHA = 2.0 ** 0.25
LN_EPS = 1e-5

OFF_CB = 0
OFF_CC = CONV_DIM
OFF_CH = 2 * CONV_DIM
OFF_Q = 3 * CONV_DIM
OFF_MQ = OFF_Q + 3 * DIL_DIM
OFF_GATE = OFF_MQ + MEM_DIM

LANES = 128
SUBLANES = 8
HALF = SEQ // 2
VMEM_LIMIT = 56 * 1024 * 1024

ROUTE_TILE = 1024
EXPERT_TILE = 512
N_EXPERT_TILES = 2 * TOKENS // EXPERT_TILE + N_EXPERTS
COMBINE_TILE = 256
ROUTER_LANE0 = 8


def _params(sem, limit=VMEM_LIMIT):
    return pltpu.CompilerParams(dimension_semantics=sem, vmem_limit_bytes=limit)


def _layer_norm(x, g, b):
    mu = jnp.mean(x, axis=-1, keepdims=True)
    xc = x - mu
    var = jnp.mean(xc * xc, axis=-1, keepdims=True)
    return xc * lax.rsqrt(var + LN_EPS) * g + b


def _dot(a, b):
    return jnp.dot(a, b, preferred_element_type=jnp.float32)


def _dot_t(a, b):
    return lax.dot_general(a, b, (((1,), (1,)), ((), ())), preferred_element_type=jnp.float32)


MEMKV_BATCHES = 2


def _memkv_kernel(mem_ref, g_ref, b_ref, w_ref, kv_ref):
    rows = MEMKV_BATCHES * MEM_LEN
    y = _layer_norm(mem_ref[...].reshape(rows, D_MODEL), g_ref[...], b_ref[...])
    kv = _dot(y, w_ref[...])
    kv_ref[...] = kv.astype(kv_ref.dtype).reshape(MEMKV_BATCHES, MEM_LEN, 2 * MEM_DIM)


def _memkv(mem, g, b, w):
    return pl.pallas_call(
        _memkv_kernel,
        grid=(BATCH // MEMKV_BATCHES,),
        in_specs=[pl.BlockSpec((MEMKV_BATCHES, MEM_LEN, D_MODEL), lambda i: (i, 0, 0)),
                  pl.BlockSpec((1, D_MODEL), lambda i: (0, 0)),
                  pl.BlockSpec((1, D_MODEL), lambda i: (0, 0)),
                  pl.BlockSpec((D_MODEL, 2 * MEM_DIM), lambda i: (0, 0))],
        out_specs=pl.BlockSpec((MEMKV_BATCHES, MEM_LEN, 2 * MEM_DIM), lambda i: (i, 0, 0)),
        out_shape=jax.ShapeDtypeStruct((BATCH, MEM_LEN, 2 * MEM_DIM), jnp.bfloat16),
        compiler_params=_params(("parallel",)),
        name="mem_kv",
    )(mem, g, b, w)


CONV_TC = 512


def _conv_kernel(x_ref, wb_ref, wc_ref, wh_ref, bb_ref, bc_ref, bh_ref, wconv_ref, s_ref, xb_ref,
                 u_sc, carry_sc):
    half = pl.program_id(1)
    c = pl.program_id(2)

    @pl.when(c == 0)
    def _():
        xb_ref[...] = x_ref[...].astype(xb_ref.dtype)

    x = xb_ref[...]
    cc = _dot(x, wc_ref[...]) + bc_ref[...]
    ch = _dot(x, wh_ref[...]) + bh_ref[...]
    u = cc * ch
    pre = SUBLANES
    u_sc[0:pre, :] = jnp.where(half == 0, 0.0, carry_sc[c])
    u_sc[pre:pre + HALF, :] = u
    carry_sc[c] = u[HALF - pre:HALF, :]
    wconv = wconv_ref[...]
    y = wconv[CONV_WIDTH - 1:CONV_WIDTH, :] * u
    for back in range(1, CONV_WIDTH):
        tap = CONV_WIDTH - 1 - back
        y = y + wconv[tap:tap + 1, :] * u_sc[pre - back:pre - back + HALF, :]
    cb = _dot(x, wb_ref[...]) + bb_ref[...]
    s_ref[...] = (cb * y).astype(s_ref.dtype)


def _conv_branch(x, w_in, b_in, w_conv):
    nb = lambda off: off // CONV_TC
    wspec = lambda off: pl.BlockSpec((D_MODEL, CONV_TC), lambda b, h, c, o=nb(off): (0, o + c))
    bspec = lambda off: pl.BlockSpec((1, CONV_TC), lambda b, h, c, o=nb(off): (0, o + c))
    return pl.pallas_call(
        _conv_kernel,
        grid=(BATCH, 2, CONV_DIM // CONV_TC),
        in_specs=[pl.BlockSpec((None, HALF, D_MODEL), lambda b, h, c: (b, h, 0)),
                  wspec(OFF_CB), wspec(OFF_CC), wspec(OFF_CH),
                  bspec(OFF_CB), bspec(OFF_CC), bspec(OFF_CH),
                  pl.BlockSpec((CONV_WIDTH, CONV_TC), lambda b, h, c: (0, c))],
        out_specs=[pl.BlockSpec((None, HALF, CONV_TC), lambda b, h, c: (b, h, c)),
                   pl.BlockSpec((None, HALF, D_MODEL), lambda b, h, c: (b, h, 0))],
        out_shape=[jax.ShapeDtypeStruct((BATCH, SEQ, CONV_DIM), jnp.bfloat16),
                   jax.ShapeDtypeStruct((BATCH, SEQ, D_MODEL), jnp.bfloat16)],
        scratch_shapes=[pltpu.VMEM((HALF + SUBLANES, CONV_TC), jnp.float32),
                        pltpu.VMEM((CONV_DIM // CONV_TC, SUBLANES, CONV_TC), jnp.float32)],
        compiler_params=_params(("arbitrary", "arbitrary", "arbitrary")),
        name="conv_branch",
    )(x, w_in, w_in, w_in, b_in, b_in, b_in, w_conv)


QKV_TN = 512
QKV_CHUNKS = QKV_TN // LANES


def _qkv_kernel(x_ref, w_ref, b_ref, o_ref, sc_ref, sc2_ref):
    x = x_ref[...]
    for gi in (2, 1, 0):
        cols = slice(gi * QKV_TN, (gi + 1) * QKV_TN)
        acc = _dot(x, w_ref[:, cols]) + b_ref[:, cols]
        d = DIL_PATTERNS[gi][1]
        if d == 1:
            o_ref[:, cols] = acc.astype(o_ref.dtype)
            continue
        rows = HALF // d
        for c in range(QKV_CHUNKS):
            sc_ref[gi - 1, c] = acc[:, c * LANES:(c + 1) * LANES]
        if d == 16:
            q4 = HALF // 4
            for c in range(QKV_CHUNKS):
                for r in range(4):
                    sc2_ref[c, r * q4:(r + 1) * q4, :] = sc_ref[gi - 1, c, pl.ds(r, q4, stride=4), :]
            for c in range(QKV_CHUNKS):
                lo = gi * QKV_TN + c * LANES
                for r in range(d):
                    r_lo, r_hi = r % 4, r // 4
                    o_ref[r * rows:(r + 1) * rows, lo:lo + LANES] = (
                        sc2_ref[c, pl.ds(r_lo * q4 + r_hi, rows, stride=4), :].astype(o_ref.dtype))
            continue
        for c in range(QKV_CHUNKS):
            for r in range(d):
                lo = gi * QKV_TN + c * LANES
                o_ref[r * rows:(r + 1) * rows, lo:lo + LANES] = (
                    sc_ref[gi - 1, c, pl.ds(r, rows, stride=d), :].astype(o_ref.dtype))


def _qkv_proj(xb, w_in, b_in):
    n0 = OFF_Q // DIL_DIM
    return pl.pallas_call(
        _qkv_kernel,
        grid=(BATCH, 2, 3),
        in_specs=[pl.BlockSpec((None, HALF, D_MODEL), lambda b, h, n: (b, h, 0)),
                  pl.BlockSpec((D_MODEL, DIL_DIM), lambda b, h, n: (0, n0 + n)),
                  pl.BlockSpec((1, DIL_DIM), lambda b, h, n: (0, n0 + n))],
        out_specs=pl.BlockSpec((None, HALF, DIL_DIM), lambda b, h, n: (b, h, n)),
        out_shape=jax.ShapeDtypeStruct((BATCH, SEQ, 3 * DIL_DIM), jnp.bfloat16),
        scratch_shapes=[pltpu.VMEM((N_DIL_GROUPS - 1, QKV_CHUNKS, HALF, LANES), jnp.float32),
                        pltpu.VMEM((QKV_CHUNKS, HALF, LANES), jnp.float32)],
        compiler_params=_params(("parallel", "parallel", "arbitrary")),
        name="qkv_proj",
    )(xb, w_in, b_in)


MEM_HEADS_PER_STEP = 2
MEM_TN = MEM_HEADS_PER_STEP * MEM_HEAD_DIM


def _memattn_kernel(x_ref, w_ref, b_ref, mk_ref, mv_ref, o_ref):
    mq_all = (_dot(x_ref[...], w_ref[...]) + b_ref[...]).astype(jnp.bfloat16)
    for hh in range(MEM_HEADS_PER_STEP):
        cols = slice(hh * MEM_HEAD_DIM, (hh + 1) * MEM_HEAD_DIM)
        s = _dot_t(mq_all[:, cols], mk_ref[:, cols]) * (MEM_HEAD_DIM ** -0.5 * math.log2(math.e))
        m = jnp.max(s, axis=-1, keepdims=True)
        p = jnp.exp2(s - m)
        den = jnp.sum(p, axis=-1, keepdims=True)
        o = _dot(p.astype(jnp.bfloat16), mv_ref[:, cols]) / den
        o_ref[:, cols] = o.astype(o_ref.dtype)


def _mem_branch(xb, w_in, b_in, kv):
    n0 = OFF_MQ // MEM_TN
    nv = MEM_DIM // MEM_TN
    return pl.pallas_call(
        _memattn_kernel,
        grid=(BATCH, 2, MEM_DIM // MEM_TN),
        in_specs=[pl.BlockSpec((None, HALF, D_MODEL), lambda b, h, n: (b, h, 0)),
                  pl.BlockSpec((D_MODEL, MEM_TN), lambda b, h, n: (0, n0 + n)),
                  pl.BlockSpec((1, MEM_TN), lambda b, h, n: (0, n0 + n)),
                  pl.BlockSpec((None, MEM_LEN, MEM_TN), lambda b, h, n: (b, 0, n)),
                  pl.BlockSpec((None, MEM_LEN, MEM_TN), lambda b, h, n: (b, 0, nv + n))],
        out_specs=pl.BlockSpec((None, HALF, MEM_TN), lambda b, h, n: (b, h, n)),
        out_shape=jax.ShapeDtypeStruct((BATCH, SEQ, MEM_DIM), jnp.bfloat16),
        compiler_params=_params(("parallel", "parallel", "arbitrary")),
        name="mem_branch",
    )(xb, w_in, b_in, kv, kv)


ATT_UNROLL = 16
MIX_ROWS = 256


def _dilattn_kernel(slopes_ref,
                    q0_ref, q1_ref, q2_ref, k0_ref, k1_ref, k2_ref, v0_ref, v1_ref, v2_ref,
                    o_ref, o_sc, l_sc):
    h = pl.program_id(1)
    blk = ATT_BLOCK
    log2e = math.log2(math.e)
    scale = HEAD_DIM ** -0.5 * log2e
    qi = lax.broadcasted_iota(jnp.int32, (blk, 2 * blk), 0) + blk
    kj = lax.broadcasted_iota(jnp.int32, (blk, 2 * blk), 1)
    jrel = qi - kj
    valid = (jrel >= 0) & (jrel <= blk)
    jrel_f = jrel.astype(jnp.float32)

    def bias_for(g):
        slope = slopes_ref[g * HEADS_PER_GROUP + h]
        d = float(DIL_PATTERNS[g][1])
        return jnp.where(valid, (-slope * d * log2e) * jrel_f, -jnp.inf)

    def put(g, row_slice, o, lse):
        o_sc[g, row_slice, :] = o
        l_sc[g, row_slice, :] = jnp.broadcast_to(lse, (blk, HEAD_DIM))

    def run_blocks(g, blocks):
        scores = [_dot_t(q, k) * scale + bias for q, k, _, bias, _ in blocks]
        stats = []
        for s in scores:
            m = jnp.max(s, axis=-1, keepdims=True)
            p = jnp.exp2(s - m)
            stats.append((m, p, jnp.sum(p, axis=-1, keepdims=True)))
        outs = [_dot(p.astype(jnp.bfloat16), blkdef[2]) / den
                for (m, p, den), blkdef in zip(stats, blocks)]
        for o, (m, p, den), blkdef in zip(outs, stats, blocks):
            put(g, blkdef[4], o, m + jnp.log2(den))

    bias0 = bias_for(0)
    prev_cols = kj < blk

    def g0_body(it, carry):
        blocks = []
        for k in range(ATT_UNROLL):
            n = it * ATT_UNROLL + k
            q0 = pl.multiple_of(n * blk, blk)
            p0 = pl.multiple_of(jnp.maximum(n - 1, 0) * blk, blk)
            ctx = lambda ref, p0=p0, q0=q0: jnp.concatenate(
                [ref[pl.ds(p0, blk), :], ref[pl.ds(q0, blk), :]], axis=0)
            bias = jnp.where(prev_cols & (n == 0), -jnp.inf, bias0)
            blocks.append((q0_ref[pl.ds(q0, blk), :], ctx(k0_ref), ctx(v0_ref), bias, pl.ds(q0, blk)))
        run_blocks(0, blocks)
        return carry

    lax.fori_loop(0, SEQ // blk // ATT_UNROLL, g0_body, 0)

    d1 = DIL_PATTERNS[1][1]
    cls1 = HALF // d1
    per_half = cls1 // blk
    bias1 = bias_for(1)

    def row1(r, n):
        return (n // per_half) * HALF + r * cls1 + (n % per_half) * blk

    nblk1 = SEQ // d1 // blk
    cls_per_trip = ATT_UNROLL // nblk1

    def g1_body(it, carry):
        blocks = []
        for c in range(cls_per_trip):
            r = it * cls_per_trip + c
            for n in range(nblk1):
                cur = pl.multiple_of(row1(r, n), blk)
                q = q1_ref[pl.ds(cur, blk), :]
                dst = pl.ds(n * blk * d1 + r, blk, stride=d1)
                if n == 0:
                    blocks.append((q, k1_ref[pl.ds(cur, blk), :], v1_ref[pl.ds(cur, blk), :],
                                   bias1[:, blk:], dst))
                else:
                    prev = pl.multiple_of(row1(r, n - 1), blk)
                    kc = jnp.concatenate([k1_ref[pl.ds(prev, blk), :], k1_ref[pl.ds(cur, blk), :]], axis=0)
                    vc = jnp.concatenate([v1_ref[pl.ds(prev, blk), :], v1_ref[pl.ds(cur, blk), :]], axis=0)
                    blocks.append((q, kc, vc, bias1, dst))
        run_blocks(1, blocks)
        return carry

    lax.fori_loop(0, d1 // cls_per_trip, g1_body, 0)

    d2 = DIL_PATTERNS[2][1]
    cls2 = HALF // d2
    bias2 = bias_for(2)

    def g2_body(it, carry):
        blocks = []
        for k in range(ATT_UNROLL):
            r = it * ATT_UNROLL + k
            a = pl.multiple_of(r * cls2, cls2)
            b = pl.multiple_of(HALF + r * cls2, cls2)
            cat = lambda ref, a=a, b=b: jnp.concatenate(
                [ref[pl.ds(a, cls2), :], ref[pl.ds(b, cls2), :]], axis=0)
            blocks.append((cat(q2_ref), cat(k2_ref), cat(v2_ref), bias2[:, blk:],
                           pl.ds(r, blk, stride=d2)))
        run_blocks(2, blocks)
        return carry

    lax.fori_loop(0, d2 // ATT_UNROLL, g2_body, 0)

    for t in range(SEQ // MIX_ROWS):
        sl = pl.ds(t * MIX_ROWS, MIX_ROWS)
        l0, l1, l2 = l_sc[0, sl, :], l_sc[1, sl, :], l_sc[2, sl, :]
        m = jnp.maximum(jnp.maximum(l0, l1), l2)
        e0, e1, e2 = jnp.exp2(l0 - m), jnp.exp2(l1 - m), jnp.exp2(l2 - m)
        mix = (e0 * o_sc[0, sl, :] + e1 * o_sc[1, sl, :] + e2 * o_sc[2, sl, :]) / (e0 + e1 + e2)
        o_ref[sl, :] = mix.astype(o_ref.dtype)


def _dil_branch(qkv, slopes):
    nq = DIL_DIM // HEAD_DIM

    def spec(section, g):
        return pl.BlockSpec((None, SEQ, HEAD_DIM),
                            lambda b, h, s_ref, o=section * nq + g * HEADS_PER_GROUP: (b, 0, o + h))

    grid_spec = pltpu.PrefetchScalarGridSpec(
        num_scalar_prefetch=1,
        grid=(BATCH, HEADS_PER_GROUP),
        in_specs=[spec(sec, g) for sec in range(3) for g in range(N_DIL_GROUPS)],
        out_specs=pl.BlockSpec((None, SEQ, HEAD_DIM), lambda b, h, s_ref: (b, 0, h)),
        scratch_shapes=[pltpu.VMEM((N_DIL_GROUPS, SEQ, HEAD_DIM), jnp.float32),
                        pltpu.VMEM((N_DIL_GROUPS, SEQ, HEAD_DIM), jnp.float32)],
    )
    return pl.pallas_call(
        _dilattn_kernel,
        grid_spec=grid_spec,
        out_shape=jax.ShapeDtypeStruct((BATCH, SEQ, DIL_OUT_DIM), jnp.bfloat16),
        compiler_params=_params(("parallel", "arbitrary")),
        name="dil_attn",
    )(slopes, *([qkv] * 9))


MERGE_TM = 1024
MERGE_TN = 512


def _merge_kernel(x_ref, wg0_ref, wg1_ref, wg2_ref, bg0_ref, bg1_ref, bg2_ref,
                  sc_ref, od_ref, om_ref, wout_ref, o_ref):
    r1, r2 = CONV_DIM, CONV_DIM + DIL_OUT_DIM
    x = x_ref[...]
    g0 = jax.nn.sigmoid(_dot(x, wg0_ref[...]) + bg0_ref[...])
    acc = g0 * _dot(sc_ref[...], wout_ref[0:r1, :])
    g1 = jax.nn.sigmoid(_dot(x, wg1_ref[...]) + bg1_ref[...])
    acc = acc + g1 * _dot(od_ref[...], wout_ref[r1:r2, :])
    g2 = jax.nn.sigmoid(_dot(x, wg2_ref[...]) + bg2_ref[...])
    acc = acc + g2 * _dot(om_ref[...], wout_ref[r2:r2 + MEM_DIM, :])
    o_ref[...] = acc.astype(o_ref.dtype)


def _merge(x2, w_in, b_in, s_conv, o_dil, o_mem, w_out):
    nb = lambda br: (OFF_GATE + br * D_MODEL) // MERGE_TN
    gspec = lambda br: pl.BlockSpec((D_MODEL, MERGE_TN), lambda i, n, o=nb(br): (0, o + n))
    bspec = lambda br: pl.BlockSpec((1, MERGE_TN), lambda i, n, o=nb(br): (0, o + n))
    act = lambda width: pl.BlockSpec((MERGE_TM, width), lambda i, n: (i, 0))
    wout = lambda width: pl.BlockSpec((width, MERGE_TN), lambda i, n: (0, n))
    return pl.pallas_call(
        _merge_kernel,
        grid=(TOKENS // MERGE_TM, D_MODEL // MERGE_TN),
        in_specs=[act(D_MODEL), gspec(0), gspec(1), gspec(2), bspec(0), bspec(1), bspec(2),
                  act(CONV_DIM), act(DIL_OUT_DIM), act(MEM_DIM),
                  wout(CONV_DIM + DIL_OUT_DIM + MEM_DIM)],
        out_specs=pl.BlockSpec((MERGE_TM, MERGE_TN), lambda i, n: (i, n)),
        out_shape=jax.ShapeDtypeStruct((TOKENS, D_MODEL), jnp.bfloat16),
        compiler_params=_params(("parallel", "arbitrary")),
        name="gated_merge",
    )(x2, w_in, w_in, w_in, b_in, b_in, b_in, s_conv, o_dil, o_mem, w_out)


OPROJ_TM = 512


OPROJ_PARTS = 4


def _oproj_kernel(x_ref, m_ref, wo_ref, g_ref, b_ref, wr_ref, br_ref, x1_ref, logit_ref):
    rows = OPROJ_TM // OPROJ_PARTS
    parts = [pl.ds(i * rows, rows) for i in range(OPROJ_PARTS)]
    proj = [_dot(m_ref[p, :], wo_ref[...]) for p in parts]
    for p, y in zip(parts, proj):
        x1 = _layer_norm(ALPHA * x_ref[p, :] + y, g_ref[...], b_ref[...])
        x1_ref[p, :] = x1
        logit_ref[p, :] = _dot(x1, wr_ref[...]) + br_ref[...]


def _oproj(x2, merged, w_o, g, b, w_route, b_route):
    row = lambda width: pl.BlockSpec((OPROJ_TM, width), lambda i: (i, 0))
    full = lambda r, c: pl.BlockSpec((r, c), lambda i: (0, 0))
    return pl.pallas_call(
        _oproj_kernel,
        grid=(TOKENS // OPROJ_TM,),
        in_specs=[row(D_MODEL), row(D_MODEL), full(D_MODEL, D_MODEL), full(1, D_MODEL), full(1, D_MODEL),
                  full(D_MODEL, LANES), full(1, LANES)],
        out_specs=[row(D_MODEL), row(LANES)],
        out_shape=[jax.ShapeDtypeStruct((TOKENS, D_MODEL), jnp.float32),
                   jax.ShapeDtypeStruct((TOKENS, LANES), jnp.float32)],
        compiler_params=_params(("parallel",)),
        name="oproj_ln1",
    )(x2, merged, w_o, g, b, w_route, b_route)


def _route_kernel(logit_ref, rec_ref, rect_ref, cnt_ref, carry_sc):
    @pl.when(pl.program_id(0) == 0)
    def _():
        carry_sc[...] = jnp.zeros_like(carry_sc)

    tm = ROUTE_TILE
    zt = jnp.transpose(logit_ref[...])
    row = lax.broadcasted_iota(jnp.int32, (SUBLANES, tm), 0)
    neg = -jnp.inf
    colmax = lambda a: jnp.max(a, axis=0, keepdims=True)
    first = lambda hit: jnp.min(jnp.where(hit, row, SUBLANES), axis=0, keepdims=True)

    glog = jnp.where(row < N_GROUPS, zt[0:SUBLANES], neg)
    gmax = colmax(glog)
    gsel = first(glog == gmax)
    gw = 1.0 / jnp.sum(jnp.exp(glog - gmax), axis=0, keepdims=True)

    elog = zt[ROUTER_LANE0:ROUTER_LANE0 + EXPERTS_PER_GROUP]
    for g in range(1, N_GROUPS):
        lo = ROUTER_LANE0 + g * EXPERTS_PER_GROUP
        elog = jnp.where(gsel == g, zt[lo:lo + EXPERTS_PER_GROUP], elog)
    top1 = colmax(elog)
    i1 = first(elog == top1)
    rest = jnp.where(row == i1, neg, elog)
    top2 = colmax(rest)
    i2 = first(rest == top2)
    t = jnp.exp(top2 - top1)
    w1 = gw / (1.0 + t)
    w2 = w1 * t

    hit = (row == i1) | (row == i2)
    onehot = jnp.concatenate([jnp.where(hit & (gsel == g), 1.0, 0.0) for g in range(N_GROUPS)], axis=0)
    chunks = tm // LANES
    stacked = jnp.concatenate([onehot[:, c * LANES:(c + 1) * LANES] for c in range(chunks)], axis=0)
    ri = lax.broadcasted_iota(jnp.int32, (LANES, LANES), 0)
    ci = lax.broadcasted_iota(jnp.int32, (LANES, LANES), 1)
    upper = jnp.where(ri <= ci, 1.0, 0.0).astype(jnp.bfloat16)
    within = _dot(stacked.astype(jnp.bfloat16), upper)
    run = carry_sc[:, 0:1]
    pieces = []
    for c in range(chunks):
        cum = within[c * N_EXPERTS:(c + 1) * N_EXPERTS, :] + run
        pieces.append(cum)
        run = cum[:, LANES - 1:LANES]
    before = jnp.concatenate(pieces, axis=1) - 1.0
    carry_sc[...] = jnp.broadcast_to(run, carry_sc.shape)
    cnt_ref[...] = jnp.broadcast_to(run, cnt_ref.shape)

    rank_a = jnp.zeros((1, tm), jnp.float32)
    rank_b = jnp.zeros((1, tm), jnp.float32)
    for g in range(N_GROUPS):
        b8 = before[g * EXPERTS_PER_GROUP:(g + 1) * EXPERTS_PER_GROUP, :]
        mine = gsel == g
        rank_a = rank_a + jnp.sum(jnp.where(mine & (row == i1), b8, 0.0), axis=0, keepdims=True)
        rank_b = rank_b + jnp.sum(jnp.where(mine & (row == i2), b8, 0.0), axis=0, keepdims=True)

    base = (gsel * EXPERTS_PER_GROUP).astype(jnp.float32)
    vals = [base + i1.astype(jnp.float32), base + i2.astype(jnp.float32), rank_a, rank_b, w1, w2]
    rect = jnp.zeros((SUBLANES, tm), jnp.float32)
    for k, val in enumerate(vals):
        rect = jnp.where(row == k, val, rect)
    rect_ref[...] = rect
    padded = jnp.concatenate([rect, jnp.zeros((LANES - SUBLANES, tm), jnp.float32)], axis=0)
    rec_ref[...] = jnp.transpose(padded)


def _route(logits):
    return pl.pallas_call(
        _route_kernel,
        grid=(TOKENS // ROUTE_TILE,),
        in_specs=[pl.BlockSpec((ROUTE_TILE, LANES), lambda i: (i, 0))],
        out_specs=[pl.BlockSpec((ROUTE_TILE, LANES), lambda i: (i, 0)),
                   pl.BlockSpec((SUBLANES, ROUTE_TILE), lambda i: (0, i)),
                   pl.BlockSpec((N_EXPERTS, LANES), lambda i: (0, 0))],
        out_shape=[jax.ShapeDtypeStruct((TOKENS, LANES), jnp.float32),
                   jax.ShapeDtypeStruct((SUBLANES, TOKENS), jnp.float32),
                   jax.ShapeDtypeStruct((N_EXPERTS, LANES), jnp.float32)],
        scratch_shapes=[pltpu.VMEM((N_EXPERTS, LANES), jnp.float32)],
        compiler_params=_params(("arbitrary",)),
        name="route",
    )(logits)


def _slots_kernel(start_ref, rect_ref, pos_ref):
    r = rect_ref[...]
    row = lax.broadcasted_iota(jnp.int32, r.shape, 0)

    def pos_of(e, rank):
        e = e.astype(jnp.int32)
        start = jnp.zeros(e.shape, jnp.int32)
        for k in range(N_EXPERTS):
            start = jnp.where(e == k, start_ref[k], start)
        return start + rank.astype(jnp.int32)

    pos_a = pos_of(r[0:1], r[2:3])
    pos_b = pos_of(r[1:2], r[3:4])
    pos_ref[...] = jnp.where(row == 0, pos_a, jnp.where(row == 1, pos_b, 0))


def _slots(starts, rect):
    grid_spec = pltpu.PrefetchScalarGridSpec(
        num_scalar_prefetch=1,
        grid=(TOKENS // ROUTE_TILE,),
        in_specs=[pl.BlockSpec((SUBLANES, ROUTE_TILE), lambda i, st: (0, i))],
        out_specs=pl.BlockSpec((SUBLANES, ROUTE_TILE), lambda i, st: (0, i)),
    )
    return pl.pallas_call(
        _slots_kernel,
        grid_spec=grid_spec,
        out_shape=jax.ShapeDtypeStruct((SUBLANES, TOKENS), jnp.int32),
        compiler_params=_params(("parallel",)),
        name="slots",
    )(starts, rect)


DISPATCH_TILE = 512
SORTED_ROWS = N_EXPERT_TILES * EXPERT_TILE


def _dispatch_kernel(pad_start_ref, pad_n_ref, nvalid_ref, pos_ref, x_ref, xs_hbm, zero_sc, stage_sc, sem,
                     row_sem):
    groups = DISPATCH_TILE // SUBLANES

    @pl.when(pl.program_id(0) == 0)
    def _():
        zero_sc[...] = jnp.zeros_like(zero_sc)

        def tile_copy(t):
            return pltpu.make_async_copy(zero_sc, xs_hbm.at[pl.ds(pl.multiple_of(t * EXPERT_TILE, EXPERT_TILE),
                                                                  EXPERT_TILE)], sem.at[2])

        def start_tile(t, c):
            tile_copy(t).start()
            return c

        def wait_tile(t, c):
            tile_copy(t).wait()
            return c
        lax.fori_loop(nvalid_ref[0], N_EXPERT_TILES, start_tile, 0)
        lax.fori_loop(nvalid_ref[0], N_EXPERT_TILES, wait_tile, 0)

        def pad_copies(e, act):
            n = pad_n_ref[e]
            cur = pad_start_ref[e]
            for bit in range((EXPERT_TILE - 1).bit_length()):
                size = 1 << bit
                has = lax.bitwise_and(n, size)

                @pl.when(has != 0)
                def _(cur=cur, size=size):
                    at = pl.multiple_of(cur, math.gcd(size, EXPERT_TILE))
                    act(pltpu.make_async_copy(zero_sc.at[pl.ds(0, size)], xs_hbm.at[pl.ds(at, size)],
                                              sem.at[1]))
                cur = cur + has

        def start_pads(e, c):
            pad_copies(e, lambda cp: cp.start())
            return c

        def wait_pads(e, c):
            pad_copies(e, lambda cp: cp.wait())
            return c
        lax.fori_loop(0, N_EXPERTS, start_pads, 0)
        lax.fori_loop(0, N_EXPERTS, wait_pads, 0)

    j = pl.program_id(0)
    slot = lax.rem(j, 2)
    stage_sc[slot] = x_ref[...].reshape(groups, SUBLANES, D_MODEL)

    def body(gi, c):
        for s in range(SUBLANES):
            for k in range(2):
                p = pos_ref[0, 0, k * DISPATCH_TILE + gi * SUBLANES + s]
                pltpu.make_async_copy(stage_sc.at[slot, gi, pl.ds(s, 1)], xs_hbm.at[pl.ds(p, 1)],
                                      row_sem.at[slot]).start(priority=k)
        return c
    lax.fori_loop(0, groups, body, 0)

    def wait_slot(sl):
        for _ in range(2 * groups):
            pltpu.make_async_copy(stage_sc.at[sl, 0], xs_hbm.at[pl.ds(0, SUBLANES)], row_sem.at[sl]).wait()

    @pl.when(j > 0)
    def _():
        wait_slot(1 - slot)

    @pl.when(j == pl.num_programs(0) - 1)
    def _():
        wait_slot(slot)


def _dispatch(pad_start, pad_n, n_valid_tiles, pos3, x1):
    nt = TOKENS // DISPATCH_TILE
    groups = DISPATCH_TILE // SUBLANES
    grid_spec = pltpu.PrefetchScalarGridSpec(
        num_scalar_prefetch=3,
        grid=(nt,),
        in_specs=[pl.BlockSpec((1, 1, 2 * DISPATCH_TILE), lambda j, *_: (j, 0, 0), memory_space=pltpu.SMEM),
                  pl.BlockSpec((DISPATCH_TILE, D_MODEL), lambda j, *_: (j, 0))],
        out_specs=pl.BlockSpec(memory_space=pl.ANY),
        scratch_shapes=[pltpu.VMEM((EXPERT_TILE, D_MODEL), jnp.float32),
                        pltpu.VMEM((2, groups, SUBLANES, D_MODEL), jnp.float32),
                        pltpu.SemaphoreType.DMA((3,)),
                        pltpu.SemaphoreType.DMA((2,))],
    )
    return pl.pallas_call(
        _dispatch_kernel,
        grid_spec=grid_spec,
        out_shape=jax.ShapeDtypeStruct((SORTED_ROWS, D_MODEL), jnp.float32),
        compiler_params=_params(("arbitrary",)),
        name="dispatch",
    )(pad_start, pad_n, n_valid_tiles, pos3, x1)


def _ffn_kernel(texp_ref, tvalid_ref, tslot_ref, tfirst_ref, tnext_ref, nvalid_ref,
                xs_ref, wg_hbm, wu_hbm, wd_hbm, y_ref, wg_buf, wu_buf, wd_buf, sem):
    j = pl.program_id(0)
    valid = tvalid_ref[j] == 1
    slot = tslot_ref[j]

    def weight_copies(e, s):
        return (pltpu.make_async_copy(wg_hbm.at[e], wg_buf.at[s], sem.at[s, 0]),
                pltpu.make_async_copy(wu_hbm.at[e], wu_buf.at[s], sem.at[s, 1]),
                pltpu.make_async_copy(wd_hbm.at[e], wd_buf.at[s], sem.at[s, 2]))

    @pl.when(valid & (tfirst_ref[j] == 1))
    def _():
        @pl.when(j == 0)
        def _():
            for cp in weight_copies(texp_ref[j], slot):
                cp.start()

        for cp in weight_copies(texp_ref[j], slot):
            cp.wait()

        @pl.when(tnext_ref[j] >= 0)
        def _():
            for cp in weight_copies(tnext_ref[j], 1 - slot):
                cp.start(priority=1)

    @pl.when(valid)
    def _():
        xt = xs_ref[...]
        a = _dot(xt, wg_buf[slot])
        u = _dot(xt, wu_buf[slot])
        hid = (a * jax.nn.sigmoid(a)) * u
        y_ref[...] = _dot(hid, wd_buf[slot])

    @pl.when(jnp.logical_not(valid))
    def _():
        y_ref[...] = jnp.zeros_like(y_ref)


def _expert_ffn(tables, xs, w_gate, w_up, w_down):
    nt = N_EXPERT_TILES
    xmap = lambda j, te, tv, ts, tf, tn, nv: (jnp.minimum(j, nv[0] - 1), 0)
    grid_spec = pltpu.PrefetchScalarGridSpec(
        num_scalar_prefetch=6,
        grid=(nt,),
        in_specs=[pl.BlockSpec((EXPERT_TILE, D_MODEL), xmap),
                  pl.BlockSpec(memory_space=pl.ANY),
                  pl.BlockSpec(memory_space=pl.ANY),
                  pl.BlockSpec(memory_space=pl.ANY)],
        out_specs=pl.BlockSpec((EXPERT_TILE, D_MODEL), lambda j, *_: (j, 0)),
        scratch_shapes=[pltpu.VMEM((2, D_MODEL, D_EXPERT), jnp.float32),
                        pltpu.VMEM((2, D_MODEL, D_EXPERT), jnp.float32),
                        pltpu.VMEM((2, D_EXPERT, D_MODEL), jnp.float32),
                        pltpu.SemaphoreType.DMA((2, 3))],
    )
    return pl.pallas_call(
        _ffn_kernel,
        grid_spec=grid_spec,
        out_shape=jax.ShapeDtypeStruct((nt * EXPERT_TILE, D_MODEL), jnp.float32),
        compiler_params=_params(("arbitrary",)),
        name="expert_ffn",
    )(*tables, xs, w_gate, w_up, w_down)


def _combine_kernel(pos_cur_ref, pos_next_ref, r_ref, x1_ref, y_hbm, g_ref, b_ref, o_ref, ybuf, sem):
    j = pl.program_id(0)
    nt = pl.num_programs(0)
    slot = lax.rem(j, 2)
    groups = COMBINE_TILE // SUBLANES

    def gather(pos_ref, dst_slot):
        def body(gi, c):
            for s in range(SUBLANES):
                for k in range(2):
                    p = pos_ref[0, 0, k * COMBINE_TILE + gi * SUBLANES + s]
                    pltpu.make_async_copy(y_hbm.at[pl.ds(p, 1)], ybuf.at[dst_slot, k, gi, pl.ds(s, 1)],
                                          sem.at[dst_slot]).start(priority=k)
            return c
        lax.fori_loop(0, groups, body, 0)

    @pl.when(j == 0)
    def _():
        gather(pos_cur_ref, 0)

    @pl.when(j + 1 < nt)
    def _():
        gather(pos_next_ref, 1 - slot)

    for _ in range(2 * groups):
        pltpu.make_async_copy(y_hbm.at[pl.ds(0, SUBLANES)], ybuf.at[slot, 0, 0], sem.at[slot]).wait()
    r = r_ref[...]
    ya = ybuf[slot, 0].reshape(COMBINE_TILE, D_MODEL)
    yb = ybuf[slot, 1].reshape(COMBINE_TILE, D_MODEL)
    hres = ALPHA * x1_ref[...] + r[:, 4:5] * ya + r[:, 5:6] * yb
    o_ref[...] = _layer_norm(hres, g_ref[...], b_ref[...])


def _combine(pos3, routing, x1, y_sorted, g, b):
    nt = TOKENS // COMBINE_TILE
    groups = COMBINE_TILE // SUBLANES
    return pl.pallas_call(
        _combine_kernel,
        grid=(nt,),
        in_specs=[pl.BlockSpec((1, 1, 2 * COMBINE_TILE), lambda j: (j, 0, 0), memory_space=pltpu.SMEM),
                  pl.BlockSpec((1, 1, 2 * COMBINE_TILE), lambda j: (jnp.minimum(j + 1, nt - 1), 0, 0),
                               memory_space=pltpu.SMEM),
                  pl.BlockSpec((COMBINE_TILE, LANES), lambda j: (j, 0)),
                  pl.BlockSpec((COMBINE_TILE, D_MODEL), lambda j: (j, 0)),
                  pl.BlockSpec(memory_space=pl.ANY),
                  pl.BlockSpec((1, D_MODEL), lambda j: (0, 0)),
                  pl.BlockSpec((1, D_MODEL), lambda j: (0, 0))],
        out_specs=pl.BlockSpec((COMBINE_TILE, D_MODEL), lambda j: (j, 0)),
        out_shape=jax.ShapeDtypeStruct((TOKENS, D_MODEL), jnp.float32),
        scratch_shapes=[pltpu.VMEM((2, 2, groups, SUBLANES, D_MODEL), jnp.float32),
                        pltpu.SemaphoreType.DMA((2,))],
        compiler_params=_params(("arbitrary",)),
        name="combine_ln2",
    )(pos3, pos3, routing, x1, y_sorted, g, b)


def _alibi_slopes():
    n = N_DIL_GROUPS * HEADS_PER_GROUP
    return jnp.asarray(2.0 ** (-ALIBI_MAX * np.arange(1, n + 1, dtype=np.float32) / n), jnp.float32)


def kernel(x, mem, ln_mem_g, ln_mem_b, w_in, b_in, w_conv, w_conv_out, w_dil_out, w_mem_kv, w_mem_out, w_o, ln1_g, ln1_b, w_group, b_group, w_router, b_router, w_gate, w_up, w_down, ln2_g, ln2_b):
    assert x.shape == (BATCH, SEQ, D_MODEL) and w_in.shape == (1, D_MODEL, IN_DIM)
    bf16 = jnp.bfloat16
    row = lambda v: v.reshape(1, -1)
    w_in2 = w_in[0].astype(bf16)
    b_in2 = b_in

    kv = _memkv(mem, row(ln_mem_g), row(ln_mem_b), w_mem_kv[0])
    s_conv, xb = _conv_branch(x, w_in2, b_in2, w_conv[0])
    qkv = _qkv_proj(xb, w_in2, b_in2)
    o_mem = _mem_branch(xb, w_in2, b_in2, kv)
    o_dil = _dil_branch(qkv, _alibi_slopes())

    x2 = x.reshape(TOKENS, D_MODEL)
    merged = _merge(xb.reshape(TOKENS, D_MODEL), w_in2, b_in2,
                    s_conv.reshape(TOKENS, CONV_DIM), o_dil.reshape(TOKENS, DIL_OUT_DIM),
                    o_mem.reshape(TOKENS, MEM_DIM),
                    jnp.concatenate([w_conv_out[0], w_dil_out[0], w_mem_out[0]], axis=0).astype(bf16))

    gap = ROUTER_LANE0 - N_GROUPS
    tail = LANES - ROUTER_LANE0 - N_EXPERTS
    w_route = jnp.concatenate(
        [w_group[0], jnp.zeros((D_MODEL, gap), jnp.float32),
         jnp.transpose(w_router[0], (1, 0, 2)).reshape(D_MODEL, N_EXPERTS),
         jnp.zeros((D_MODEL, tail), jnp.float32)], axis=1)
    b_route = jnp.concatenate(
        [b_group[0], jnp.zeros((gap,), jnp.float32), b_router[0].reshape(N_EXPERTS),
         jnp.zeros((tail,), jnp.float32)]).reshape(1, LANES)
    x1, logits = _oproj(x2, merged, w_o[0].astype(bf16), ln1_g, ln1_b, w_route, b_route)

    routing, routing_t, counts_b = _route(logits)

    i32 = jnp.int32
    counts = counts_b[:, 0].astype(i32)
    padded = ((counts + EXPERT_TILE - 1) // EXPERT_TILE) * EXPERT_TILE
    ends = jnp.cumsum(padded)
    starts = ends - padded
    pos = _slots(starts.astype(i32), routing_t)[0:2]

    def pos_tiles(tile):
        return jnp.transpose(pos.reshape(2, TOKENS // tile, tile), (1, 0, 2)).reshape(TOKENS // tile, 1, 2 * tile)

    tile_start = jnp.arange(N_EXPERT_TILES, dtype=i32) * EXPERT_TILE
    tile_expert = jnp.minimum(jnp.sum((ends[None, :] <= tile_start[:, None]).astype(i32), axis=1),
                              N_EXPERTS - 1)
    tile_valid = tile_start < ends[-1]
    prev_expert = jnp.concatenate([jnp.full((1,), -1, i32), tile_expert[:-1]])
    tile_first = tile_valid & (tile_expert != prev_expert)
    tile_slot = (jnp.cumsum(tile_first.astype(i32)) - 1) & 1
    big = N_EXPERTS
    idx = jnp.where(counts > 0, jnp.arange(N_EXPERTS, dtype=i32), big)
    later = jnp.concatenate([lax.cummin(idx[::-1])[::-1][1:], jnp.full((1,), big, i32)])
    next_used = jnp.where(later == big, -1, later)
    n_valid_tiles = (ends[-1:] // EXPERT_TILE).astype(i32)
    tables = (tile_expert, tile_valid.astype(i32), tile_slot, tile_first.astype(i32),
              next_used[tile_expert], n_valid_tiles)

    xs = _dispatch(starts + counts, padded - counts, n_valid_tiles,
                   pos_tiles(DISPATCH_TILE), x1)
    y_sorted = _expert_ffn(tables, xs,
                           w_gate.reshape(N_EXPERTS, D_MODEL, D_EXPERT),
                           w_up.reshape(N_EXPERTS, D_MODEL, D_EXPERT),
                           w_down.reshape(N_EXPERTS, D_EXPERT, D_MODEL))
    out = _combine(pos_tiles(COMBINE_TILE), routing, x1, y_sorted,
                   ln2_g, ln2_b)
    return out.reshape(BATCH, SEQ, D_MODEL)
```

```python
import math

import numpy as np
import jax
import jax.numpy as jnp
from jax import lax
from jax.experimental import pallas as pl
from jax.experimental.pallas import tpu as pltpu

D_MODEL = 2048
BATCH = 8
SEQ = 2048
TOKENS = BATCH * SEQ
CONV_DIM = 1024
CONV_WIDTH = 3
DIL_PATTERNS = ((128, 1), (512, 4), (2048, 16))
N_DIL_GROUPS = 3
HEADS_PER_GROUP = 4
HEAD_DIM = 128
DIL_DIM = N_DIL_GROUPS * HEADS_PER_GROUP * HEAD_DIM
DIL_OUT_DIM = HEADS_PER_GROUP * HEAD_DIM
ATT_BLOCK = 128
ALIBI_MAX = 8.0
MEM_LEN = 256
MEM_HEADS = 4
MEM_HEAD_DIM = 256
MEM_DIM = MEM_HEADS * MEM_HEAD_DIM
N_BRANCHES = 3
IN_DIM = 3 * CONV_DIM + 3 * DIL_DIM + MEM_DIM + N_BRANCHES * D_MODEL
N_GROUPS = 4
EXPERTS_PER_GROUP = 8
N_EXPERTS = N_GROUPS * EXPERTS_PER_GROUP
D_EXPERT = 512
ALPHA = 2.0 ** 0.25
LN_EPS = 1e-5

OFF_CB = 0
OFF_CC = CONV_DIM
OFF_CH = 2 * CONV_DIM
OFF_Q = 3 * CONV_DIM
OFF_MQ = OFF_Q + 3 * DIL_DIM
OFF_GATE = OFF_MQ + MEM_DIM

LANES = 128
SUBLANES = 8
HALF = SEQ // 2
VMEM_LIMIT = 56 * 1024 * 1024

ROUTE_TILE = 1024
EXPERT_TILE = 384
N_EXPERT_TILES = 2 * TOKENS // EXPERT_TILE + N_EXPERTS
COMBINE_TILE = 512
ROUTER_LANE0 = 8


def _params(sem, limit=VMEM_LIMIT):
    return pltpu.CompilerParams(dimension_semantics=sem, vmem_limit_bytes=limit)


def _layer_norm(x, g, b):
    mu = jnp.mean(x, axis=-1, keepdims=True)
    xc = x - mu
    var = jnp.mean(xc * xc, axis=-1, keepdims=True)
    return xc * lax.rsqrt(var + LN_EPS) * g + b


def _dot(a, b):
    return jnp.dot(a, b, preferred_element_type=jnp.float32)


def _dot_t(a, b):
    return lax.dot_general(a, b, (((1,), (1,)), ((), ())), preferred_element_type=jnp.float32)


MEMKV_BATCHES = 2


def _memkv_kernel(mem_ref, g_ref, b_ref, w_ref, kv_ref):
    rows = MEMKV_BATCHES * MEM_LEN
    y = _layer_norm(mem_ref[...].reshape(rows, D_MODEL), g_ref[...], b_ref[...])
    kv = _dot(y, w_ref[...])
    kv_ref[...] = kv.astype(kv_ref.dtype).reshape(MEMKV_BATCHES, MEM_LEN, 2 * MEM_DIM)


def _memkv(mem, g, b, w):
    return pl.pallas_call(
        _memkv_kernel,
        grid=(BATCH // MEMKV_BATCHES,),
        in_specs=[pl.BlockSpec((MEMKV_BATCHES, MEM_LEN, D_MODEL), lambda i: (i, 0, 0)),
                  pl.BlockSpec((1, D_MODEL), lambda i: (0, 0)),
                  pl.BlockSpec((1, D_MODEL), lambda i: (0, 0)),
                  pl.BlockSpec((D_MODEL, 2 * MEM_DIM), lambda i: (0, 0))],
        out_specs=pl.BlockSpec((MEMKV_BATCHES, MEM_LEN, 2 * MEM_DIM), lambda i: (i, 0, 0)),
        out_shape=jax.ShapeDtypeStruct((BATCH, MEM_LEN, 2 * MEM_DIM), jnp.bfloat16),
        compiler_params=_params(("parallel",)),
        name="mem_kv",
    )(mem, g, b, w)


CONV_TC = 512


def _conv_kernel(x_ref, wb_ref, wc_ref, wh_ref, bb_ref, bc_ref, bh_ref, wconv_ref, s_ref, xb_ref,
                 u_sc, carry_sc):
    half = pl.program_id(1)
    c = pl.program_id(2)

    @pl.when(c == 0)
    def _():
        xb_ref[...] = x_ref[...].astype(xb_ref.dtype)

    x = xb_ref[...]
    cc = _dot(x, wc_ref[...]) + bc_ref[...]
    ch = _dot(x, wh_ref[...]) + bh_ref[...]
    u = cc * ch
    pre = SUBLANES
    u_sc[0:pre, :] = jnp.where(half == 0, 0.0, carry_sc[c])
    u_sc[pre:pre + HALF, :] = u
    carry_sc[c] = u[HALF - pre:HALF, :]
    wconv = wconv_ref[...]
    y = wconv[CONV_WIDTH - 1:CONV_WIDTH, :] * u
    for back in range(1, CONV_WIDTH):
        tap = CONV_WIDTH - 1 - back
        y = y + wconv[tap:tap + 1, :] * u_sc[pre - back:pre - back + HALF, :]
    cb = _dot(x, wb_ref[...]) + bb_ref[...]
    s_ref[...] = (cb * y).astype(s_ref.dtype)


def _conv_branch(x, w_in, b_in, w_conv):
    nb = lambda off: off // CONV_TC
    wspec = lambda off: pl.BlockSpec((D_MODEL, CONV_TC), lambda b, h, c, o=nb(off): (0, o + c))
    bspec = lambda off: pl.BlockSpec((1, CONV_TC), lambda b, h, c, o=nb(off): (0, o + c))
    return pl.pallas_call(
        _conv_kernel,
        grid=(BATCH, 2, CONV_DIM // CONV_TC),
        in_specs=[pl.BlockSpec((None, HALF, D_MODEL), lambda b, h, c: (b, h, 0)),
                  wspec(OFF_CB), wspec(OFF_CC), wspec(OFF_CH),
                  bspec(OFF_CB), bspec(OFF_CC), bspec(OFF_CH),
                  pl.BlockSpec((CONV_WIDTH, CONV_TC), lambda b, h, c: (0, c))],
        out_specs=[pl.BlockSpec((None, HALF, CONV_TC), lambda b, h, c: (b, h, c)),
                   pl.BlockSpec((None, HALF, D_MODEL), lambda b, h, c: (b, h, 0))],
        out_shape=[jax.ShapeDtypeStruct((BATCH, SEQ, CONV_DIM), jnp.bfloat16),
                   jax.ShapeDtypeStruct((BATCH, SEQ, D_MODEL), jnp.bfloat16)],
        scratch_shapes=[pltpu.VMEM((HALF + SUBLANES, CONV_TC), jnp.float32),
                        pltpu.VMEM((CONV_DIM // CONV_TC, SUBLANES, CONV_TC), jnp.float32)],
        compiler_params=_params(("arbitrary", "arbitrary", "arbitrary")),
        name="conv_branch",
    )(x, w_in, w_in, w_in, b_in, b_in, b_in, w_conv)


QKV_TN = 512
QKV_CHUNKS = QKV_TN // LANES


def _qkv_kernel(x_ref, w_ref, b_ref, o_ref, sc_ref, sc2_ref):
    x = x_ref[...]
    for gi in (2, 1, 0):
        cols = slice(gi * QKV_TN, (gi + 1) * QKV_TN)
        acc = _dot(x, w_ref[:, cols]) + b_ref[:, cols]
        d = DIL_PATTERNS[gi][1]
        if d == 1:
            o_ref[:, cols] = acc.astype(o_ref.dtype)
            continue
        rows = HALF // d
        for c in range(QKV_CHUNKS):
            sc_ref[gi - 1, c] = acc[:, c * LANES:(c + 1) * LANES]
        if d == 16:
            q4 = HALF // 4
            for c in range(QKV_CHUNKS):
                for r in range(4):
                    sc2_ref[c, r * q4:(r + 1) * q4, :] = sc_ref[gi - 1, c, pl.ds(r, q4, stride=4), :]
            for c in range(QKV_CHUNKS):
                lo = gi * QKV_TN + c * LANES
                for r in range(d):
                    r_lo, r_hi = r % 4, r // 4
                    o_ref[r * rows:(r + 1) * rows, lo:lo + LANES] = (
                        sc2_ref[c, pl.ds(r_lo * q4 + r_hi, rows, stride=4), :].astype(o_ref.dtype))
            continue
        for c in range(QKV_CHUNKS):
            for r in range(d):
                lo = gi * QKV_TN + c * LANES
                o_ref[r * rows:(r + 1) * rows, lo:lo + LANES] = (
                    sc_ref[gi - 1, c, pl.ds(r, rows, stride=d), :].astype(o_ref.dtype))


def _qkv_proj(xb, w_in, b_in):
    n0 = OFF_Q // DIL_DIM
    return pl.pallas_call(
        _qkv_kernel,
        grid=(BATCH, 2, 3),
        in_specs=[pl.BlockSpec((None, HALF, D_MODEL), lambda b, h, n: (b, h, 0)),
                  pl.BlockSpec((D_MODEL, DIL_DIM), lambda b, h, n: (0, n0 + n)),
                  pl.BlockSpec((1, DIL_DIM), lambda b, h, n: (0, n0 + n))],
        out_specs=pl.BlockSpec((None, HALF, DIL_DIM), lambda b, h, n: (b, h, n)),
        out_shape=jax.ShapeDtypeStruct((BATCH, SEQ, 3 * DIL_DIM), jnp.bfloat16),
        scratch_shapes=[pltpu.VMEM((N_DIL_GROUPS - 1, QKV_CHUNKS, HALF, LANES), jnp.float32),
                        pltpu.VMEM((QKV_CHUNKS, HALF, LANES), jnp.float32)],
        compiler_params=_params(("parallel", "parallel", "arbitrary")),
        name="qkv_proj",
    )(xb, w_in, b_in)


MEM_HEADS_PER_STEP = 2
MEM_TN = MEM_HEADS_PER_STEP * MEM_HEAD_DIM


def _memattn_kernel(x_ref, w_ref, b_ref, mk_ref, mv_ref, o_ref):
    mq_all = (_dot(x_ref[...], w_ref[...]) + b_ref[...]).astype(jnp.bfloat16)
    for hh in range(MEM_HEADS_PER_STEP):
        cols = slice(hh * MEM_HEAD_DIM, (hh + 1) * MEM_HEAD_DIM)
        s = _dot_t(mq_all[:, cols], mk_ref[:, cols]) * (MEM_HEAD_DIM ** -0.5 * math.log2(math.e))
        m = jnp.max(s, axis=-1, keepdims=True)
        p = jnp.exp2(s - m)
        den = jnp.sum(p, axis=-1, keepdims=True)
        o = _dot(p.astype(jnp.bfloat16), mv_ref[:, cols]) / den
        o_ref[:, cols] = o.astype(o_ref.dtype)


def _mem_branch(xb, w_in, b_in, kv):
    n0 = OFF_MQ // MEM_TN
    nv = MEM_DIM // MEM_TN
    return pl.pallas_call(
        _memattn_kernel,
        grid=(BATCH, 2, MEM_DIM // MEM_TN),
        in_specs=[pl.BlockSpec((None, HALF, D_MODEL), lambda b, h, n: (b, h, 0)),
                  pl.BlockSpec((D_MODEL, MEM_TN), lambda b, h, n: (0, n0 + n)),
                  pl.BlockSpec((1, MEM_TN), lambda b, h, n: (0, n0 + n)),
                  pl.BlockSpec((None, MEM_LEN, MEM_TN), lambda b, h, n: (b, 0, n)),
                  pl.BlockSpec((None, MEM_LEN, MEM_TN), lambda b, h, n: (b, 0, nv + n))],
        out_specs=pl.BlockSpec((None, HALF, MEM_TN), lambda b, h, n: (b, h, n)),
        out_shape=jax.ShapeDtypeStruct((BATCH, SEQ, MEM_DIM), jnp.bfloat16),
        compiler_params=_params(("parallel", "parallel", "arbitrary")),
        name="mem_branch",
    )(xb, w_in, b_in, kv, kv)


ATT_UNROLL = 16
MIX_ROWS = 256


def _dilattn_kernel(slopes_ref,
                    q0_ref, q1_ref, q2_ref, k0_ref, k1_ref, k2_ref, v0_ref, v1_ref, v2_ref,
                    o_ref, o_sc, l_sc):
    h = pl.program_id(1)
    blk = ATT_BLOCK
    log2e = math.log2(math.e)
    scale = HEAD_DIM ** -0.5 * log2e
    qi = lax.broadcasted_iota(jnp.int32, (blk, 2 * blk), 0) + blk
    kj = lax.broadcasted_iota(jnp.int32, (blk, 2 * blk), 1)
    jrel = qi - kj
    valid = (jrel >= 0) & (jrel <= blk)
    jrel_f = jrel.astype(jnp.float32)

    def bias_for(g):
        slope = slopes_ref[g * HEADS_PER_GROUP + h]
        d = float(DIL_PATTERNS[g][1])
        return jnp.where(valid, (-slope * d * log2e) * jrel_f, -jnp.inf)

    def put(g, row_slice, o, lse):
        o_sc[g, row_slice, :] = o
        l_sc[g, row_slice, :] = jnp.broadcast_to(lse, (blk, HEAD_DIM))

    def run_blocks(g, blocks):
        scores = [_dot_t(q, k) * scale + bias for q, k, _, bias, _ in blocks]
        stats = []
        for s in scores:
            m = jnp.max(s, axis=-1, keepdims=True)
            p = jnp.exp2(s - m)
            stats.append((m, p, jnp.sum(p, axis=-1, keepdims=True)))
        outs = [_dot(p.astype(jnp.bfloat16), blkdef[2]) / den
                for (m, p, den), blkdef in zip(stats, blocks)]
        for o, (m, p, den), blkdef in zip(outs, stats, blocks):
            put(g, blkdef[4], o, m + jnp.log2(den))

    bias0 = bias_for(0)
    prev_cols = kj < blk

    def g0_body(it, carry):
        blocks = []
        for k in range(ATT_UNROLL):
            n = it * ATT_UNROLL + k
            q0 = pl.multiple_of(n * blk, blk)
            p0 = pl.multiple_of(jnp.maximum(n - 1, 0) * blk, blk)
            ctx = lambda ref, p0=p0, q0=q0: jnp.concatenate(
                [ref[pl.ds(p0, blk), :], ref[pl.ds(q0, blk), :]], axis=0)
            bias = jnp.where(prev_cols & (n == 0), -jnp.inf, bias0)
            blocks.append((q0_ref[pl.ds(q0, blk), :], ctx(k0_ref), ctx(v0_ref), bias, pl.ds(q0, blk)))
        run_blocks(0, blocks)
        return carry

    lax.fori_loop(0, SEQ // blk // ATT_UNROLL, g0_body, 0)

    d1 = DIL_PATTERNS[1][1]
    cls1 = HALF // d1
    per_half = cls1 // blk
    bias1 = bias_for(1)

    def row1(r, n):
        return (n // per_half) * HALF + r * cls1 + (n % per_half) * blk

    nblk1 = SEQ // d1 // blk
    cls_per_trip = ATT_UNROLL // nblk1

    def g1_body(it, carry):
        blocks = []
        for c in range(cls_per_trip):
            r = it * cls_per_trip + c
            for n in range(nblk1):
                cur = pl.multiple_of(row1(r, n), blk)
                q = q1_ref[pl.ds(cur, blk), :]
                dst = pl.ds(n * blk * d1 + r, blk, stride=d1)
                if n == 0:
                    blocks.append((q, k1_ref[pl.ds(cur, blk), :], v1_ref[pl.ds(cur, blk), :],
                                   bias1[:, blk:], dst))
                else:
                    prev = pl.multiple_of(row1(r, n - 1), blk)
                    kc = jnp.concatenate([k1_ref[pl.ds(prev, blk), :], k1_ref[pl.ds(cur, blk), :]], axis=0)
                    vc = jnp.concatenate([v1_ref[pl.ds(prev, blk), :], v1_ref[pl.ds(cur, blk), :]], axis=0)
                    blocks.append((q, kc, vc, bias1, dst))
        run_blocks(1, blocks)
        return carry

    lax.fori_loop(0, d1 // cls_per_trip, g1_body, 0)

    d2 = DIL_PATTERNS[2][1]
    cls2 = HALF // d2
    bias2 = bias_for(2)

    def g2_body(it, carry):
        blocks = []
        for k in range(ATT_UNROLL):
            r = it * ATT_UNROLL + k
            a = pl.multiple_of(r * cls2, cls2)
            b = pl.multiple_of(HALF + r * cls2, cls2)
            cat = lambda ref, a=a, b=b: jnp.concatenate(
                [ref[pl.ds(a, cls2), :], ref[pl.ds(b, cls2), :]], axis=0)
            blocks.append((cat(q2_ref), cat(k2_ref), cat(v2_ref), bias2[:, blk:],
                           pl.ds(r, blk, stride=d2)))
        run_blocks(2, blocks)
        return carry

    lax.fori_loop(0, d2 // ATT_UNROLL, g2_body, 0)

    for t in range(SEQ // MIX_ROWS):
        sl = pl.ds(t * MIX_ROWS, MIX_ROWS)
        l0, l1, l2 = l_sc[0, sl, :], l_sc[1, sl, :], l_sc[2, sl, :]
        m = jnp.maximum(jnp.maximum(l0, l1), l2)
        e0, e1, e2 = jnp.exp2(l0 - m), jnp.exp2(l1 - m), jnp.exp2(l2 - m)
        mix = (e0 * o_sc[0, sl, :] + e1 * o_sc[1, sl, :] + e2 * o_sc[2, sl, :]) / (e0 + e1 + e2)
        o_ref[sl, :] = mix.astype(o_ref.dtype)


def _dil_branch(qkv, slopes):
    nq = DIL_DIM // HEAD_DIM

    def spec(section, g):
        return pl.BlockSpec((None, SEQ, HEAD_DIM),
                            lambda b, h, s_ref, o=section * nq + g * HEADS_PER_GROUP: (b, 0, o + h))

    grid_spec = pltpu.PrefetchScalarGridSpec(
        num_scalar_prefetch=1,
        grid=(BATCH, HEADS_PER_GROUP),
        in_specs=[spec(sec, g) for sec in range(3) for g in range(N_DIL_GROUPS)],
        out_specs=pl.BlockSpec((None, SEQ, HEAD_DIM), lambda b, h, s_ref: (b, 0, h)),
        scratch_shapes=[pltpu.VMEM((N_DIL_GROUPS, SEQ, HEAD_DIM), jnp.float32),
                        pltpu.VMEM((N_DIL_GROUPS, SEQ, HEAD_DIM), jnp.float32)],
    )
    return pl.pallas_call(
        _dilattn_kernel,
        grid_spec=grid_spec,
        out_shape=jax.ShapeDtypeStruct((BATCH, SEQ, DIL_OUT_DIM), jnp.bfloat16),
        compiler_params=_params(("parallel", "arbitrary")),
        name="dil_attn",
    )(slopes, *([qkv] * 9))


MERGE_TM = 1024
MERGE_TN = 512


def _merge_kernel(x_ref, wg0_ref, wg1_ref, wg2_ref, bg0_ref, bg1_ref, bg2_ref,
                  sc_ref, od_ref, om_ref, wout_ref, o_ref):
    r1, r2 = CONV_DIM, CONV_DIM + DIL_OUT_DIM
    x = x_ref[...]
    g0 = jax.nn.sigmoid(_dot(x, wg0_ref[...]) + bg0_ref[...])
    acc = g0 * _dot(sc_ref[...], wout_ref[0:r1, :])
    g1 = jax.nn.sigmoid(_dot(x, wg1_ref[...]) + bg1_ref[...])
    acc = acc + g1 * _dot(od_ref[...], wout_ref[r1:r2, :])
    g2 = jax.nn.sigmoid(_dot(x, wg2_ref[...]) + bg2_ref[...])
    acc = acc + g2 * _dot(om_ref[...], wout_ref[r2:r2 + MEM_DIM, :])
    o_ref[...] = acc.astype(o_ref.dtype)


def _merge(x2, w_in, b_in, s_conv, o_dil, o_mem, w_out):
    nb = lambda br: (OFF_GATE + br * D_MODEL) // MERGE_TN
    gspec = lambda br: pl.BlockSpec((D_MODEL, MERGE_TN), lambda i, n, o=nb(br): (0, o + n))
    bspec = lambda br: pl.BlockSpec((1, MERGE_TN), lambda i, n, o=nb(br): (0, o + n))
    act = lambda width: pl.BlockSpec((MERGE_TM, width), lambda i, n: (i, 0))
    wout = lambda width: pl.BlockSpec((width, MERGE_TN), lambda i, n: (0, n))
    return pl.pallas_call(
        _merge_kernel,
        grid=(TOKENS // MERGE_TM, D_MODEL // MERGE_TN),
        in_specs=[act(D_MODEL), gspec(0), gspec(1), gspec(2), bspec(0), bspec(1), bspec(2),
                  act(CONV_DIM), act(DIL_OUT_DIM), act(MEM_DIM),
                  wout(CONV_DIM + DIL_OUT_DIM + MEM_DIM)],
        out_specs=pl.BlockSpec((MERGE_TM, MERGE_TN), lambda i, n: (i, n)),
        out_shape=jax.ShapeDtypeStruct((TOKENS, D_MODEL), jnp.bfloat16),
        compiler_params=_params(("parallel", "arbitrary")),
        name="gated_merge",
    )(x2, w_in, w_in, w_in, b_in, b_in, b_in, s_conv, o_dil, o_mem, w_out)


OPROJ_TM = 512


OPROJ_PARTS = 4


def _oproj_kernel(x_ref, m_ref, wo_ref, g_ref, b_ref, wr_ref, br_ref, x1_ref, logit_ref):
    rows = OPROJ_TM // OPROJ_PARTS
    parts = [pl.ds(i * rows, rows) for i in range(OPROJ_PARTS)]
    proj = [_dot(m_ref[p, :], wo_ref[...]) for p in parts]
    for p, y in zip(parts, proj):
        x1 = _layer_norm(ALPHA * x_ref[p, :] + y, g_ref[...], b_ref[...])
        x1_ref[p, :] = x1
        logit_ref[p, :] = _dot(x1, wr_ref[...]) + br_ref[...]


def _oproj(x2, merged, w_o, g, b, w_route, b_route):
    row = lambda width: pl.BlockSpec((OPROJ_TM, width), lambda i: (i, 0))
    full = lambda r, c: pl.BlockSpec((r, c), lambda i: (0, 0))
    return pl.pallas_call(
        _oproj_kernel,
        grid=(TOKENS // OPROJ_TM,),
        in_specs=[row(D_MODEL), row(D_MODEL), full(D_MODEL, D_MODEL), full(1, D_MODEL), full(1, D_MODEL),
                  full(D_MODEL, LANES), full(1, LANES)],
        out_specs=[row(D_MODEL), row(LANES)],
        out_shape=[jax.ShapeDtypeStruct((TOKENS, D_MODEL), jnp.float32),
                   jax.ShapeDtypeStruct((TOKENS, LANES), jnp.float32)],
        compiler_params=_params(("parallel",)),
        name="oproj_ln1",
    )(x2, merged, w_o, g, b, w_route, b_route)


def _route_kernel(logit_ref, rec_ref, rect_ref, cnt_ref, carry_sc):
    @pl.when(pl.program_id(0) == 0)
    def _():
        carry_sc[...] = jnp.zeros_like(carry_sc)

    tm = ROUTE_TILE
    zt = jnp.transpose(logit_ref[...])
    row = lax.broadcasted_iota(jnp.int32, (SUBLANES, tm), 0)
    neg = -jnp.inf
    colmax = lambda a: jnp.max(a, axis=0, keepdims=True)
    first = lambda hit: jnp.min(jnp.where(hit, row, SUBLANES), axis=0, keepdims=True)

    glog = jnp.where(row < N_GROUPS, zt[0:SUBLANES], neg)
    gmax = colmax(glog)
    gsel = first(glog == gmax)
    gw = 1.0 / jnp.sum(jnp.exp(glog - gmax), axis=0, keepdims=True)

    elog = zt[ROUTER_LANE0:ROUTER_LANE0 + EXPERTS_PER_GROUP]
    for g in range(1, N_GROUPS):
        lo = ROUTER_LANE0 + g * EXPERTS_PER_GROUP
        elog = jnp.where(gsel == g, zt[lo:lo + EXPERTS_PER_GROUP], elog)
    top1 = colmax(elog)
    i1 = first(elog == top1)
    rest = jnp.where(row == i1, neg, elog)
    top2 = colmax(rest)
    i2 = first(rest == top2)
    t = jnp.exp(top2 - top1)
    w1 = gw / (1.0 + t)
    w2 = w1 * t

    hit = (row == i1) | (row == i2)
    onehot = jnp.concatenate([jnp.where(hit & (gsel == g), 1.0, 0.0) for g in range(N_GROUPS)], axis=0)
    chunks = tm // LANES
    stacked = jnp.concatenate([onehot[:, c * LANES:(c + 1) * LANES] for c in range(chunks)], axis=0)
    ri = lax.broadcasted_iota(jnp.int32, (LANES, LANES), 0)
    ci = lax.broadcasted_iota(jnp.int32, (LANES, LANES), 1)
    upper = jnp.where(ri <= ci, 1.0, 0.0).astype(jnp.bfloat16)
    within = _dot(stacked.astype(jnp.bfloat16), upper)
    run = carry_sc[:, 0:1]
    pieces = []
    for c in range(chunks):
        cum = within[c * N_EXPERTS:(c + 1) * N_EXPERTS, :] + run
        pieces.append(cum)
        run = cum[:, LANES - 1:LANES]
    before = jnp.concatenate(pieces, axis=1) - 1.0
    carry_sc[...] = jnp.broadcast_to(run, carry_sc.shape)
    cnt_ref[...] = jnp.broadcast_to(run, cnt_ref.shape)

    rank_a = jnp.zeros((1, tm), jnp.float32)
    rank_b = jnp.zeros((1, tm), jnp.float32)
    for g in range(N_GROUPS):
        b8 = before[g * EXPERTS_PER_GROUP:(g + 1) * EXPERTS_PER_GROUP, :]
        mine = gsel == g
        rank_a = rank_a + jnp.sum(jnp.where(mine & (row == i1), b8, 0.0), axis=0, keepdims=True)
        rank_b = rank_b + jnp.sum(jnp.where(mine & (row == i2), b8, 0.0), axis=0, keepdims=True)

    base = (gsel * EXPERTS_PER_GROUP).astype(jnp.float32)
    vals = [base + i1.astype(jnp.float32), base + i2.astype(jnp.float32), rank_a, rank_b, w1, w2]
    rect = jnp.zeros((SUBLANES, tm), jnp.float32)
    for k, val in enumerate(vals):
        rect = jnp.where(row == k, val, rect)
    rect_ref[...] = rect
    padded = jnp.concatenate([rect, jnp.zeros((LANES - SUBLANES, tm), jnp.float32)], axis=0)
    rec_ref[...] = jnp.transpose(padded)


def _route(logits):
    return pl.pallas_call(
        _route_kernel,
        grid=(TOKENS // ROUTE_TILE,),
        in_specs=[pl.BlockSpec((ROUTE_TILE, LANES), lambda i: (i, 0))],
        out_specs=[pl.BlockSpec((ROUTE_TILE, LANES), lambda i: (i, 0)),
                   pl.BlockSpec((SUBLANES, ROUTE_TILE), lambda i: (0, i)),
                   pl.BlockSpec((N_EXPERTS, LANES), lambda i: (0, 0))],
        out_shape=[jax.ShapeDtypeStruct((TOKENS, LANES), jnp.float32),
                   jax.ShapeDtypeStruct((SUBLANES, TOKENS), jnp.float32),
                   jax.ShapeDtypeStruct((N_EXPERTS, LANES), jnp.float32)],
        scratch_shapes=[pltpu.VMEM((N_EXPERTS, LANES), jnp.float32)],
        compiler_params=_params(("arbitrary",)),
        name="route",
    )(logits)


def _slots_kernel(start_ref, rect_ref, pos_ref):
    r = rect_ref[...]
    row = lax.broadcasted_iota(jnp.int32, r.shape, 0)

    def pos_of(e, rank):
        e = e.astype(jnp.int32)
        start = jnp.zeros(e.shape, jnp.int32)
        for k in range(N_EXPERTS):
            start = jnp.where(e == k, start_ref[k], start)
        return start + rank.astype(jnp.int32)

    pos_a = pos_of(r[0:1], r[2:3])
    pos_b = pos_of(r[1:2], r[3:4])
    pos_ref[...] = jnp.where(row == 0, pos_a, jnp.where(row == 1, pos_b, 0))


def _slots(starts, rect):
    grid_spec = pltpu.PrefetchScalarGridSpec(
        num_scalar_prefetch=1,
        grid=(TOKENS // ROUTE_TILE,),
        in_specs=[pl.BlockSpec((SUBLANES, ROUTE_TILE), lambda i, st: (0, i))],
        out_specs=pl.BlockSpec((SUBLANES, ROUTE_TILE), lambda i, st: (0, i)),
    )
    return pl.pallas_call(
        _slots_kernel,
        grid_spec=grid_spec,
        out_shape=jax.ShapeDtypeStruct((SUBLANES, TOKENS), jnp.int32),
        compiler_params=_params(("parallel",)),
        name="slots",
    )(starts, rect)


DISPATCH_TILE = 512
SORTED_ROWS = N_EXPERT_TILES * EXPERT_TILE


def _dispatch_kernel(pad_start_ref, pad_n_ref, nvalid_ref, pos_ref, x_ref, xs_hbm, zero_sc, stage_sc, sem,
                     row_sem):
    groups = DISPATCH_TILE // SUBLANES

    @pl.when(pl.program_id(0) == 0)
    def _():
        zero_sc[...] = jnp.zeros_like(zero_sc)

        def tile_copy(t):
            return pltpu.make_async_copy(zero_sc, xs_hbm.at[pl.ds(pl.multiple_of(t * EXPERT_TILE, EXPERT_TILE),
                                                                  EXPERT_TILE)], sem.at[2])

        def start_tile(t, c):
            tile_copy(t).start()
            return c

        def wait_tile(t, c):
            tile_copy(t).wait()
            return c
        lax.fori_loop(nvalid_ref[0], N_EXPERT_TILES, start_tile, 0)
        lax.fori_loop(nvalid_ref[0], N_EXPERT_TILES, wait_tile, 0)

        def pad_copies(e, act):
            n = pad_n_ref[e]
            cur = pad_start_ref[e]
            for bit in range((EXPERT_TILE - 1).bit_length()):
                size = 1 << bit
                has = lax.bitwise_and(n, size)

                @pl.when(has != 0)
                def _(cur=cur, size=size):
                    at = pl.multiple_of(cur, math.gcd(size, EXPERT_TILE))
                    act(pltpu.make_async_copy(zero_sc.at[pl.ds(0, size)], xs_hbm.at[pl.ds(at, size)],
                                              sem.at[1]))
                cur = cur + has

        def start_pads(e, c):
            pad_copies(e, lambda cp: cp.start())
            return c

        def wait_pads(e, c):
            pad_copies(e, lambda cp: cp.wait())
            return c
        lax.fori_loop(0, N_EXPERTS, start_pads, 0)
        lax.fori_loop(0, N_EXPERTS, wait_pads, 0)

    j = pl.program_id(0)
    slot = lax.rem(j, 2)
    stage_sc[slot] = x_ref[...].reshape(groups, SUBLANES, D_MODEL)

    def body(gi, c):
        for s in range(SUBLANES):
            for k in range(2):
                p = pos_ref[0, 0, k * DISPATCH_TILE + gi * SUBLANES + s]
                pltpu.make_async_copy(stage_sc.at[slot, gi, pl.ds(s, 1)], xs_hbm.at[pl.ds(p, 1)],
                                      row_sem.at[slot]).start(priority=k)
        return c
    lax.fori_loop(0, groups, body, 0)

    def wait_slot(sl):
        for _ in range(2 * groups):
            pltpu.make_async_copy(stage_sc.at[sl, 0], xs_hbm.at[pl.ds(0, SUBLANES)], row_sem.at[sl]).wait()

    @pl.when(j > 0)
    def _():
        wait_slot(1 - slot)

    @pl.when(j == pl.num_programs(0) - 1)
    def _():
        wait_slot(slot)


def _dispatch(pad_start, pad_n, n_valid_tiles, pos3, x1):
    nt = TOKENS // DISPATCH_TILE
    groups = DISPATCH_TILE // SUBLANES
    grid_spec = pltpu.PrefetchScalarGridSpec(
        num_scalar_prefetch=3,
        grid=(nt,),
        in_specs=[pl.BlockSpec((1, 1, 2 * DISPATCH_TILE), lambda j, *_: (j, 0, 0), memory_space=pltpu.SMEM),
                  pl.BlockSpec((DISPATCH_TILE, D_MODEL), lambda j, *_: (j, 0))],
        out_specs=pl.BlockSpec(memory_space=pl.ANY),
        scratch_shapes=[pltpu.VMEM((EXPERT_TILE, D_MODEL), jnp.float32),
                        pltpu.VMEM((2, groups, SUBLANES, D_MODEL), jnp.float32),
                        pltpu.SemaphoreType.DMA((3,)),
                        pltpu.SemaphoreType.DMA((2,))],
    )
    return pl.pallas_call(
        _dispatch_kernel,
        grid_spec=grid_spec,
        out_shape=jax.ShapeDtypeStruct((SORTED_ROWS, D_MODEL), jnp.float32),
        compiler_params=_params(("arbitrary",)),
        name="dispatch",
    )(pad_start, pad_n, n_valid_tiles, pos3, x1)


def _ffn_kernel(texp_ref, tvalid_ref, tslot_ref, tfirst_ref, tnext_ref, nvalid_ref,
                xs_ref, wg_hbm, wu_hbm, wd_hbm, y_ref, wg_buf, wu_buf, wd_buf, sem):
    j = pl.program_id(0)
    valid = tvalid_ref[j] == 1
    slot = tslot_ref[j]

    def weight_copies(e, s):
        return (pltpu.make_async_copy(wg_hbm.at[e], wg_buf.at[s], sem.at[s, 0]),
                pltpu.make_async_copy(wu_hbm.at[e], wu_buf.at[s], sem.at[s, 1]),
                pltpu.make_async_copy(wd_hbm.at[e], wd_buf.at[s], sem.at[s, 2]))

    @pl.when(valid & (tfirst_ref[j] == 1))
    def _():
        @pl.when(j == 0)
        def _():
            for cp in weight_copies(texp_ref[j], slot):
                cp.start()

        for cp in weight_copies(texp_ref[j], slot):
            cp.wait()

        @pl.when(tnext_ref[j] >= 0)
        def _():
            for cp in weight_copies(tnext_ref[j], 1 - slot):
                cp.start(priority=1)

    @pl.when(valid)
    def _():
        xt = xs_ref[...]
        a = _dot(xt, wg_buf[slot])
        u = _dot(xt, wu_buf[slot])
        hid = (a * jax.nn.sigmoid(a)) * u
        y_ref[...] = _dot(hid, wd_buf[slot])

    @pl.when(jnp.logical_not(valid))
    def _():
        y_ref[...] = jnp.zeros_like(y_ref)


def _expert_ffn(tables, xs, w_gate, w_up, w_down):
    nt = N_EXPERT_TILES
    xmap = lambda j, te, tv, ts, tf, tn, nv: (jnp.minimum(j, nv[0] - 1), 0)
    grid_spec = pltpu.PrefetchScalarGridSpec(
        num_scalar_prefetch=6,
        grid=(nt,),
        in_specs=[pl.BlockSpec((EXPERT_TILE, D_MODEL), xmap),
                  pl.BlockSpec(memory_space=pl.ANY),
                  pl.BlockSpec(memory_space=pl.ANY),
                  pl.BlockSpec(memory_space=pl.ANY)],
        out_specs=pl.BlockSpec((EXPERT_TILE, D_MODEL), lambda j, *_: (j, 0)),
        scratch_shapes=[pltpu.VMEM((2, D_MODEL, D_EXPERT), jnp.float32),
                        pltpu.VMEM((2, D_MODEL, D_EXPERT), jnp.float32),
                        pltpu.VMEM((2, D_EXPERT, D_MODEL), jnp.float32),
                        pltpu.SemaphoreType.DMA((2, 3))],
    )
    return pl.pallas_call(
        _ffn_kernel,
        grid_spec=grid_spec,
        out_shape=jax.ShapeDtypeStruct((nt * EXPERT_TILE, D_MODEL), jnp.float32),
        compiler_params=_params(("arbitrary",)),
        name="expert_ffn",
    )(*tables, xs, w_gate, w_up, w_down)


def _combine_kernel(pos_cur_ref, pos_next_ref, r_ref, x1_ref, y_hbm, g_ref, b_ref, o_ref, ybuf, sem):
    j = pl.program_id(0)
    nt = pl.num_programs(0)
    slot = lax.rem(j, 2)
    groups = COMBINE_TILE // SUBLANES

    def gather(pos_ref, dst_slot):
        def body(gi, c):
            for s in range(SUBLANES):
                for k in range(2):
                    p = pos_ref[0, 0, k * COMBINE_TILE + gi * SUBLANES + s]
                    pltpu.make_async_copy(y_hbm.at[pl.ds(p, 1)], ybuf.at[dst_slot, k, gi, pl.ds(s, 1)],
                                          sem.at[dst_slot]).start(priority=k)
            return c
        lax.fori_loop(0, groups, body, 0)

    @pl.when(j == 0)
    def _():
        gather(pos_cur_ref, 0)

    @pl.when(j + 1 < nt)
    def _():
        gather(pos_next_ref, 1 - slot)

    for _ in range(2 * groups):
        pltpu.make_async_copy(y_hbm.at[pl.ds(0, SUBLANES)], ybuf.at[slot, 0, 0], sem.at[slot]).wait()
    r = r_ref[...]
    ya = ybuf[slot, 0].reshape(COMBINE_TILE, D_MODEL)
    yb = ybuf[slot, 1].reshape(COMBINE_TILE, D_MODEL)
    hres = ALPHA * x1_ref[...] + r[:, 4:5] * ya + r[:, 5:6] * yb
    o_ref[...] = _layer_norm(hres, g_ref[...], b_ref[...])


def _combine(pos3, routing, x1, y_sorted, g, b):
    nt = TOKENS // COMBINE_TILE
    groups = COMBINE_TILE // SUBLANES
    return pl.pallas_call(
        _combine_kernel,
        grid=(nt,),
        in_specs=[pl.BlockSpec((1, 1, 2 * COMBINE_TILE), lambda j: (j, 0, 0), memory_space=pltpu.SMEM),
                  pl.BlockSpec((1, 1, 2 * COMBINE_TILE), lambda j: (jnp.minimum(j + 1, nt - 1), 0, 0),
                               memory_space=pltpu.SMEM),
                  pl.BlockSpec((COMBINE_TILE, LANES), lambda j: (j, 0)),
                  pl.BlockSpec((COMBINE_TILE, D_MODEL), lambda j: (j, 0)),
                  pl.BlockSpec(memory_space=pl.ANY),
                  pl.BlockSpec((1, D_MODEL), lambda j: (0, 0)),
                  pl.BlockSpec((1, D_MODEL), lambda j: (0, 0))],
        out_specs=pl.BlockSpec((COMBINE_TILE, D_MODEL), lambda j: (j, 0)),
        out_shape=jax.ShapeDtypeStruct((TOKENS, D_MODEL), jnp.float32),
        scratch_shapes=[pltpu.VMEM((2, 2, groups, SUBLANES, D_MODEL), jnp.float32),
                        pltpu.SemaphoreType.DMA((2,))],
        compiler_params=_params(("arbitrary",)),
        name="combine_ln2",
    )(pos3, pos3, routing, x1, y_sorted, g, b)


def _alibi_slopes():
    n = N_DIL_GROUPS * HEADS_PER_GROUP
    return jnp.asarray(2.0 ** (-ALIBI_MAX * np.arange(1, n + 1, dtype=np.float32) / n), jnp.float32)


def kernel(x, mem, ln_mem_g, ln_mem_b, w_in, b_in, w_conv, w_conv_out, w_dil_out, w_mem_kv, w_mem_out, w_o, ln1_g, ln1_b, w_group, b_group, w_router, b_router, w_gate, w_up, w_down, ln2_g, ln2_b):
    assert x.shape == (BATCH, SEQ, D_MODEL) and w_in.shape == (1, D_MODEL, IN_DIM)
    bf16 = jnp.bfloat16
    row = lambda v: v.reshape(1, -1)
    w_in2 = w_in[0].astype(bf16)
    b_in2 = b_in

    kv = _memkv(mem, row(ln_mem_g), row(ln_mem_b), w_mem_kv[0])
    s_conv, xb = _conv_branch(x, w_in2, b_in2, w_conv[0])
    qkv = _qkv_proj(xb, w_in2, b_in2)
    o_mem = _mem_branch(xb, w_in2, b_in2, kv)
    o_dil = _dil_branch(qkv, _alibi_slopes())

    x2 = x.reshape(TOKENS, D_MODEL)
    merged = _merge(xb.reshape(TOKENS, D_MODEL), w_in2, b_in2,
                    s_conv.reshape(TOKENS, CONV_DIM), o_dil.reshape(TOKENS, DIL_OUT_DIM),
                    o_mem.reshape(TOKENS, MEM_DIM),
                    jnp.concatenate([w_conv_out[0], w_dil_out[0], w_mem_out[0]], axis=0).astype(bf16))

    gap = ROUTER_LANE0 - N_GROUPS
    tail = LANES - ROUTER_LANE0 - N_EXPERTS
    w_route = jnp.concatenate(
        [w_group[0], jnp.zeros((D_MODEL, gap), jnp.float32),
         jnp.transpose(w_router[0], (1, 0, 2)).reshape(D_MODEL, N_EXPERTS),
         jnp.zeros((D_MODEL, tail), jnp.float32)], axis=1)
    b_route = jnp.concatenate(
        [b_group[0], jnp.zeros((gap,), jnp.float32), b_router[0].reshape(N_EXPERTS),
         jnp.zeros((tail,), jnp.float32)]).reshape(1, LANES)
    x1, logits = _oproj(x2, merged, w_o[0].astype(bf16), ln1_g, ln1_b, w_route, b_route)

    routing, routing_t, counts_b = _route(logits)

    i32 = jnp.int32
    counts = counts_b[:, 0].astype(i32)
    padded = ((counts + EXPERT_TILE - 1) // EXPERT_TILE) * EXPERT_TILE
    ends = jnp.cumsum(padded)
    starts = ends - padded
    pos = _slots(starts.astype(i32), routing_t)[0:2]

    def pos_tiles(tile):
        return jnp.transpose(pos.reshape(2, TOKENS // tile, tile), (1, 0, 2)).reshape(TOKENS // tile, 1, 2 * tile)

    tile_start = jnp.arange(N_EXPERT_TILES, dtype=i32) * EXPERT_TILE
    tile_expert = jnp.minimum(jnp.sum((ends[None, :] <= tile_start[:, None]).astype(i32), axis=1),
                              N_EXPERTS - 1)
    tile_valid = tile_start < ends[-1]
    prev_expert = jnp.concatenate([jnp.full((1,), -1, i32), tile_expert[:-1]])
    tile_first = tile_valid & (tile_expert != prev_expert)
    tile_slot = (jnp.cumsum(tile_first.astype(i32)) - 1) & 1
    big = N_EXPERTS
    idx = jnp.where(counts > 0, jnp.arange(N_EXPERTS, dtype=i32), big)
    later = jnp.concatenate([lax.cummin(idx[::-1])[::-1][1:], jnp.full((1,), big, i32)])
    next_used = jnp.where(later == big, -1, later)
    n_valid_tiles = (ends[-1:] // EXPERT_TILE).astype(i32)
    tables = (tile_expert, tile_valid.astype(i32), tile_slot, tile_first.astype(i32),
              next_used[tile_expert], n_valid_tiles)

    xs = _dispatch(starts + counts, padded - counts, n_valid_tiles,
                   pos_tiles(DISPATCH_TILE), x1)
    y_sorted = _expert_ffn(tables, xs,
                           w_gate.reshape(N_EXPERTS, D_MODEL, D_EXPERT),
                           w_up.reshape(N_EXPERTS, D_MODEL, D_EXPERT),
                           w_down.reshape(N_EXPERTS, D_EXPERT, D_MODEL))
    out = _combine(pos_tiles(COMBINE_TILE), routing, x1, y_sorted,
                   ln2_g, ln2_b)
    return out.reshape(BATCH, SEQ, D_MODEL)
```

```python
import math

import numpy as np
import jax
import jax.numpy as jnp
from jax import lax
from jax.experimental import pallas as pl
from jax.experimental.pallas import tpu as pltpu

D_MODEL = 2048
BATCH = 8
SEQ = 2048
TOKENS = BATCH * SEQ
CONV_DIM = 1024
CONV_WIDTH = 3
DIL_PATTERNS = ((128, 1), (512, 4), (2048, 16))
N_DIL_GROUPS = 3
HEADS_PER_GROUP = 4
HEAD_DIM = 128
DIL_DIM = N_DIL_GROUPS * HEADS_PER_GROUP * HEAD_DIM
DIL_OUT_DIM = HEADS_PER_GROUP * HEAD_DIM
ATT_BLOCK = 128
ALIBI_MAX = 8.0
MEM_LEN = 256
MEM_HEADS = 4
MEM_HEAD_DIM = 256
MEM_DIM = MEM_HEADS * MEM_HEAD_DIM
N_BRANCHES = 3
IN_DIM = 3 * CONV_DIM + 3 * DIL_DIM + MEM_DIM + N_BRANCHES * D_MODEL
N_GROUPS = 4
EXPERTS_PER_GROUP = 8
N_EXPERTS = N_GROUPS * EXPERTS_PER_GROUP
D_EXPERT = 512
ALPHA = 2.0 ** 0.25
LN_EPS = 1e-5

OFF_CB = 0
OFF_CC = CONV_DIM
OFF_CH = 2 * CONV_DIM
OFF_Q = 3 * CONV_DIM
OFF_MQ = OFF_Q + 3 * DIL_DIM
OFF_GATE = OFF_MQ + MEM_DIM

LANES = 128
SUBLANES = 8
HALF = SEQ // 2
VMEM_LIMIT = 56 * 1024 * 1024

ROUTE_TILE = 2048
EXPERT_TILE = 384
N_EXPERT_TILES = 2 * TOKENS // EXPERT_TILE + N_EXPERTS
COMBINE_TILE = 256
ROUTER_LANE0 = 8


def _params(sem, limit=VMEM_LIMIT):
    return pltpu.CompilerParams(dimension_semantics=sem, vmem_limit_bytes=limit)


def _layer_norm(x, g, b):
    mu = jnp.mean(x, axis=-1, keepdims=True)
    xc = x - mu
    var = jnp.mean(xc * xc, axis=-1, keepdims=True)
    return xc * lax.rsqrt(var + LN_EPS) * g + b


def _dot(a, b):
    return jnp.dot(a, b, preferred_element_type=jnp.float32)


def _dot_t(a, b):
    return lax.dot_general(a, b, (((1,), (1,)), ((), ())), preferred_element_type=jnp.float32)


MEMKV_BATCHES = 2


def _memkv_kernel(mem_ref, g_ref, b_ref, w_ref, kv_ref):
    rows = MEMKV_BATCHES * MEM_LEN
    y = _layer_norm(mem_ref[...].reshape(rows, D_MODEL), g_ref[...], b_ref[...])
    kv = _dot(y, w_ref[...])
    kv_ref[...] = kv.astype(kv_ref.dtype).reshape(MEMKV_BATCHES, MEM_LEN, 2 * MEM_DIM)


def _memkv(mem, g, b, w):
    return pl.pallas_call(
        _memkv_kernel,
        grid=(BATCH // MEMKV_BATCHES,),
        in_specs=[pl.BlockSpec((MEMKV_BATCHES, MEM_LEN, D_MODEL), lambda i: (i, 0, 0)),
                  pl.BlockSpec((1, D_MODEL), lambda i: (0, 0)),
                  pl.BlockSpec((1, D_MODEL), lambda i: (0, 0)),
                  pl.BlockSpec((D_MODEL, 2 * MEM_DIM), lambda i: (0, 0))],
        out_specs=pl.BlockSpec((MEMKV_BATCHES, MEM_LEN, 2 * MEM_DIM), lambda i: (i, 0, 0)),
        out_shape=jax.ShapeDtypeStruct((BATCH, MEM_LEN, 2 * MEM_DIM), jnp.bfloat16),
        compiler_params=_params(("parallel",)),
        name="mem_kv",
    )(mem, g, b, w)


CONV_TC = 512


def _conv_kernel(x_ref, wb_ref, wc_ref, wh_ref, bb_ref, bc_ref, bh_ref, wconv_ref, s_ref, xb_ref,
                 u_sc, carry_sc):
    half = pl.program_id(1)
    c = pl.program_id(2)

    @pl.when(c == 0)
    def _():
        xb_ref[...] = x_ref[...].astype(xb_ref.dtype)

    x = xb_ref[...]
    cc = _dot(x, wc_ref[...]) + bc_ref[...]
    ch = _dot(x, wh_ref[...]) + bh_ref[...]
    u = cc * ch
    pre = SUBLANES
    u_sc[0:pre, :] = jnp.where(half == 0, 0.0, carry_sc[c])
    u_sc[pre:pre + HALF, :] = u
    carry_sc[c] = u[HALF - pre:HALF, :]
    wconv = wconv_ref[...]
    y = wconv[CONV_WIDTH - 1:CONV_WIDTH, :] * u
    for back in range(1, CONV_WIDTH):
        tap = CONV_WIDTH - 1 - back
        y = y + wconv[tap:tap + 1, :] * u_sc[pre - back:pre - back + HALF, :]
    cb = _dot(x, wb_ref[...]) + bb_ref[...]
    s_ref[...] = (cb * y).astype(s_ref.dtype)


def _conv_branch(x, w_in, b_in, w_conv):
    nb = lambda off: off // CONV_TC
    wspec = lambda off: pl.BlockSpec((D_MODEL, CONV_TC), lambda b, h, c, o=nb(off): (0, o + c))
    bspec = lambda off: pl.BlockSpec((1, CONV_TC), lambda b, h, c, o=nb(off): (0, o + c))
    return pl.pallas_call(
        _conv_kernel,
        grid=(BATCH, 2, CONV_DIM // CONV_TC),
        in_specs=[pl.BlockSpec((None, HALF, D_MODEL), lambda b, h, c: (b, h, 0)),
                  wspec(OFF_CB), wspec(OFF_CC), wspec(OFF_CH),
                  bspec(OFF_CB), bspec(OFF_CC), bspec(OFF_CH),
                  pl.BlockSpec((CONV_WIDTH, CONV_TC), lambda b, h, c: (0, c))],
        out_specs=[pl.BlockSpec((None, HALF, CONV_TC), lambda b, h, c: (b, h, c)),
                   pl.BlockSpec((None, HALF, D_MODEL), lambda b, h, c: (b, h, 0))],
        out_shape=[jax.ShapeDtypeStruct((BATCH, SEQ, CONV_DIM), jnp.bfloat16),
                   jax.ShapeDtypeStruct((BATCH, SEQ, D_MODEL), jnp.bfloat16)],
        scratch_shapes=[pltpu.VMEM((HALF + SUBLANES, CONV_TC), jnp.float32),
                        pltpu.VMEM((CONV_DIM // CONV_TC, SUBLANES, CONV_TC), jnp.float32)],
        compiler_params=_params(("arbitrary", "arbitrary", "arbitrary")),
        name="conv_branch",
    )(x, w_in, w_in, w_in, b_in, b_in, b_in, w_conv)


QKV_TN = 512
QKV_CHUNKS = QKV_TN // LANES


def _qkv_kernel(x_ref, w_ref, b_ref, o_ref, sc_ref, sc2_ref):
    x = x_ref[...]
    for gi in (2, 1, 0):
        cols = slice(gi * QKV_TN, (gi + 1) * QKV_TN)
        acc = _dot(x, w_ref[:, cols]) + b_ref[:, cols]
        d = DIL_PATTERNS[gi][1]
        if d == 1:
            o_ref[:, cols] = acc.astype(o_ref.dtype)
            continue
        rows = HALF // d
        for c in range(QKV_CHUNKS):
            sc_ref[gi - 1, c] = acc[:, c * LANES:(c + 1) * LANES]
        if d == 16:
            q4 = HALF // 4
            for c in range(QKV_CHUNKS):
                for r in range(4):
                    sc2_ref[c, r * q4:(r + 1) * q4, :] = sc_ref[gi - 1, c, pl.ds(r, q4, stride=4), :]
            for c in range(QKV_CHUNKS):
                lo = gi * QKV_TN + c * LANES
                for r in range(d):
                    r_lo, r_hi = r % 4, r // 4
                    o_ref[r * rows:(r + 1) * rows, lo:lo + LANES] = (
                        sc2_ref[c, pl.ds(r_lo * q4 + r_hi, rows, stride=4), :].astype(o_ref.dtype))
            continue
        for c in range(QKV_CHUNKS):
            for r in range(d):
                lo = gi * QKV_TN + c * LANES
                o_ref[r * rows:(r + 1) * rows, lo:lo + LANES] = (
                    sc_ref[gi - 1, c, pl.ds(r, rows, stride=d), :].astype(o_ref.dtype))


def _qkv_proj(xb, w_in, b_in):
    n0 = OFF_Q // DIL_DIM
    return pl.pallas_call(
        _qkv_kernel,
        grid=(BATCH, 2, 3),
        in_specs=[pl.BlockSpec((None, HALF, D_MODEL), lambda b, h, n: (b, h, 0)),
                  pl.BlockSpec((D_MODEL, DIL_DIM), lambda b, h, n: (0, n0 + n)),
                  pl.BlockSpec((1, DIL_DIM), lambda b, h, n: (0, n0 + n))],
        out_specs=pl.BlockSpec((None, HALF, DIL_DIM), lambda b, h, n: (b, h, n)),
        out_shape=jax.ShapeDtypeStruct((BATCH, SEQ, 3 * DIL_DIM), jnp.bfloat16),
        scratch_shapes=[pltpu.VMEM((N_DIL_GROUPS - 1, QKV_CHUNKS, HALF, LANES), jnp.float32),
                        pltpu.VMEM((QKV_CHUNKS, HALF, LANES), jnp.float32)],
        compiler_params=_params(("parallel", "parallel", "arbitrary")),
        name="qkv_proj",
    )(xb, w_in, b_in)


MEM_HEADS_PER_STEP = 2
MEM_TN = MEM_HEADS_PER_STEP * MEM_HEAD_DIM


def _memattn_kernel(x_ref, w_ref, b_ref, mk_ref, mv_ref, o_ref):
    mq_all = (_dot(x_ref[...], w_ref[...]) + b_ref[...]).astype(jnp.bfloat16)
    for hh in range(MEM_HEADS_PER_STEP):
        cols = slice(hh * MEM_HEAD_DIM, (hh + 1) * MEM_HEAD_DIM)
        s = _dot_t(mq_all[:, cols], mk_ref[:, cols]) * (MEM_HEAD_DIM ** -0.5 * math.log2(math.e))
        m = jnp.max(s, axis=-1, keepdims=True)
        p = jnp.exp2(s - m)
        den = jnp.sum(p, axis=-1, keepdims=True)
        o = _dot(p.astype(jnp.bfloat16), mv_ref[:, cols]) / den
        o_ref[:, cols] = o.astype(o_ref.dtype)


def _mem_branch(xb, w_in, b_in, kv):
    n0 = OFF_MQ // MEM_TN
    nv = MEM_DIM // MEM_TN
    return pl.pallas_call(
        _memattn_kernel,
        grid=(BATCH, 2, MEM_DIM // MEM_TN),
        in_specs=[pl.BlockSpec((None, HALF, D_MODEL), lambda b, h, n: (b, h, 0)),
                  pl.BlockSpec((D_MODEL, MEM_TN), lambda b, h, n: (0, n0 + n)),
                  pl.BlockSpec((1, MEM_TN), lambda b, h, n: (0, n0 + n)),
                  pl.BlockSpec((None, MEM_LEN, MEM_TN), lambda b, h, n: (b, 0, n)),
                  pl.BlockSpec((None, MEM_LEN, MEM_TN), lambda b, h, n: (b, 0, nv + n))],
        out_specs=pl.BlockSpec((None, HALF, MEM_TN), lambda b, h, n: (b, h, n)),
        out_shape=jax.ShapeDtypeStruct((BATCH, SEQ, MEM_DIM), jnp.bfloat16),
        compiler_params=_params(("parallel", "parallel", "arbitrary")),
        name="mem_branch",
    )(xb, w_in, b_in, kv, kv)


ATT_UNROLL = 16
MIX_ROWS = 256


def _dilattn_kernel(slopes_ref,
                    q0_ref, q1_ref, q2_ref, k0_ref, k1_ref, k2_ref, v0_ref, v1_ref, v2_ref,
                    o_ref, o_sc, l_sc):
    h = pl.program_id(1)
    blk = ATT_BLOCK
    log2e = math.log2(math.e)
    scale = HEAD_DIM ** -0.5 * log2e
    qi = lax.broadcasted_iota(jnp.int32, (blk, 2 * blk), 0) + blk
    kj = lax.broadcasted_iota(jnp.int32, (blk, 2 * blk), 1)
    jrel = qi - kj
    valid = (jrel >= 0) & (jrel <= blk)
    jrel_f = jrel.astype(jnp.float32)

    def bias_for(g):
        slope = slopes_ref[g * HEADS_PER_GROUP + h]
        d = float(DIL_PATTERNS[g][1])
        return jnp.where(valid, (-slope * d * log2e) * jrel_f, -jnp.inf)

    def put(g, row_slice, o, lse):
        o_sc[g, row_slice, :] = o
        l_sc[g, row_slice, :] = jnp.broadcast_to(lse, (blk, HEAD_DIM))

    def run_blocks(g, blocks):
        scores = [_dot_t(q, k) * scale + bias for q, k, _, bias, _ in blocks]
        stats = []
        for s in scores:
            m = jnp.max(s, axis=-1, keepdims=True)
            p = jnp.exp2(s - m)
            stats.append((m, p, jnp.sum(p, axis=-1, keepdims=True)))
        outs = [_dot(p.astype(jnp.bfloat16), blkdef[2]) / den
                for (m, p, den), blkdef in zip(stats, blocks)]
        for o, (m, p, den), blkdef in zip(outs, stats, blocks):
            put(g, blkdef[4], o, m + jnp.log2(den))

    bias0 = bias_for(0)
    prev_cols = kj < blk

    def g0_body(it, carry):
        blocks = []
        for k in range(ATT_UNROLL):
            n = it * ATT_UNROLL + k
            q0 = pl.multiple_of(n * blk, blk)
            p0 = pl.multiple_of(jnp.maximum(n - 1, 0) * blk, blk)
            ctx = lambda ref, p0=p0, q0=q0: jnp.concatenate(
                [ref[pl.ds(p0, blk), :], ref[pl.ds(q0, blk), :]], axis=0)
            bias = jnp.where(prev_cols & (n == 0), -jnp.inf, bias0)
            blocks.append((q0_ref[pl.ds(q0, blk), :], ctx(k0_ref), ctx(v0_ref), bias, pl.ds(q0, blk)))
        run_blocks(0, blocks)
        return carry

    lax.fori_loop(0, SEQ // blk // ATT_UNROLL, g0_body, 0)

    d1 = DIL_PATTERNS[1][1]
    cls1 = HALF // d1
    per_half = cls1 // blk
    bias1 = bias_for(1)

    def row1(r, n):
        return (n // per_half) * HALF + r * cls1 + (n % per_half) * blk

    nblk1 = SEQ // d1 // blk
    cls_per_trip = ATT_UNROLL // nblk1

    def g1_body(it, carry):
        blocks = []
        for c in range(cls_per_trip):
            r = it * cls_per_trip + c
            for n in range(nblk1):
                cur = pl.multiple_of(row1(r, n), blk)
                q = q1_ref[pl.ds(cur, blk), :]
                dst = pl.ds(n * blk * d1 + r, blk, stride=d1)
                if n == 0:
                    blocks.append((q, k1_ref[pl.ds(cur, blk), :], v1_ref[pl.ds(cur, blk), :],
                                   bias1[:, blk:], dst))
                else:
                    prev = pl.multiple_of(row1(r, n - 1), blk)
                    kc = jnp.concatenate([k1_ref[pl.ds(prev, blk), :], k1_ref[pl.ds(cur, blk), :]], axis=0)
                    vc = jnp.concatenate([v1_ref[pl.ds(prev, blk), :], v1_ref[pl.ds(cur, blk), :]], axis=0)
                    blocks.append((q, kc, vc, bias1, dst))
        run_blocks(1, blocks)
        return carry

    lax.fori_loop(0, d1 // cls_per_trip, g1_body, 0)

    d2 = DIL_PATTERNS[2][1]
    cls2 = HALF // d2
    bias2 = bias_for(2)

    def g2_body(it, carry):
        blocks = []
        for k in range(ATT_UNROLL):
            r = it * ATT_UNROLL + k
            a = pl.multiple_of(r * cls2, cls2)
            b = pl.multiple_of(HALF + r * cls2, cls2)
            cat = lambda ref, a=a, b=b: jnp.concatenate(
                [ref[pl.ds(a, cls2), :], ref[pl.ds(b, cls2), :]], axis=0)
            blocks.append((cat(q2_ref), cat(k2_ref), cat(v2_ref), bias2[:, blk:],
                           pl.ds(r, blk, stride=d2)))
        run_blocks(2, blocks)
        return carry

    lax.fori_loop(0, d2 // ATT_UNROLL, g2_body, 0)

    for t in range(SEQ // MIX_ROWS):
        sl = pl.ds(t * MIX_ROWS, MIX_ROWS)
        l0, l1, l2 = l_sc[0, sl, :], l_sc[1, sl, :], l_sc[2, sl, :]
        m = jnp.maximum(jnp.maximum(l0, l1), l2)
        e0, e1, e2 = jnp.exp2(l0 - m), jnp.exp2(l1 - m), jnp.exp2(l2 - m)
        mix = (e0 * o_sc[0, sl, :] + e1 * o_sc[1, sl, :] + e2 * o_sc[2, sl, :]) / (e0 + e1 + e2)
        o_ref[sl, :] = mix.astype(o_ref.dtype)


def _dil_branch(qkv, slopes):
    nq = DIL_DIM // HEAD_DIM

    def spec(section, g):
        return pl.BlockSpec((None, SEQ, HEAD_DIM),
                            lambda b, h, s_ref, o=section * nq + g * HEADS_PER_GROUP: (b, 0, o + h))

    grid_spec = pltpu.PrefetchScalarGridSpec(
        num_scalar_prefetch=1,
        grid=(BATCH, HEADS_PER_GROUP),
        in_specs=[spec(sec, g) for sec in range(3) for g in range(N_DIL_GROUPS)],
        out_specs=pl.BlockSpec((None, SEQ, HEAD_DIM), lambda b, h, s_ref: (b, 0, h)),
        scratch_shapes=[pltpu.VMEM((N_DIL_GROUPS, SEQ, HEAD_DIM), jnp.float32),
                        pltpu.VMEM((N_DIL_GROUPS, SEQ, HEAD_DIM), jnp.float32)],
    )
    return pl.pallas_call(
        _dilattn_kernel,
        grid_spec=grid_spec,
        out_shape=jax.ShapeDtypeStruct((BATCH, SEQ, DIL_OUT_DIM), jnp.bfloat16),
        compiler_params=_params(("parallel", "arbitrary")),
        name="dil_attn",
    )(slopes, *([qkv] * 9))


MERGE_TM = 1024
MERGE_TN = 512


def _merge_kernel(x_ref, wg0_ref, wg1_ref, wg2_ref, bg0_ref, bg1_ref, bg2_ref,
                  sc_ref, od_ref, om_ref, wout_ref, o_ref):
    r1, r2 = CONV_DIM, CONV_DIM + DIL_OUT_DIM
    x = x_ref[...]
    g0 = jax.nn.sigmoid(_dot(x, wg0_ref[...]) + bg0_ref[...])
    acc = g0 * _dot(sc_ref[...], wout_ref[0:r1, :])
    g1 = jax.nn.sigmoid(_dot(x, wg1_ref[...]) + bg1_ref[...])
    acc = acc + g1 * _dot(od_ref[...], wout_ref[r1:r2, :])
    g2 = jax.nn.sigmoid(_dot(x, wg2_ref[...]) + bg2_ref[...])
    acc = acc + g2 * _dot(om_ref[...], wout_ref[r2:r2 + MEM_DIM, :])
    o_ref[...] = acc.astype(o_ref.dtype)


def _merge(x2, w_in, b_in, s_conv, o_dil, o_mem, w_out):
    nb = lambda br: (OFF_GATE + br * D_MODEL) // MERGE_TN
    gspec = lambda br: pl.BlockSpec((D_MODEL, MERGE_TN), lambda i, n, o=nb(br): (0, o + n))
    bspec = lambda br: pl.BlockSpec((1, MERGE_TN), lambda i, n, o=nb(br): (0, o + n))
    act = lambda width: pl.BlockSpec((MERGE_TM, width), lambda i, n: (i, 0))
    wout = lambda width: pl.BlockSpec((width, MERGE_TN), lambda i, n: (0, n))
    return pl.pallas_call(
        _merge_kernel,
        grid=(TOKENS // MERGE_TM, D_MODEL // MERGE_TN),
        in_specs=[act(D_MODEL), gspec(0), gspec(1), gspec(2), bspec(0), bspec(1), bspec(2),
                  act(CONV_DIM), act(DIL_OUT_DIM), act(MEM_DIM),
                  wout(CONV_DIM + DIL_OUT_DIM + MEM_DIM)],
        out_specs=pl.BlockSpec((MERGE_TM, MERGE_TN), lambda i, n: (i, n)),
        out_shape=jax.ShapeDtypeStruct((TOKENS, D_MODEL), jnp.bfloat16),
        compiler_params=_params(("parallel", "arbitrary")),
        name="gated_merge",
    )(x2, w_in, w_in, w_in, b_in, b_in, b_in, s_conv, o_dil, o_mem, w_out)


OPROJ_TM = 512


OPROJ_PARTS = 4


def _oproj_kernel(x_ref, m_ref, wo_ref, g_ref, b_ref, wr_ref, br_ref, x1_ref, logit_ref):
    rows = OPROJ_TM // OPROJ_PARTS
    parts = [pl.ds(i * rows, rows) for i in range(OPROJ_PARTS)]
    proj = [_dot(m_ref[p, :], wo_ref[...]) for p in parts]
    for p, y in zip(parts, proj):
        x1 = _layer_norm(ALPHA * x_ref[p, :] + y, g_ref[...], b_ref[...])
        x1_ref[p, :] = x1
        logit_ref[p, :] = _dot(x1, wr_ref[...]) + br_ref[...]


def _oproj(x2, merged, w_o, g, b, w_route, b_route):
    row = lambda width: pl.BlockSpec((OPROJ_TM, width), lambda i: (i, 0))
    full = lambda r, c: pl.BlockSpec((r, c), lambda i: (0, 0))
    return pl.pallas_call(
        _oproj_kernel,
        grid=(TOKENS // OPROJ_TM,),
        in_specs=[row(D_MODEL), row(D_MODEL), full(D_MODEL, D_MODEL), full(1, D_MODEL), full(1, D_MODEL),
                  full(D_MODEL, LANES), full(1, LANES)],
        out_specs=[row(D_MODEL), row(LANES)],
        out_shape=[jax.ShapeDtypeStruct((TOKENS, D_MODEL), jnp.float32),
                   jax.ShapeDtypeStruct((TOKENS, LANES), jnp.float32)],
        compiler_params=_params(("parallel",)),
        name="oproj_ln1",
    )(x2, merged, w_o, g, b, w_route, b_route)


def _route_kernel(logit_ref, rec_ref, rect_ref, cnt_ref, carry_sc):
    @pl.when(pl.program_id(0) == 0)
    def _():
        carry_sc[...] = jnp.zeros_like(carry_sc)

    tm = ROUTE_TILE
    zt = jnp.transpose(logit_ref[...])
    row = lax.broadcasted_iota(jnp.int32, (SUBLANES, tm), 0)
    neg = -jnp.inf
    colmax = lambda a: jnp.max(a, axis=0, keepdims=True)
    first = lambda hit: jnp.min(jnp.where(hit, row, SUBLANES), axis=0, keepdims=True)

    glog = jnp.where(row < N_GROUPS, zt[0:SUBLANES], neg)
    gmax = colmax(glog)
    gsel = first(glog == gmax)
    gw = 1.0 / jnp.sum(jnp.exp(glog - gmax), axis=0, keepdims=True)

    elog = zt[ROUTER_LANE0:ROUTER_LANE0 + EXPERTS_PER_GROUP]
    for g in range(1, N_GROUPS):
        lo = ROUTER_LANE0 + g * EXPERTS_PER_GROUP
        elog = jnp.where(gsel == g, zt[lo:lo + EXPERTS_PER_GROUP], elog)
    top1 = colmax(elog)
    i1 = first(elog == top1)
    rest = jnp.where(row == i1, neg, elog)
    top2 = colmax(rest)
    i2 = first(rest == top2)
    t = jnp.exp(top2 - top1)
    w1 = gw / (1.0 + t)
    w2 = w1 * t

    hit = (row == i1) | (row == i2)
    onehot = jnp.concatenate([jnp.where(hit & (gsel == g), 1.0, 0.0) for g in range(N_GROUPS)], axis=0)
    chunks = tm // LANES
    stacked = jnp.concatenate([onehot[:, c * LANES:(c + 1) * LANES] for c in range(chunks)], axis=0)
    ri = lax.broadcasted_iota(jnp.int32, (LANES, LANES), 0)
    ci = lax.broadcasted_iota(jnp.int32, (LANES, LANES), 1)
    upper = jnp.where(ri <= ci, 1.0, 0.0).astype(jnp.bfloat16)
    within = _dot(stacked.astype(jnp.bfloat16), upper)
    run = carry_sc[:, 0:1]
    pieces = []
    for c in range(chunks):
        cum = within[c * N_EXPERTS:(c + 1) * N_EXPERTS, :] + run
        pieces.append(cum)
        run = cum[:, LANES - 1:LANES]
    before = jnp.concatenate(pieces, axis=1) - 1.0
    carry_sc[...] = jnp.broadcast_to(run, carry_sc.shape)
    cnt_ref[...] = jnp.broadcast_to(run, cnt_ref.shape)

    rank_a = jnp.zeros((1, tm), jnp.float32)
    rank_b = jnp.zeros((1, tm), jnp.float32)
    for g in range(N_GROUPS):
        b8 = before[g * EXPERTS_PER_GROUP:(g + 1) * EXPERTS_PER_GROUP, :]
        mine = gsel == g
        rank_a = rank_a + jnp.sum(jnp.where(mine & (row == i1), b8, 0.0), axis=0, keepdims=True)
        rank_b = rank_b + jnp.sum(jnp.where(mine & (row == i2), b8, 0.0), axis=0, keepdims=True)

    base = (gsel * EXPERTS_PER_GROUP).astype(jnp.float32)
    vals = [base + i1.astype(jnp.float32), base + i2.astype(jnp.float32), rank_a, rank_b, w1, w2]
    rect = jnp.zeros((SUBLANES, tm), jnp.float32)
    for k, val in enumerate(vals):
        rect = jnp.where(row == k, val, rect)
    rect_ref[...] = rect
    padded = jnp.concatenate([rect, jnp.zeros((LANES - SUBLANES, tm), jnp.float32)], axis=0)
    rec_ref[...] = jnp.transpose(padded)


def _route(logits):
    return pl.pallas_call(
        _route_kernel,
        grid=(TOKENS // ROUTE_TILE,),
        in_specs=[pl.BlockSpec((ROUTE_TILE, LANES), lambda i: (i, 0))],
        out_specs=[pl.BlockSpec((ROUTE_TILE, LANES), lambda i: (i, 0)),
                   pl.BlockSpec((SUBLANES, ROUTE_TILE), lambda i: (0, i)),
                   pl.BlockSpec((N_EXPERTS, LANES), lambda i: (0, 0))],
        out_shape=[jax.ShapeDtypeStruct((TOKENS, LANES), jnp.float32),
                   jax.ShapeDtypeStruct((SUBLANES, TOKENS), jnp.float32),
                   jax.ShapeDtypeStruct((N_EXPERTS, LANES), jnp.float32)],
        scratch_shapes=[pltpu.VMEM((N_EXPERTS, LANES), jnp.float32)],
        compiler_params=_params(("arbitrary",)),
        name="route",
    )(logits)


def _slots_kernel(start_ref, rect_ref, pos_ref):
    r = rect_ref[...]
    row = lax.broadcasted_iota(jnp.int32, r.shape, 0)

    def pos_of(e, rank):
        e = e.astype(jnp.int32)
        start = jnp.zeros(e.shape, jnp.int32)
        for k in range(N_EXPERTS):
            start = jnp.where(e == k, start_ref[k], start)
        return start + rank.astype(jnp.int32)

    pos_a = pos_of(r[0:1], r[2:3])
    pos_b = pos_of(r[1:2], r[3:4])
    pos_ref[...] = jnp.where(row == 0, pos_a, jnp.where(row == 1, pos_b, 0))


def _slots(starts, rect):
    grid_spec = pltpu.PrefetchScalarGridSpec(
        num_scalar_prefetch=1,
        grid=(TOKENS // ROUTE_TILE,),
        in_specs=[pl.BlockSpec((SUBLANES, ROUTE_TILE), lambda i, st: (0, i))],
        out_specs=pl.BlockSpec((SUBLANES, ROUTE_TILE), lambda i, st: (0, i)),
    )
    return pl.pallas_call(
        _slots_kernel,
        grid_spec=grid_spec,
        out_shape=jax.ShapeDtypeStruct((SUBLANES, TOKENS), jnp.int32),
        compiler_params=_params(("parallel",)),
        name="slots",
    )(starts, rect)


DISPATCH_TILE = 512
SORTED_ROWS = N_EXPERT_TILES * EXPERT_TILE


def _dispatch_kernel(pad_start_ref, pad_n_ref, nvalid_ref, pos_ref, x_ref, xs_hbm, zero_sc, stage_sc, sem,
                     row_sem):
    groups = DISPATCH_TILE // SUBLANES

    @pl.when(pl.program_id(0) == 0)
    def _():
        zero_sc[...] = jnp.zeros_like(zero_sc)

        def tile_copy(t):
            return pltpu.make_async_copy(zero_sc, xs_hbm.at[pl.ds(pl.multiple_of(t * EXPERT_TILE, EXPERT_TILE),
                                                                  EXPERT_TILE)], sem.at[2])

        def start_tile(t, c):
            tile_copy(t).start()
            return c

        def wait_tile(t, c):
            tile_copy(t).wait()
            return c
        lax.fori_loop(nvalid_ref[0], N_EXPERT_TILES, start_tile, 0)
        lax.fori_loop(nvalid_ref[0], N_EXPERT_TILES, wait_tile, 0)

        def pad_copies(e, act):
            n = pad_n_ref[e]
            cur = pad_start_ref[e]
            for bit in range((EXPERT_TILE - 1).bit_length()):
                size = 1 << bit
                has = lax.bitwise_and(n, size)

                @pl.when(has != 0)
                def _(cur=cur, size=size):
                    at = pl.multiple_of(cur, math.gcd(size, EXPERT_TILE))
                    act(pltpu.make_async_copy(zero_sc.at[pl.ds(0, size)], xs_hbm.at[pl.ds(at, size)],
                                              sem.at[1]))
                cur = cur + has

        def start_pads(e, c):
            pad_copies(e, lambda cp: cp.start())
            return c

        def wait_pads(e, c):
            pad_copies(e, lambda cp: cp.wait())
            return c
        lax.fori_loop(0, N_EXPERTS, start_pads, 0)
        lax.fori_loop(0, N_EXPERTS, wait_pads, 0)

    j = pl.program_id(0)
    slot = lax.rem(j, 2)
    stage_sc[slot] = x_ref[...].reshape(groups, SUBLANES, D_MODEL)

    def body(gi, c):
        for s in range(SUBLANES):
            for k in range(2):
                p = pos_ref[0, 0, k * DISPATCH_TILE + gi * SUBLANES + s]
                pltpu.make_async_copy(stage_sc.at[slot, gi, pl.ds(s, 1)], xs_hbm.at[pl.ds(p, 1)],
                                      row_sem.at[slot]).start(priority=k)
        return c
    lax.fori_loop(0, groups, body, 0)

    def wait_slot(sl):
        for _ in range(2 * groups):
            pltpu.make_async_copy(stage_sc.at[sl, 0], xs_hbm.at[pl.ds(0, SUBLANES)], row_sem.at[sl]).wait()

    @pl.when(j > 0)
    def _():
        wait_slot(1 - slot)

    @pl.when(j == pl.num_programs(0) - 1)
    def _():
        wait_slot(slot)


def _dispatch(pad_start, pad_n, n_valid_tiles, pos3, x1):
    nt = TOKENS // DISPATCH_TILE
    groups = DISPATCH_TILE // SUBLANES
    grid_spec = pltpu.PrefetchScalarGridSpec(
        num_scalar_prefetch=3,
        grid=(nt,),
        in_specs=[pl.BlockSpec((1, 1, 2 * DISPATCH_TILE), lambda j, *_: (j, 0, 0), memory_space=pltpu.SMEM),
                  pl.BlockSpec((DISPATCH_TILE, D_MODEL), lambda j, *_: (j, 0))],
        out_specs=pl.BlockSpec(memory_space=pl.ANY),
        scratch_shapes=[pltpu.VMEM((EXPERT_TILE, D_MODEL), jnp.float32),
                        pltpu.VMEM((2, groups, SUBLANES, D_MODEL), jnp.float32),
                        pltpu.SemaphoreType.DMA((3,)),
                        pltpu.SemaphoreType.DMA((2,))],
    )
    return pl.pallas_call(
        _dispatch_kernel,
        grid_spec=grid_spec,
        out_shape=jax.ShapeDtypeStruct((SORTED_ROWS, D_MODEL), jnp.float32),
        compiler_params=_params(("arbitrary",)),
        name="dispatch",
    )(pad_start, pad_n, n_valid_tiles, pos3, x1)


def _ffn_kernel(texp_ref, tvalid_ref, tslot_ref, tfirst_ref, tnext_ref, nvalid_ref,
                xs_ref, wg_hbm, wu_hbm, wd_hbm, y_ref, wg_buf, wu_buf, wd_buf, sem):
    j = pl.program_id(0)
    valid = tvalid_ref[j] == 1
    slot = tslot_ref[j]

    def weight_copies(e, s):
        return (pltpu.make_async_copy(wg_hbm.at[e], wg_buf.at[s], sem.at[s, 0]),
                pltpu.make_async_copy(wu_hbm.at[e], wu_buf.at[s], sem.at[s, 1]),
                pltpu.make_async_copy(wd_hbm.at[e], wd_buf.at[s], sem.at[s, 2]))

    @pl.when(valid & (tfirst_ref[j] == 1))
    def _():
        @pl.when(j == 0)
        def _():
            for cp in weight_copies(texp_ref[j], slot):
                cp.start()

        for cp in weight_copies(texp_ref[j], slot):
            cp.wait()

        @pl.when(tnext_ref[j] >= 0)
        def _():
            for cp in weight_copies(tnext_ref[j], 1 - slot):
                cp.start(priority=1)

    @pl.when(valid)
    def _():
        xt = xs_ref[...]
        a = _dot(xt, wg_buf[slot])
        u = _dot(xt, wu_buf[slot])
        hid = (a * jax.nn.sigmoid(a)) * u
        y_ref[...] = _dot(hid, wd_buf[slot])

    @pl.when(jnp.logical_not(valid))
    def _():
        y_ref[...] = jnp.zeros_like(y_ref)


def _expert_ffn(tables, xs, w_gate, w_up, w_down):
    nt = N_EXPERT_TILES
    xmap = lambda j, te, tv, ts, tf, tn, nv: (jnp.minimum(j, nv[0] - 1), 0)
    grid_spec = pltpu.PrefetchScalarGridSpec(
        num_scalar_prefetch=6,
        grid=(nt,),
        in_specs=[pl.BlockSpec((EXPERT_TILE, D_MODEL), xmap),
                  pl.BlockSpec(memory_space=pl.ANY),
                  pl.BlockSpec(memory_space=pl.ANY),
                  pl.BlockSpec(memory_space=pl.ANY)],
        out_specs=pl.BlockSpec((EXPERT_TILE, D_MODEL), lambda j, *_: (j, 0)),
        scratch_shapes=[pltpu.VMEM((2, D_MODEL, D_EXPERT), jnp.float32),
                        pltpu.VMEM((2, D_MODEL, D_EXPERT), jnp.float32),
                        pltpu.VMEM((2, D_EXPERT, D_MODEL), jnp.float32),
                        pltpu.SemaphoreType.DMA((2, 3))],
    )
    return pl.pallas_call(
        _ffn_kernel,
        grid_spec=grid_spec,
        out_shape=jax.ShapeDtypeStruct((nt * EXPERT_TILE, D_MODEL), jnp.float32),
        compiler_params=_params(("arbitrary",)),
        name="expert_ffn",
    )(*tables, xs, w_gate, w_up, w_down)


def _combine_kernel(pos_cur_ref, pos_next_ref, r_ref, x1_ref, y_hbm, g_ref, b_ref, o_ref, ybuf, sem):
    j = pl.program_id(0)
    nt = pl.num_programs(0)
    slot = lax.rem(j, 2)
    groups = COMBINE_TILE // SUBLANES

    def gather(pos_ref, dst_slot):
        def body(gi, c):
            for s in range(SUBLANES):
                for k in range(2):
                    p = pos_ref[0, 0, k * COMBINE_TILE + gi * SUBLANES + s]
                    pltpu.make_async_copy(y_hbm.at[pl.ds(p, 1)], ybuf.at[dst_slot, k, gi, pl.ds(s, 1)],
                                          sem.at[dst_slot]).start(priority=k)
            return c
        lax.fori_loop(0, groups, body, 0)

    @pl.when(j == 0)
    def _():
        gather(pos_cur_ref, 0)

    @pl.when(j + 1 < nt)
    def _():
        gather(pos_next_ref, 1 - slot)

    for _ in range(2 * groups):
        pltpu.make_async_copy(y_hbm.at[pl.ds(0, SUBLANES)], ybuf.at[slot, 0, 0], sem.at[slot]).wait()
    r = r_ref[...]
    ya = ybuf[slot, 0].reshape(COMBINE_TILE, D_MODEL)
    yb = ybuf[slot, 1].reshape(COMBINE_TILE, D_MODEL)
    hres = ALPHA * x1_ref[...] + r[:, 4:5] * ya + r[:, 5:6] * yb
    o_ref[...] = _layer_norm(hres, g_ref[...], b_ref[...])


def _combine(pos3, routing, x1, y_sorted, g, b):
    nt = TOKENS // COMBINE_TILE
    groups = COMBINE_TILE // SUBLANES
    return pl.pallas_call(
        _combine_kernel,
        grid=(nt,),
        in_specs=[pl.BlockSpec((1, 1, 2 * COMBINE_TILE), lambda j: (j, 0, 0), memory_space=pltpu.SMEM),
                  pl.BlockSpec((1, 1, 2 * COMBINE_TILE), lambda j: (jnp.minimum(j + 1, nt - 1), 0, 0),
                               memory_space=pltpu.SMEM),
                  pl.BlockSpec((COMBINE_TILE, LANES), lambda j: (j, 0)),
                  pl.BlockSpec((COMBINE_TILE, D_MODEL), lambda j: (j, 0)),
                  pl.BlockSpec(memory_space=pl.ANY),
                  pl.BlockSpec((1, D_MODEL), lambda j: (0, 0)),
                  pl.BlockSpec((1, D_MODEL), lambda j: (0, 0))],
        out_specs=pl.BlockSpec((COMBINE_TILE, D_MODEL), lambda j: (j, 0)),
        out_shape=jax.ShapeDtypeStruct((TOKENS, D_MODEL), jnp.float32),
        scratch_shapes=[pltpu.VMEM((2, 2, groups, SUBLANES, D_MODEL), jnp.float32),
                        pltpu.SemaphoreType.DMA((2,))],
        compiler_params=_params(("arbitrary",)),
        name="combine_ln2",
    )(pos3, pos3, routing, x1, y_sorted, g, b)


def _alibi_slopes():
    n = N_DIL_GROUPS * HEADS_PER_GROUP
    return jnp.asarray(2.0 ** (-ALIBI_MAX * np.arange(1, n + 1, dtype=np.float32) / n), jnp.float32)


def kernel(x, mem, ln_mem_g, ln_mem_b, w_in, b_in, w_conv, w_conv_out, w_dil_out, w_mem_kv, w_mem_out, w_o, ln1_g, ln1_b, w_group, b_group, w_router, b_router, w_gate, w_up, w_down, ln2_g, ln2_b):
    assert x.shape == (BATCH, SEQ, D_MODEL) and w_in.shape == (1, D_MODEL, IN_DIM)
    bf16 = jnp.bfloat16
    row = lambda v: v.reshape(1, -1)
    w_in2 = w_in[0].astype(bf16)
    b_in2 = b_in

    kv = _memkv(mem, row(ln_mem_g), row(ln_mem_b), w_mem_kv[0])
    s_conv, xb = _conv_branch(x, w_in2, b_in2, w_conv[0])
    qkv = _qkv_proj(xb, w_in2, b_in2)
    o_mem = _mem_branch(xb, w_in2, b_in2, kv)
    o_dil = _dil_branch(qkv, _alibi_slopes())

    x2 = x.reshape(TOKENS, D_MODEL)
    merged = _merge(xb.reshape(TOKENS, D_MODEL), w_in2, b_in2,
                    s_conv.reshape(TOKENS, CONV_DIM), o_dil.reshape(TOKENS, DIL_OUT_DIM),
                    o_mem.reshape(TOKENS, MEM_DIM),
                    jnp.concatenate([w_conv_out[0], w_dil_out[0], w_mem_out[0]], axis=0).astype(bf16))

    gap = ROUTER_LANE0 - N_GROUPS
    tail = LANES - ROUTER_LANE0 - N_EXPERTS
    w_route = jnp.concatenate(
        [w_group[0], jnp.zeros((D_MODEL, gap), jnp.float32),
         jnp.transpose(w_router[0], (1, 0, 2)).reshape(D_MODEL, N_EXPERTS),
         jnp.zeros((D_MODEL, tail), jnp.float32)], axis=1)
    b_route = jnp.concatenate(
        [b_group[0], jnp.zeros((gap,), jnp.float32), b_router[0].reshape(N_EXPERTS),
         jnp.zeros((tail,), jnp.float32)]).reshape(1, LANES)
    x1, logits = _oproj(x2, merged, w_o[0].astype(bf16), ln1_g, ln1_b, w_route, b_route)

    routing, routing_t, counts_b = _route(logits)

    i32 = jnp.int32
    counts = counts_b[:, 0].astype(i32)
    padded = ((counts + EXPERT_TILE - 1) // EXPERT_TILE) * EXPERT_TILE
    ends = jnp.cumsum(padded)
    starts = ends - padded
    pos = _slots(starts.astype(i32), routing_t)[0:2]

    def pos_tiles(tile):
        return jnp.transpose(pos.reshape(2, TOKENS // tile, tile), (1, 0, 2)).reshape(TOKENS // tile, 1, 2 * tile)

    tile_start = jnp.arange(N_EXPERT_TILES, dtype=i32) * EXPERT_TILE
    tile_expert = jnp.minimum(jnp.sum((ends[None, :] <= tile_start[:, None]).astype(i32), axis=1),
                              N_EXPERTS - 1)
    tile_valid = tile_start < ends[-1]
    prev_expert = jnp.concatenate([jnp.full((1,), -1, i32), tile_expert[:-1]])
    tile_first = tile_valid & (tile_expert != prev_expert)
    tile_slot = (jnp.cumsum(tile_first.astype(i32)) - 1) & 1
    big = N_EXPERTS
    idx = jnp.where(counts > 0, jnp.arange(N_EXPERTS, dtype=i32), big)
    later = jnp.concatenate([lax.cummin(idx[::-1])[::-1][1:], jnp.full((1,), big, i32)])
    next_used = jnp.where(later == big, -1, later)
    n_valid_tiles = (ends[-1:] // EXPERT_TILE).astype(i32)
    tables = (tile_expert, tile_valid.astype(i32), tile_slot, tile_first.astype(i32),
              next_used[tile_expert], n_valid_tiles)

    xs = _dispatch(starts + counts, padded - counts, n_valid_tiles,
                   pos_tiles(DISPATCH_TILE), x1)
    y_sorted = _expert_ffn(tables, xs,
                           w_gate.reshape(N_EXPERTS, D_MODEL, D_EXPERT),
                           w_up.reshape(N_EXPERTS, D_MODEL, D_EXPERT),
                           w_down.reshape(N_EXPERTS, D_EXPERT, D_MODEL))
    out = _combine(pos_tiles(COMBINE_TILE), routing, x1, y_sorted,
                   ln2_g, ln2_b)
    return out.reshape(BATCH, SEQ, D_MODEL)
```

```python
import math

import numpy as np
import jax
import jax.numpy as jnp
from jax import lax
from jax.experimental import pallas as pl
from jax.experimental.pallas import tpu as pltpu

D_MODEL = 2048
BATCH = 8
SEQ = 2048
TOKENS = BATCH * SEQ
CONV_DIM = 1024
CONV_WIDTH = 3
DIL_PATTERNS = ((128, 1), (512, 4), (2048, 16))
N_DIL_GROUPS = 3
HEADS_PER_GROUP = 4
HEAD_DIM = 128
DIL_DIM = N_DIL_GROUPS * HEADS_PER_GROUP * HEAD_DIM
DIL_OUT_DIM = HEADS_PER_GROUP * HEAD_DIM
ATT_BLOCK = 128
ALIBI_MAX = 8.0
MEM_LEN = 256
MEM_HEADS = 4
MEM_HEAD_DIM = 256
MEM_DIM = MEM_HEADS * MEM_HEAD_DIM
N_BRANCHES = 3
IN_DIM = 3 * CONV_DIM + 3 * DIL_DIM + MEM_DIM + N_BRANCHES * D_MODEL
N_GROUPS = 4
EXPERTS_PER_GROUP = 8
N_EXPERTS = N_GROUPS * EXPERTS_PER_GROUP
D_EXPERT = 512
ALPHA = 2.0 ** 0.25
LN_EPS = 1e-5

OFF_CB = 0
OFF_CC = CONV_DIM
OFF_CH = 2 * CONV_DIM
OFF_Q = 3 * CONV_DIM
OFF_MQ = OFF_Q + 3 * DIL_DIM
OFF_GATE = OFF_MQ + MEM_DIM

LANES = 128
SUBLANES = 8
HALF = SEQ // 2
VMEM_LIMIT = 56 * 1024 * 1024

ROUTE_TILE = 2048
EXPERT_TILE = 320
N_EXPERT_TILES = 2 * TOKENS // EXPERT_TILE + N_EXPERTS
COMBINE_TILE = 256
ROUTER_LANE0 = 8


def _params(sem, limit=VMEM_LIMIT):
    return pltpu.CompilerParams(dimension_semantics=sem, vmem_limit_bytes=limit)


def _layer_norm(x, g, b):
    mu = jnp.mean(x, axis=-1, keepdims=True)
    xc = x - mu
    var = jnp.mean(xc * xc, axis=-1, keepdims=True)
    return xc * lax.rsqrt(var + LN_EPS) * g + b


def _dot(a, b):
    return jnp.dot(a, b, preferred_element_type=jnp.float32)


def _dot_t(a, b):
    return lax.dot_general(a, b, (((1,), (1,)), ((), ())), preferred_element_type=jnp.float32)


MEMKV_BATCHES = 2


def _memkv_kernel(mem_ref, g_ref, b_ref, w_ref, kv_ref):
    rows = MEMKV_BATCHES * MEM_LEN
    y = _layer_norm(mem_ref[...].reshape(rows, D_MODEL), g_ref[...], b_ref[...])
    kv = _dot(y, w_ref[...])
    kv_ref[...] = kv.astype(kv_ref.dtype).reshape(MEMKV_BATCHES, MEM_LEN, 2 * MEM_DIM)


def _memkv(mem, g, b, w):
    return pl.pallas_call(
        _memkv_kernel,
        grid=(BATCH // MEMKV_BATCHES,),
        in_specs=[pl.BlockSpec((MEMKV_BATCHES, MEM_LEN, D_MODEL), lambda i: (i, 0, 0)),
                  pl.BlockSpec((1, D_MODEL), lambda i: (0, 0)),
                  pl.BlockSpec((1, D_MODEL), lambda i: (0, 0)),
                  pl.BlockSpec((D_MODEL, 2 * MEM_DIM), lambda i: (0, 0))],
        out_specs=pl.BlockSpec((MEMKV_BATCHES, MEM_LEN, 2 * MEM_DIM), lambda i: (i, 0, 0)),
        out_shape=jax.ShapeDtypeStruct((BATCH, MEM_LEN, 2 * MEM_DIM), jnp.bfloat16),
        compiler_params=_params(("parallel",)),
        name="mem_kv",
    )(mem, g, b, w)


CONV_TC = 512


def _conv_kernel(x_ref, wb_ref, wc_ref, wh_ref, bb_ref, bc_ref, bh_ref, wconv_ref, s_ref, xb_ref,
                 u_sc, carry_sc):
    half = pl.program_id(1)
    c = pl.program_id(2)

    @pl.when(c == 0)
    def _():
        xb_ref[...] = x_ref[...].astype(xb_ref.dtype)

    x = xb_ref[...]
    cc = _dot(x, wc_ref[...]) + bc_ref[...]
    ch = _dot(x, wh_ref[...]) + bh_ref[...]
    u = cc * ch
    pre = SUBLANES
    u_sc[0:pre, :] = jnp.where(half == 0, 0.0, carry_sc[c])
    u_sc[pre:pre + HALF, :] = u
    carry_sc[c] = u[HALF - pre:HALF, :]
    wconv = wconv_ref[...]
    y = wconv[CONV_WIDTH - 1:CONV_WIDTH, :] * u
    for back in range(1, CONV_WIDTH):
        tap = CONV_WIDTH - 1 - back
        y = y + wconv[tap:tap + 1, :] * u_sc[pre - back:pre - back + HALF, :]
    cb = _dot(x, wb_ref[...]) + bb_ref[...]
    s_ref[...] = (cb * y).astype(s_ref.dtype)


def _conv_branch(x, w_in, b_in, w_conv):
    nb = lambda off: off // CONV_TC
    wspec = lambda off: pl.BlockSpec((D_MODEL, CONV_TC), lambda b, h, c, o=nb(off): (0, o + c))
    bspec = lambda off: pl.BlockSpec((1, CONV_TC), lambda b, h, c, o=nb(off): (0, o + c))
    return pl.pallas_call(
        _conv_kernel,
        grid=(BATCH, 2, CONV_DIM // CONV_TC),
        in_specs=[pl.BlockSpec((None, HALF, D_MODEL), lambda b, h, c: (b, h, 0)),
                  wspec(OFF_CB), wspec(OFF_CC), wspec(OFF_CH),
                  bspec(OFF_CB), bspec(OFF_CC), bspec(OFF_CH),
                  pl.BlockSpec((CONV_WIDTH, CONV_TC), lambda b, h, c: (0, c))],
        out_specs=[pl.BlockSpec((None, HALF, CONV_TC), lambda b, h, c: (b, h, c)),
                   pl.BlockSpec((None, HALF, D_MODEL), lambda b, h, c: (b, h, 0))],
        out_shape=[jax.ShapeDtypeStruct((BATCH, SEQ, CONV_DIM), jnp.bfloat16),
                   jax.ShapeDtypeStruct((BATCH, SEQ, D_MODEL), jnp.bfloat16)],
        scratch_shapes=[pltpu.VMEM((HALF + SUBLANES, CONV_TC), jnp.float32),
                        pltpu.VMEM((CONV_DIM // CONV_TC, SUBLANES, CONV_TC), jnp.float32)],
        compiler_params=_params(("arbitrary", "arbitrary", "arbitrary")),
        name="conv_branch",
    )(x, w_in, w_in, w_in, b_in, b_in, b_in, w_conv)


QKV_TN = 512
QKV_CHUNKS = QKV_TN // LANES


def _qkv_kernel(x_ref, w_ref, b_ref, o_ref, sc_ref, sc2_ref):
    x = x_ref[...]
    for gi in (2, 1, 0):
        cols = slice(gi * QKV_TN, (gi + 1) * QKV_TN)
        acc = _dot(x, w_ref[:, cols]) + b_ref[:, cols]
        d = DIL_PATTERNS[gi][1]
        if d == 1:
            o_ref[:, cols] = acc.astype(o_ref.dtype)
            continue
        rows = HALF // d
        for c in range(QKV_CHUNKS):
            sc_ref[gi - 1, c] = acc[:, c * LANES:(c + 1) * LANES]
        if d == 16:
            q4 = HALF // 4
            for c in range(QKV_CHUNKS):
                for r in range(4):
                    sc2_ref[c, r * q4:(r + 1) * q4, :] = sc_ref[gi - 1, c, pl.ds(r, q4, stride=4), :]
            for c in range(QKV_CHUNKS):
                lo = gi * QKV_TN + c * LANES
                for r in range(d):
                    r_lo, r_hi = r % 4, r // 4
                    o_ref[r * rows:(r + 1) * rows, lo:lo + LANES] = (
                        sc2_ref[c, pl.ds(r_lo * q4 + r_hi, rows, stride=4), :].astype(o_ref.dtype))
            continue
        for c in range(QKV_CHUNKS):
            for r in range(d):
                lo = gi * QKV_TN + c * LANES
                o_ref[r * rows:(r + 1) * rows, lo:lo + LANES] = (
                    sc_ref[gi - 1, c, pl.ds(r, rows, stride=d), :].astype(o_ref.dtype))


def _qkv_proj(xb, w_in, b_in):
    n0 = OFF_Q // DIL_DIM
    return pl.pallas_call(
        _qkv_kernel,
        grid=(BATCH, 2, 3),
        in_specs=[pl.BlockSpec((None, HALF, D_MODEL), lambda b, h, n: (b, h, 0)),
                  pl.BlockSpec((D_MODEL, DIL_DIM), lambda b, h, n: (0, n0 + n)),
                  pl.BlockSpec((1, DIL_DIM), lambda b, h, n: (0, n0 + n))],
        out_specs=pl.BlockSpec((None, HALF, DIL_DIM), lambda b, h, n: (b, h, n)),
        out_shape=jax.ShapeDtypeStruct((BATCH, SEQ, 3 * DIL_DIM), jnp.bfloat16),
        scratch_shapes=[pltpu.VMEM((N_DIL_GROUPS - 1, QKV_CHUNKS, HALF, LANES), jnp.float32),
                        pltpu.VMEM((QKV_CHUNKS, HALF, LANES), jnp.float32)],
        compiler_params=_params(("parallel", "parallel", "arbitrary")),
        name="qkv_proj",
    )(xb, w_in, b_in)


MEM_HEADS_PER_STEP = 2
MEM_TN = MEM_HEADS_PER_STEP * MEM_HEAD_DIM


def _memattn_kernel(x_ref, w_ref, b_ref, mk_ref, mv_ref, o_ref):
    mq_all = (_dot(x_ref[...], w_ref[...]) + b_ref[...]).astype(jnp.bfloat16)
    for hh in range(MEM_HEADS_PER_STEP):
        cols = slice(hh * MEM_HEAD_DIM, (hh + 1) * MEM_HEAD_DIM)
        s = _dot_t(mq_all[:, cols], mk_ref[:, cols]) * (MEM_HEAD_DIM ** -0.5 * math.log2(math.e))
        m = jnp.max(s, axis=-1, keepdims=True)
        p = jnp.exp2(s - m)
        den = jnp.sum(p, axis=-1, keepdims=True)
        o = _dot(p.astype(jnp.bfloat16), mv_ref[:, cols]) / den
        o_ref[:, cols] = o.astype(o_ref.dtype)


def _mem_branch(xb, w_in, b_in, kv):
    n0 = OFF_MQ // MEM_TN
    nv = MEM_DIM // MEM_TN
    return pl.pallas_call(
        _memattn_kernel,
        grid=(BATCH, 2, MEM_DIM // MEM_TN),
        in_specs=[pl.BlockSpec((None, HALF, D_MODEL), lambda b, h, n: (b, h, 0)),
                  pl.BlockSpec((D_MODEL, MEM_TN), lambda b, h, n: (0, n0 + n)),
                  pl.BlockSpec((1, MEM_TN), lambda b, h, n: (0, n0 + n)),
                  pl.BlockSpec((None, MEM_LEN, MEM_TN), lambda b, h, n: (b, 0, n)),
                  pl.BlockSpec((None, MEM_LEN, MEM_TN), lambda b, h, n: (b, 0, nv + n))],
        out_specs=pl.BlockSpec((None, HALF, MEM_TN), lambda b, h, n: (b, h, n)),
        out_shape=jax.ShapeDtypeStruct((BATCH, SEQ, MEM_DIM), jnp.bfloat16),
        compiler_params=_params(("parallel", "parallel", "arbitrary")),
        name="mem_branch",
    )(xb, w_in, b_in, kv, kv)


ATT_UNROLL = 16
MIX_ROWS = 256


def _dilattn_kernel(slopes_ref,
                    q0_ref, q1_ref, q2_ref, k0_ref, k1_ref, k2_ref, v0_ref, v1_ref, v2_ref,
                    o_ref, o_sc, l_sc):
    h = pl.program_id(1)
    blk = ATT_BLOCK
    log2e = math.log2(math.e)
    scale = HEAD_DIM ** -0.5 * log2e
    qi = lax.broadcasted_iota(jnp.int32, (blk, 2 * blk), 0) + blk
    kj = lax.broadcasted_iota(jnp.int32, (blk, 2 * blk), 1)
    jrel = qi - kj
    valid = (jrel >= 0) & (jrel <= blk)
    jrel_f = jrel.astype(jnp.float32)

    def bias_for(g):
        slope = slopes_ref[g * HEADS_PER_GROUP + h]
        d = float(DIL_PATTERNS[g][1])
        return jnp.where(valid, (-slope * d * log2e) * jrel_f, -jnp.inf)

    def put(g, row_slice, o, lse):
        o_sc[g, row_slice, :] = o
        l_sc[g, row_slice, :] = jnp.broadcast_to(lse, (blk, HEAD_DIM))

    def run_blocks(g, blocks):
        scores = [_dot_t(q, k) * scale + bias for q, k, _, bias, _ in blocks]
        stats = []
        for s in scores:
            m = jnp.max(s, axis=-1, keepdims=True)
            p = jnp.exp2(s - m)
            stats.append((m, p, jnp.sum(p, axis=-1, keepdims=True)))
        outs = [_dot(p.astype(jnp.bfloat16), blkdef[2]) / den
                for (m, p, den), blkdef in zip(stats, blocks)]
        for o, (m, p, den), blkdef in zip(outs, stats, blocks):
            put(g, blkdef[4], o, m + jnp.log2(den))

    bias0 = bias_for(0)
    prev_cols = kj < blk

    def g0_body(it, carry):
        blocks = []
        for k in range(ATT_UNROLL):
            n = it * ATT_UNROLL + k
            q0 = pl.multiple_of(n * blk, blk)
            p0 = pl.multiple_of(jnp.maximum(n - 1, 0) * blk, blk)
            ctx = lambda ref, p0=p0, q0=q0: jnp.concatenate(
                [ref[pl.ds(p0, blk), :], ref[pl.ds(q0, blk), :]], axis=0)
            bias = jnp.where(prev_cols & (n == 0), -jnp.inf, bias0)
            blocks.append((q0_ref[pl.ds(q0, blk), :], ctx(k0_ref), ctx(v0_ref), bias, pl.ds(q0, blk)))
        run_blocks(0, blocks)
        return carry

    lax.fori_loop(0, SEQ // blk // ATT_UNROLL, g0_body, 0)

    d1 = DIL_PATTERNS[1][1]
    cls1 = HALF // d1
    per_half = cls1 // blk
    bias1 = bias_for(1)

    def row1(r, n):
        return (n // per_half) * HALF + r * cls1 + (n % per_half) * blk

    nblk1 = SEQ // d1 // blk
    cls_per_trip = ATT_UNROLL // nblk1

    def g1_body(it, carry):
        blocks = []
        for c in range(cls_per_trip):
            r = it * cls_per_trip + c
            for n in range(nblk1):
                cur = pl.multiple_of(row1(r, n), blk)
                q = q1_ref[pl.ds(cur, blk), :]
                dst = pl.ds(n * blk * d1 + r, blk, stride=d1)
                if n == 0:
                    blocks.append((q, k1_ref[pl.ds(cur, blk), :], v1_ref[pl.ds(cur, blk), :],
                                   bias1[:, blk:], dst))
                else:
                    prev = pl.multiple_of(row1(r, n - 1), blk)
                    kc = jnp.concatenate([k1_ref[pl.ds(prev, blk), :], k1_ref[pl.ds(cur, blk), :]], axis=0)
                    vc = jnp.concatenate([v1_ref[pl.ds(prev, blk), :], v1_ref[pl.ds(cur, blk), :]], axis=0)
                    blocks.append((q, kc, vc, bias1, dst))
        run_blocks(1, blocks)
        return carry

    lax.fori_loop(0, d1 // cls_per_trip, g1_body, 0)

    d2 = DIL_PATTERNS[2][1]
    cls2 = HALF // d2
    bias2 = bias_for(2)

    def g2_body(it, carry):
        blocks = []
        for k in range(ATT_UNROLL):
            r = it * ATT_UNROLL + k
            a = pl.multiple_of(r * cls2, cls2)
            b = pl.multiple_of(HALF + r * cls2, cls2)
            cat = lambda ref, a=a, b=b: jnp.concatenate(
                [ref[pl.ds(a, cls2), :], ref[pl.ds(b, cls2), :]], axis=0)
            blocks.append((cat(q2_ref), cat(k2_ref), cat(v2_ref), bias2[:, blk:],
                           pl.ds(r, blk, stride=d2)))
        run_blocks(2, blocks)
        return carry

    lax.fori_loop(0, d2 // ATT_UNROLL, g2_body, 0)

    for t in range(SEQ // MIX_ROWS):
        sl = pl.ds(t * MIX_ROWS, MIX_ROWS)
        l0, l1, l2 = l_sc[0, sl, :], l_sc[1, sl, :], l_sc[2, sl, :]
        m = jnp.maximum(jnp.maximum(l0, l1), l2)
        e0, e1, e2 = jnp.exp2(l0 - m), jnp.exp2(l1 - m), jnp.exp2(l2 - m)
        mix = (e0 * o_sc[0, sl, :] + e1 * o_sc[1, sl, :] + e2 * o_sc[2, sl, :]) / (e0 + e1 + e2)
        o_ref[sl, :] = mix.astype(o_ref.dtype)


def _dil_branch(qkv, slopes):
    nq = DIL_DIM // HEAD_DIM

    def spec(section, g):
        return pl.BlockSpec((None, SEQ, HEAD_DIM),
                            lambda b, h, s_ref, o=section * nq + g * HEADS_PER_GROUP: (b, 0, o + h))

    grid_spec = pltpu.PrefetchScalarGridSpec(
        num_scalar_prefetch=1,
        grid=(BATCH, HEADS_PER_GROUP),
        in_specs=[spec(sec, g) for sec in range(3) for g in range(N_DIL_GROUPS)],
        out_specs=pl.BlockSpec((None, SEQ, HEAD_DIM), lambda b, h, s_ref: (b, 0, h)),
        scratch_shapes=[pltpu.VMEM((N_DIL_GROUPS, SEQ, HEAD_DIM), jnp.float32),
                        pltpu.VMEM((N_DIL_GROUPS, SEQ, HEAD_DIM), jnp.float32)],
    )
    return pl.pallas_call(
        _dilattn_kernel,
        grid_spec=grid_spec,
        out_shape=jax.ShapeDtypeStruct((BATCH, SEQ, DIL_OUT_DIM), jnp.bfloat16),
        compiler_params=_params(("parallel", "arbitrary")),
        name="dil_attn",
    )(slopes, *([qkv] * 9))


MERGE_TM = 1024
MERGE_TN = 512


def _merge_kernel(x_ref, wg0_ref, wg1_ref, wg2_ref, bg0_ref, bg1_ref, bg2_ref,
                  sc_ref, od_ref, om_ref, wout_ref, o_ref):
    r1, r2 = CONV_DIM, CONV_DIM + DIL_OUT_DIM
    x = x_ref[...]
    g0 = jax.nn.sigmoid(_dot(x, wg0_ref[...]) + bg0_ref[...])
    acc = g0 * _dot(sc_ref[...], wout_ref[0:r1, :])
    g1 = jax.nn.sigmoid(_dot(x, wg1_ref[...]) + bg1_ref[...])
    acc = acc + g1 * _dot(od_ref[...], wout_ref[r1:r2, :])
    g2 = jax.nn.sigmoid(_dot(x, wg2_ref[...]) + bg2_ref[...])
    acc = acc + g2 * _dot(om_ref[...], wout_ref[r2:r2 + MEM_DIM, :])
    o_ref[...] = acc.astype(o_ref.dtype)


def _merge(x2, w_in, b_in, s_conv, o_dil, o_mem, w_out):
    nb = lambda br: (OFF_GATE + br * D_MODEL) // MERGE_TN
    gspec = lambda br: pl.BlockSpec((D_MODEL, MERGE_TN), lambda i, n, o=nb(br): (0, o + n))
    bspec = lambda br: pl.BlockSpec((1, MERGE_TN), lambda i, n, o=nb(br): (0, o + n))
    act = lambda width: pl.BlockSpec((MERGE_TM, width), lambda i, n: (i, 0))
    wout = lambda width: pl.BlockSpec((width, MERGE_TN), lambda i, n: (0, n))
    return pl.pallas_call(
        _merge_kernel,
        grid=(TOKENS // MERGE_TM, D_MODEL // MERGE_TN),
        in_specs=[act(D_MODEL), gspec(0), gspec(1), gspec(2), bspec(0), bspec(1), bspec(2),
                  act(CONV_DIM), act(DIL_OUT_DIM), act(MEM_DIM),
                  wout(CONV_DIM + DIL_OUT_DIM + MEM_DIM)],
        out_specs=pl.BlockSpec((MERGE_TM, MERGE_TN), lambda i, n: (i, n)),
        out_shape=jax.ShapeDtypeStruct((TOKENS, D_MODEL), jnp.bfloat16),
        compiler_params=_params(("parallel", "arbitrary")),
        name="gated_merge",
    )(x2, w_in, w_in, w_in, b_in, b_in, b_in, s_conv, o_dil, o_mem, w_out)


OPROJ_TM = 512


OPROJ_PARTS = 4


def _oproj_kernel(x_ref, m_ref, wo_ref, g_ref, b_ref, wr_ref, br_ref, x1_ref, logit_ref):
    rows = OPROJ_TM // OPROJ_PARTS
    parts = [pl.ds(i * rows, rows) for i in range(OPROJ_PARTS)]
    proj = [_dot(m_ref[p, :], wo_ref[...]) for p in parts]
    for p, y in zip(parts, proj):
        x1 = _layer_norm(ALPHA * x_ref[p, :] + y, g_ref[...], b_ref[...])
        x1_ref[p, :] = x1
        logit_ref[p, :] = _dot(x1, wr_ref[...]) + br_ref[...]


def _oproj(x2, merged, w_o, g, b, w_route, b_route):
    row = lambda width: pl.BlockSpec((OPROJ_TM, width), lambda i: (i, 0))
    full = lambda r, c: pl.BlockSpec((r, c), lambda i: (0, 0))
    return pl.pallas_call(
        _oproj_kernel,
        grid=(TOKENS // OPROJ_TM,),
        in_specs=[row(D_MODEL), row(D_MODEL), full(D_MODEL, D_MODEL), full(1, D_MODEL), full(1, D_MODEL),
                  full(D_MODEL, LANES), full(1, LANES)],
        out_specs=[row(D_MODEL), row(LANES)],
        out_shape=[jax.ShapeDtypeStruct((TOKENS, D_MODEL), jnp.float32),
                   jax.ShapeDtypeStruct((TOKENS, LANES), jnp.float32)],
        compiler_params=_params(("parallel",)),
        name="oproj_ln1",
    )(x2, merged, w_o, g, b, w_route, b_route)


def _route_kernel(logit_ref, rec_ref, rect_ref, cnt_ref, carry_sc):
    @pl.when(pl.program_id(0) == 0)
    def _():
        carry_sc[...] = jnp.zeros_like(carry_sc)

    tm = ROUTE_TILE
    zt = jnp.transpose(logit_ref[...])
    row = lax.broadcasted_iota(jnp.int32, (SUBLANES, tm), 0)
    neg = -jnp.inf
    colmax = lambda a: jnp.max(a, axis=0, keepdims=True)
    first = lambda hit: jnp.min(jnp.where(hit, row, SUBLANES), axis=0, keepdims=True)

    glog = jnp.where(row < N_GROUPS, zt[0:SUBLANES], neg)
    gmax = colmax(glog)
    gsel = first(glog == gmax)
    gw = 1.0 / jnp.sum(jnp.exp(glog - gmax), axis=0, keepdims=True)

    elog = zt[ROUTER_LANE0:ROUTER_LANE0 + EXPERTS_PER_GROUP]
    for g in range(1, N_GROUPS):
        lo = ROUTER_LANE0 + g * EXPERTS_PER_GROUP
        elog = jnp.where(gsel == g, zt[lo:lo + EXPERTS_PER_GROUP], elog)
    top1 = colmax(elog)
    i1 = first(elog == top1)
    rest = jnp.where(row == i1, neg, elog)
    top2 = colmax(rest)
    i2 = first(rest == top2)
    t = jnp.exp(top2 - top1)
    w1 = gw / (1.0 + t)
    w2 = w1 * t

    hit = (row == i1) | (row == i2)
    onehot = jnp.concatenate([jnp.where(hit & (gsel == g), 1.0, 0.0) for g in range(N_GROUPS)], axis=0)
    chunks = tm // LANES
    stacked = jnp.concatenate([onehot[:, c * LANES:(c + 1) * LANES] for c in range(chunks)], axis=0)
    ri = lax.broadcasted_iota(jnp.int32, (LANES, LANES), 0)
    ci = lax.broadcasted_iota(jnp.int32, (LANES, LANES), 1)
    upper = jnp.where(ri <= ci, 1.0, 0.0).astype(jnp.bfloat16)
    within = _dot(stacked.astype(jnp.bfloat16), upper)
    run = carry_sc[:, 0:1]
    pieces = []
    for c in range(chunks):
        cum = within[c * N_EXPERTS:(c + 1) * N_EXPERTS, :] + run
        pieces.append(cum)
        run = cum[:, LANES - 1:LANES]
    before = jnp.concatenate(pieces, axis=1) - 1.0
    carry_sc[...] = jnp.broadcast_to(run, carry_sc.shape)
    cnt_ref[...] = jnp.broadcast_to(run, cnt_ref.shape)

    rank_a = jnp.zeros((1, tm), jnp.float32)
    rank_b = jnp.zeros((1, tm), jnp.float32)
    for g in range(N_GROUPS):
        b8 = before[g * EXPERTS_PER_GROUP:(g + 1) * EXPERTS_PER_GROUP, :]
        mine = gsel == g
        rank_a = rank_a + jnp.sum(jnp.where(mine & (row == i1), b8, 0.0), axis=0, keepdims=True)
        rank_b = rank_b + jnp.sum(jnp.where(mine & (row == i2), b8, 0.0), axis=0, keepdims=True)

    base = (gsel * EXPERTS_PER_GROUP).astype(jnp.float32)
    vals = [base + i1.astype(jnp.float32), base + i2.astype(jnp.float32), rank_a, rank_b, w1, w2]
    rect = jnp.zeros((SUBLANES, tm), jnp.float32)
    for k, val in enumerate(vals):
        rect = jnp.where(row == k, val, rect)
    rect_ref[...] = rect
    padded = jnp.concatenate([rect, jnp.zeros((LANES - SUBLANES, tm), jnp.float32)], axis=0)
    rec_ref[...] = jnp.transpose(padded)


def _route(logits):
    return pl.pallas_call(
        _route_kernel,
        grid=(TOKENS // ROUTE_TILE,),
        in_specs=[pl.BlockSpec((ROUTE_TILE, LANES), lambda i: (i, 0))],
        out_specs=[pl.BlockSpec((ROUTE_TILE, LANES), lambda i: (i, 0)),
                   pl.BlockSpec((SUBLANES, ROUTE_TILE), lambda i: (0, i)),
                   pl.BlockSpec((N_EXPERTS, LANES), lambda i: (0, 0))],
        out_shape=[jax.ShapeDtypeStruct((TOKENS, LANES), jnp.float32),
                   jax.ShapeDtypeStruct((SUBLANES, TOKENS), jnp.float32),
                   jax.ShapeDtypeStruct((N_EXPERTS, LANES), jnp.float32)],
        scratch_shapes=[pltpu.VMEM((N_EXPERTS, LANES), jnp.float32)],
        compiler_params=_params(("arbitrary",)),
        name="route",
    )(logits)


def _slots_kernel(start_ref, rect_ref, pos_ref):
    r = rect_ref[...]
    row = lax.broadcasted_iota(jnp.int32, r.shape, 0)

    def pos_of(e, rank):
        e = e.astype(jnp.int32)
        start = jnp.zeros(e.shape, jnp.int32)
        for k in range(N_EXPERTS):
            start = jnp.where(e == k, start_ref[k], start)
        return start + rank.astype(jnp.int32)

    pos_a = pos_of(r[0:1], r[2:3])
    pos_b = pos_of(r[1:2], r[3:4])
    pos_ref[...] = jnp.where(row == 0, pos_a, jnp.where(row == 1, pos_b, 0))


def _slots(starts, rect):
    grid_spec = pltpu.PrefetchScalarGridSpec(
        num_scalar_prefetch=1,
        grid=(TOKENS // ROUTE_TILE,),
        in_specs=[pl.BlockSpec((SUBLANES, ROUTE_TILE), lambda i, st: (0, i))],
        out_specs=pl.BlockSpec((SUBLANES, ROUTE_TILE), lambda i, st: (0, i)),
    )
    return pl.pallas_call(
        _slots_kernel,
        grid_spec=grid_spec,
        out_shape=jax.ShapeDtypeStruct((SUBLANES, TOKENS), jnp.int32),
        compiler_params=_params(("parallel",)),
        name="slots",
    )(starts, rect)


DISPATCH_TILE = 512
SORTED_ROWS = N_EXPERT_TILES * EXPERT_TILE


def _dispatch_kernel(pad_start_ref, pad_n_ref, nvalid_ref, pos_ref, x_ref, xs_hbm, zero_sc, stage_sc, sem,
                     row_sem):
    groups = DISPATCH_TILE // SUBLANES

    @pl.when(pl.program_id(0) == 0)
    def _():
        zero_sc[...] = jnp.zeros_like(zero_sc)

        def tile_copy(t):
            return pltpu.make_async_copy(zero_sc, xs_hbm.at[pl.ds(pl.multiple_of(t * EXPERT_TILE, EXPERT_TILE),
                                                                  EXPERT_TILE)], sem.at[2])

        def start_tile(t, c):
            tile_copy(t).start()
            return c

        def wait_tile(t, c):
            tile_copy(t).wait()
            return c
        lax.fori_loop(nvalid_ref[0], N_EXPERT_TILES, start_tile, 0)
        lax.fori_loop(nvalid_ref[0], N_EXPERT_TILES, wait_tile, 0)

        def pad_copies(e, act):
            n = pad_n_ref[e]
            cur = pad_start_ref[e]
            for bit in range((EXPERT_TILE - 1).bit_length()):
                size = 1 << bit
                has = lax.bitwise_and(n, size)

                @pl.when(has != 0)
                def _(cur=cur, size=size):
                    at = pl.multiple_of(cur, math.gcd(size, EXPERT_TILE))
                    act(pltpu.make_async_copy(zero_sc.at[pl.ds(0, size)], xs_hbm.at[pl.ds(at, size)],
                                              sem.at[1]))
                cur = cur + has

        def start_pads(e, c):
            pad_copies(e, lambda cp: cp.start())
            return c

        def wait_pads(e, c):
            pad_copies(e, lambda cp: cp.wait())
            return c
        lax.fori_loop(0, N_EXPERTS, start_pads, 0)
        lax.fori_loop(0, N_EXPERTS, wait_pads, 0)

    j = pl.program_id(0)
    slot = lax.rem(j, 2)
    stage_sc[slot] = x_ref[...].reshape(groups, SUBLANES, D_MODEL)

    def body(gi, c):
        for s in range(SUBLANES):
            for k in range(2):
                p = pos_ref[0, 0, k * DISPATCH_TILE + gi * SUBLANES + s]
                pltpu.make_async_copy(stage_sc.at[slot, gi, pl.ds(s, 1)], xs_hbm.at[pl.ds(p, 1)],
                                      row_sem.at[slot]).start(priority=k)
        return c
    lax.fori_loop(0, groups, body, 0)

    def wait_slot(sl):
        for _ in range(2 * groups):
            pltpu.make_async_copy(stage_sc.at[sl, 0], xs_hbm.at[pl.ds(0, SUBLANES)], row_sem.at[sl]).wait()

    @pl.when(j > 0)
    def _():
        wait_slot(1 - slot)

    @pl.when(j == pl.num_programs(0) - 1)
    def _():
        wait_slot(slot)


def _dispatch(pad_start, pad_n, n_valid_tiles, pos3, x1):
    nt = TOKENS // DISPATCH_TILE
    groups = DISPATCH_TILE // SUBLANES
    grid_spec = pltpu.PrefetchScalarGridSpec(
        num_scalar_prefetch=3,
        grid=(nt,),
        in_specs=[pl.BlockSpec((1, 1, 2 * DISPATCH_TILE), lambda j, *_: (j, 0, 0), memory_space=pltpu.SMEM),
                  pl.BlockSpec((DISPATCH_TILE, D_MODEL), lambda j, *_: (j, 0))],
        out_specs=pl.BlockSpec(memory_space=pl.ANY),
        scratch_shapes=[pltpu.VMEM((EXPERT_TILE, D_MODEL), jnp.float32),
                        pltpu.VMEM((2, groups, SUBLANES, D_MODEL), jnp.float32),
                        pltpu.SemaphoreType.DMA((3,)),
                        pltpu.SemaphoreType.DMA((2,))],
    )
    return pl.pallas_call(
        _dispatch_kernel,
        grid_spec=grid_spec,
        out_shape=jax.ShapeDtypeStruct((SORTED_ROWS, D_MODEL), jnp.float32),
        compiler_params=_params(("arbitrary",)),
        name="dispatch",
    )(pad_start, pad_n, n_valid_tiles, pos3, x1)


def _ffn_kernel(texp_ref, tvalid_ref, tslot_ref, tfirst_ref, tnext_ref, nvalid_ref,
                xs_ref, wg_hbm, wu_hbm, wd_hbm, y_ref, wg_buf, wu_buf, wd_buf, sem):
    j = pl.program_id(0)
    valid = tvalid_ref[j] == 1
    slot = tslot_ref[j]

    def weight_copies(e, s):
        return (pltpu.make_async_copy(wg_hbm.at[e], wg_buf.at[s], sem.at[s, 0]),
                pltpu.make_async_copy(wu_hbm.at[e], wu_buf.at[s], sem.at[s, 1]),
                pltpu.make_async_copy(wd_hbm.at[e], wd_buf.at[s], sem.at[s, 2]))

    @pl.when(valid & (tfirst_ref[j] == 1))
    def _():
        @pl.when(j == 0)
        def _():
            for cp in weight_copies(texp_ref[j], slot):
                cp.start()

        for cp in weight_copies(texp_ref[j], slot):
            cp.wait()

        @pl.when(tnext_ref[j] >= 0)
        def _():
            for cp in weight_copies(tnext_ref[j], 1 - slot):
                cp.start(priority=1)

    @pl.when(valid)
    def _():
        xt = xs_ref[...]
        a = _dot(xt, wg_buf[slot])
        u = _dot(xt, wu_buf[slot])
        hid = (a * jax.nn.sigmoid(a)) * u
        y_ref[...] = _dot(hid, wd_buf[slot])

    @pl.when(jnp.logical_not(valid))
    def _():
        y_ref[...] = jnp.zeros_like(y_ref)


def _expert_ffn(tables, xs, w_gate, w_up, w_down):
    nt = N_EXPERT_TILES
    xmap = lambda j, te, tv, ts, tf, tn, nv: (jnp.minimum(j, nv[0] - 1), 0)
    grid_spec = pltpu.PrefetchScalarGridSpec(
        num_scalar_prefetch=6,
        grid=(nt,),
        in_specs=[pl.BlockSpec((EXPERT_TILE, D_MODEL), xmap),
                  pl.BlockSpec(memory_space=pl.ANY),
                  pl.BlockSpec(memory_space=pl.ANY),
                  pl.BlockSpec(memory_space=pl.ANY)],
        out_specs=pl.BlockSpec((EXPERT_TILE, D_MODEL), lambda j, *_: (j, 0)),
        scratch_shapes=[pltpu.VMEM((2, D_MODEL, D_EXPERT), jnp.float32),
                        pltpu.VMEM((2, D_MODEL, D_EXPERT), jnp.float32),
                        pltpu.VMEM((2, D_EXPERT, D_MODEL), jnp.float32),
                        pltpu.SemaphoreType.DMA((2, 3))],
    )
    return pl.pallas_call(
        _ffn_kernel,
        grid_spec=grid_spec,
        out_shape=jax.ShapeDtypeStruct((nt * EXPERT_TILE, D_MODEL), jnp.float32),
        compiler_params=_params(("arbitrary",)),
        name="expert_ffn",
    )(*tables, xs, w_gate, w_up, w_down)


def _combine_kernel(pos_cur_ref, pos_next_ref, r_ref, x1_ref, y_hbm, g_ref, b_ref, o_ref, ybuf, sem):
    j = pl.program_id(0)
    nt = pl.num_programs(0)
    slot = lax.rem(j, 2)
    groups = COMBINE_TILE // SUBLANES

    def gather(pos_ref, dst_slot):
        def body(gi, c):
            for s in range(SUBLANES):
                for k in range(2):
                    p = pos_ref[0, 0, k * COMBINE_TILE + gi * SUBLANES + s]
                    pltpu.make_async_copy(y_hbm.at[pl.ds(p, 1)], ybuf.at[dst_slot, k, gi, pl.ds(s, 1)],
                                          sem.at[dst_slot]).start(priority=k)
            return c
        lax.fori_loop(0, groups, body, 0)

    @pl.when(j == 0)
    def _():
        gather(pos_cur_ref, 0)

    @pl.when(j + 1 < nt)
    def _():
        gather(pos_next_ref, 1 - slot)

    for _ in range(2 * groups):
        pltpu.make_async_copy(y_hbm.at[pl.ds(0, SUBLANES)], ybuf.at[slot, 0, 0], sem.at[slot]).wait()
    r = r_ref[...]
    ya = ybuf[slot, 0].reshape(COMBINE_TILE, D_MODEL)
    yb = ybuf[slot, 1].reshape(COMBINE_TILE, D_MODEL)
    hres = ALPHA * x1_ref[...] + r[:, 4:5] * ya + r[:, 5:6] * yb
    o_ref[...] = _layer_norm(hres, g_ref[...], b_ref[...])


def _combine(pos3, routing, x1, y_sorted, g, b):
    nt = TOKENS // COMBINE_TILE
    groups = COMBINE_TILE // SUBLANES
    return pl.pallas_call(
        _combine_kernel,
        grid=(nt,),
        in_specs=[pl.BlockSpec((1, 1, 2 * COMBINE_TILE), lambda j: (j, 0, 0), memory_space=pltpu.SMEM),
                  pl.BlockSpec((1, 1, 2 * COMBINE_TILE), lambda j: (jnp.minimum(j + 1, nt - 1), 0, 0),
                               memory_space=pltpu.SMEM),
                  pl.BlockSpec((COMBINE_TILE, LANES), lambda j: (j, 0)),
                  pl.BlockSpec((COMBINE_TILE, D_MODEL), lambda j: (j, 0)),
                  pl.BlockSpec(memory_space=pl.ANY),
                  pl.BlockSpec((1, D_MODEL), lambda j: (0, 0)),
                  pl.BlockSpec((1, D_MODEL), lambda j: (0, 0))],
        out_specs=pl.BlockSpec((COMBINE_TILE, D_MODEL), lambda j: (j, 0)),
        out_shape=jax.ShapeDtypeStruct((TOKENS, D_MODEL), jnp.float32),
        scratch_shapes=[pltpu.VMEM((2, 2, groups, SUBLANES, D_MODEL), jnp.float32),
                        pltpu.SemaphoreType.DMA((2,))],
        compiler_params=_params(("arbitrary",)),
        name="combine_ln2",
    )(pos3, pos3, routing, x1, y_sorted, g, b)


def _alibi_slopes():
    n = N_DIL_GROUPS * HEADS_PER_GROUP
    return jnp.asarray(2.0 ** (-ALIBI_MAX * np.arange(1, n + 1, dtype=np.float32) / n), jnp.float32)


def kernel(x, mem, ln_mem_g, ln_mem_b, w_in, b_in, w_conv, w_conv_out, w_dil_out, w_mem_kv, w_mem_out, w_o, ln1_g, ln1_b, w_group, b_group, w_router, b_router, w_gate, w_up, w_down, ln2_g, ln2_b):
    assert x.shape == (BATCH, SEQ, D_MODEL) and w_in.shape == (1, D_MODEL, IN_DIM)
    bf16 = jnp.bfloat16
    row = lambda v: v.reshape(1, -1)
    w_in2 = w_in[0].astype(bf16)
    b_in2 = b_in

    kv = _memkv(mem, row(ln_mem_g), row(ln_mem_b), w_mem_kv[0])
    s_conv, xb = _conv_branch(x, w_in2, b_in2, w_conv[0])
    qkv = _qkv_proj(xb, w_in2, b_in2)
    o_mem = _mem_branch(xb, w_in2, b_in2, kv)
    o_dil = _dil_branch(qkv, _alibi_slopes())

    x2 = x.reshape(TOKENS, D_MODEL)
    merged = _merge(xb.reshape(TOKENS, D_MODEL), w_in2, b_in2,
                    s_conv.reshape(TOKENS, CONV_DIM), o_dil.reshape(TOKENS, DIL_OUT_DIM),
                    o_mem.reshape(TOKENS, MEM_DIM),
                    jnp.concatenate([w_conv_out[0], w_dil_out[0], w_mem_out[0]], axis=0).astype(bf16))

    gap = ROUTER_LANE0 - N_GROUPS
    tail = LANES - ROUTER_LANE0 - N_EXPERTS
    w_route = jnp.concatenate(
        [w_group[0], jnp.zeros((D_MODEL, gap), jnp.float32),
         jnp.transpose(w_router[0], (1, 0, 2)).reshape(D_MODEL, N_EXPERTS),
         jnp.zeros((D_MODEL, tail), jnp.float32)], axis=1)
    b_route = jnp.concatenate(
        [b_group[0], jnp.zeros((gap,), jnp.float32), b_router[0].reshape(N_EXPERTS),
         jnp.zeros((tail,), jnp.float32)]).reshape(1, LANES)
    x1, logits = _oproj(x2, merged, w_o[0].astype(bf16), ln1_g, ln1_b, w_route, b_route)

    routing, routing_t, counts_b = _route(logits)

    i32 = jnp.int32
    counts = counts_b[:, 0].astype(i32)
    padded = ((counts + EXPERT_TILE - 1) // EXPERT_TILE) * EXPERT_TILE
    ends = jnp.cumsum(padded)
    starts = ends - padded
    pos = _slots(starts.astype(i32), routing_t)[0:2]

    def pos_tiles(tile):
        return jnp.transpose(pos.reshape(2, TOKENS // tile, tile), (1, 0, 2)).reshape(TOKENS // tile, 1, 2 * tile)

    tile_start = jnp.arange(N_EXPERT_TILES, dtype=i32) * EXPERT_TILE
    tile_expert = jnp.minimum(jnp.sum((ends[None, :] <= tile_start[:, None]).astype(i32), axis=1),
                              N_EXPERTS - 1)
    tile_valid = tile_start < ends[-1]
    prev_expert = jnp.concatenate([jnp.full((1,), -1, i32), tile_expert[:-1]])
    tile_first = tile_valid & (tile_expert != prev_expert)
    tile_slot = (jnp.cumsum(tile_first.astype(i32)) - 1) & 1
    big = N_EXPERTS
    idx = jnp.where(counts > 0, jnp.arange(N_EXPERTS, dtype=i32), big)
    later = jnp.concatenate([lax.cummin(idx[::-1])[::-1][1:], jnp.full((1,), big, i32)])
    next_used = jnp.where(later == big, -1, later)
    n_valid_tiles = (ends[-1:] // EXPERT_TILE).astype(i32)
    tables = (tile_expert, tile_valid.astype(i32), tile_slot, tile_first.astype(i32),
              next_used[tile_expert], n_valid_tiles)

    xs = _dispatch(starts + counts, padded - counts, n_valid_tiles,
                   pos_tiles(DISPATCH_TILE), x1)
    y_sorted = _expert_ffn(tables, xs,
                           w_gate.reshape(N_EXPERTS, D_MODEL, D_EXPERT),
                           w_up.reshape(N_EXPERTS, D_MODEL, D_EXPERT),
                           w_down.reshape(N_EXPERTS, D_EXPERT, D_MODEL))
    out = _combine(pos_tiles(COMBINE_TILE), routing, x1, y_sorted,
                   ln2_g, ln2_b)
    return out.reshape(BATCH, SEQ, D_MODEL)
```
